```python
import math
import jax, jax.numpy as jnp
from jax import lax
import numpy as np

D_MODEL = 1024
BATCH = 8
SEQ = 2048
DEPTH = 2
DEC_BATCH = 128
DEC_SEQ = 8
PAST_LEN = 16384
PAGE_SIZE = 128

N_EVEN = (DEPTH + 1) // 2
N_ODD = DEPTH // 2
H_M = 8
DH_M = D_MODEL // 16
W_M = H_M * DH_M
MLSTM_CHUNK = 64
W_LRU = D_MODEL // 2
LRU_BLOCKS = 8
LRU_BW = W_LRU // LRU_BLOCKS
LRU_CONV = 4
LRU_C = 8.0
IN_EVEN = 4 * W_M + 2 * H_M + 2 * W_LRU
OUT_EVEN = W_M + W_LRU
W_CONV = D_MODEL // 2
CONV_C = 31
H_R = 8
DH_R = D_MODEL // 16
W_R = H_R * DH_R
LORA_W = 64
LORA_A = 64
LORA_G = 128
W_SHIFT = 3 * W_R + LORA_W + LORA_A + LORA_G
IN_ODD = 2 * W_CONV + W_SHIFT
OUT_ODD = W_CONV + W_R
RWKV_DECAY = 0.606531
RWKV_GN_EPS = 64e-5
N_GROUPS = 4
E_PER_GROUP = 4
N_EXPERTS = N_GROUPS * E_PER_GROUP
TOP_K = 2
D_EXPERT = D_MODEL // 4
RMS_EPS = 1e-6
LN_EPS = 1e-5

kernel_name = 'hybrid_mlstm_rglru_conformer_rwkv7_hmoe_step'

F32 = jnp.float32


def split_cols(p, sizes):
    idx = [int(s) for s in np.cumsum(sizes)[:-1]]
    return jnp.split(p, idx, axis=-1)


def rmsnorm(x, g):
    xf = x.astype(F32)
    y = xf * lax.rsqrt(jnp.mean(xf * xf, axis=-1, keepdims=True) + RMS_EPS)
    return (y * g.astype(F32)).astype(x.dtype)


def layernorm(x, g, b, eps):
    xf = x.astype(F32)
    mu = jnp.mean(xf, axis=-1, keepdims=True)
    var = jnp.mean(jnp.square(xf - mu), axis=-1, keepdims=True)
    return ((xf - mu) * lax.rsqrt(var + eps) * g.astype(F32) + b.astype(F32)).astype(x.dtype)


def head_norm(x, n_heads, eps):
    B, T, W = x.shape
    xf = x.astype(F32).reshape(B, T, n_heads, W // n_heads)
    mu = jnp.mean(xf, axis=-1, keepdims=True)
    var = jnp.mean(jnp.square(xf - mu), axis=-1, keepdims=True)
    return ((xf - mu) * lax.rsqrt(var + eps)).reshape(B, T, W)


def modulate(h, shift, scale):
    return h * (1.0 + scale[:, None, :]) + shift[:, None, :]


def causal_dwconv(x, buf, w, b):
    K = w.shape[0]
    xp = jnp.concatenate([buf.astype(x.dtype), x], axis=1)
    y = lax.conv_general_dilated(xp, w[:, None, :].astype(x.dtype), window_strides=(1,),
                                 padding='VALID', dimension_numbers=('NWC', 'WIO', 'NWC'),
                                 feature_group_count=x.shape[-1])
    return y + b.astype(x.dtype), xp[:, xp.shape[1] - (K - 1):]


def mlstm_chunked(q, k, v, i_pre, f_pre, C0, n0, m0):
    B, T, H, D = q.shape
    L = math.gcd(T, MLSTM_CHUNK)
    NC = T // L

    def chunks(a):
        a = a.astype(F32).reshape((B, NC, L) + a.shape[2:])
        return jnp.moveaxis(a, 1, 0)

    xs = (chunks(q), chunks(k), chunks(v), chunks(i_pre), chunks(jax.nn.log_sigmoid(f_pre.astype(F32))))
    causal = jnp.tril(jnp.ones((L, L), dtype=bool))

    def step(carry, blk):
        C, n, m = carry
        qb, kb, vb, ib, fb = blk
        bcum = jnp.cumsum(fb, axis=1)
        g_inter = bcum + m[:, None, :]
        dmat = bcum[:, :, None, :] - bcum[:, None, :, :] + ib[:, None, :, :]
        dmat = jnp.where(causal[None, :, :, None], dmat, -jnp.inf)
        m_t = jnp.maximum(g_inter, jnp.max(dmat, axis=2))
        s = jnp.einsum('bthd,bshd->btsh', qb, kb) * jnp.exp(dmat - m_t[:, :, None, :])
        w_inter = jnp.exp(g_inter - m_t)
        num = jnp.einsum('btsh,bshv->bthv', s, vb) + w_inter[..., None] * jnp.einsum('bthd,bhdv->bthv', qb, C)
        den = jnp.sum(s, axis=2) + w_inter * jnp.einsum('bthd,bhd->bth', qb, n)
        h = num / jnp.maximum(jnp.abs(den), jnp.exp(-m_t))[..., None]
        b_last = bcum[:, -1]
        m_new = m_t[:, -1]
        w_s = jnp.exp(b_last[:, None, :] - bcum + ib - m_new[:, None, :])
        decay = jnp.exp(b_last + m - m_new)
        C_new = decay[..., None, None] * C + jnp.einsum('bsh,bshd,bshv->bhdv', w_s, kb, vb)
        n_new = decay[..., None] * n + jnp.einsum('bsh,bshd->bhd', w_s, kb)
        return (C_new, n_new, m_new), h

    (C1, n1, m1), h = lax.scan(step, (C0.astype(F32), n0.astype(F32), m0.astype(F32)), xs)
    h = jnp.moveaxis(h, 0, 1).reshape(B, T, H, D)
    return h, C1, n1, m1


def _lin_combine(c1, c2):
    a1, b1 = c1
    a2, b2 = c2
    return a1 * a2, a2 * b1 + b2


def rglru(x, h0, w_r, b_r, w_i, b_i, lam):
    B, T, W = x.shape
    xf = x.astype(F32)
    xb = xf.reshape(B, T, LRU_BLOCKS, LRU_BW)
    r = jax.nn.sigmoid(jnp.einsum('btnd,nde->btne', xb, w_r.astype(F32)).reshape(B, T, W) + b_r.astype(F32))
    ig = jax.nn.sigmoid(jnp.einsum('btnd,nde->btne', xb, w_i.astype(F32)).reshape(B, T, W) + b_i.astype(F32))
    log_a = -LRU_C * r * jax.nn.softplus(-lam.astype(F32))
    a = jnp.exp(log_a)
    u = jnp.sqrt(-jnp.expm1(2.0 * log_a)) * (ig * xf)
    u = u.at[:, 0].add(a[:, 0] * h0.astype(F32))
    _, h = lax.associative_scan(_lin_combine, (a, u), axis=1)
    return h, h[:, -1]


def rwkv7_scan(r, w, k, v, kk, a, S0):
    def step(S, xs):
        r_t, w_t, k_t, v_t, kk_t, a_t = xs
        sa = jnp.einsum('bhvk,bhk->bhv', S, kk_t)
        S = (S * w_t[:, :, None, :] - sa[..., None] * (kk_t * a_t)[:, :, None, :]
             + v_t[..., None] * k_t[:, :, None, :])
        return S, jnp.einsum('bhvk,bhk->bhv', S, r_t)

    xs = tuple(jnp.moveaxis(t, 1, 0) for t in (r, w, k, v, kk, a))
    S1, y = lax.scan(step, S0.astype(F32), xs)
    return jnp.moveaxis(y, 0, 1), S1


def even_mixer(h, C0, n0, m0, lh0, lconv0, w_in, b_if, g_head, w_conv, b_conv,
               w_r, b_r, w_i, b_i, lam, w_out):
    B, T, _ = h.shape
    p = jnp.dot(h, w_in)
    q, k, v, o, gif, xr, xg = split_cols(p, [W_M, W_M, W_M, W_M, 2 * H_M, W_LRU, W_LRU])
    gif = gif + b_if
    q = q.reshape(B, T, H_M, DH_M)
    k = k.reshape(B, T, H_M, DH_M) * (DH_M ** -0.5)
    v = v.reshape(B, T, H_M, DH_M)
    hm, C1, n1, m1 = mlstm_chunked(q, k, v, gif[..., :H_M], gif[..., H_M:], C0, n0, m0)
    hm = head_norm(hm.reshape(B, T, W_M), H_M, RMS_EPS) * g_head.astype(F32)
    hm = (hm * jax.nn.sigmoid(o.astype(F32))).astype(h.dtype)
    xc, conv1 = causal_dwconv(xr, lconv0, w_conv, b_conv)
    hl, lh1 = rglru(xc, lh0, w_r, b_r, w_i, b_i, lam)
    hl = (hl * jax.nn.gelu(xg.astype(F32))).astype(h.dtype)
    out = jnp.dot(jnp.concatenate([hm, hl], axis=-1), w_out)
    return out, C1.astype(h.dtype), n1.astype(h.dtype), m1.astype(h.dtype), lh1.astype(h.dtype), conv1


def odd_mixer(h, cc0, S0, sh0, w_in, b_glu, w_dw, b_dw, g_ln, b_ln, mu, w0, wB, a0, aB, gB,
              kk_w, ka_w, rk, g_gn, b_gn, w_out):
    B, T, _ = h.shape
    p = jnp.dot(h, w_in)
    pc, pr = split_cols(p, [2 * W_CONV, W_SHIFT])
    u, gte = split_cols(pc + b_glu, [W_CONV, W_CONV])
    glu = u * jax.nn.sigmoid(gte)
    cc, cc1 = causal_dwconv(glu, cc0, w_dw, b_dw)
    cc = jax.nn.silu(layernorm(cc, g_ln, b_ln, LN_EPS))
    prev = jnp.concatenate([sh0[:, None, :].astype(pr.dtype), pr[:, :-1]], axis=1)
    z = (pr + (prev - pr) * mu).astype(F32)
    sh1 = pr[:, -1]
    r, k, v, zw, za, zg = split_cols(z, [W_R, W_R, W_R, LORA_W, LORA_A, LORA_G])
    w = jnp.exp(-RWKV_DECAY * jax.nn.sigmoid(w0 + jnp.dot(jnp.tanh(zw), wB.astype(F32))))
    a = jax.nn.sigmoid(a0 + jnp.dot(za, aB.astype(F32)))
    g = jnp.dot(jax.nn.sigmoid(zg), gB.astype(F32))
    kk = (k * kk_w).reshape(B, T, H_R, DH_R)
    kk = kk * lax.rsqrt(jnp.sum(kk * kk, axis=-1, keepdims=True) + 1e-12)
    k = k * (1.0 + (a - 1.0) * ka_w)
    hd = lambda t: t.reshape(B, T, H_R, DH_R)
    rh, kh, vh, ah = hd(r), hd(k), hd(v), hd(a)
    y, S1 = rwkv7_scan(rh, hd(w), kh, vh, kk, ah, S0)
    y = head_norm(y.reshape(B, T, W_R), H_R, RWKV_GN_EPS) * g_gn.astype(F32) + b_gn.astype(F32)
    rk_h = rk.astype(F32).reshape(H_R, DH_R)
    bonus = jnp.sum(rh * kh * rk_h, axis=-1, keepdims=True) * vh
    y = ((y + bonus.reshape(B, T, W_R)) * g).astype(h.dtype)
    out = jnp.dot(jnp.concatenate([cc, y], axis=-1), w_out)
    return out, cc1, S1.astype(h.dtype), sh1


def hier_moe(x, wg, bg, we, be, w1, w3, w2):
    B, T, D = x.shape
    xf = x.reshape(B * T, D)
    lg = jnp.dot(xf, wg).astype(F32) + bg.astype(F32)
    pg = jax.nn.softmax(lg, axis=-1)
    gsel = jnp.argmax(lg, axis=-1)
    pg_sel = jnp.take_along_axis(pg, gsel[:, None], axis=1)
    le = (jnp.dot(xf, we).astype(F32) + be.astype(F32)).reshape(B * T, N_GROUPS, E_PER_GROUP)
    le_sel = jnp.take_along_axis(le, gsel[:, None, None], axis=1)[:, 0]
    top_v, top_i = lax.top_k(le_sel, TOP_K)
    gate = pg_sel * jax.nn.softmax(top_v, axis=-1)
    eidx = gsel[:, None] * E_PER_GROUP + top_i
    dense_gate = jnp.sum(jax.nn.one_hot(eidx, N_EXPERTS, dtype=F32) * gate[..., None], axis=1)
    hg = jnp.einsum('nd,edf->nef', xf, w1)
    hu = jnp.einsum('nd,edf->nef', xf, w3)
    hh = jax.nn.silu(hg) * hu * dense_gate[..., None].astype(x.dtype)
    return jnp.einsum('nef,efd->nd', hh, w2).reshape(B, T, D)


def setup_inputs(seed: int = 0) -> dict:
    key = jax.random.key(seed)
    ks = iter(jax.random.split(key, 64))
    nrm = lambda shape, s: s * jax.random.normal(next(ks), shape, F32)
    gain = lambda shape: 1.0 + 0.05 * jax.random.normal(next(ks), shape, F32)
    inp = {}
    inp['x_prompt'] = nrm((BATCH, SEQ, D_MODEL), 1.0)
    inp['x_sample'] = nrm((DEC_BATCH, DEC_SEQ, D_MODEL), 1.0)
    inp['c_prompt'] = nrm((BATCH, D_MODEL), 1.0)
    inp['c_sample'] = nrm((DEC_BATCH, D_MODEL), 1.0)
    inp['state_mlstm_C'] = nrm((N_EVEN, DEC_BATCH, H_M, DH_M, DH_M), 0.1)
    inp['state_mlstm_n'] = nrm((N_EVEN, DEC_BATCH, H_M, DH_M), 0.1)
    inp['state_mlstm_m'] = nrm((N_EVEN, DEC_BATCH, H_M), 1.0)
    inp['state_lru_h'] = nrm((N_EVEN, DEC_BATCH, W_LRU), 0.5)
    inp['cache_lru_conv'] = nrm((N_EVEN, DEC_BATCH, LRU_CONV - 1, W_LRU), 1.0)
    inp['cache_conformer_conv'] = nrm((N_ODD, DEC_BATCH, CONV_C - 1, W_CONV), 0.5)
    inp['state_rwkv_S'] = nrm((N_ODD, DEC_BATCH, H_R, DH_R, DH_R), 0.1)
    inp['cache_rwkv_shift'] = nrm((N_ODD, DEC_BATCH, W_SHIFT), 1.0)
    inp['w_ada'] = nrm((DEPTH, D_MODEL, 6 * D_MODEL), 0.3 * D_MODEL ** -0.5)
    inp['b_ada'] = nrm((DEPTH, 6 * D_MODEL), 0.1)
    inp['g_norm_mix'] = gain((DEPTH, D_MODEL))
    inp['g_norm_ffn'] = gain((DEPTH, D_MODEL))
    inp['w_in_even'] = nrm((N_EVEN, D_MODEL, IN_EVEN), D_MODEL ** -0.5)
    b_i = nrm((N_EVEN, H_M), 0.1)
    b_f = 3.0 + nrm((N_EVEN, H_M), 0.5)
    inp['b_mlstm_if'] = jnp.concatenate([b_i, b_f], axis=-1)
    inp['g_mlstm_head'] = gain((N_EVEN, W_M))
    inp['w_lru_conv'] = nrm((N_EVEN, LRU_CONV, W_LRU), LRU_CONV ** -0.5)
    inp['b_lru_conv'] = nrm((N_EVEN, W_LRU), 0.02)
    inp['w_lru_r'] = nrm((N_EVEN, LRU_BLOCKS, LRU_BW, LRU_BW), LRU_BW ** -0.5)
    inp['b_lru_r'] = nrm((N_EVEN, W_LRU), 0.02)
    inp['w_lru_i'] = nrm((N_EVEN, LRU_BLOCKS, LRU_BW, LRU_BW), LRU_BW ** -0.5)
    inp['b_lru_i'] = nrm((N_EVEN, W_LRU), 0.02)
    a_base = jax.random.uniform(next(ks), (N_EVEN, W_LRU), F32, 0.9, 0.999) ** (1.0 / LRU_C)
    inp['lru_lambda'] = jnp.log(a_base) - jnp.log1p(-a_base)
    inp['w_out_even'] = nrm((N_EVEN, OUT_EVEN, D_MODEL), OUT_EVEN ** -0.5)
    inp['w_in_odd'] = nrm((N_ODD, D_MODEL, IN_ODD), D_MODEL ** -0.5)
    inp['b_glu'] = nrm((N_ODD, 2 * W_CONV), 0.02)
    inp['w_cc_dw'] = nrm((N_ODD, CONV_C, W_CONV), CONV_C ** -0.5)
    inp['b_cc_dw'] = nrm((N_ODD, W_CONV), 0.02)
    inp['g_cc_ln'] = gain((N_ODD, W_CONV))
    inp['b_cc_ln'] = nrm((N_ODD, W_CONV), 0.02)
    inp['rwkv_mu'] = jax.random.uniform(next(ks), (N_ODD, W_SHIFT), F32)
    inp['rwkv_w0'] = nrm((N_ODD, W_R), 0.5)
    inp['rwkv_wB'] = nrm((N_ODD, LORA_W, W_R), 0.5 * LORA_W ** -0.5)
    inp['rwkv_a0'] = nrm((N_ODD, W_R), 0.1)
    inp['rwkv_aB'] = nrm((N_ODD, LORA_A, W_R), 0.5 * LORA_A ** -0.5)
    inp['rwkv_gB'] = nrm((N_ODD, LORA_G, W_R), LORA_G ** -0.5)
    inp['rwkv_kk'] = 0.85 + nrm((N_ODD, W_R), 0.05)
    inp['rwkv_ka'] = gain((N_ODD, W_R))
    inp['rwkv_rk'] = nrm((N_ODD, W_R), 0.1)
    inp['g_rwkv_gn'] = gain((N_ODD, W_R))
    inp['b_rwkv_gn'] = nrm((N_ODD, W_R), 0.02)
    inp['w_out_odd'] = nrm((N_ODD, OUT_ODD, D_MODEL), OUT_ODD ** -0.5)
    inp['w_router_g'] = nrm((DEPTH, D_MODEL, N_GROUPS), D_MODEL ** -0.5)
    inp['b_router_g'] = nrm((DEPTH, N_GROUPS), 0.01)
    inp['w_router_e'] = nrm((DEPTH, D_MODEL, N_EXPERTS), D_MODEL ** -0.5)
    inp['b_router_e'] = nrm((DEPTH, N_EXPERTS), 0.01)
    inp['w_exp_gate'] = nrm((DEPTH, N_EXPERTS, D_MODEL, D_EXPERT), D_MODEL ** -0.5)
    inp['w_exp_up'] = nrm((DEPTH, N_EXPERTS, D_MODEL, D_EXPERT), D_MODEL ** -0.5)
    inp['w_exp_down'] = nrm((DEPTH, N_EXPERTS, D_EXPERT, D_MODEL), D_EXPERT ** -0.5)
    inp['g_final'] = gain((D_MODEL,))
    return inp


def reference(x_prompt, x_sample, c_prompt, c_sample,
              state_mlstm_C, state_mlstm_n, state_mlstm_m, state_lru_h, cache_lru_conv,
              cache_conformer_conv, state_rwkv_S, cache_rwkv_shift,
              w_ada, b_ada, g_norm_mix, g_norm_ffn,
              w_in_even, b_mlstm_if, g_mlstm_head, w_lru_conv, b_lru_conv,
              w_lru_r, b_lru_r, w_lru_i, b_lru_i, lru_lambda, w_out_even,
              w_in_odd, b_glu, w_cc_dw, b_cc_dw, g_cc_ln, b_cc_ln,
              rwkv_mu, rwkv_w0, rwkv_wB, rwkv_a0, rwkv_aB, rwkv_gB, rwkv_kk, rwkv_ka, rwkv_rk,
              g_rwkv_gn, b_rwkv_gn, w_out_odd,
              w_router_g, b_router_g, w_router_e, b_router_e,
              w_exp_gate, w_exp_up, w_exp_down, g_final):

    def run(x, c, mC, mn, mm, lh, lconv, ccb, rS, rsh):
        oC, on, om, oh, oconv, occ, oS, osh = [], [], [], [], [], [], [], []
        for l in range(DEPTH):
            mod = jnp.dot(c, w_ada[l]) + b_ada[l]
            sh1, sc1, gt1, sh2, sc2, gt2 = jnp.split(mod, 6, axis=-1)
            h = modulate(rmsnorm(x, g_norm_mix[l]), sh1, sc1)
            if l % 2 == 0:
                j = l // 2
                mix, C1, n1, m1, h1, conv1 = even_mixer(
                    h, mC[j], mn[j], mm[j], lh[j], lconv[j], w_in_even[j], b_mlstm_if[j],
                    g_mlstm_head[j], w_lru_conv[j], b_lru_conv[j], w_lru_r[j], b_lru_r[j],
                    w_lru_i[j], b_lru_i[j], lru_lambda[j], w_out_even[j])
                oC.append(C1); on.append(n1); om.append(m1); oh.append(h1); oconv.append(conv1)
            else:
                j = l // 2
                mix, cc1, S1, s1 = odd_mixer(
                    h, ccb[j], rS[j], rsh[j], w_in_odd[j], b_glu[j], w_cc_dw[j], b_cc_dw[j],
                    g_cc_ln[j], b_cc_ln[j], rwkv_mu[j], rwkv_w0[j], rwkv_wB[j], rwkv_a0[j],
                    rwkv_aB[j], rwkv_gB[j], rwkv_kk[j], rwkv_ka[j], rwkv_rk[j], g_rwkv_gn[j],
                    b_rwkv_gn[j], w_out_odd[j])
                occ.append(cc1); oS.append(S1); osh.append(s1)
            x = x + gt1[:, None, :] * mix
            h = modulate(rmsnorm(x, g_norm_ffn[l]), sh2, sc2)
            x = x + gt2[:, None, :] * hier_moe(h, w_router_g[l], b_router_g[l], w_router_e[l],
                                               b_router_e[l], w_exp_gate[l], w_exp_up[l], w_exp_down[l])
        y = rmsnorm(x, g_final)
        return y, (jnp.stack(oC), jnp.stack(on), jnp.stack(om), jnp.stack(oh), jnp.stack(oconv),
                   jnp.stack(occ), jnp.stack(oS), jnp.stack(osh))

    B = x_prompt.shape[0]
    dt = x_prompt.dtype
    y_prompt, st_p = run(
        x_prompt, c_prompt,
        jnp.zeros((N_EVEN, B, H_M, DH_M, DH_M), dt), jnp.zeros((N_EVEN, B, H_M, DH_M), dt),
        jnp.zeros((N_EVEN, B, H_M), dt), jnp.zeros((N_EVEN, B, W_LRU), dt),
        jnp.zeros((N_EVEN, B, LRU_CONV - 1, W_LRU), dt), jnp.zeros((N_ODD, B, CONV_C - 1, W_CONV), dt),
        jnp.zeros((N_ODD, B, H_R, DH_R, DH_R), dt), jnp.zeros((N_ODD, B, W_SHIFT), dt))
    y_sample, st_s = run(x_sample, c_sample, state_mlstm_C, state_mlstm_n, state_mlstm_m,
                         state_lru_h, cache_lru_conv, cache_conformer_conv, state_rwkv_S,
                         cache_rwkv_shift)
    p_C, p_n, p_m, p_h, p_conv, p_cc, p_S, p_sh = st_p
    s_C, s_n, s_m, s_h, s_conv, s_cc, s_S, s_sh = st_s
    return (y_prompt, y_sample, p_C, p_n, p_m, p_h, p_conv, p_cc, p_S, p_sh,
            s_C, s_n, s_m, s_h, s_conv, s_cc, s_S, s_sh)
```

```python
import functools

import jax
import jax.numpy as jnp
from jax import lax
from jax.experimental import pallas as pl
from jax.experimental.pallas import tpu as pltpu

F32 = jnp.float32
BF = jnp.bfloat16

D_MODEL = 1024
DEPTH = 2
H = 8
DH = 64
W = 512
LRU_CONV = 4
LRU_C = 8.0
CONV_C = 31
LORA_W = 64
LORA_A = 64
LORA_G = 128
RWKV_DECAY = 0.606531
RWKV_GN_EPS = 64e-5
N_GROUPS = 4
E_PER_GROUP = 4
N_EXPERTS = 16
D_EXPERT = 256
RMS_EPS = 1e-6
LN_EPS = 1e-5
IN_EVEN_PAD = 6 * W + 128
IN_ODD = 2 * W + 3 * W + LORA_W + LORA_A + LORA_G

ROW_TILE = 512
VMEM_LIMIT = 48 * 1024 * 1024


def _mm(a, b):
    return jnp.dot(a.astype(BF), b.astype(BF), preferred_element_type=F32)


def _mm_nt(a, b):
    return lax.dot_general(a.astype(BF), b.astype(BF), (((1,), (1,)), ((), ())),
                           preferred_element_type=F32)


def _mm_tn(a, b):
    return lax.dot_general(a.astype(BF), b.astype(BF), (((0,), (0,)), ((), ())),
                           preferred_element_type=F32)


def _split3(x):
    hi = x.astype(BF)
    r = x - hi.astype(F32)
    mid = r.astype(BF)
    lo = (r - mid.astype(F32)).astype(BF)
    return hi, mid, lo


def _mm_mask_l(mask, x):
    return sum(jnp.dot(mask, p, preferred_element_type=F32) for p in _split3(x))


def _mm_mask_r(x, mask):
    return sum(jnp.dot(p, mask, preferred_element_type=F32) for p in _split3(x))


def _sigmoid(x):
    return 1.0 / (1.0 + jnp.exp(-x))


def _rows(v, tm):
    nb, _, c = v.shape
    if nb == 1:
        return v[0]
    return jnp.broadcast_to(v, (nb, tm // nb, c)).reshape(tm, c)


def _mod_spec(T, tm, c):
    if tm <= T:
        per = T // tm
        return pl.BlockSpec((1, 1, c), lambda i, *_: (i // per, 0, 0))
    return pl.BlockSpec((tm // T, 1, c), lambda i, *_: (i, 0, 0))


def _norm_mod(x, g, sh, sc):
    y = x * lax.rsqrt(jnp.mean(x * x, axis=-1, keepdims=True) + RMS_EPS) * g
    return y * (1.0 + sc) + sh


def _params(sem):
    return pltpu.CompilerParams(dimension_semantics=sem, vmem_limit_bytes=VMEM_LIMIT)


def _ada_kernel(c_ref, w_ref, b_ref, o_ref):
    o_ref[0] = _mm(c_ref[...], w_ref[0]) + b_ref[0]


def _ada(c_all, w_bf, b):
    nb = c_all.shape[0]
    tn = 1536
    return pl.pallas_call(
        _ada_kernel,
        grid=(DEPTH, 6 * D_MODEL // tn),
        in_specs=[pl.BlockSpec((nb, D_MODEL), lambda l, j: (0, 0)),
                  pl.BlockSpec((1, D_MODEL, tn), lambda l, j: (l, 0, j)),
                  pl.BlockSpec((1, 1, tn), lambda l, j: (l, 0, j))],
        out_specs=pl.BlockSpec((1, nb, tn), lambda l, j: (l, 0, j)),
        out_shape=jax.ShapeDtypeStruct((DEPTH, nb, 6 * D_MODEL), F32),
        compiler_params=_params(("parallel", "parallel")),
        name="ada",
    )(c_all, w_bf, b.reshape(DEPTH, 1, 6 * D_MODEL))


def _inproj_kernel(x_ref, g_ref, sh_ref, sc_ref, w_ref, o_ref, h_scr):
    tm = x_ref.shape[0]

    @pl.when(pl.program_id(1) == 0)
    def _():
        h = _norm_mod(x_ref[...], g_ref[...], _rows(sh_ref[...], tm), _rows(sc_ref[...], tm))
        h_scr[...] = h.astype(BF)

    o_ref[...] = jnp.dot(h_scr[...], w_ref[...], preferred_element_type=F32)


def _in_proj(x2, g, sh, sc, w_bf, T, tn):
    n, d = x2.shape
    cols = w_bf.shape[1]
    tm = ROW_TILE
    return pl.pallas_call(
        _inproj_kernel,
        grid=(n // tm, cols // tn),
        in_specs=[pl.BlockSpec((tm, d), lambda i, j: (i, 0)),
                  pl.BlockSpec((1, d), lambda i, j: (0, 0)),
                  _mod_spec(T, tm, d), _mod_spec(T, tm, d),
                  pl.BlockSpec((d, tn), lambda i, j: (0, j))],
        out_specs=pl.BlockSpec((tm, tn), lambda i, j: (i, j)),
        out_shape=jax.ShapeDtypeStruct((n, cols), F32),
        scratch_shapes=[pltpu.VMEM((tm, d), BF)],
        compiler_params=_params(("parallel", "arbitrary")),
        name="in_proj",
    )(x2, g.reshape(1, d), sh, sc, w_bf)


def _outproj_kernel(x_ref, a_ref, b_ref, w_ref, gt_ref, o_ref):
    tm = x_ref.shape[0]
    mix = (jnp.dot(a_ref[...], w_ref[0:W, :], preferred_element_type=F32)
           + jnp.dot(b_ref[...], w_ref[W:2 * W, :], preferred_element_type=F32))
    o_ref[...] = x_ref[...] + _rows(gt_ref[...], tm) * mix


def _out_proj(x2, a2, b2, w_bf, gt, T):
    n, d = x2.shape
    tm = ROW_TILE
    return pl.pallas_call(
        _outproj_kernel,
        grid=(n // tm,),
        in_specs=[pl.BlockSpec((tm, d), lambda i: (i, 0)),
                  pl.BlockSpec((tm, W), lambda i: (i, 0)),
                  pl.BlockSpec((tm, W), lambda i: (i, 0)),
                  pl.BlockSpec((2 * W, d), lambda i: (0, 0)),
                  _mod_spec(T, tm, d)],
        out_specs=pl.BlockSpec((tm, d), lambda i: (i, 0)),
        out_shape=jax.ShapeDtypeStruct((n, d), F32),
        compiler_params=_params(("parallel",)),
        name="out_proj",
    )(x2, a2, b2, w_bf, gt)


def _route(logits):
    lane = lax.broadcasted_iota(jnp.int32, logits.shape, 1).astype(F32)
    neg = -jnp.inf
    is_g = lane < N_GROUPS
    lg = jnp.where(is_g, logits, neg)
    mg = jnp.max(lg, axis=1, keepdims=True)
    gsel = jnp.min(jnp.where(lg == mg, lane, 128.0), axis=1, keepdims=True)
    psum = jnp.sum(jnp.where(is_g, jnp.exp(lg - mg), 0.0), axis=1, keepdims=True)
    pg_sel = 1.0 / psum
    lo = N_GROUPS + E_PER_GROUP * gsel
    le = jnp.where((lane >= lo) & (lane < lo + E_PER_GROUP), logits, neg)
    v1 = jnp.max(le, axis=1, keepdims=True)
    i1 = jnp.min(jnp.where(le == v1, lane, 128.0), axis=1, keepdims=True)
    le2 = jnp.where(lane == i1, neg, le)
    v2 = jnp.max(le2, axis=1, keepdims=True)
    i2 = jnp.min(jnp.where(le2 == v2, lane, 128.0), axis=1, keepdims=True)
    e2 = jnp.exp(v2 - v1)
    p1 = 1.0 / (1.0 + e2)
    p2 = e2 / (1.0 + e2)
    return pg_sel * jnp.where(lane == i1, p1, jnp.where(lane == i2, p2, 0.0))


def _moe_kernel(x_ref, g_ref, sh_ref, sc_ref, gt_ref, wrh_ref, wrl_ref, br_ref, w1_ref, w3_ref, w2_ref,
                gf_ref, o_ref, h_scr, gate_scr, acc_scr, *, final_norm):
    tm = x_ref.shape[0]
    e = pl.program_id(1)

    @pl.when(e == 0)
    def _():
        h = _norm_mod(x_ref[...], g_ref[...], _rows(sh_ref[...], tm), _rows(sc_ref[...], tm))
        hb = h.astype(BF)
        h_scr[...] = hb
        hl = (h - hb.astype(F32)).astype(BF)
        logits = (jnp.dot(hb, wrh_ref[...], preferred_element_type=F32)
                  + jnp.dot(hl, wrh_ref[...], preferred_element_type=F32)
                  + jnp.dot(hb, wrl_ref[...], preferred_element_type=F32)) + br_ref[...]
        gate_scr[...] = _route(logits)
        acc_scr[...] = jnp.zeros_like(acc_scr)

    hb = h_scr[...]
    hg = jnp.dot(hb, w1_ref[0], preferred_element_type=F32)
    hu = jnp.dot(hb, w3_ref[0], preferred_element_type=F32)
    lane = lax.broadcasted_iota(jnp.int32, gate_scr.shape, 1)
    ge = jnp.sum(jnp.where(lane == e + N_GROUPS, gate_scr[...], 0.0), axis=1, keepdims=True)
    hh = hg * _sigmoid(hg) * hu * ge
    acc_scr[...] += jnp.dot(hh.astype(BF), w2_ref[0], preferred_element_type=F32)

    @pl.when(e == N_EXPERTS - 1)
    def _():
        y = x_ref[...] + _rows(gt_ref[...], tm) * acc_scr[...]
        if final_norm:
            y = y * lax.rsqrt(jnp.mean(y * y, axis=-1, keepdims=True) + RMS_EPS) * gf_ref[...]
        o_ref[...] = y


def _moe(x2, g, sh, sc, gt, wr_hi, wr_lo, br, w1, w3, w2, g_final, T, final_norm):
    n, d = x2.shape
    tm = ROW_TILE
    return pl.pallas_call(
        functools.partial(_moe_kernel, final_norm=final_norm),
        grid=(n // tm, N_EXPERTS),
        in_specs=[pl.BlockSpec((tm, d), lambda i, e: (i, 0)),
                  pl.BlockSpec((1, d), lambda i, e: (0, 0)),
                  _mod_spec(T, tm, d), _mod_spec(T, tm, d), _mod_spec(T, tm, d),
                  pl.BlockSpec((d, 128), lambda i, e: (0, 0)),
                  pl.BlockSpec((d, 128), lambda i, e: (0, 0)),
                  pl.BlockSpec((1, 128), lambda i, e: (0, 0)),
                  pl.BlockSpec((1, d, D_EXPERT), lambda i, e: (e, 0, 0)),
                  pl.BlockSpec((1, d, D_EXPERT), lambda i, e: (e, 0, 0)),
                  pl.BlockSpec((1, D_EXPERT, d), lambda i, e: (e, 0, 0)),
                  pl.BlockSpec((1, d), lambda i, e: (0, 0))],
        out_specs=pl.BlockSpec((tm, d), lambda i, e: (i, 0)),
        out_shape=jax.ShapeDtypeStruct((n, d), F32),
        scratch_shapes=[pltpu.VMEM((tm, d), BF), pltpu.VMEM((tm, 128), F32), pltpu.VMEM((tm, d), F32)],
        compiler_params=_params(("parallel", "arbitrary")),
        name="moe",
    )(x2, g.reshape(1, d), sh, sc, gt, wr_hi, wr_lo, br, w1, w3, w2, g_final.reshape(1, d))


def _log_sigmoid(x):
    return jnp.minimum(x, 0.0) - jnp.log(1.0 + jnp.exp(-jnp.abs(x)))


def _mlstm_kernel(q_ref, k_ref, v_ref, o_ref, g_ref, gt_ref, bif_ref, bift_ref, gh_ref, c0_ref, n0_ref, m0_ref,
                  h_ref, c1_ref, n1_ref, m1_ref, c_scr, n_scr, m_scr, *, L, nc):
    c = pl.program_id(1)

    @pl.when(c == 0)
    def _():
        c_scr[...] = c0_ref[0]
        n_scr[...] = n0_ref[0]
        m_scr[...] = m0_ref[0]

    g = g_ref[0] + bif_ref[...]
    gt = gt_ref[0] + bift_ref[...]
    row = lax.broadcasted_iota(jnp.int32, (L, L), 0)
    col = lax.broadcasted_iota(jnp.int32, (L, L), 1)
    causal = row >= col
    tri = jnp.where(causal, 1.0, 0.0).astype(BF)
    tri_u = jnp.where(row <= col, 1.0, 0.0).astype(BF)
    bcum = _mm_mask_l(tri, _log_sigmoid(g))
    bcum_t = _mm_mask_r(_log_sigmoid(gt), tri_u)
    m_prev = m_scr[...]
    outs, m_news = [], []
    for h in range(H):
        sl = slice(h * DH, (h + 1) * DH)
        qh = q_ref[0, :, sl]
        kh = k_ref[0, :, sl] * (DH ** -0.5)
        vh = v_ref[0, :, sl]
        bc = bcum[:, H + h:H + h + 1]
        ic = g[:, h:h + 1]
        xrow = gt[h:h + 1, :] - bcum_t[H + h:H + h + 1, :]
        mp = m_prev[:, h:h + 1]
        dmat = jnp.where(causal, bc + xrow, -jnp.inf)
        g_inter = bc + mp
        m_t = jnp.maximum(g_inter, jnp.max(dmat, axis=1, keepdims=True))
        s = _mm_nt(qh, kh) * jnp.exp(dmat - m_t)
        w_inter = jnp.exp(g_inter - m_t)
        ch = c_scr[h]
        nh = n_scr[h:h + 1, :]
        num = _mm(s, vh) + w_inter * _mm(qh, ch)
        den = jnp.sum(s, axis=1, keepdims=True) + w_inter * jnp.sum(qh * nh, axis=1, keepdims=True)
        hh = num / jnp.maximum(jnp.abs(den), jnp.exp(-m_t))
        b_last = bc[L - 1:L, :]
        m_new = m_t[L - 1:L, :]
        w_s = jnp.exp(b_last - bc + ic - m_new)
        decay = jnp.exp(b_last + mp - m_new)
        kw = kh * w_s
        c_scr[h] = decay * ch + _mm_tn(kw, vh)
        n_scr[h:h + 1, :] = decay * nh + jnp.sum(kw, axis=0, keepdims=True)
        m_news.append(m_new)
        mu = jnp.mean(hh, axis=1, keepdims=True)
        var = jnp.mean(jnp.square(hh - mu), axis=1, keepdims=True)
        hn = (hh - mu) * lax.rsqrt(var + RMS_EPS) * gh_ref[:, sl]
        outs.append(hn * _sigmoid(o_ref[0, :, sl]))
    h_ref[0] = jnp.concatenate(outs, axis=1).astype(BF)
    m_scr[...] = jnp.concatenate(m_news, axis=1)

    @pl.when(c == nc - 1)
    def _():
        c1_ref[0] = c_scr[...]
        n1_ref[0] = n_scr[...]
        m1_ref[0] = m_scr[...]


def _mlstm(p3, gif_t, bif, g_head, c0, n0, m0, L):
    b, t, _ = p3.shape
    nc = t // L
    col = lambda j: pl.BlockSpec((1, L, W), lambda i, c, j=j: (i, c, j))
    bif_pad = jnp.zeros((1, 128), F32).at[0, :2 * H].set(bif)
    return pl.pallas_call(
        functools.partial(_mlstm_kernel, L=L, nc=nc),
        grid=(b, nc),
        in_specs=[col(0), col(1), col(2), col(3),
                  pl.BlockSpec((1, L, 128), lambda i, c: (i, c, 6 * W // 128)),
                  pl.BlockSpec((1, 2 * H, L), lambda i, c: (i, 0, c)),
                  pl.BlockSpec((1, 128), lambda i, c: (0, 0)),
                  pl.BlockSpec((2 * H, 1), lambda i, c: (0, 0)),
                  pl.BlockSpec((1, W), lambda i, c: (0, 0)),
                  pl.BlockSpec((1, H, DH, DH), lambda i, c: (i, 0, 0, 0)),
                  pl.BlockSpec((1, H, DH), lambda i, c: (i, 0, 0)),
                  pl.BlockSpec((1, 1, H), lambda i, c: (i, 0, 0))],
        out_specs=[pl.BlockSpec((1, L, W), lambda i, c: (i, c, 0)),
                   pl.BlockSpec((1, H, DH, DH), lambda i, c: (i, 0, 0, 0)),
                   pl.BlockSpec((1, H, DH), lambda i, c: (i, 0, 0)),
                   pl.BlockSpec((1, 1, H), lambda i, c: (i, 0, 0))],
        out_shape=[jax.ShapeDtypeStruct((b, t, W), BF),
                   jax.ShapeDtypeStruct((b, H, DH, DH), F32),
                   jax.ShapeDtypeStruct((b, H, DH), F32),
                   jax.ShapeDtypeStruct((b, 1, H), F32)],
        scratch_shapes=[pltpu.VMEM((H, DH, DH), F32), pltpu.VMEM((H, DH), F32), pltpu.VMEM((1, H), F32)],
        compiler_params=_params(("parallel", "arbitrary")),
        name="mlstm",
    )(p3, p3, p3, p3, p3, gif_t, bif_pad, bif.reshape(2 * H, 1), g_head.reshape(1, W),
      c0, n0, m0.reshape(b, 1, H))


def _gelu_tanh(x):
    return 0.5 * x * (1.0 + jnp.tanh(0.7978845608028654 * (x + 0.044715 * x * x * x)))


def _lru_kernel(xr_ref, xg_ref, cache_ref, h0_ref, wc_ref, bc_ref, wg_ref, bg_ref, lam_ref,
                o_ref, h1_ref, ext_scr, a_scr, u_scr, hs_scr, hc_scr, *, Tc, nc):
    c = pl.program_id(1)
    K1 = LRU_CONV - 1

    @pl.when(c == 0)
    def _():
        ext_scr[8 - K1:8, :] = cache_ref[0]
        hc_scr[...] = h0_ref[0]

    x = xr_ref[0]
    ext_scr[8:8 + Tc, :] = x
    xc = bc_ref[...] + wc_ref[K1:K1 + 1, :] * x
    for d in range(1, LRU_CONV):
        xc = xc + wc_ref[K1 - d:K1 - d + 1, :] * ext_scr[8 - d:8 - d + Tc, :]
    ext_scr[8 - K1:8, :] = ext_scr[8 + Tc - K1:8 + Tc, :]
    gates = _mm(xc, wg_ref[...]) + bg_ref[...]
    r = _sigmoid(gates[:, 0:W])
    ig = _sigmoid(gates[:, W:2 * W])
    lam = lam_ref[...]
    softplus_neg = jnp.maximum(-lam, 0.0) + jnp.log(1.0 + jnp.exp(-jnp.abs(lam)))
    log_a = -LRU_C * r * softplus_neg
    a_scr[...] = jnp.exp(log_a)
    th = jnp.tanh(log_a)
    one_minus_a2 = -2.0 * th / (1.0 - th)
    u_scr[...] = jnp.sqrt(one_minus_a2) * (ig * xc)

    def body(t, h):
        h = a_scr[pl.ds(t, 1), :] * h + u_scr[pl.ds(t, 1), :]
        hs_scr[pl.ds(t, 1), :] = h
        return h

    h_fin = lax.fori_loop(0, Tc, body, hc_scr[...], unroll=8)
    hc_scr[...] = h_fin
    o_ref[0] = (hs_scr[...] * _gelu_tanh(xg_ref[0])).astype(BF)

    @pl.when(c == nc - 1)
    def _():
        h1_ref[0] = h_fin


def _lru(p3, cache, h0, w_conv, b_conv, wg_bf, bg, lam, Tc):
    b, t, _ = p3.shape
    nc = t // Tc
    return pl.pallas_call(
        functools.partial(_lru_kernel, Tc=Tc, nc=nc),
        grid=(b, nc),
        in_specs=[pl.BlockSpec((1, Tc, W), lambda i, c: (i, c, 4)),
                  pl.BlockSpec((1, Tc, W), lambda i, c: (i, c, 5)),
                  pl.BlockSpec((1, LRU_CONV - 1, W), lambda i, c: (i, 0, 0)),
                  pl.BlockSpec((1, 1, W), lambda i, c: (i, 0, 0)),
                  pl.BlockSpec((LRU_CONV, W), lambda i, c: (0, 0)),
                  pl.BlockSpec((1, W), lambda i, c: (0, 0)),
                  pl.BlockSpec((W, 2 * W), lambda i, c: (0, 0)),
                  pl.BlockSpec((1, 2 * W), lambda i, c: (0, 0)),
                  pl.BlockSpec((1, W), lambda i, c: (0, 0))],
        out_specs=[pl.BlockSpec((1, Tc, W), lambda i, c: (i, c, 0)),
                   pl.BlockSpec((1, 1, W), lambda i, c: (i, 0, 0))],
        out_shape=[jax.ShapeDtypeStruct((b, t, W), BF), jax.ShapeDtypeStruct((b, 1, W), F32)],
        scratch_shapes=[pltpu.VMEM((8 + Tc, W), F32), pltpu.VMEM((Tc, W), F32), pltpu.VMEM((Tc, W), F32),
                        pltpu.VMEM((Tc, W), F32), pltpu.VMEM((1, W), F32)],
        compiler_params=_params(("parallel", "arbitrary")),
        name="lru",
    )(p3, p3, cache, h0.reshape(b, 1, W), w_conv, b_conv.reshape(1, W), wg_bf, bg.reshape(1, 2 * W),
      lam.reshape(1, W))


CONF_ROWS = 32


def _conf_kernel(u_ref, gte_ref, cache_ref, bu_ref, bg_ref, wdw_ref, bdw_ref, gln_ref, bln_ref,
                 o_ref, cache1_ref, ext_scr, *, Tc, nc):
    c = pl.program_id(1)
    K1 = CONV_C - 1
    base = 32 - K1

    @pl.when(c == 0)
    def _():
        ext_scr[base:32, :] = cache_ref[0]

    u = u_ref[0] + bu_ref[...]
    gte = gte_ref[0] + bg_ref[...]
    ext_scr[32:32 + Tc, :] = u * _sigmoid(gte)
    rb = min(CONF_ROWS, Tc)
    for r0 in range(0, Tc, rb):
        acc = bdw_ref[...] + wdw_ref[0:1, :] * ext_scr[base + r0:base + r0 + rb, :]
        for j in range(1, CONV_C):
            acc = acc + wdw_ref[j:j + 1, :] * ext_scr[base + r0 + j:base + r0 + j + rb, :]
        mu = jnp.mean(acc, axis=1, keepdims=True)
        var = jnp.mean(jnp.square(acc - mu), axis=1, keepdims=True)
        y = (acc - mu) * lax.rsqrt(var + LN_EPS) * gln_ref[...] + bln_ref[...]
        o_ref[0, r0:r0 + rb, :] = (y * _sigmoid(y)).astype(BF)
    tail = ext_scr[base + Tc:32 + Tc, :]
    ext_scr[base:32, :] = tail

    @pl.when(c == nc - 1)
    def _():
        cache1_ref[0] = tail


def _conf(p3, cache, b_glu, w_dw, b_dw, g_ln, b_ln, Tc):
    b, t, _ = p3.shape
    nc = t // Tc
    vec = lambda: pl.BlockSpec((1, W), lambda i, c: (0, 0))
    return pl.pallas_call(
        functools.partial(_conf_kernel, Tc=Tc, nc=nc),
        grid=(b, nc),
        in_specs=[pl.BlockSpec((1, Tc, W), lambda i, c: (i, c, 0)),
                  pl.BlockSpec((1, Tc, W), lambda i, c: (i, c, 1)),
                  pl.BlockSpec((1, CONV_C - 1, W), lambda i, c: (i, 0, 0)),
                  vec(), vec(),
                  pl.BlockSpec((CONV_C, W), lambda i, c: (0, 0)),
                  vec(), vec(), vec()],
        out_specs=[pl.BlockSpec((1, Tc, W), lambda i, c: (i, c, 0)),
                   pl.BlockSpec((1, CONV_C - 1, W), lambda i, c: (i, 0, 0))],
        out_shape=[jax.ShapeDtypeStruct((b, t, W), BF), jax.ShapeDtypeStruct((b, CONV_C - 1, W), F32)],
        scratch_shapes=[pltpu.VMEM((32 + Tc, W), F32)],
        compiler_params=_params(("parallel", "arbitrary")),
        name="conformer",
    )(p3, p3, cache, b_glu[:W].reshape(1, W), b_glu[W:].reshape(1, W), w_dw, b_dw.reshape(1, W),
      g_ln.reshape(1, W), b_ln.reshape(1, W))


def _seg_sum(x, ones_bd):
    hi = x.astype(BF)
    lo = (x - hi.astype(F32)).astype(BF)
    return (jnp.dot(hi, ones_bd, preferred_element_type=F32)
            + jnp.dot(lo, ones_bd, preferred_element_type=F32))


def _rprep_kernel(pr_ref, pk_ref, pv_ref, pz_ref, sr_ref, sk_ref, sv_ref, sz_ref,
                  mr_ref, mk_ref, mv_ref, mz_ref, w0_ref, a0_ref, wba_ref, gb_ref, kkw_ref, kaw_ref, rk_ref,
                  ones_ref, r_ref, lw_ref, k_ref, v_ref, kk_ref, b_ref, bonus_ref, g_ref,
                  qr_scr, qk_scr, qv_scr, qz_scr, *, Tc):
    c = pl.program_id(1)

    @pl.when(c == 0)
    def _():
        qr_scr[...] = sr_ref[0]
        qk_scr[...] = sk_ref[0]
        qv_scr[...] = sv_ref[0]
        qz_scr[...] = sz_ref[0]

    def shift_mix(x_ref, prev_scr, mu_ref):
        x = x_ref[0]
        first = lax.broadcasted_iota(jnp.int32, x.shape, 0) == 0
        prev = jnp.where(first, prev_scr[...], pltpu.roll(x, 1, 0))
        prev_scr[...] = x[Tc - 1:Tc, :]
        return x + (prev - x) * mu_ref[...]

    r = shift_mix(pr_ref, qr_scr, mr_ref)
    k = shift_mix(pk_ref, qk_scr, mk_ref)
    v = shift_mix(pv_ref, qv_scr, mv_ref)
    z = shift_mix(pz_ref, qz_scr, mz_ref)
    zwa = z[:, 0:128]
    lane = lax.broadcasted_iota(jnp.int32, zwa.shape, 1)
    wa = _mm(jnp.where(lane < LORA_W, jnp.tanh(zwa), zwa), wba_ref[...])
    lw = -RWKV_DECAY * _sigmoid(w0_ref[...] + wa[:, 0:W])
    a = _sigmoid(a0_ref[...] + wa[:, W:2 * W])
    g = _mm(_sigmoid(z[:, 128:256]), gb_ref[...])
    ones_bd = ones_ref[...]
    kk = k * kkw_ref[...]
    kk = kk * lax.rsqrt(_seg_sum(kk * kk, ones_bd) + 1e-12)
    k2 = k * (1.0 + (a - 1.0) * kaw_ref[...])
    r_ref[0] = r
    lw_ref[0] = lw
    k_ref[0] = k2
    v_ref[0] = v
    kk_ref[0] = kk
    b_ref[0] = kk * a
    bonus_ref[0] = _seg_sum(r * k2 * rk_ref[...], ones_bd) * v
    g_ref[0] = g


def _rprep(p3, shift, mu, w0, a0, wba_bf, gb_bf, kkw, kaw, rk, ones_bd, Tc):
    b, t, _ = p3.shape
    nc = t // Tc
    vec = lambda: pl.BlockSpec((1, W), lambda i, c: (0, 0))
    st = lambda width: pl.BlockSpec((1, 1, width), lambda i, c: (i, 0, 0))
    out = pl.BlockSpec((1, Tc, W), lambda i, c: (i, c, 0))
    sh3 = shift.reshape(b, 1, -1)
    mu2 = mu.reshape(1, -1)
    return pl.pallas_call(
        functools.partial(_rprep_kernel, Tc=Tc),
        grid=(b, nc),
        in_specs=[pl.BlockSpec((1, Tc, W), lambda i, c: (i, c, 2)),
                  pl.BlockSpec((1, Tc, W), lambda i, c: (i, c, 3)),
                  pl.BlockSpec((1, Tc, W), lambda i, c: (i, c, 4)),
                  pl.BlockSpec((1, Tc, 256), lambda i, c: (i, c, 10)),
                  st(W), st(W), st(W), st(256),
                  vec(), vec(), vec(), pl.BlockSpec((1, 256), lambda i, c: (0, 0)),
                  vec(), vec(),
                  pl.BlockSpec((128, 2 * W), lambda i, c: (0, 0)),
                  pl.BlockSpec((LORA_G, W), lambda i, c: (0, 0)),
                  vec(), vec(), vec(),
                  pl.BlockSpec((W, W), lambda i, c: (0, 0))],
        out_specs=[out] * 8,
        out_shape=[jax.ShapeDtypeStruct((b, t, W), F32)] * 8,
        scratch_shapes=[pltpu.VMEM((1, W), F32), pltpu.VMEM((1, W), F32), pltpu.VMEM((1, W), F32),
                        pltpu.VMEM((1, 256), F32)],
        compiler_params=_params(("parallel", "arbitrary")),
        name="rwkv_prep",
    )(p3, p3, p3, p3,
      sh3[:, :, 0:W], sh3[:, :, W:2 * W], sh3[:, :, 2 * W:3 * W], sh3[:, :, 3 * W:],
      mu2[:, 0:W], mu2[:, W:2 * W], mu2[:, 2 * W:3 * W], mu2[:, 3 * W:],
      w0.reshape(1, W), a0.reshape(1, W), wba_bf, gb_bf, kkw.reshape(1, W), kaw.reshape(1, W),
      rk.reshape(1, W), ones_bd)


def _rscan_kernel(r_ref, lw_ref, k_ref, v_ref, kk_ref, b_ref, bonus_ref, g_ref, gn_ref, bn_ref, s0_ref,
                  y_ref, s1_ref, s_scr, *, L, nc):
    c = pl.program_id(1)

    @pl.when(c == 0)
    def _():
        s_scr[...] = s0_ref[0]

    row = lax.broadcasted_iota(jnp.int32, (L, L), 0)
    col = lax.broadcasted_iota(jnp.int32, (L, L), 1)
    lower = row >= col
    strict = row > col
    eye = jnp.where(row == col, 1.0, 0.0)
    tri = jnp.where(lower, 1.0, 0.0).astype(BF)
    lw = lw_ref[0]
    cum = _mm_mask_l(tri, lw)
    p_in = jnp.exp(cum)
    p_inv = jnp.exp(-cum)
    p_last = p_in[L - 1:L, :]
    kt = (kk_ref[0] * jnp.exp(cum - lw)).astype(BF)
    rt = (r_ref[0] * p_in).astype(BF)
    kh = k_ref[0] * p_inv
    bh = b_ref[0] * p_inv
    khl = (kh * p_last).astype(BF)
    bhl = (bh * p_last).astype(BF)
    kh = kh.astype(BF)
    bh = bh.astype(BF)
    vb = v_ref[0].astype(BF)
    outs = []
    for h in range(H):
        sl = slice(h * DH, (h + 1) * DH)
        kt_h, rt_h, kh_h, bh_h, v_h = kt[:, sl], rt[:, sl], kh[:, sl], bh[:, sl], vb[:, sl]
        a_kb = jnp.where(strict, _mm_nt(kt_h, bh_h), 0.0)
        a_kk = jnp.where(strict, _mm_nt(kt_h, kh_h), 0.0)
        r_k = jnp.where(lower, _mm_nt(rt_h, kh_h), 0.0)
        r_b = jnp.where(lower, _mm_nt(rt_h, bh_h), 0.0)
        x = -a_kb
        inv = eye + x
        n = 2
        while n < L:
            x = _mm(x, x)
            inv = inv + _mm(inv, x)
            n *= 2
        s = s_scr[h]
        u = _mm(inv, _mm_nt(kt_h, s) + _mm(a_kk, v_h))
        y = _mm_nt(rt_h, s) + _mm(r_k, v_h) - _mm(r_b, u)
        s_scr[h] = s * p_last[:, sl] + _mm_tn(v_h, khl[:, sl]) - _mm_tn(u, bhl[:, sl])
        mu = jnp.mean(y, axis=1, keepdims=True)
        var = jnp.mean(jnp.square(y - mu), axis=1, keepdims=True)
        yn = (y - mu) * lax.rsqrt(var + RWKV_GN_EPS) * gn_ref[:, sl] + bn_ref[:, sl]
        outs.append((yn + bonus_ref[0, :, sl]) * g_ref[0, :, sl])
    y_ref[0] = jnp.concatenate(outs, axis=1).astype(BF)

    @pl.when(c == nc - 1)
    def _():
        s1_ref[0] = s_scr[...]


def _rscan(r, lw, k, v, kk, bb, bonus, g, g_gn, b_gn, s0, L):
    b, t, _ = r.shape
    nc = t // L
    blk = lambda: pl.BlockSpec((1, L, W), lambda i, c: (i, c, 0))
    vec = lambda: pl.BlockSpec((1, W), lambda i, c: (0, 0))
    st = pl.BlockSpec((1, H, DH, DH), lambda i, c: (i, 0, 0, 0))
    return pl.pallas_call(
        functools.partial(_rscan_kernel, L=L, nc=nc),
        grid=(b, nc),
        in_specs=[blk() for _ in range(8)] + [vec(), vec(), st],
        out_specs=[blk(), st],
        out_shape=[jax.ShapeDtypeStruct((b, t, W), BF), jax.ShapeDtypeStruct((b, H, DH, DH), F32)],
        scratch_shapes=[pltpu.VMEM((H, DH, DH), F32)],
        compiler_params=_params(("parallel", "arbitrary")),
        name="rwkv_scan",
    )(r, lw, k, v, kk, bb, bonus, g, g_gn.reshape(1, W), b_gn.reshape(1, W), s0)


def _block_diag(w):
    nb, bw, _ = w.shape
    return (jnp.eye(nb, dtype=w.dtype)[:, None, :, None] * w[:, :, None, :]).reshape(nb * bw, nb * bw)


def _chunk(t, target):
    return target if t % target == 0 else t


def _run_group(x, mods, st, wts):
    b, t, d = x.shape
    x2 = x.reshape(b * t, d)
    (mc, mn, mm, lh, lconv, ccb, rs, rsh) = st

    sh1, sc1, gt1, sh2, sc2, gt2 = mods[0]
    e = wts["even"]
    p = _in_proj(x2, wts["g_mix"][0], sh1, sc1, e["w_in"], t, 640)
    p3 = p.reshape(b, t, IN_EVEN_PAD)
    gif_t = jnp.transpose(p3[:, :, 6 * W:6 * W + 2 * H], (0, 2, 1))
    hm, c1, n1, m1 = _mlstm(p3, gif_t, e["b_if"], e["g_head"], mc[0], mn[0], mm[0], _chunk(t, 128))
    hl, lh1 = _lru(p3, lconv[0], lh[0], e["w_conv"], e["b_conv"], e["w_gate"], e["b_gate"], e["lam"],
                   _chunk(t, 256))
    xr = p3[:, :, 4 * W:5 * W]
    conv1 = jnp.concatenate([lconv[0], xr], axis=1)[:, -(LRU_CONV - 1):]
    x2 = _out_proj(x2, hm.reshape(b * t, W), hl.reshape(b * t, W), e["w_out"], gt1, t)
    m = wts["moe"][0]
    x2 = _moe(x2, wts["g_ffn"][0], sh2, sc2, gt2, m["wr_hi"], m["wr_lo"], m["br"], m["w1"], m["w3"], m["w2"],
              wts["g_final"], t, False)

    sh1, sc1, gt1, sh2, sc2, gt2 = mods[1]
    o = wts["odd"]
    p = _in_proj(x2, wts["g_mix"][1], sh1, sc1, o["w_in"], t, 1408)
    p3 = p.reshape(b, t, IN_ODD)
    cc, cc1 = _conf(p3, ccb[0], o["b_glu"], o["w_dw"], o["b_dw"], o["g_ln"], o["b_ln"], _chunk(t, 256))
    r, lw, k2, v, kk, bb, bonus, g = _rprep(p3, rsh[0], o["mu"], o["w0"], o["a0"], o["wba"], o["gb"],
                                            o["kkw"], o["kaw"], o["rk"], wts["ones_bd"], _chunk(t, 256))
    y, s1 = _rscan(r, lw, k2, v, kk, bb, bonus, g, o["g_gn"], o["b_gn"], rs[0], _chunk(t, 64))
    sh_out = p3[:, t - 1, 2 * W:]
    x2 = _out_proj(x2, cc.reshape(b * t, W), y.reshape(b * t, W), o["w_out"], gt1, t)
    m = wts["moe"][1]
    y2 = _moe(x2, wts["g_ffn"][1], sh2, sc2, gt2, m["wr_hi"], m["wr_lo"], m["br"], m["w1"], m["w3"], m["w2"],
              wts["g_final"], t, True)
    states = (c1[None], n1[None], m1.reshape(1, b, H), lh1.reshape(1, b, W), conv1[None], cc1[None], s1[None],
              sh_out[None])
    return y2.reshape(b, t, d), states


def kernel(x_prompt, x_sample, c_prompt, c_sample, state_mlstm_C, state_mlstm_n, state_mlstm_m, state_lru_h,
           cache_lru_conv, cache_conformer_conv, state_rwkv_S, cache_rwkv_shift, w_ada, b_ada, g_norm_mix,
           g_norm_ffn, w_in_even, b_mlstm_if, g_mlstm_head, w_lru_conv, b_lru_conv, w_lru_r, b_lru_r, w_lru_i,
           b_lru_i, lru_lambda, w_out_even, w_in_odd, b_glu, w_cc_dw, b_cc_dw, g_cc_ln, b_cc_ln, rwkv_mu,
           rwkv_w0, rwkv_wB, rwkv_a0, rwkv_aB, rwkv_gB, rwkv_kk, rwkv_ka, rwkv_rk, g_rwkv_gn, b_rwkv_gn,
           w_out_odd, w_router_g, b_router_g, w_router_e, b_router_e, w_exp_gate, w_exp_up, w_exp_down, g_final):
    bp, bs = x_prompt.shape[0], x_sample.shape[0]

    wi = w_in_even[0]
    gcol = 4 * W
    w_in_e = jnp.concatenate([wi[:, :gcol], wi[:, gcol + 2 * H:], wi[:, gcol:gcol + 2 * H],
                              jnp.zeros((D_MODEL, 128 - 2 * H), F32)], axis=1).astype(BF)
    even = dict(
        w_in=w_in_e, b_if=b_mlstm_if[0], g_head=g_mlstm_head[0], w_conv=w_lru_conv[0], b_conv=b_lru_conv[0],
        w_gate=jnp.concatenate([_block_diag(w_lru_r[0]), _block_diag(w_lru_i[0])], axis=1).astype(BF),
        b_gate=jnp.concatenate([b_lru_r[0], b_lru_i[0]]), lam=lru_lambda[0], w_out=w_out_even[0].astype(BF))
    zl = jnp.zeros((LORA_W, W), F32)
    odd = dict(
        w_in=w_in_odd[0].astype(BF), b_glu=b_glu[0], w_dw=w_cc_dw[0], b_dw=b_cc_dw[0], g_ln=g_cc_ln[0],
        b_ln=b_cc_ln[0], mu=rwkv_mu[0], w0=rwkv_w0[0], a0=rwkv_a0[0],
        wba=jnp.concatenate([jnp.concatenate([rwkv_wB[0], zl], axis=1),
                             jnp.concatenate([zl, rwkv_aB[0]], axis=1)], axis=0).astype(BF),
        gb=rwkv_gB[0].astype(BF), kkw=rwkv_kk[0], kaw=rwkv_ka[0], rk=rwkv_rk[0], g_gn=g_rwkv_gn[0],
        b_gn=b_rwkv_gn[0], w_out=w_out_odd[0].astype(BF))
    moe = []
    for l in range(DEPTH):
        wr = jnp.concatenate([w_router_g[l], w_router_e[l],
                              jnp.zeros((D_MODEL, 128 - N_GROUPS - N_EXPERTS), F32)], axis=1)
        wr_hi = wr.astype(BF)
        wr_lo = (wr - wr_hi.astype(F32)).astype(BF)
        br = jnp.concatenate([b_router_g[l], b_router_e[l],
                              jnp.zeros((128 - N_GROUPS - N_EXPERTS,), F32)]).reshape(1, 128)
        moe.append(dict(wr_hi=wr_hi, wr_lo=wr_lo, br=br, w1=w_exp_gate[l].astype(BF),
                        w3=w_exp_up[l].astype(BF), w2=w_exp_down[l].astype(BF)))
    ones_bd = _block_diag(jnp.ones((H, DH, DH), F32)).astype(BF)
    wts = dict(even=even, odd=odd, moe=moe, g_mix=g_norm_mix, g_ffn=g_norm_ffn, g_final=g_final, ones_bd=ones_bd)

    mod = _ada(jnp.concatenate([c_prompt, c_sample], axis=0), w_ada.astype(BF), b_ada)

    def mods_of(lo, hi):
        return [tuple(mod[l, lo:hi, j * D_MODEL:(j + 1) * D_MODEL].reshape(hi - lo, 1, D_MODEL) for j in range(6))
                for l in range(DEPTH)]

    z = lambda *s: jnp.zeros(s, F32)
    st_p = (z(1, bp, H, DH, DH), z(1, bp, H, DH), z(1, bp, H), z(1, bp, W), z(1, bp, LRU_CONV - 1, W),
            z(1, bp, CONV_C - 1, W), z(1, bp, H, DH, DH), z(1, bp, 3 * W + LORA_W + LORA_A + LORA_G))
    st_s = (state_mlstm_C, state_mlstm_n, state_mlstm_m, state_lru_h, cache_lru_conv, cache_conformer_conv,
            state_rwkv_S, cache_rwkv_shift)
    y_p, out_p = _run_group(x_prompt, mods_of(0, bp), st_p, wts)
    y_s, out_s = _run_group(x_sample, mods_of(bp, bp + bs), st_s, wts)
    return (y_p, y_s) + tuple(out_p) + tuple(out_s)
```

```python
import functools

import jax
import jax.numpy as jnp
from jax import lax
from jax.experimental import pallas as pl
from jax.experimental.pallas import tpu as pltpu

F32 = jnp.float32
BF = jnp.bfloat16

D_MODEL = 1024
DEPTH = 2
H = 8
DH = 64
W = 512
LRU_CONV = 4
LRU_C = 8.0
CONV_C = 31
LORA_W = 64
LORA_A = 64
LORA_G = 128
RWKV_DECAY = 0.606531
RWKV_GN_EPS = 64e-5
N_GROUPS = 4
E_PER_GROUP = 4
N_EXPERTS = 16
D_EXPERT = 256
RMS_EPS = 1e-6
LN_EPS = 1e-5
IN_EVEN_PAD = 6 * W + 128
IN_ODD = 2 * W + 3 * W + LORA_W + LORA_A + LORA_G

ROW_TILE = 1024
VMEM_LIMIT = 48 * 1024 * 1024


def _mm(a, b):
    return jnp.dot(a.astype(BF), b.astype(BF), preferred_element_type=F32)


def _mm_nt(a, b):
    return lax.dot_general(a.astype(BF), b.astype(BF), (((1,), (1,)), ((), ())),
                           preferred_element_type=F32)


def _mm_tn(a, b):
    return lax.dot_general(a.astype(BF), b.astype(BF), (((0,), (0,)), ((), ())),
                           preferred_element_type=F32)


def _split3(x):
    hi = x.astype(BF)
    r = x - hi.astype(F32)
    mid = r.astype(BF)
    lo = (r - mid.astype(F32)).astype(BF)
    return hi, mid, lo


def _mm_mask_l(mask, x):
    return sum(jnp.dot(mask, p, preferred_element_type=F32) for p in _split3(x))


def _mm_mask_r(x, mask):
    return sum(jnp.dot(p, mask, preferred_element_type=F32) for p in _split3(x))


def _sigmoid(x):
    return 1.0 / (1.0 + jnp.exp(-x))


def _rows(v, tm):
    nb, _, c = v.shape
    if nb == 1:
        return v[0]
    return jnp.broadcast_to(v, (nb, tm // nb, c)).reshape(tm, c)


def _mod_spec(T, tm, c):
    if tm <= T:
        per = T // tm
        return pl.BlockSpec((1, 1, c), lambda i, *_: (i // per, 0, 0))
    return pl.BlockSpec((tm // T, 1, c), lambda i, *_: (i, 0, 0))


def _norm_mod(x, g, sh, sc):
    y = x * lax.rsqrt(jnp.mean(x * x, axis=-1, keepdims=True) + RMS_EPS) * g
    return y * (1.0 + sc) + sh


def _params(sem):
    return pltpu.CompilerParams(dimension_semantics=sem, vmem_limit_bytes=VMEM_LIMIT)


def _ada_kernel(c_ref, w_ref, b_ref, o_ref):
    o_ref[0] = _mm(c_ref[...], w_ref[0]) + b_ref[0]


def _ada(c_all, w_bf, b):
    nb = c_all.shape[0]
    tn = 1536
    return pl.pallas_call(
        _ada_kernel,
        grid=(DEPTH, 6 * D_MODEL // tn),
        in_specs=[pl.BlockSpec((nb, D_MODEL), lambda l, j: (0, 0)),
                  pl.BlockSpec((1, D_MODEL, tn), lambda l, j: (l, 0, j)),
                  pl.BlockSpec((1, 1, tn), lambda l, j: (l, 0, j))],
        out_specs=pl.BlockSpec((1, nb, tn), lambda l, j: (l, 0, j)),
        out_shape=jax.ShapeDtypeStruct((DEPTH, nb, 6 * D_MODEL), F32),
        compiler_params=_params(("parallel", "parallel")),
        name="ada",
    )(c_all, w_bf, b.reshape(DEPTH, 1, 6 * D_MODEL))


def _inproj_kernel(x_ref, g_ref, sh_ref, sc_ref, w_ref, o_ref, h_scr):
    tm = x_ref.shape[0]

    @pl.when(pl.program_id(1) == 0)
    def _():
        h = _norm_mod(x_ref[...], g_ref[...], _rows(sh_ref[...], tm), _rows(sc_ref[...], tm))
        h_scr[...] = h.astype(BF)

    o_ref[...] = jnp.dot(h_scr[...], w_ref[...], preferred_element_type=F32)


def _in_proj(x2, g, sh, sc, w_bf, T, tn):
    n, d = x2.shape
    cols = w_bf.shape[1]
    tm = ROW_TILE
    return pl.pallas_call(
        _inproj_kernel,
        grid=(n // tm, cols // tn),
        in_specs=[pl.BlockSpec((tm, d), lambda i, j: (i, 0)),
                  pl.BlockSpec((1, d), lambda i, j: (0, 0)),
                  _mod_spec(T, tm, d), _mod_spec(T, tm, d),
                  pl.BlockSpec((d, tn), lambda i, j: (0, j))],
        out_specs=pl.BlockSpec((tm, tn), lambda i, j: (i, j)),
        out_shape=jax.ShapeDtypeStruct((n, cols), F32),
        scratch_shapes=[pltpu.VMEM((tm, d), BF)],
        compiler_params=_params(("parallel", "arbitrary")),
        name="in_proj",
    )(x2, g.reshape(1, d), sh, sc, w_bf)


def _outproj_kernel(x_ref, a_ref, b_ref, w_ref, gt_ref, o_ref):
    tm = x_ref.shape[0]
    mix = (jnp.dot(a_ref[...], w_ref[0:W, :], preferred_element_type=F32)
           + jnp.dot(b_ref[...], w_ref[W:2 * W, :], preferred_element_type=F32))
    o_ref[...] = x_ref[...] + _rows(gt_ref[...], tm) * mix


def _out_proj(x2, a2, b2, w_bf, gt, T):
    n, d = x2.shape
    tm = ROW_TILE
    return pl.pallas_call(
        _outproj_kernel,
        grid=(n // tm,),
        in_specs=[pl.BlockSpec((tm, d), lambda i: (i, 0)),
                  pl.BlockSpec((tm, W), lambda i: (i, 0)),
                  pl.BlockSpec((tm, W), lambda i: (i, 0)),
                  pl.BlockSpec((2 * W, d), lambda i: (0, 0)),
                  _mod_spec(T, tm, d)],
        out_specs=pl.BlockSpec((tm, d), lambda i: (i, 0)),
        out_shape=jax.ShapeDtypeStruct((n, d), F32),
        compiler_params=_params(("parallel",)),
        name="out_proj",
    )(x2, a2, b2, w_bf, gt)


def _route(logits):
    lane = lax.broadcasted_iota(jnp.int32, logits.shape, 1).astype(F32)
    neg = -jnp.inf
    is_g = lane < N_GROUPS
    lg = jnp.where(is_g, logits, neg)
    mg = jnp.max(lg, axis=1, keepdims=True)
    gsel = jnp.min(jnp.where(lg == mg, lane, 128.0), axis=1, keepdims=True)
    psum = jnp.sum(jnp.where(is_g, jnp.exp(lg - mg), 0.0), axis=1, keepdims=True)
    pg_sel = 1.0 / psum
    lo = N_GROUPS + E_PER_GROUP * gsel
    le = jnp.where((lane >= lo) & (lane < lo + E_PER_GROUP), logits, neg)
    v1 = jnp.max(le, axis=1, keepdims=True)
    i1 = jnp.min(jnp.where(le == v1, lane, 128.0), axis=1, keepdims=True)
    le2 = jnp.where(lane == i1, neg, le)
    v2 = jnp.max(le2, axis=1, keepdims=True)
    i2 = jnp.min(jnp.where(le2 == v2, lane, 128.0), axis=1, keepdims=True)
    e2 = jnp.exp(v2 - v1)
    p1 = 1.0 / (1.0 + e2)
    p2 = e2 / (1.0 + e2)
    return pg_sel * jnp.where(lane == i1, p1, jnp.where(lane == i2, p2, 0.0))


def _moe_kernel(x_ref, g_ref, sh_ref, sc_ref, gt_ref, wrh_ref, wrl_ref, br_ref, w1_ref, w3_ref, w2_ref,
                gf_ref, o_ref, h_scr, gate_scr, acc_scr, *, final_norm):
    tm = x_ref.shape[0]
    e = pl.program_id(1)

    @pl.when(e == 0)
    def _():
        h = _norm_mod(x_ref[...], g_ref[...], _rows(sh_ref[...], tm), _rows(sc_ref[...], tm))
        hb = h.astype(BF)
        h_scr[...] = hb
        hl = (h - hb.astype(F32)).astype(BF)
        logits = (jnp.dot(hb, wrh_ref[...], preferred_element_type=F32)
                  + jnp.dot(hl, wrh_ref[...], preferred_element_type=F32)
                  + jnp.dot(hb, wrl_ref[...], preferred_element_type=F32)) + br_ref[...]
        gate_scr[...] = _route(logits)
        acc_scr[...] = jnp.zeros_like(acc_scr)

    hb = h_scr[...]
    hg = jnp.dot(hb, w1_ref[0], preferred_element_type=F32)
    hu = jnp.dot(hb, w3_ref[0], preferred_element_type=F32)
    lane = lax.broadcasted_iota(jnp.int32, gate_scr.shape, 1)
    ge = jnp.sum(jnp.where(lane == e + N_GROUPS, gate_scr[...], 0.0), axis=1, keepdims=True)
    hh = hg * _sigmoid(hg) * hu * ge
    acc_scr[...] += jnp.dot(hh.astype(BF), w2_ref[0], preferred_element_type=F32)

    @pl.when(e == N_EXPERTS - 1)
    def _():
        y = x_ref[...] + _rows(gt_ref[...], tm) * acc_scr[...]
        if final_norm:
            y = y * lax.rsqrt(jnp.mean(y * y, axis=-1, keepdims=True) + RMS_EPS) * gf_ref[...]
        o_ref[...] = y


def _moe(x2, g, sh, sc, gt, wr_hi, wr_lo, br, w1, w3, w2, g_final, T, final_norm):
    n, d = x2.shape
    tm = ROW_TILE
    return pl.pallas_call(
        functools.partial(_moe_kernel, final_norm=final_norm),
        grid=(n // tm, N_EXPERTS),
        in_specs=[pl.BlockSpec((tm, d), lambda i, e: (i, 0)),
                  pl.BlockSpec((1, d), lambda i, e: (0, 0)),
                  _mod_spec(T, tm, d), _mod_spec(T, tm, d), _mod_spec(T, tm, d),
                  pl.BlockSpec((d, 128), lambda i, e: (0, 0)),
                  pl.BlockSpec((d, 128), lambda i, e: (0, 0)),
                  pl.BlockSpec((1, 128), lambda i, e: (0, 0)),
                  pl.BlockSpec((1, d, D_EXPERT), lambda i, e: (e, 0, 0)),
                  pl.BlockSpec((1, d, D_EXPERT), lambda i, e: (e, 0, 0)),
                  pl.BlockSpec((1, D_EXPERT, d), lambda i, e: (e, 0, 0)),
                  pl.BlockSpec((1, d), lambda i, e: (0, 0))],
        out_specs=pl.BlockSpec((tm, d), lambda i, e: (i, 0)),
        out_shape=jax.ShapeDtypeStruct((n, d), F32),
        scratch_shapes=[pltpu.VMEM((tm, d), BF), pltpu.VMEM((tm, 128), F32), pltpu.VMEM((tm, d), F32)],
        compiler_params=_params(("parallel", "arbitrary")),
        name="moe",
    )(x2, g.reshape(1, d), sh, sc, gt, wr_hi, wr_lo, br, w1, w3, w2, g_final.reshape(1, d))


def _log_sigmoid(x):
    return jnp.minimum(x, 0.0) - jnp.log(1.0 + jnp.exp(-jnp.abs(x)))


def _mlstm_kernel(q_ref, k_ref, v_ref, o_ref, g_ref, gt_ref, bif_ref, bift_ref, gh_ref, c0_ref, n0_ref, m0_ref,
                  h_ref, c1_ref, n1_ref, m1_ref, c_scr, n_scr, m_scr, *, L, nc):
    c = pl.program_id(1)

    @pl.when(c == 0)
    def _():
        c_scr[...] = c0_ref[0]
        n_scr[...] = n0_ref[0]
        m_scr[...] = m0_ref[0]

    g = g_ref[0] + bif_ref[...]
    gt = gt_ref[0] + bift_ref[...]
    row = lax.broadcasted_iota(jnp.int32, (L, L), 0)
    col = lax.broadcasted_iota(jnp.int32, (L, L), 1)
    causal = row >= col
    tri = jnp.where(causal, 1.0, 0.0).astype(BF)
    tri_u = jnp.where(row <= col, 1.0, 0.0).astype(BF)
    bcum = _mm_mask_l(tri, _log_sigmoid(g))
    bcum_t = _mm_mask_r(_log_sigmoid(gt), tri_u)
    m_prev = m_scr[...]
    hs = range(H)
    sls = [slice(h * DH, (h + 1) * DH) for h in hs]
    q = [q_ref[0, :, sl].astype(BF) for sl in sls]
    k = [k_ref[0, :, sl] * (DH ** -0.5) for sl in sls]
    v = [v_ref[0, :, sl].astype(BF) for sl in sls]
    qk = [_mm_nt(q[h], k[h]) for h in hs]
    cs = [c_scr[h] for h in hs]
    ns = [n_scr[h:h + 1, :] for h in hs]
    qc = [_mm(q[h], cs[h]) for h in hs]
    bc = [bcum[:, H + h:H + h + 1] for h in hs]
    mp = [m_prev[:, h:h + 1] for h in hs]
    dmat = [jnp.where(causal, bc[h] + (gt[h:h + 1, :] - bcum_t[H + h:H + h + 1, :]), -jnp.inf) for h in hs]
    g_inter = [bc[h] + mp[h] for h in hs]
    m_t = [jnp.maximum(g_inter[h], jnp.max(dmat[h], axis=1, keepdims=True)) for h in hs]
    s = [qk[h] * jnp.exp(dmat[h] - m_t[h]) for h in hs]
    w_inter = [jnp.exp(g_inter[h] - m_t[h]) for h in hs]
    num = [_mm(s[h], v[h]) + w_inter[h] * qc[h] for h in hs]
    den = [jnp.sum(s[h], axis=1, keepdims=True)
           + w_inter[h] * jnp.sum(q_ref[0, :, sls[h]] * ns[h], axis=1, keepdims=True) for h in hs]
    hh = [num[h] / jnp.maximum(jnp.abs(den[h]), jnp.exp(-m_t[h])) for h in hs]
    m_new = [m_t[h][L - 1:L, :] for h in hs]
    b_last = [bc[h][L - 1:L, :] for h in hs]
    kw = [k[h] * jnp.exp(b_last[h] - bc[h] + g[:, h:h + 1] - m_new[h]) for h in hs]
    decay = [jnp.exp(b_last[h] + mp[h] - m_new[h]) for h in hs]
    for h in hs:
        c_scr[h] = decay[h] * cs[h] + _mm_tn(kw[h], v[h])
        n_scr[h:h + 1, :] = decay[h] * ns[h] + jnp.sum(kw[h], axis=0, keepdims=True)
    outs = []
    for h in hs:
        mu = jnp.mean(hh[h], axis=1, keepdims=True)
        var = jnp.mean(jnp.square(hh[h] - mu), axis=1, keepdims=True)
        hn = (hh[h] - mu) * lax.rsqrt(var + RMS_EPS) * gh_ref[:, sls[h]]
        outs.append(hn * _sigmoid(o_ref[0, :, sls[h]]))
    h_ref[0] = jnp.concatenate(outs, axis=1).astype(BF)
    m_scr[...] = jnp.concatenate(m_new, axis=1)

    @pl.when(c == nc - 1)
    def _():
        c1_ref[0] = c_scr[...]
        n1_ref[0] = n_scr[...]
        m1_ref[0] = m_scr[...]


def _mlstm(p3, gif_t, bif, g_head, c0, n0, m0, L):
    b, t, _ = p3.shape
    nc = t // L
    col = lambda j: pl.BlockSpec((1, L, W), lambda i, c, j=j: (i, c, j))
    bif_pad = jnp.zeros((1, 128), F32).at[0, :2 * H].set(bif)
    return pl.pallas_call(
        functools.partial(_mlstm_kernel, L=L, nc=nc),
        grid=(b, nc),
        in_specs=[col(0), col(1), col(2), col(3),
                  pl.BlockSpec((1, L, 128), lambda i, c: (i, c, 6 * W // 128)),
                  pl.BlockSpec((1, 2 * H, L), lambda i, c: (i, 0, c)),
                  pl.BlockSpec((1, 128), lambda i, c: (0, 0)),
                  pl.BlockSpec((2 * H, 1), lambda i, c: (0, 0)),
                  pl.BlockSpec((1, W), lambda i, c: (0, 0)),
                  pl.BlockSpec((1, H, DH, DH), lambda i, c: (i, 0, 0, 0)),
                  pl.BlockSpec((1, H, DH), lambda i, c: (i, 0, 0)),
                  pl.BlockSpec((1, 1, H), lambda i, c: (i, 0, 0))],
        out_specs=[pl.BlockSpec((1, L, W), lambda i, c: (i, c, 0)),
                   pl.BlockSpec((1, H, DH, DH), lambda i, c: (i, 0, 0, 0)),
                   pl.BlockSpec((1, H, DH), lambda i, c: (i, 0, 0)),
                   pl.BlockSpec((1, 1, H), lambda i, c: (i, 0, 0))],
        out_shape=[jax.ShapeDtypeStruct((b, t, W), BF),
                   jax.ShapeDtypeStruct((b, H, DH, DH), F32),
                   jax.ShapeDtypeStruct((b, H, DH), F32),
                   jax.ShapeDtypeStruct((b, 1, H), F32)],
        scratch_shapes=[pltpu.VMEM((H, DH, DH), F32), pltpu.VMEM((H, DH), F32), pltpu.VMEM((1, H), F32)],
        compiler_params=_params(("parallel", "arbitrary")),
        name="mlstm",
    )(p3, p3, p3, p3, p3, gif_t, bif_pad, bif.reshape(2 * H, 1), g_head.reshape(1, W),
      c0, n0, m0.reshape(b, 1, H))


def _gelu_tanh(x):
    return 0.5 * x * (1.0 + jnp.tanh(0.7978845608028654 * (x + 0.044715 * x * x * x)))


def _lru_kernel(xr_ref, xg_ref, cache_ref, h0_ref, wc_ref, bc_ref, wg_ref, bg_ref, lam_ref,
                o_ref, h1_ref, ext_scr, a_scr, u_scr, hs_scr, hc_scr, *, Tc, nc):
    c = pl.program_id(1)
    K1 = LRU_CONV - 1

    @pl.when(c == 0)
    def _():
        ext_scr[8 - K1:8, :] = cache_ref[0]
        hc_scr[...] = h0_ref[0]

    x = xr_ref[0]
    ext_scr[8:8 + Tc, :] = x
    xc = bc_ref[...] + wc_ref[K1:K1 + 1, :] * x
    for d in range(1, LRU_CONV):
        xc = xc + wc_ref[K1 - d:K1 - d + 1, :] * ext_scr[8 - d:8 - d + Tc, :]
    ext_scr[8 - K1:8, :] = ext_scr[8 + Tc - K1:8 + Tc, :]
    gates = _mm(xc, wg_ref[...]) + bg_ref[...]
    r = _sigmoid(gates[:, 0:W])
    ig = _sigmoid(gates[:, W:2 * W])
    lam = lam_ref[...]
    softplus_neg = jnp.maximum(-lam, 0.0) + jnp.log(1.0 + jnp.exp(-jnp.abs(lam)))
    log_a = -LRU_C * r * softplus_neg
    a_scr[...] = jnp.exp(log_a)
    th = jnp.tanh(log_a)
    one_minus_a2 = -2.0 * th / (1.0 - th)
    u_scr[...] = jnp.sqrt(one_minus_a2) * (ig * xc)

    def body(t, h):
        h = a_scr[pl.ds(t, 1), :] * h + u_scr[pl.ds(t, 1), :]
        hs_scr[pl.ds(t, 1), :] = h
        return h

    h_fin = lax.fori_loop(0, Tc, body, hc_scr[...], unroll=8)
    hc_scr[...] = h_fin
    o_ref[0] = (hs_scr[...] * _gelu_tanh(xg_ref[0])).astype(BF)

    @pl.when(c == nc - 1)
    def _():
        h1_ref[0] = h_fin


def _lru(p3, cache, h0, w_conv, b_conv, wg_bf, bg, lam, Tc):
    b, t, _ = p3.shape
    nc = t // Tc
    return pl.pallas_call(
        functools.partial(_lru_kernel, Tc=Tc, nc=nc),
        grid=(b, nc),
        in_specs=[pl.BlockSpec((1, Tc, W), lambda i, c: (i, c, 4)),
                  pl.BlockSpec((1, Tc, W), lambda i, c: (i, c, 5)),
                  pl.BlockSpec((1, LRU_CONV - 1, W), lambda i, c: (i, 0, 0)),
                  pl.BlockSpec((1, 1, W), lambda i, c: (i, 0, 0)),
                  pl.BlockSpec((LRU_CONV, W), lambda i, c: (0, 0)),
                  pl.BlockSpec((1, W), lambda i, c: (0, 0)),
                  pl.BlockSpec((W, 2 * W), lambda i, c: (0, 0)),
                  pl.BlockSpec((1, 2 * W), lambda i, c: (0, 0)),
                  pl.BlockSpec((1, W), lambda i, c: (0, 0))],
        out_specs=[pl.BlockSpec((1, Tc, W), lambda i, c: (i, c, 0)),
                   pl.BlockSpec((1, 1, W), lambda i, c: (i, 0, 0))],
        out_shape=[jax.ShapeDtypeStruct((b, t, W), BF), jax.ShapeDtypeStruct((b, 1, W), F32)],
        scratch_shapes=[pltpu.VMEM((8 + Tc, W), F32), pltpu.VMEM((Tc, W), F32), pltpu.VMEM((Tc, W), F32),
                        pltpu.VMEM((Tc, W), F32), pltpu.VMEM((1, W), F32)],
        compiler_params=_params(("parallel", "arbitrary")),
        name="lru",
    )(p3, p3, cache, h0.reshape(b, 1, W), w_conv, b_conv.reshape(1, W), wg_bf, bg.reshape(1, 2 * W),
      lam.reshape(1, W))


CONF_ROWS = 32


def _conf_kernel(u_ref, gte_ref, cache_ref, bu_ref, bg_ref, wdw_ref, bdw_ref, gln_ref, bln_ref,
                 o_ref, cache1_ref, ext_scr, *, Tc, nc):
    c = pl.program_id(1)
    K1 = CONV_C - 1
    base = 32 - K1

    @pl.when(c == 0)
    def _():
        ext_scr[base:32, :] = cache_ref[0]

    u = u_ref[0] + bu_ref[...]
    gte = gte_ref[0] + bg_ref[...]
    ext_scr[32:32 + Tc, :] = u * _sigmoid(gte)
    rb = min(CONF_ROWS, Tc)
    for r0 in range(0, Tc, rb):
        acc = bdw_ref[...] + wdw_ref[0:1, :] * ext_scr[base + r0:base + r0 + rb, :]
        for j in range(1, CONV_C):
            acc = acc + wdw_ref[j:j + 1, :] * ext_scr[base + r0 + j:base + r0 + j + rb, :]
        mu = jnp.mean(acc, axis=1, keepdims=True)
        var = jnp.mean(jnp.square(acc - mu), axis=1, keepdims=True)
        y = (acc - mu) * lax.rsqrt(var + LN_EPS) * gln_ref[...] + bln_ref[...]
        o_ref[0, r0:r0 + rb, :] = (y * _sigmoid(y)).astype(BF)
    tail = ext_scr[base + Tc:32 + Tc, :]
    ext_scr[base:32, :] = tail

    @pl.when(c == nc - 1)
    def _():
        cache1_ref[0] = tail


def _conf(p3, cache, b_glu, w_dw, b_dw, g_ln, b_ln, Tc):
    b, t, _ = p3.shape
    nc = t // Tc
    vec = lambda: pl.BlockSpec((1, W), lambda i, c: (0, 0))
    return pl.pallas_call(
        functools.partial(_conf_kernel, Tc=Tc, nc=nc),
        grid=(b, nc),
        in_specs=[pl.BlockSpec((1, Tc, W), lambda i, c: (i, c, 0)),
                  pl.BlockSpec((1, Tc, W), lambda i, c: (i, c, 1)),
                  pl.BlockSpec((1, CONV_C - 1, W), lambda i, c: (i, 0, 0)),
                  vec(), vec(),
                  pl.BlockSpec((CONV_C, W), lambda i, c: (0, 0)),
                  vec(), vec(), vec()],
        out_specs=[pl.BlockSpec((1, Tc, W), lambda i, c: (i, c, 0)),
                   pl.BlockSpec((1, CONV_C - 1, W), lambda i, c: (i, 0, 0))],
        out_shape=[jax.ShapeDtypeStruct((b, t, W), BF), jax.ShapeDtypeStruct((b, CONV_C - 1, W), F32)],
        scratch_shapes=[pltpu.VMEM((32 + Tc, W), F32)],
        compiler_params=_params(("parallel", "arbitrary")),
        name="conformer",
    )(p3, p3, cache, b_glu[:W].reshape(1, W), b_glu[W:].reshape(1, W), w_dw, b_dw.reshape(1, W),
      g_ln.reshape(1, W), b_ln.reshape(1, W))


def _seg_sum(x, ones_bd):
    hi = x.astype(BF)
    lo = (x - hi.astype(F32)).astype(BF)
    return (jnp.dot(hi, ones_bd, preferred_element_type=F32)
            + jnp.dot(lo, ones_bd, preferred_element_type=F32))


def _rprep_kernel(pr_ref, pk_ref, pv_ref, pz_ref, sr_ref, sk_ref, sv_ref, sz_ref,
                  mr_ref, mk_ref, mv_ref, mz_ref, w0_ref, a0_ref, wba_ref, gb_ref, kkw_ref, kaw_ref, rk_ref,
                  ones_ref, r_ref, lw_ref, k_ref, v_ref, kk_ref, b_ref, bonus_ref, g_ref,
                  qr_scr, qk_scr, qv_scr, qz_scr, *, Tc):
    c = pl.program_id(1)

    @pl.when(c == 0)
    def _():
        qr_scr[...] = sr_ref[0]
        qk_scr[...] = sk_ref[0]
        qv_scr[...] = sv_ref[0]
        qz_scr[...] = sz_ref[0]

    def shift_mix(x_ref, prev_scr, mu_ref):
        x = x_ref[0]
        first = lax.broadcasted_iota(jnp.int32, x.shape, 0) == 0
        prev = jnp.where(first, prev_scr[...], pltpu.roll(x, 1, 0))
        prev_scr[...] = x[Tc - 1:Tc, :]
        return x + (prev - x) * mu_ref[...]

    r = shift_mix(pr_ref, qr_scr, mr_ref)
    k = shift_mix(pk_ref, qk_scr, mk_ref)
    v = shift_mix(pv_ref, qv_scr, mv_ref)
    z = shift_mix(pz_ref, qz_scr, mz_ref)
    zwa = z[:, 0:128]
    lane = lax.broadcasted_iota(jnp.int32, zwa.shape, 1)
    wa = _mm(jnp.where(lane < LORA_W, jnp.tanh(zwa), zwa), wba_ref[...])
    lw = -RWKV_DECAY * _sigmoid(w0_ref[...] + wa[:, 0:W])
    a = _sigmoid(a0_ref[...] + wa[:, W:2 * W])
    g = _mm(_sigmoid(z[:, 128:256]), gb_ref[...])
    ones_bd = ones_ref[...]
    kk = k * kkw_ref[...]
    kk = kk * lax.rsqrt(_seg_sum(kk * kk, ones_bd) + 1e-12)
    k2 = k * (1.0 + (a - 1.0) * kaw_ref[...])
    r_ref[0] = r
    lw_ref[0] = lw
    k_ref[0] = k2
    v_ref[0] = v
    kk_ref[0] = kk
    b_ref[0] = kk * a
    bonus_ref[0] = _seg_sum(r * k2 * rk_ref[...], ones_bd) * v
    g_ref[0] = g


def _rprep(p3, shift, mu, w0, a0, wba_bf, gb_bf, kkw, kaw, rk, ones_bd, Tc):
    b, t, _ = p3.shape
    nc = t // Tc
    vec = lambda: pl.BlockSpec((1, W), lambda i, c: (0, 0))
    st = lambda width: pl.BlockSpec((1, 1, width), lambda i, c: (i, 0, 0))
    out = pl.BlockSpec((1, Tc, W), lambda i, c: (i, c, 0))
    sh3 = shift.reshape(b, 1, -1)
    mu2 = mu.reshape(1, -1)
    return pl.pallas_call(
        functools.partial(_rprep_kernel, Tc=Tc),
        grid=(b, nc),
        in_specs=[pl.BlockSpec((1, Tc, W), lambda i, c: (i, c, 2)),
                  pl.BlockSpec((1, Tc, W), lambda i, c: (i, c, 3)),
                  pl.BlockSpec((1, Tc, W), lambda i, c: (i, c, 4)),
                  pl.BlockSpec((1, Tc, 256), lambda i, c: (i, c, 10)),
                  st(W), st(W), st(W), st(256),
                  vec(), vec(), vec(), pl.BlockSpec((1, 256), lambda i, c: (0, 0)),
                  vec(), vec(),
                  pl.BlockSpec((128, 2 * W), lambda i, c: (0, 0)),
                  pl.BlockSpec((LORA_G, W), lambda i, c: (0, 0)),
                  vec(), vec(), vec(),
                  pl.BlockSpec((W, W), lambda i, c: (0, 0))],
        out_specs=[out] * 8,
        out_shape=[jax.ShapeDtypeStruct((b, t, W), F32)] * 8,
        scratch_shapes=[pltpu.VMEM((1, W), F32), pltpu.VMEM((1, W), F32), pltpu.VMEM((1, W), F32),
                        pltpu.VMEM((1, 256), F32)],
        compiler_params=_params(("parallel", "arbitrary")),
        name="rwkv_prep",
    )(p3, p3, p3, p3,
      sh3[:, :, 0:W], sh3[:, :, W:2 * W], sh3[:, :, 2 * W:3 * W], sh3[:, :, 3 * W:],
      mu2[:, 0:W], mu2[:, W:2 * W], mu2[:, 2 * W:3 * W], mu2[:, 3 * W:],
      w0.reshape(1, W), a0.reshape(1, W), wba_bf, gb_bf, kkw.reshape(1, W), kaw.reshape(1, W),
      rk.reshape(1, W), ones_bd)


def _rscan_kernel(r_ref, lw_ref, k_ref, v_ref, kk_ref, b_ref, bonus_ref, g_ref, gn_ref, bn_ref, s0_ref,
                  y_ref, s1_ref, s_scr, *, L, nc):
    c = pl.program_id(1)

    @pl.when(c == 0)
    def _():
        s_scr[...] = s0_ref[0]

    row = lax.broadcasted_iota(jnp.int32, (L, L), 0)
    col = lax.broadcasted_iota(jnp.int32, (L, L), 1)
    lower = row >= col
    strict = row > col
    eye = jnp.where(row == col, 1.0, 0.0)
    tri = jnp.where(lower, 1.0, 0.0).astype(BF)
    lw = lw_ref[0]
    cum = _mm_mask_l(tri, lw)
    p_in = jnp.exp(cum)
    p_inv = jnp.exp(-cum)
    p_last = p_in[L - 1:L, :]
    kt = (kk_ref[0] * jnp.exp(cum - lw)).astype(BF)
    rt = (r_ref[0] * p_in).astype(BF)
    kh = k_ref[0] * p_inv
    bh = b_ref[0] * p_inv
    khl = (kh * p_last).astype(BF)
    bhl = (bh * p_last).astype(BF)
    kh = kh.astype(BF)
    bh = bh.astype(BF)
    vb = v_ref[0].astype(BF)
    sls = [slice(h * DH, (h + 1) * DH) for h in range(H)]
    a_kb = [jnp.where(strict, _mm_nt(kt[:, sl], bh[:, sl]), 0.0) for sl in sls]
    a_kk = [jnp.where(strict, _mm_nt(kt[:, sl], kh[:, sl]), 0.0) for sl in sls]
    r_k = [jnp.where(lower, _mm_nt(rt[:, sl], kh[:, sl]), 0.0) for sl in sls]
    r_b = [jnp.where(lower, _mm_nt(rt[:, sl], bh[:, sl]), 0.0) for sl in sls]
    xs = [-a for a in a_kb]
    invs = [eye + x for x in xs]
    n = 2
    while n < L:
        xs = [_mm(x, x) for x in xs]
        invs = [inv + _mm(inv, x) for inv, x in zip(invs, xs)]
        n *= 2
    ss = [s_scr[h] for h in range(H)]
    rhs = [_mm_nt(kt[:, sl], s) + _mm(a, vb[:, sl]) for sl, s, a in zip(sls, ss, a_kk)]
    us = [_mm(inv, r) for inv, r in zip(invs, rhs)]
    ys = [_mm_nt(rt[:, sl], s) + _mm(rk_, vb[:, sl]) - _mm(rb_, u)
          for sl, s, rk_, rb_, u in zip(sls, ss, r_k, r_b, us)]
    for h, (sl, s, u) in enumerate(zip(sls, ss, us)):
        s_scr[h] = s * p_last[:, sl] + _mm_tn(vb[:, sl], khl[:, sl]) - _mm_tn(u, bhl[:, sl])
    outs = []
    for sl, y in zip(sls, ys):
        mu = jnp.mean(y, axis=1, keepdims=True)
        var = jnp.mean(jnp.square(y - mu), axis=1, keepdims=True)
        yn = (y - mu) * lax.rsqrt(var + RWKV_GN_EPS) * gn_ref[:, sl] + bn_ref[:, sl]
        outs.append((yn + bonus_ref[0, :, sl]) * g_ref[0, :, sl])
    y_ref[0] = jnp.concatenate(outs, axis=1).astype(BF)

    @pl.when(c == nc - 1)
    def _():
        s1_ref[0] = s_scr[...]


def _rscan(r, lw, k, v, kk, bb, bonus, g, g_gn, b_gn, s0, L):
    b, t, _ = r.shape
    nc = t // L
    blk = lambda: pl.BlockSpec((1, L, W), lambda i, c: (i, c, 0))
    vec = lambda: pl.BlockSpec((1, W), lambda i, c: (0, 0))
    st = pl.BlockSpec((1, H, DH, DH), lambda i, c: (i, 0, 0, 0))
    return pl.pallas_call(
        functools.partial(_rscan_kernel, L=L, nc=nc),
        grid=(b, nc),
        in_specs=[blk() for _ in range(8)] + [vec(), vec(), st],
        out_specs=[blk(), st],
        out_shape=[jax.ShapeDtypeStruct((b, t, W), BF), jax.ShapeDtypeStruct((b, H, DH, DH), F32)],
        scratch_shapes=[pltpu.VMEM((H, DH, DH), F32)],
        compiler_params=_params(("parallel", "arbitrary")),
        name="rwkv_scan",
    )(r, lw, k, v, kk, bb, bonus, g, g_gn.reshape(1, W), b_gn.reshape(1, W), s0)


def _block_diag(w):
    nb, bw, _ = w.shape
    return (jnp.eye(nb, dtype=w.dtype)[:, None, :, None] * w[:, :, None, :]).reshape(nb * bw, nb * bw)


def _chunk(t, target):
    return target if t % target == 0 else t


def _run_group(x, mods, st, wts):
    b, t, d = x.shape
    x2 = x.reshape(b * t, d)
    (mc, mn, mm, lh, lconv, ccb, rs, rsh) = st

    sh1, sc1, gt1, sh2, sc2, gt2 = mods[0]
    e = wts["even"]
    p = _in_proj(x2, wts["g_mix"][0], sh1, sc1, e["w_in"], t, 640)
    p3 = p.reshape(b, t, IN_EVEN_PAD)
    gif_t = jnp.transpose(p3[:, :, 6 * W:6 * W + 2 * H], (0, 2, 1))
    hm, c1, n1, m1 = _mlstm(p3, gif_t, e["b_if"], e["g_head"], mc[0], mn[0], mm[0], _chunk(t, 128))
    hl, lh1 = _lru(p3, lconv[0], lh[0], e["w_conv"], e["b_conv"], e["w_gate"], e["b_gate"], e["lam"],
                   _chunk(t, 256))
    xr = p3[:, :, 4 * W:5 * W]
    conv1 = jnp.concatenate([lconv[0], xr], axis=1)[:, -(LRU_CONV - 1):]
    x2 = _out_proj(x2, hm.reshape(b * t, W), hl.reshape(b * t, W), e["w_out"], gt1, t)
    m = wts["moe"][0]
    x2 = _moe(x2, wts["g_ffn"][0], sh2, sc2, gt2, m["wr_hi"], m["wr_lo"], m["br"], m["w1"], m["w3"], m["w2"],
              wts["g_final"], t, False)

    sh1, sc1, gt1, sh2, sc2, gt2 = mods[1]
    o = wts["odd"]
    p = _in_proj(x2, wts["g_mix"][1], sh1, sc1, o["w_in"], t, 1408)
    p3 = p.reshape(b, t, IN_ODD)
    cc, cc1 = _conf(p3, ccb[0], o["b_glu"], o["w_dw"], o["b_dw"], o["g_ln"], o["b_ln"], _chunk(t, 256))
    r, lw, k2, v, kk, bb, bonus, g = _rprep(p3, rsh[0], o["mu"], o["w0"], o["a0"], o["wba"], o["gb"],
                                            o["kkw"], o["kaw"], o["rk"], wts["ones_bd"], _chunk(t, 256))
    y, s1 = _rscan(r, lw, k2, v, kk, bb, bonus, g, o["g_gn"], o["b_gn"], rs[0], _chunk(t, 64))
    sh_out = p3[:, t - 1, 2 * W:]
    x2 = _out_proj(x2, cc.reshape(b * t, W), y.reshape(b * t, W), o["w_out"], gt1, t)
    m = wts["moe"][1]
    y2 = _moe(x2, wts["g_ffn"][1], sh2, sc2, gt2, m["wr_hi"], m["wr_lo"], m["br"], m["w1"], m["w3"], m["w2"],
              wts["g_final"], t, True)
    states = (c1[None], n1[None], m1.reshape(1, b, H), lh1.reshape(1, b, W), conv1[None], cc1[None], s1[None],
              sh_out[None])
    return y2.reshape(b, t, d), states


def kernel(x_prompt, x_sample, c_prompt, c_sample, state_mlstm_C, state_mlstm_n, state_mlstm_m, state_lru_h,
           cache_lru_conv, cache_conformer_conv, state_rwkv_S, cache_rwkv_shift, w_ada, b_ada, g_norm_mix,
           g_norm_ffn, w_in_even, b_mlstm_if, g_mlstm_head, w_lru_conv, b_lru_conv, w_lru_r, b_lru_r, w_lru_i,
           b_lru_i, lru_lambda, w_out_even, w_in_odd, b_glu, w_cc_dw, b_cc_dw, g_cc_ln, b_cc_ln, rwkv_mu,
           rwkv_w0, rwkv_wB, rwkv_a0, rwkv_aB, rwkv_gB, rwkv_kk, rwkv_ka, rwkv_rk, g_rwkv_gn, b_rwkv_gn,
           w_out_odd, w_router_g, b_router_g, w_router_e, b_router_e, w_exp_gate, w_exp_up, w_exp_down, g_final):
    bp, bs = x_prompt.shape[0], x_sample.shape[0]

    wi = w_in_even[0]
    gcol = 4 * W
    w_in_e = jnp.concatenate([wi[:, :gcol], wi[:, gcol + 2 * H:], wi[:, gcol:gcol + 2 * H],
                              jnp.zeros((D_MODEL, 128 - 2 * H), F32)], axis=1).astype(BF)
    even = dict(
        w_in=w_in_e, b_if=b_mlstm_if[0], g_head=g_mlstm_head[0], w_conv=w_lru_conv[0], b_conv=b_lru_conv[0],
        w_gate=jnp.concatenate([_block_diag(w_lru_r[0]), _block_diag(w_lru_i[0])], axis=1).astype(BF),
        b_gate=jnp.concatenate([b_lru_r[0], b_lru_i[0]]), lam=lru_lambda[0], w_out=w_out_even[0].astype(BF))
    zl = jnp.zeros((LORA_W, W), F32)
    odd = dict(
        w_in=w_in_odd[0].astype(BF), b_glu=b_glu[0], w_dw=w_cc_dw[0], b_dw=b_cc_dw[0], g_ln=g_cc_ln[0],
        b_ln=b_cc_ln[0], mu=rwkv_mu[0], w0=rwkv_w0[0], a0=rwkv_a0[0],
        wba=jnp.concatenate([jnp.concatenate([rwkv_wB[0], zl], axis=1),
                             jnp.concatenate([zl, rwkv_aB[0]], axis=1)], axis=0).astype(BF),
        gb=rwkv_gB[0].astype(BF), kkw=rwkv_kk[0], kaw=rwkv_ka[0], rk=rwkv_rk[0], g_gn=g_rwkv_gn[0],
        b_gn=b_rwkv_gn[0], w_out=w_out_odd[0].astype(BF))
    moe = []
    for l in range(DEPTH):
        wr = jnp.concatenate([w_router_g[l], w_router_e[l],
                              jnp.zeros((D_MODEL, 128 - N_GROUPS - N_EXPERTS), F32)], axis=1)
        wr_hi = wr.astype(BF)
        wr_lo = (wr - wr_hi.astype(F32)).astype(BF)
        br = jnp.concatenate([b_router_g[l], b_router_e[l],
                              jnp.zeros((128 - N_GROUPS - N_EXPERTS,), F32)]).reshape(1, 128)
        moe.append(dict(wr_hi=wr_hi, wr_lo=wr_lo, br=br, w1=w_exp_gate[l].astype(BF),
                        w3=w_exp_up[l].astype(BF), w2=w_exp_down[l].astype(BF)))
    ones_bd = _block_diag(jnp.ones((H, DH, DH), F32)).astype(BF)
    wts = dict(even=even, odd=odd, moe=moe, g_mix=g_norm_mix, g_ffn=g_norm_ffn, g_final=g_final, ones_bd=ones_bd)

    mod = _ada(jnp.concatenate([c_prompt, c_sample], axis=0), w_ada.astype(BF), b_ada)

    def mods_of(lo, hi):
        return [tuple(mod[l, lo:hi, j * D_MODEL:(j + 1) * D_MODEL].reshape(hi - lo, 1, D_MODEL) for j in range(6))
                for l in range(DEPTH)]

    z = lambda *s: jnp.zeros(s, F32)
    st_p = (z(1, bp, H, DH, DH), z(1, bp, H, DH), z(1, bp, H), z(1, bp, W), z(1, bp, LRU_CONV - 1, W),
            z(1, bp, CONV_C - 1, W), z(1, bp, H, DH, DH), z(1, bp, 3 * W + LORA_W + LORA_A + LORA_G))
    st_s = (state_mlstm_C, state_mlstm_n, state_mlstm_m, state_lru_h, cache_lru_conv, cache_conformer_conv,
            state_rwkv_S, cache_rwkv_shift)
    y_p, out_p = _run_group(x_prompt, mods_of(0, bp), st_p, wts)
    y_s, out_s = _run_group(x_sample, mods_of(bp, bp + bs), st_s, wts)
    return (y_p, y_s) + tuple(out_p) + tuple(out_s)
```

```python
import functools

import jax
import jax.numpy as jnp
from jax import lax
from jax.experimental import pallas as pl
from jax.experimental.pallas import tpu as pltpu

F32 = jnp.float32
BF = jnp.bfloat16

D_MODEL = 1024
DEPTH = 2
H = 8
DH = 64
W = 512
LRU_CONV = 4
LRU_C = 8.0
CONV_C = 31
LORA_W = 64
LORA_A = 64
LORA_G = 128
RWKV_DECAY = 0.606531
RWKV_GN_EPS = 64e-5
N_GROUPS = 4
E_PER_GROUP = 4
N_EXPERTS = 16
D_EXPERT = 256
RMS_EPS = 1e-6
LN_EPS = 1e-5
IN_EVEN_PAD = 6 * W + 128
IN_ODD = 2 * W + 3 * W + LORA_W + LORA_A + LORA_G

ROW_TILE = 1024
VMEM_LIMIT = 48 * 1024 * 1024


def _mm(a, b):
    return jnp.dot(a.astype(BF), b.astype(BF), preferred_element_type=F32)


def _mm_nt(a, b):
    return lax.dot_general(a.astype(BF), b.astype(BF), (((1,), (1,)), ((), ())),
                           preferred_element_type=F32)


def _mm_tn(a, b):
    return lax.dot_general(a.astype(BF), b.astype(BF), (((0,), (0,)), ((), ())),
                           preferred_element_type=F32)


def _split3(x):
    hi = x.astype(BF)
    r = x - hi.astype(F32)
    mid = r.astype(BF)
    lo = (r - mid.astype(F32)).astype(BF)
    return hi, mid, lo


def _mm_mask_l(mask, x):
    return sum(jnp.dot(mask, p, preferred_element_type=F32) for p in _split3(x))


def _mm_mask_r(x, mask):
    return sum(jnp.dot(p, mask, preferred_element_type=F32) for p in _split3(x))


def _sigmoid(x):
    return 1.0 / (1.0 + jnp.exp(-x))


def _rows(v, tm):
    nb, _, c = v.shape
    if nb == 1:
        return v[0]
    return jnp.broadcast_to(v, (nb, tm // nb, c)).reshape(tm, c)


def _mod_spec(T, tm, c):
    if tm <= T:
        per = T // tm
        return pl.BlockSpec((1, 1, c), lambda i, *_: (i // per, 0, 0))
    return pl.BlockSpec((tm // T, 1, c), lambda i, *_: (i, 0, 0))


def _norm_mod(x, g, sh, sc):
    y = x * lax.rsqrt(jnp.mean(x * x, axis=-1, keepdims=True) + RMS_EPS) * g
    return y * (1.0 + sc) + sh


def _params(sem):
    return pltpu.CompilerParams(dimension_semantics=sem, vmem_limit_bytes=VMEM_LIMIT)


def _ada_kernel(c_ref, w_ref, b_ref, o_ref):
    o_ref[0] = _mm(c_ref[...], w_ref[0]) + b_ref[0]


def _ada(c_all, w_bf, b):
    nb = c_all.shape[0]
    tn = 1536
    return pl.pallas_call(
        _ada_kernel,
        grid=(DEPTH, 6 * D_MODEL // tn),
        in_specs=[pl.BlockSpec((nb, D_MODEL), lambda l, j: (0, 0)),
                  pl.BlockSpec((1, D_MODEL, tn), lambda l, j: (l, 0, j)),
                  pl.BlockSpec((1, 1, tn), lambda l, j: (l, 0, j))],
        out_specs=pl.BlockSpec((1, nb, tn), lambda l, j: (l, 0, j)),
        out_shape=jax.ShapeDtypeStruct((DEPTH, nb, 6 * D_MODEL), F32),
        compiler_params=_params(("parallel", "parallel")),
        name="ada",
    )(c_all, w_bf, b.reshape(DEPTH, 1, 6 * D_MODEL))


def _inproj_kernel(x_ref, g_ref, sh_ref, sc_ref, w_ref, o_ref, h_scr):
    tm = x_ref.shape[0]

    @pl.when(pl.program_id(1) == 0)
    def _():
        h = _norm_mod(x_ref[...], g_ref[...], _rows(sh_ref[...], tm), _rows(sc_ref[...], tm))
        h_scr[...] = h.astype(BF)

    o_ref[...] = jnp.dot(h_scr[...], w_ref[...], preferred_element_type=F32)


def _in_proj(x2, g, sh, sc, w_bf, T, tn):
    n, d = x2.shape
    cols = w_bf.shape[1]
    tm = ROW_TILE
    return pl.pallas_call(
        _inproj_kernel,
        grid=(n // tm, cols // tn),
        in_specs=[pl.BlockSpec((tm, d), lambda i, j: (i, 0)),
                  pl.BlockSpec((1, d), lambda i, j: (0, 0)),
                  _mod_spec(T, tm, d), _mod_spec(T, tm, d),
                  pl.BlockSpec((d, tn), lambda i, j: (0, j))],
        out_specs=pl.BlockSpec((tm, tn), lambda i, j: (i, j)),
        out_shape=jax.ShapeDtypeStruct((n, cols), F32),
        scratch_shapes=[pltpu.VMEM((tm, d), BF)],
        compiler_params=_params(("parallel", "arbitrary")),
        name="in_proj",
    )(x2, g.reshape(1, d), sh, sc, w_bf)


def _outproj_kernel(x_ref, a_ref, b_ref, w_ref, gt_ref, o_ref):
    tm = x_ref.shape[0]
    mix = (jnp.dot(a_ref[...], w_ref[0:W, :], preferred_element_type=F32)
           + jnp.dot(b_ref[...], w_ref[W:2 * W, :], preferred_element_type=F32))
    o_ref[...] = x_ref[...] + _rows(gt_ref[...], tm) * mix


def _out_proj(x2, a2, b2, w_bf, gt, T):
    n, d = x2.shape
    tm = ROW_TILE
    return pl.pallas_call(
        _outproj_kernel,
        grid=(n // tm,),
        in_specs=[pl.BlockSpec((tm, d), lambda i: (i, 0)),
                  pl.BlockSpec((tm, W), lambda i: (i, 0)),
                  pl.BlockSpec((tm, W), lambda i: (i, 0)),
                  pl.BlockSpec((2 * W, d), lambda i: (0, 0)),
                  _mod_spec(T, tm, d)],
        out_specs=pl.BlockSpec((tm, d), lambda i: (i, 0)),
        out_shape=jax.ShapeDtypeStruct((n, d), F32),
        compiler_params=_params(("parallel",)),
        name="out_proj",
    )(x2, a2, b2, w_bf, gt)


def _route(logits):
    lane = lax.broadcasted_iota(jnp.int32, logits.shape, 1).astype(F32)
    neg = -jnp.inf
    is_g = lane < N_GROUPS
    lg = jnp.where(is_g, logits, neg)
    mg = jnp.max(lg, axis=1, keepdims=True)
    gsel = jnp.min(jnp.where(lg == mg, lane, 128.0), axis=1, keepdims=True)
    psum = jnp.sum(jnp.where(is_g, jnp.exp(lg - mg), 0.0), axis=1, keepdims=True)
    pg_sel = 1.0 / psum
    lo = N_GROUPS + E_PER_GROUP * gsel
    le = jnp.where((lane >= lo) & (lane < lo + E_PER_GROUP), logits, neg)
    v1 = jnp.max(le, axis=1, keepdims=True)
    i1 = jnp.min(jnp.where(le == v1, lane, 128.0), axis=1, keepdims=True)
    le2 = jnp.where(lane == i1, neg, le)
    v2 = jnp.max(le2, axis=1, keepdims=True)
    i2 = jnp.min(jnp.where(le2 == v2, lane, 128.0), axis=1, keepdims=True)
    e2 = jnp.exp(v2 - v1)
    p1 = 1.0 / (1.0 + e2)
    p2 = e2 / (1.0 + e2)
    return pg_sel * jnp.where(lane == i1, p1, jnp.where(lane == i2, p2, 0.0))


def _moe_kernel(x_ref, g_ref, sh_ref, sc_ref, gt_ref, wrh_ref, wrl_ref, br_ref, w1_ref, w3_ref, w2_ref,
                gf_ref, o_ref, h_scr, gate_scr, acc_scr, *, final_norm):
    tm = x_ref.shape[0]
    e = pl.program_id(1)

    @pl.when(e == 0)
    def _():
        h = _norm_mod(x_ref[...], g_ref[...], _rows(sh_ref[...], tm), _rows(sc_ref[...], tm))
        hb = h.astype(BF)
        h_scr[...] = hb
        hl = (h - hb.astype(F32)).astype(BF)
        logits = (jnp.dot(hb, wrh_ref[...], preferred_element_type=F32)
                  + jnp.dot(hl, wrh_ref[...], preferred_element_type=F32)
                  + jnp.dot(hb, wrl_ref[...], preferred_element_type=F32)) + br_ref[...]
        gate_scr[...] = _route(logits)
        acc_scr[...] = jnp.zeros_like(acc_scr)

    hb = h_scr[...]
    hg = jnp.dot(hb, w1_ref[0], preferred_element_type=F32)
    hu = jnp.dot(hb, w3_ref[0], preferred_element_type=F32)
    lane = lax.broadcasted_iota(jnp.int32, gate_scr.shape, 1)
    ge = jnp.sum(jnp.where(lane == e + N_GROUPS, gate_scr[...], 0.0), axis=1, keepdims=True)
    hh = hg * _sigmoid(hg) * hu * ge
    acc_scr[...] += jnp.dot(hh.astype(BF), w2_ref[0], preferred_element_type=F32)

    @pl.when(e == N_EXPERTS - 1)
    def _():
        y = x_ref[...] + _rows(gt_ref[...], tm) * acc_scr[...]
        if final_norm:
            y = y * lax.rsqrt(jnp.mean(y * y, axis=-1, keepdims=True) + RMS_EPS) * gf_ref[...]
        o_ref[...] = y


def _moe(x2, g, sh, sc, gt, wr_hi, wr_lo, br, w1, w3, w2, g_final, T, final_norm):
    n, d = x2.shape
    tm = ROW_TILE
    return pl.pallas_call(
        functools.partial(_moe_kernel, final_norm=final_norm),
        grid=(n // tm, N_EXPERTS),
        in_specs=[pl.BlockSpec((tm, d), lambda i, e: (i, 0)),
                  pl.BlockSpec((1, d), lambda i, e: (0, 0)),
                  _mod_spec(T, tm, d), _mod_spec(T, tm, d), _mod_spec(T, tm, d),
                  pl.BlockSpec((d, 128), lambda i, e: (0, 0)),
                  pl.BlockSpec((d, 128), lambda i, e: (0, 0)),
                  pl.BlockSpec((1, 128), lambda i, e: (0, 0)),
                  pl.BlockSpec((1, d, D_EXPERT), lambda i, e: (e, 0, 0)),
                  pl.BlockSpec((1, d, D_EXPERT), lambda i, e: (e, 0, 0)),
                  pl.BlockSpec((1, D_EXPERT, d), lambda i, e: (e, 0, 0)),
                  pl.BlockSpec((1, d), lambda i, e: (0, 0))],
        out_specs=pl.BlockSpec((tm, d), lambda i, e: (i, 0)),
        out_shape=jax.ShapeDtypeStruct((n, d), F32),
        scratch_shapes=[pltpu.VMEM((tm, d), BF), pltpu.VMEM((tm, 128), F32), pltpu.VMEM((tm, d), F32)],
        compiler_params=_params(("parallel", "arbitrary")),
        name="moe",
    )(x2, g.reshape(1, d), sh, sc, gt, wr_hi, wr_lo, br, w1, w3, w2, g_final.reshape(1, d))


def _log_sigmoid(x):
    return jnp.minimum(x, 0.0) - jnp.log(1.0 + jnp.exp(-jnp.abs(x)))


def _mlstm_kernel(q_ref, k_ref, v_ref, o_ref, g_ref, gt_ref, bif_ref, bift_ref, gh_ref, c0_ref, n0_ref, m0_ref,
                  h_ref, c1_ref, n1_ref, m1_ref, c_scr, n_scr, m_scr, *, L, nb, nc):
    c = pl.program_id(1)

    @pl.when(c == 0)
    def _():
        c_scr[...] = c0_ref[...]
        n_scr[...] = n0_ref[...]
        m_scr[...] = m0_ref[...]

    row = lax.broadcasted_iota(jnp.int32, (L, L), 0)
    col = lax.broadcasted_iota(jnp.int32, (L, L), 1)
    causal = row >= col
    tri = jnp.where(causal, 1.0, 0.0).astype(BF)
    tri_u = jnp.where(row <= col, 1.0, 0.0).astype(BF)
    ibs = range(nb)
    g = [g_ref[ib] + bif_ref[...] for ib in ibs]
    gt = [gt_ref[ib] + bift_ref[...] for ib in ibs]
    bcum = [_mm_mask_l(tri, _log_sigmoid(g[ib])) for ib in ibs]
    bcum_t = [_mm_mask_r(_log_sigmoid(gt[ib]), tri_u) for ib in ibs]
    m_prev = [m_scr[ib] for ib in ibs]
    sls = [slice(h * DH, (h + 1) * DH) for h in range(H)]
    ch = [(ib, h) for ib in ibs for h in range(H)]
    q = {u: q_ref[u[0], :, sls[u[1]]].astype(BF) for u in ch}
    k = {u: k_ref[u[0], :, sls[u[1]]] * (DH ** -0.5) for u in ch}
    v = {u: v_ref[u[0], :, sls[u[1]]].astype(BF) for u in ch}
    qk = {u: _mm_nt(q[u], k[u]) for u in ch}
    cs = {u: c_scr[u[0], u[1]] for u in ch}
    ns = {u: n_scr[u[0], u[1]:u[1] + 1, :] for u in ch}
    qc = {u: _mm(q[u], cs[u]) for u in ch}
    bc = {u: bcum[u[0]][:, H + u[1]:H + u[1] + 1] for u in ch}
    mp = {u: m_prev[u[0]][:, u[1]:u[1] + 1] for u in ch}
    dmat = {u: jnp.where(causal, bc[u] + (gt[u[0]][u[1]:u[1] + 1, :] - bcum_t[u[0]][H + u[1]:H + u[1] + 1, :]),
                         -jnp.inf) for u in ch}
    g_inter = {u: bc[u] + mp[u] for u in ch}
    m_t = {u: jnp.maximum(g_inter[u], jnp.max(dmat[u], axis=1, keepdims=True)) for u in ch}
    s = {u: qk[u] * jnp.exp(dmat[u] - m_t[u]) for u in ch}
    w_inter = {u: jnp.exp(g_inter[u] - m_t[u]) for u in ch}
    num = {u: _mm(s[u], v[u]) + w_inter[u] * qc[u] for u in ch}
    den = {u: jnp.sum(s[u], axis=1, keepdims=True)
           + w_inter[u] * jnp.sum(q_ref[u[0], :, sls[u[1]]] * ns[u], axis=1, keepdims=True) for u in ch}
    hh = {u: num[u] / jnp.maximum(jnp.abs(den[u]), jnp.exp(-m_t[u])) for u in ch}
    m_new = {u: m_t[u][L - 1:L, :] for u in ch}
    b_last = {u: bc[u][L - 1:L, :] for u in ch}
    kw = {u: k[u] * jnp.exp(b_last[u] - bc[u] + g[u[0]][:, u[1]:u[1] + 1] - m_new[u]) for u in ch}
    decay = {u: jnp.exp(b_last[u] + mp[u] - m_new[u]) for u in ch}
    for u in ch:
        c_scr[u[0], u[1]] = decay[u] * cs[u] + _mm_tn(kw[u], v[u])
        n_scr[u[0], u[1]:u[1] + 1, :] = decay[u] * ns[u] + jnp.sum(kw[u], axis=0, keepdims=True)
    for ib in ibs:
        outs = []
        for h in range(H):
            x = hh[ib, h]
            mu = jnp.mean(x, axis=1, keepdims=True)
            var = jnp.mean(jnp.square(x - mu), axis=1, keepdims=True)
            hn = (x - mu) * lax.rsqrt(var + RMS_EPS) * gh_ref[:, sls[h]]
            outs.append(hn * _sigmoid(o_ref[ib, :, sls[h]]))
        h_ref[ib] = jnp.concatenate(outs, axis=1).astype(BF)
        m_scr[ib] = jnp.concatenate([m_new[ib, h] for h in range(H)], axis=1)

    @pl.when(c == nc - 1)
    def _():
        c1_ref[...] = c_scr[...]
        n1_ref[...] = n_scr[...]
        m1_ref[...] = m_scr[...]


def _mlstm(p3, gif_t, bif, g_head, c0, n0, m0, L, nb):
    b, t, _ = p3.shape
    nc = t // L
    col = lambda j: pl.BlockSpec((nb, L, W), lambda i, c, j=j: (i, c, j))
    bif_pad = jnp.zeros((1, 128), F32).at[0, :2 * H].set(bif)
    return pl.pallas_call(
        functools.partial(_mlstm_kernel, L=L, nb=nb, nc=nc),
        grid=(b // nb, nc),
        in_specs=[col(0), col(1), col(2), col(3),
                  pl.BlockSpec((nb, L, 128), lambda i, c: (i, c, 6 * W // 128)),
                  pl.BlockSpec((nb, 2 * H, L), lambda i, c: (i, 0, c)),
                  pl.BlockSpec((1, 128), lambda i, c: (0, 0)),
                  pl.BlockSpec((2 * H, 1), lambda i, c: (0, 0)),
                  pl.BlockSpec((1, W), lambda i, c: (0, 0)),
                  pl.BlockSpec((nb, H, DH, DH), lambda i, c: (i, 0, 0, 0)),
                  pl.BlockSpec((nb, H, DH), lambda i, c: (i, 0, 0)),
                  pl.BlockSpec((nb, 1, H), lambda i, c: (i, 0, 0))],
        out_specs=[pl.BlockSpec((nb, L, W), lambda i, c: (i, c, 0)),
                   pl.BlockSpec((nb, H, DH, DH), lambda i, c: (i, 0, 0, 0)),
                   pl.BlockSpec((nb, H, DH), lambda i, c: (i, 0, 0)),
                   pl.BlockSpec((nb, 1, H), lambda i, c: (i, 0, 0))],
        out_shape=[jax.ShapeDtypeStruct((b, t, W), BF),
                   jax.ShapeDtypeStruct((b, H, DH, DH), F32),
                   jax.ShapeDtypeStruct((b, H, DH), F32),
                   jax.ShapeDtypeStruct((b, 1, H), F32)],
        scratch_shapes=[pltpu.VMEM((nb, H, DH, DH), F32), pltpu.VMEM((nb, H, DH), F32),
                        pltpu.VMEM((nb, 1, H), F32)],
        compiler_params=_params(("parallel", "arbitrary")),
        name="mlstm",
    )(p3, p3, p3, p3, p3, gif_t, bif_pad, bif.reshape(2 * H, 1), g_head.reshape(1, W),
      c0, n0, m0.reshape(b, 1, H))


def _gelu_tanh(x):
    return 0.5 * x * (1.0 + jnp.tanh(0.7978845608028654 * (x + 0.044715 * x * x * x)))


def _lru_kernel(xr_ref, xg_ref, cache_ref, h0_ref, wc_ref, bc_ref, wg_ref, bg_ref, lam_ref,
                o_ref, h1_ref, ext_scr, a_scr, u_scr, hs_scr, hc_scr, *, Tc, nc):
    c = pl.program_id(1)
    K1 = LRU_CONV - 1

    @pl.when(c == 0)
    def _():
        ext_scr[8 - K1:8, :] = cache_ref[0]
        hc_scr[...] = h0_ref[0]

    x = xr_ref[0]
    ext_scr[8:8 + Tc, :] = x
    xc = bc_ref[...] + wc_ref[K1:K1 + 1, :] * x
    for d in range(1, LRU_CONV):
        xc = xc + wc_ref[K1 - d:K1 - d + 1, :] * ext_scr[8 - d:8 - d + Tc, :]
    ext_scr[8 - K1:8, :] = ext_scr[8 + Tc - K1:8 + Tc, :]
    gates = _mm(xc, wg_ref[...]) + bg_ref[...]
    r = _sigmoid(gates[:, 0:W])
    ig = _sigmoid(gates[:, W:2 * W])
    lam = lam_ref[...]
    softplus_neg = jnp.maximum(-lam, 0.0) + jnp.log(1.0 + jnp.exp(-jnp.abs(lam)))
    log_a = -LRU_C * r * softplus_neg
    a_scr[...] = jnp.exp(log_a)
    th = jnp.tanh(log_a)
    one_minus_a2 = -2.0 * th / (1.0 - th)
    u_scr[...] = jnp.sqrt(one_minus_a2) * (ig * xc)

    def body(t, h):
        h = a_scr[pl.ds(t, 1), :] * h + u_scr[pl.ds(t, 1), :]
        hs_scr[pl.ds(t, 1), :] = h
        return h

    h_fin = lax.fori_loop(0, Tc, body, hc_scr[...], unroll=8)
    hc_scr[...] = h_fin
    o_ref[0] = (hs_scr[...] * _gelu_tanh(xg_ref[0])).astype(BF)

    @pl.when(c == nc - 1)
    def _():
        h1_ref[0] = h_fin


def _lru(p3, cache, h0, w_conv, b_conv, wg_bf, bg, lam, Tc):
    b, t, _ = p3.shape
    nc = t // Tc
    return pl.pallas_call(
        functools.partial(_lru_kernel, Tc=Tc, nc=nc),
        grid=(b, nc),
        in_specs=[pl.BlockSpec((1, Tc, W), lambda i, c: (i, c, 4)),
                  pl.BlockSpec((1, Tc, W), lambda i, c: (i, c, 5)),
                  pl.BlockSpec((1, LRU_CONV - 1, W), lambda i, c: (i, 0, 0)),
                  pl.BlockSpec((1, 1, W), lambda i, c: (i, 0, 0)),
                  pl.BlockSpec((LRU_CONV, W), lambda i, c: (0, 0)),
                  pl.BlockSpec((1, W), lambda i, c: (0, 0)),
                  pl.BlockSpec((W, 2 * W), lambda i, c: (0, 0)),
                  pl.BlockSpec((1, 2 * W), lambda i, c: (0, 0)),
                  pl.BlockSpec((1, W), lambda i, c: (0, 0))],
        out_specs=[pl.BlockSpec((1, Tc, W), lambda i, c: (i, c, 0)),
                   pl.BlockSpec((1, 1, W), lambda i, c: (i, 0, 0))],
        out_shape=[jax.ShapeDtypeStruct((b, t, W), BF), jax.ShapeDtypeStruct((b, 1, W), F32)],
        scratch_shapes=[pltpu.VMEM((8 + Tc, W), F32), pltpu.VMEM((Tc, W), F32), pltpu.VMEM((Tc, W), F32),
                        pltpu.VMEM((Tc, W), F32), pltpu.VMEM((1, W), F32)],
        compiler_params=_params(("parallel", "arbitrary")),
        name="lru",
    )(p3, p3, cache, h0.reshape(b, 1, W), w_conv, b_conv.reshape(1, W), wg_bf, bg.reshape(1, 2 * W),
      lam.reshape(1, W))


CONF_ROWS = 32


def _conf_kernel(u_ref, gte_ref, cache_ref, bu_ref, bg_ref, wdw_ref, bdw_ref, gln_ref, bln_ref,
                 o_ref, cache1_ref, ext_scr, *, Tc, nc):
    c = pl.program_id(1)
    K1 = CONV_C - 1
    base = 32 - K1

    @pl.when(c == 0)
    def _():
        ext_scr[base:32, :] = cache_ref[0]

    u = u_ref[0] + bu_ref[...]
    gte = gte_ref[0] + bg_ref[...]
    ext_scr[32:32 + Tc, :] = u * _sigmoid(gte)
    rb = min(CONF_ROWS, Tc)
    for r0 in range(0, Tc, rb):
        acc = bdw_ref[...] + wdw_ref[0:1, :] * ext_scr[base + r0:base + r0 + rb, :]
        for j in range(1, CONV_C):
            acc = acc + wdw_ref[j:j + 1, :] * ext_scr[base + r0 + j:base + r0 + j + rb, :]
        mu = jnp.mean(acc, axis=1, keepdims=True)
        var = jnp.mean(jnp.square(acc - mu), axis=1, keepdims=True)
        y = (acc - mu) * lax.rsqrt(var + LN_EPS) * gln_ref[...] + bln_ref[...]
        o_ref[0, r0:r0 + rb, :] = (y * _sigmoid(y)).astype(BF)
    tail = ext_scr[base + Tc:32 + Tc, :]
    ext_scr[base:32, :] = tail

    @pl.when(c == nc - 1)
    def _():
        cache1_ref[0] = tail


def _conf(p3, cache, b_glu, w_dw, b_dw, g_ln, b_ln, Tc):
    b, t, _ = p3.shape
    nc = t // Tc
    vec = lambda: pl.BlockSpec((1, W), lambda i, c: (0, 0))
    return pl.pallas_call(
        functools.partial(_conf_kernel, Tc=Tc, nc=nc),
        grid=(b, nc),
        in_specs=[pl.BlockSpec((1, Tc, W), lambda i, c: (i, c, 0)),
                  pl.BlockSpec((1, Tc, W), lambda i, c: (i, c, 1)),
                  pl.BlockSpec((1, CONV_C - 1, W), lambda i, c: (i, 0, 0)),
                  vec(), vec(),
                  pl.BlockSpec((CONV_C, W), lambda i, c: (0, 0)),
                  vec(), vec(), vec()],
        out_specs=[pl.BlockSpec((1, Tc, W), lambda i, c: (i, c, 0)),
                   pl.BlockSpec((1, CONV_C - 1, W), lambda i, c: (i, 0, 0))],
        out_shape=[jax.ShapeDtypeStruct((b, t, W), BF), jax.ShapeDtypeStruct((b, CONV_C - 1, W), F32)],
        scratch_shapes=[pltpu.VMEM((32 + Tc, W), F32)],
        compiler_params=_params(("parallel", "arbitrary")),
        name="conformer",
    )(p3, p3, cache, b_glu[:W].reshape(1, W), b_glu[W:].reshape(1, W), w_dw, b_dw.reshape(1, W),
      g_ln.reshape(1, W), b_ln.reshape(1, W))


def _seg_sum(x, ones_bd):
    hi = x.astype(BF)
    lo = (x - hi.astype(F32)).astype(BF)
    return (jnp.dot(hi, ones_bd, preferred_element_type=F32)
            + jnp.dot(lo, ones_bd, preferred_element_type=F32))


def _rprep_kernel(pr_ref, pk_ref, pv_ref, pz_ref, sr_ref, sk_ref, sv_ref, sz_ref,
                  mr_ref, mk_ref, mv_ref, mz_ref, w0_ref, a0_ref, wba_ref, gb_ref, kkw_ref, kaw_ref, rk_ref,
                  ones_ref, r_ref, lw_ref, k_ref, v_ref, kk_ref, b_ref, bonus_ref, g_ref,
                  qr_scr, qk_scr, qv_scr, qz_scr, *, Tc):
    c = pl.program_id(1)

    @pl.when(c == 0)
    def _():
        qr_scr[...] = sr_ref[0]
        qk_scr[...] = sk_ref[0]
        qv_scr[...] = sv_ref[0]
        qz_scr[...] = sz_ref[0]

    def shift_mix(x_ref, prev_scr, mu_ref):
        x = x_ref[0]
        first = lax.broadcasted_iota(jnp.int32, x.shape, 0) == 0
        prev = jnp.where(first, prev_scr[...], pltpu.roll(x, 1, 0))
        prev_scr[...] = x[Tc - 1:Tc, :]
        return x + (prev - x) * mu_ref[...]

    r = shift_mix(pr_ref, qr_scr, mr_ref)
    k = shift_mix(pk_ref, qk_scr, mk_ref)
    v = shift_mix(pv_ref, qv_scr, mv_ref)
    z = shift_mix(pz_ref, qz_scr, mz_ref)
    zwa = z[:, 0:128]
    lane = lax.broadcasted_iota(jnp.int32, zwa.shape, 1)
    wa = _mm(jnp.where(lane < LORA_W, jnp.tanh(zwa), zwa), wba_ref[...])
    lw = -RWKV_DECAY * _sigmoid(w0_ref[...] + wa[:, 0:W])
    a = _sigmoid(a0_ref[...] + wa[:, W:2 * W])
    g = _mm(_sigmoid(z[:, 128:256]), gb_ref[...])
    ones_bd = ones_ref[...]
    kk = k * kkw_ref[...]
    kk = kk * lax.rsqrt(_seg_sum(kk * kk, ones_bd) + 1e-12)
    k2 = k * (1.0 + (a - 1.0) * kaw_ref[...])
    r_ref[0] = r
    lw_ref[0] = lw
    k_ref[0] = k2
    v_ref[0] = v
    kk_ref[0] = kk
    b_ref[0] = kk * a
    bonus_ref[0] = _seg_sum(r * k2 * rk_ref[...], ones_bd) * v
    g_ref[0] = g


def _rprep(p3, shift, mu, w0, a0, wba_bf, gb_bf, kkw, kaw, rk, ones_bd, Tc):
    b, t, _ = p3.shape
    nc = t // Tc
    vec = lambda: pl.BlockSpec((1, W), lambda i, c: (0, 0))
    st = lambda width: pl.BlockSpec((1, 1, width), lambda i, c: (i, 0, 0))
    out = pl.BlockSpec((1, Tc, W), lambda i, c: (i, c, 0))
    sh3 = shift.reshape(b, 1, -1)
    mu2 = mu.reshape(1, -1)
    return pl.pallas_call(
        functools.partial(_rprep_kernel, Tc=Tc),
        grid=(b, nc),
        in_specs=[pl.BlockSpec((1, Tc, W), lambda i, c: (i, c, 2)),
                  pl.BlockSpec((1, Tc, W), lambda i, c: (i, c, 3)),
                  pl.BlockSpec((1, Tc, W), lambda i, c: (i, c, 4)),
                  pl.BlockSpec((1, Tc, 256), lambda i, c: (i, c, 10)),
                  st(W), st(W), st(W), st(256),
                  vec(), vec(), vec(), pl.BlockSpec((1, 256), lambda i, c: (0, 0)),
                  vec(), vec(),
                  pl.BlockSpec((128, 2 * W), lambda i, c: (0, 0)),
                  pl.BlockSpec((LORA_G, W), lambda i, c: (0, 0)),
                  vec(), vec(), vec(),
                  pl.BlockSpec((W, W), lambda i, c: (0, 0))],
        out_specs=[out] * 8,
        out_shape=[jax.ShapeDtypeStruct((b, t, W), F32)] * 8,
        scratch_shapes=[pltpu.VMEM((1, W), F32), pltpu.VMEM((1, W), F32), pltpu.VMEM((1, W), F32),
                        pltpu.VMEM((1, 256), F32)],
        compiler_params=_params(("parallel", "arbitrary")),
        name="rwkv_prep",
    )(p3, p3, p3, p3,
      sh3[:, :, 0:W], sh3[:, :, W:2 * W], sh3[:, :, 2 * W:3 * W], sh3[:, :, 3 * W:],
      mu2[:, 0:W], mu2[:, W:2 * W], mu2[:, 2 * W:3 * W], mu2[:, 3 * W:],
      w0.reshape(1, W), a0.reshape(1, W), wba_bf, gb_bf, kkw.reshape(1, W), kaw.reshape(1, W),
      rk.reshape(1, W), ones_bd)


def _rscan_kernel(r_ref, lw_ref, k_ref, v_ref, kk_ref, b_ref, bonus_ref, g_ref, gn_ref, bn_ref, s0_ref,
                  y_ref, s1_ref, s_scr, *, L, nb, ncs, nc):
    c = pl.program_id(1)

    @pl.when(c == 0)
    def _():
        s_scr[...] = s0_ref[...]

    assert ncs == 1
    row = lax.broadcasted_iota(jnp.int32, (L, L), 0)
    col = lax.broadcasted_iota(jnp.int32, (L, L), 1)
    eye = jnp.where(row == col, 1.0, 0.0)
    tri = jnp.where(row >= col, 1.0, 0.0).astype(BF)
    row2 = lax.broadcasted_iota(jnp.int32, (2 * L, 2 * L), 0)
    col2 = lax.broadcasted_iota(jnp.int32, (2 * L, 2 * L), 1)
    cc = jnp.where(col2 >= L, col2 - L, col2)
    keep = jnp.where(row2 < L, row2 - 1, row2 - L) >= cc
    sls = [slice(h * DH, (h + 1) * DH) for h in range(H)]
    chains = [(ib, h) for ib in range(nb) for h in range(H)]
    pre = {}
    for ib in range(nb):
        lw = lw_ref[ib]
        cum = _mm_mask_l(tri, lw)
        p_in = jnp.exp(cum)
        p_inv = jnp.exp(-cum)
        p_last = p_in[L - 1:L, :]
        kh = k_ref[ib] * p_inv
        bh = b_ref[ib] * p_inv
        pre[ib] = dict(
            kr=jnp.concatenate([(kk_ref[ib] * jnp.exp(cum - lw)).astype(BF), (r_ref[ib] * p_in).astype(BF)], axis=0),
            kb=jnp.concatenate([kh.astype(BF), bh.astype(BF)], axis=0),
            kbl=jnp.concatenate([(kh * p_last).astype(BF), (bh * p_last).astype(BF)], axis=0),
            v=v_ref[ib].astype(BF), p_last=p_last)
    op = lambda u, name: pre[u[0]][name][:, sls[u[1]]]
    gm = {u: jnp.where(keep, _mm_nt(op(u, "kr"), op(u, "kb")), 0.0) for u in chains}
    akr = {u: gm[u][:, 0:L].astype(BF) for u in chains}
    r_b = {u: gm[u][L:2 * L, L:2 * L].astype(BF) for u in chains}
    xs = {u: -gm[u][0:L, L:2 * L] for u in chains}
    invs = {u: eye + xs[u] for u in chains}
    if L > 2:
        xs = {u: _mm(xs[u], xs[u]) for u in chains}
    n = 4
    while n < L:
        st = {u: _mm(jnp.concatenate([invs[u], xs[u]], axis=0), xs[u]) for u in chains}
        invs = {u: invs[u] + st[u][0:L] for u in chains}
        xs = {u: st[u][L:2 * L] for u in chains}
        n *= 2
    if L > 2:
        invs = {u: invs[u] + _mm(invs[u], xs[u]) for u in chains}
    akrv = {u: _mm(akr[u], op(u, "v")) for u in chains}
    ss = {u: s_scr[u] for u in chains}
    krs = {u: _mm_nt(op(u, "kr"), ss[u]) for u in chains}
    us = {u: _mm(invs[u], krs[u][0:L] + akrv[u][0:L]) for u in chains}
    ys = {u: krs[u][L:2 * L] + akrv[u][L:2 * L] - _mm(r_b[u], us[u]) for u in chains}
    for u in chains:
        vu = jnp.concatenate([op(u, "v"), (-us[u]).astype(BF)], axis=0)
        s_scr[u] = ss[u] * pre[u[0]]["p_last"][:, sls[u[1]]] + _mm_tn(vu, op(u, "kbl"))
    for ib in range(nb):
        outs = []
        for h in range(H):
            y = ys[ib, h]
            mu = jnp.mean(y, axis=1, keepdims=True)
            var = jnp.mean(jnp.square(y - mu), axis=1, keepdims=True)
            yn = (y - mu) * lax.rsqrt(var + RWKV_GN_EPS) * gn_ref[:, sls[h]] + bn_ref[:, sls[h]]
            outs.append((yn + bonus_ref[ib, :, sls[h]]) * g_ref[ib, :, sls[h]])
        y_ref[ib] = jnp.concatenate(outs, axis=1).astype(BF)

    @pl.when(c == nc - 1)
    def _():
        s1_ref[...] = s_scr[...]


def _rscan(r, lw, k, v, kk, bb, bonus, g, g_gn, b_gn, s0, L, nb, ncs):
    b, t, _ = r.shape
    nc = t // (L * ncs)
    blk = lambda: pl.BlockSpec((nb, L * ncs, W), lambda i, c: (i, c, 0))
    vec = lambda: pl.BlockSpec((1, W), lambda i, c: (0, 0))
    st = pl.BlockSpec((nb, H, DH, DH), lambda i, c: (i, 0, 0, 0))
    return pl.pallas_call(
        functools.partial(_rscan_kernel, L=L, nb=nb, ncs=ncs, nc=nc),
        grid=(b // nb, nc),
        in_specs=[blk() for _ in range(8)] + [vec(), vec(), st],
        out_specs=[blk(), st],
        out_shape=[jax.ShapeDtypeStruct((b, t, W), BF), jax.ShapeDtypeStruct((b, H, DH, DH), F32)],
        scratch_shapes=[pltpu.VMEM((nb, H, DH, DH), F32)],
        compiler_params=_params(("parallel", "arbitrary")),
        name="rwkv_scan",
    )(r, lw, k, v, kk, bb, bonus, g, g_gn.reshape(1, W), b_gn.reshape(1, W), s0)


def _block_diag(w):
    nb, bw, _ = w.shape
    return (jnp.eye(nb, dtype=w.dtype)[:, None, :, None] * w[:, :, None, :]).reshape(nb * bw, nb * bw)


def _chunk(t, target):
    return target if t % target == 0 else t


def _scan_blocking(b, t, target):
    L = _chunk(t, target)
    return L, min(b, 4 if L >= 64 else 8), 1


def _run_group(x, mods, st, wts):
    b, t, d = x.shape
    x2 = x.reshape(b * t, d)
    (mc, mn, mm, lh, lconv, ccb, rs, rsh) = st

    sh1, sc1, gt1, sh2, sc2, gt2 = mods[0]
    e = wts["even"]
    p = _in_proj(x2, wts["g_mix"][0], sh1, sc1, e["w_in"], t, 640)
    p3 = p.reshape(b, t, IN_EVEN_PAD)
    gif_t = jnp.transpose(p3[:, :, 6 * W:6 * W + 2 * H], (0, 2, 1))
    hm, c1, n1, m1 = _mlstm(p3, gif_t, e["b_if"], e["g_head"], mc[0], mn[0], mm[0], _chunk(t, 128), 1)
    hl, lh1 = _lru(p3, lconv[0], lh[0], e["w_conv"], e["b_conv"], e["w_gate"], e["b_gate"], e["lam"],
                   _chunk(t, 256))
    xr = p3[:, :, 4 * W:5 * W]
    conv1 = jnp.concatenate([lconv[0], xr], axis=1)[:, -(LRU_CONV - 1):]
    x2 = _out_proj(x2, hm.reshape(b * t, W), hl.reshape(b * t, W), e["w_out"], gt1, t)
    m = wts["moe"][0]
    x2 = _moe(x2, wts["g_ffn"][0], sh2, sc2, gt2, m["wr_hi"], m["wr_lo"], m["br"], m["w1"], m["w3"], m["w2"],
              wts["g_final"], t, False)

    sh1, sc1, gt1, sh2, sc2, gt2 = mods[1]
    o = wts["odd"]
    p = _in_proj(x2, wts["g_mix"][1], sh1, sc1, o["w_in"], t, 1408)
    p3 = p.reshape(b, t, IN_ODD)
    cc, cc1 = _conf(p3, ccb[0], o["b_glu"], o["w_dw"], o["b_dw"], o["g_ln"], o["b_ln"], _chunk(t, 256))
    r, lw, k2, v, kk, bb, bonus, g = _rprep(p3, rsh[0], o["mu"], o["w0"], o["a0"], o["wba"], o["gb"],
                                            o["kkw"], o["kaw"], o["rk"], wts["ones_bd"], _chunk(t, 256))
    y, s1 = _rscan(r, lw, k2, v, kk, bb, bonus, g, o["g_gn"], o["b_gn"], rs[0], *_scan_blocking(b, t, 64))
    sh_out = p3[:, t - 1, 2 * W:]
    x2 = _out_proj(x2, cc.reshape(b * t, W), y.reshape(b * t, W), o["w_out"], gt1, t)
    m = wts["moe"][1]
    y2 = _moe(x2, wts["g_ffn"][1], sh2, sc2, gt2, m["wr_hi"], m["wr_lo"], m["br"], m["w1"], m["w3"], m["w2"],
              wts["g_final"], t, True)
    states = (c1[None], n1[None], m1.reshape(1, b, H), lh1.reshape(1, b, W), conv1[None], cc1[None], s1[None],
              sh_out[None])
    return y2.reshape(b, t, d), states


def kernel(x_prompt, x_sample, c_prompt, c_sample, state_mlstm_C, state_mlstm_n, state_mlstm_m, state_lru_h,
           cache_lru_conv, cache_conformer_conv, state_rwkv_S, cache_rwkv_shift, w_ada, b_ada, g_norm_mix,
           g_norm_ffn, w_in_even, b_mlstm_if, g_mlstm_head, w_lru_conv, b_lru_conv, w_lru_r, b_lru_r, w_lru_i,
           b_lru_i, lru_lambda, w_out_even, w_in_odd, b_glu, w_cc_dw, b_cc_dw, g_cc_ln, b_cc_ln, rwkv_mu,
           rwkv_w0, rwkv_wB, rwkv_a0, rwkv_aB, rwkv_gB, rwkv_kk, rwkv_ka, rwkv_rk, g_rwkv_gn, b_rwkv_gn,
           w_out_odd, w_router_g, b_router_g, w_router_e, b_router_e, w_exp_gate, w_exp_up, w_exp_down, g_final):
    bp, bs = x_prompt.shape[0], x_sample.shape[0]

    wi = w_in_even[0]
    gcol = 4 * W
    w_in_e = jnp.concatenate([wi[:, :gcol], wi[:, gcol + 2 * H:], wi[:, gcol:gcol + 2 * H],
                              jnp.zeros((D_MODEL, 128 - 2 * H), F32)], axis=1).astype(BF)
    even = dict(
        w_in=w_in_e, b_if=b_mlstm_if[0], g_head=g_mlstm_head[0], w_conv=w_lru_conv[0], b_conv=b_lru_conv[0],
        w_gate=jnp.concatenate([_block_diag(w_lru_r[0]), _block_diag(w_lru_i[0])], axis=1).astype(BF),
        b_gate=jnp.concatenate([b_lru_r[0], b_lru_i[0]]), lam=lru_lambda[0], w_out=w_out_even[0].astype(BF))
    zl = jnp.zeros((LORA_W, W), F32)
    odd = dict(
        w_in=w_in_odd[0].astype(BF), b_glu=b_glu[0], w_dw=w_cc_dw[0], b_dw=b_cc_dw[0], g_ln=g_cc_ln[0],
        b_ln=b_cc_ln[0], mu=rwkv_mu[0], w0=rwkv_w0[0], a0=rwkv_a0[0],
        wba=jnp.concatenate([jnp.concatenate([rwkv_wB[0], zl], axis=1),
                             jnp.concatenate([zl, rwkv_aB[0]], axis=1)], axis=0).astype(BF),
        gb=rwkv_gB[0].astype(BF), kkw=rwkv_kk[0], kaw=rwkv_ka[0], rk=rwkv_rk[0], g_gn=g_rwkv_gn[0],
        b_gn=b_rwkv_gn[0], w_out=w_out_odd[0].astype(BF))
    moe = []
    for l in range(DEPTH):
        wr = jnp.concatenate([w_router_g[l], w_router_e[l],
                              jnp.zeros((D_MODEL, 128 - N_GROUPS - N_EXPERTS), F32)], axis=1)
        wr_hi = wr.astype(BF)
        wr_lo = (wr - wr_hi.astype(F32)).astype(BF)
        br = jnp.concatenate([b_router_g[l], b_router_e[l],
                              jnp.zeros((128 - N_GROUPS - N_EXPERTS,), F32)]).reshape(1, 128)
        moe.append(dict(wr_hi=wr_hi, wr_lo=wr_lo, br=br, w1=w_exp_gate[l].astype(BF),
                        w3=w_exp_up[l].astype(BF), w2=w_exp_down[l].astype(BF)))
    ones_bd = _block_diag(jnp.ones((H, DH, DH), F32)).astype(BF)
    wts = dict(even=even, odd=odd, moe=moe, g_mix=g_norm_mix, g_ffn=g_norm_ffn, g_final=g_final, ones_bd=ones_bd)

    mod = _ada(jnp.concatenate([c_prompt, c_sample], axis=0), w_ada.astype(BF), b_ada)

    def mods_of(lo, hi):
        return [tuple(mod[l, lo:hi, j * D_MODEL:(j + 1) * D_MODEL].reshape(hi - lo, 1, D_MODEL) for j in range(6))
                for l in range(DEPTH)]

    z = lambda *s: jnp.zeros(s, F32)
    st_p = (z(1, bp, H, DH, DH), z(1, bp, H, DH), z(1, bp, H), z(1, bp, W), z(1, bp, LRU_CONV - 1, W),
            z(1, bp, CONV_C - 1, W), z(1, bp, H, DH, DH), z(1, bp, 3 * W + LORA_W + LORA_A + LORA_G))
    st_s = (state_mlstm_C, state_mlstm_n, state_mlstm_m, state_lru_h, cache_lru_conv, cache_conformer_conv,
            state_rwkv_S, cache_rwkv_shift)
    y_p, out_p = _run_group(x_prompt, mods_of(0, bp), st_p, wts)
    y_s, out_s = _run_group(x_sample, mods_of(bp, bp + bs), st_s, wts)
    return (y_p, y_s) + tuple(out_p) + tuple(out_s)
```

```python
import functools

import jax
import jax.numpy as jnp
from jax import lax
from jax.experimental import pallas as pl
from jax.experimental.pallas import tpu as pltpu

F32 = jnp.float32
BF = jnp.bfloat16

D_MODEL = 1024
DEPTH = 2
H = 8
DH = 64
W = 512
LRU_CONV = 4
LRU_C = 8.0
CONV_C = 31
LORA_W = 64
LORA_A = 64
LORA_G = 128
RWKV_DECAY = 0.606531
RWKV_GN_EPS = 64e-5
N_GROUPS = 4
E_PER_GROUP = 4
N_EXPERTS = 16
D_EXPERT = 256
RMS_EPS = 1e-6
LN_EPS = 1e-5
IN_EVEN_PAD = 6 * W + 128
IN_ODD = 2 * W + 3 * W + LORA_W + LORA_A + LORA_G

ROW_TILE = 1024
VMEM_LIMIT = 48 * 1024 * 1024


def _mm(a, b):
    return jnp.dot(a.astype(BF), b.astype(BF), preferred_element_type=F32)


def _mm_nt(a, b):
    return lax.dot_general(a.astype(BF), b.astype(BF), (((1,), (1,)), ((), ())),
                           preferred_element_type=F32)


def _mm_tn(a, b):
    return lax.dot_general(a.astype(BF), b.astype(BF), (((0,), (0,)), ((), ())),
                           preferred_element_type=F32)


def _split3(x):
    hi = x.astype(BF)
    r = x - hi.astype(F32)
    mid = r.astype(BF)
    lo = (r - mid.astype(F32)).astype(BF)
    return hi, mid, lo


def _mm_mask_l(mask, x):
    return sum(jnp.dot(mask, p, preferred_element_type=F32) for p in _split3(x))


def _mm_mask_r(x, mask):
    return sum(jnp.dot(p, mask, preferred_element_type=F32) for p in _split3(x))


def _sigmoid(x):
    return 1.0 / (1.0 + jnp.exp(-x))


def _rows(v, tm):
    nb, _, c = v.shape
    if nb == 1:
        return v[0]
    return jnp.broadcast_to(v, (nb, tm // nb, c)).reshape(tm, c)


def _mod_spec(T, tm, c):
    if tm <= T:
        per = T // tm
        return pl.BlockSpec((1, 1, c), lambda i, *_: (i // per, 0, 0))
    return pl.BlockSpec((tm // T, 1, c), lambda i, *_: (i, 0, 0))


def _norm_mod(x, g, sh, sc):
    y = x * lax.rsqrt(jnp.mean(x * x, axis=-1, keepdims=True) + RMS_EPS) * g
    return y * (1.0 + sc) + sh


def _params(sem):
    return pltpu.CompilerParams(dimension_semantics=sem, vmem_limit_bytes=VMEM_LIMIT)


def _ada_kernel(c_ref, w_ref, b_ref, o_ref):
    o_ref[0] = _mm(c_ref[...], w_ref[0]) + b_ref[0]


def _ada(c_all, w_bf, b):
    nb = c_all.shape[0]
    tn = 1536
    return pl.pallas_call(
        _ada_kernel,
        grid=(DEPTH, 6 * D_MODEL // tn),
        in_specs=[pl.BlockSpec((nb, D_MODEL), lambda l, j: (0, 0)),
                  pl.BlockSpec((1, D_MODEL, tn), lambda l, j: (l, 0, j)),
                  pl.BlockSpec((1, 1, tn), lambda l, j: (l, 0, j))],
        out_specs=pl.BlockSpec((1, nb, tn), lambda l, j: (l, 0, j)),
        out_shape=jax.ShapeDtypeStruct((DEPTH, nb, 6 * D_MODEL), F32),
        compiler_params=_params(("parallel", "parallel")),
        name="ada",
    )(c_all, w_bf, b.reshape(DEPTH, 1, 6 * D_MODEL))


def _inproj_kernel(x_ref, g_ref, sh_ref, sc_ref, w_ref, o_ref, h_scr):
    tm = x_ref.shape[0]

    @pl.when(pl.program_id(1) == 0)
    def _():
        h = _norm_mod(x_ref[...], g_ref[...], _rows(sh_ref[...], tm), _rows(sc_ref[...], tm))
        h_scr[...] = h.astype(BF)

    o_ref[...] = jnp.dot(h_scr[...], w_ref[...], preferred_element_type=F32)


def _in_proj(x2, g, sh, sc, w_bf, T, tn):
    n, d = x2.shape
    cols = w_bf.shape[1]
    tm = ROW_TILE
    return pl.pallas_call(
        _inproj_kernel,
        grid=(n // tm, cols // tn),
        in_specs=[pl.BlockSpec((tm, d), lambda i, j: (i, 0)),
                  pl.BlockSpec((1, d), lambda i, j: (0, 0)),
                  _mod_spec(T, tm, d), _mod_spec(T, tm, d),
                  pl.BlockSpec((d, tn), lambda i, j: (0, j))],
        out_specs=pl.BlockSpec((tm, tn), lambda i, j: (i, j)),
        out_shape=jax.ShapeDtypeStruct((n, cols), F32),
        scratch_shapes=[pltpu.VMEM((tm, d), BF)],
        compiler_params=_params(("parallel", "arbitrary")),
        name="in_proj",
    )(x2, g.reshape(1, d), sh, sc, w_bf)


def _seg_sum(x, ones_bd):
    hi = x.astype(BF)
    lo = (x - hi.astype(F32)).astype(BF)
    return (jnp.dot(hi, ones_bd, preferred_element_type=F32)
            + jnp.dot(lo, ones_bd, preferred_element_type=F32))


def _outproj_kernel(*refs, pre_first, eps, has_add, sigmoid_mul):
    if has_add:
        x_ref, a_ref, pre_ref, mul_ref, add_ref, gain_ref, bias_ref, ones_ref, w_ref, gt_ref, o_ref = refs
    else:
        x_ref, a_ref, pre_ref, mul_ref, gain_ref, bias_ref, ones_ref, w_ref, gt_ref, o_ref = refs
    tm = x_ref.shape[0]
    pre = pre_ref[...]
    ones_bd = ones_ref[...]
    dev = pre - _seg_sum(pre, ones_bd) * (1.0 / DH)
    var = _seg_sum(dev * dev, ones_bd) * (1.0 / DH)
    y = dev * lax.rsqrt(var + eps) * gain_ref[...] + bias_ref[...]
    if has_add:
        y = y + add_ref[...]
    m = mul_ref[...]
    y = (y * (_sigmoid(m) if sigmoid_mul else m)).astype(BF)
    first, second = (y, a_ref[...]) if pre_first else (a_ref[...], y)
    mix = (jnp.dot(first, w_ref[0:W, :], preferred_element_type=F32)
           + jnp.dot(second, w_ref[W:2 * W, :], preferred_element_type=F32))
    o_ref[...] = x_ref[...] + _rows(gt_ref[...], tm) * mix


def _out_proj(x2, a2, pre2, mul2, mul_col, add2, gain, bias, ones_bd, w_bf, gt, T, pre_first, eps, sigmoid_mul):
    n, d = x2.shape
    tm = ROW_TILE
    has_add = add2 is not None
    row = lambda c=0: pl.BlockSpec((tm, W), lambda i, c=c: (i, c))
    vec = lambda: pl.BlockSpec((1, W), lambda i: (0, 0))
    in_specs = [pl.BlockSpec((tm, d), lambda i: (i, 0)), row(), row(), row(mul_col)]
    args = [x2, a2, pre2, mul2]
    if has_add:
        in_specs.append(row())
        args.append(add2)
    in_specs += [vec(), vec(), pl.BlockSpec((W, W), lambda i: (0, 0)), pl.BlockSpec((2 * W, d), lambda i: (0, 0)),
                 _mod_spec(T, tm, d)]
    args += [gain.reshape(1, W), bias.reshape(1, W), ones_bd, w_bf, gt]
    return pl.pallas_call(
        functools.partial(_outproj_kernel, pre_first=pre_first, eps=eps, has_add=has_add, sigmoid_mul=sigmoid_mul),
        grid=(n // tm,),
        in_specs=in_specs,
        out_specs=pl.BlockSpec((tm, d), lambda i: (i, 0)),
        out_shape=jax.ShapeDtypeStruct((n, d), F32),
        compiler_params=_params(("parallel",)),
        name="out_proj",
    )(*args)


def _route(logits):
    lane = lax.broadcasted_iota(jnp.int32, logits.shape, 1).astype(F32)
    neg = -jnp.inf
    is_g = lane < N_GROUPS
    lg = jnp.where(is_g, logits, neg)
    mg = jnp.max(lg, axis=1, keepdims=True)
    gsel = jnp.min(jnp.where(lg == mg, lane, 128.0), axis=1, keepdims=True)
    psum = jnp.sum(jnp.where(is_g, jnp.exp(lg - mg), 0.0), axis=1, keepdims=True)
    pg_sel = 1.0 / psum
    lo = N_GROUPS + E_PER_GROUP * gsel
    le = jnp.where((lane >= lo) & (lane < lo + E_PER_GROUP), logits, neg)
    v1 = jnp.max(le, axis=1, keepdims=True)
    i1 = jnp.min(jnp.where(le == v1, lane, 128.0), axis=1, keepdims=True)
    le2 = jnp.where(lane == i1, neg, le)
    v2 = jnp.max(le2, axis=1, keepdims=True)
    i2 = jnp.min(jnp.where(le2 == v2, lane, 128.0), axis=1, keepdims=True)
    e2 = jnp.exp(v2 - v1)
    p1 = 1.0 / (1.0 + e2)
    p2 = e2 / (1.0 + e2)
    return pg_sel * jnp.where(lane == i1, p1, jnp.where(lane == i2, p2, 0.0))


def _moe_kernel(x_ref, g_ref, sh_ref, sc_ref, gt_ref, wrh_ref, wrl_ref, br_ref, w1_ref, w3_ref, w2_ref,
                gf_ref, o_ref, h_scr, gate_scr, acc_scr, *, final_norm):
    tm = x_ref.shape[0]
    e = pl.program_id(1)

    @pl.when(e == 0)
    def _():
        h = _norm_mod(x_ref[...], g_ref[...], _rows(sh_ref[...], tm), _rows(sc_ref[...], tm))
        hb = h.astype(BF)
        h_scr[...] = hb
        hl = (h - hb.astype(F32)).astype(BF)
        logits = (jnp.dot(hb, wrh_ref[...], preferred_element_type=F32)
                  + jnp.dot(hl, wrh_ref[...], preferred_element_type=F32)
                  + jnp.dot(hb, wrl_ref[...], preferred_element_type=F32)) + br_ref[...]
        gate_scr[...] = _route(logits)
        acc_scr[...] = jnp.zeros_like(acc_scr)

    hb = h_scr[...]
    hg = jnp.dot(hb, w1_ref[0], preferred_element_type=F32)
    hu = jnp.dot(hb, w3_ref[0], preferred_element_type=F32)
    lane = lax.broadcasted_iota(jnp.int32, gate_scr.shape, 1)
    ge = jnp.sum(jnp.where(lane == e + N_GROUPS, gate_scr[...], 0.0), axis=1, keepdims=True)
    hh = hg * _sigmoid(hg) * hu * ge
    acc_scr[...] += jnp.dot(hh.astype(BF), w2_ref[0], preferred_element_type=F32)

    @pl.when(e == N_EXPERTS - 1)
    def _():
        y = x_ref[...] + _rows(gt_ref[...], tm) * acc_scr[...]
        if final_norm:
            y = y * lax.rsqrt(jnp.mean(y * y, axis=-1, keepdims=True) + RMS_EPS) * gf_ref[...]
        o_ref[...] = y


def _moe(x2, g, sh, sc, gt, wr_hi, wr_lo, br, w1, w3, w2, g_final, T, final_norm):
    n, d = x2.shape
    tm = ROW_TILE
    return pl.pallas_call(
        functools.partial(_moe_kernel, final_norm=final_norm),
        grid=(n // tm, N_EXPERTS),
        in_specs=[pl.BlockSpec((tm, d), lambda i, e: (i, 0)),
                  pl.BlockSpec((1, d), lambda i, e: (0, 0)),
                  _mod_spec(T, tm, d), _mod_spec(T, tm, d), _mod_spec(T, tm, d),
                  pl.BlockSpec((d, 128), lambda i, e: (0, 0)),
                  pl.BlockSpec((d, 128), lambda i, e: (0, 0)),
                  pl.BlockSpec((1, 128), lambda i, e: (0, 0)),
                  pl.BlockSpec((1, d, D_EXPERT), lambda i, e: (e, 0, 0)),
                  pl.BlockSpec((1, d, D_EXPERT), lambda i, e: (e, 0, 0)),
                  pl.BlockSpec((1, D_EXPERT, d), lambda i, e: (e, 0, 0)),
                  pl.BlockSpec((1, d), lambda i, e: (0, 0))],
        out_specs=pl.BlockSpec((tm, d), lambda i, e: (i, 0)),
        out_shape=jax.ShapeDtypeStruct((n, d), F32),
        scratch_shapes=[pltpu.VMEM((tm, d), BF), pltpu.VMEM((tm, 128), F32), pltpu.VMEM((tm, d), F32)],
        compiler_params=_params(("parallel", "arbitrary")),
        name="moe",
    )(x2, g.reshape(1, d), sh, sc, gt, wr_hi, wr_lo, br, w1, w3, w2, g_final.reshape(1, d))


def _log_sigmoid(x):
    return jnp.minimum(x, 0.0) - jnp.log(1.0 + jnp.exp(-jnp.abs(x)))


def _mlstm_kernel(q_ref, k_ref, v_ref, g_ref, gt_ref, bif_ref, bift_ref, c0_ref, n0_ref, m0_ref,
                  h_ref, c1_ref, n1_ref, m1_ref, st_scr, m_scr, *, L, nb, nc):
    c = pl.program_id(1)
    NP = H // 2
    r64 = lax.broadcasted_iota(jnp.int32, (DH, DH), 0)
    c64 = lax.broadcasted_iota(jnp.int32, (DH, DH), 1)
    eye64 = r64 == c64

    @pl.when(c == 0)
    def _():
        st_scr[...] = jnp.zeros_like(st_scr)
        for ib in range(nb):
            for h in range(H):
                p, o = divmod(h, 2)
                rs = slice(o * DH, (o + 1) * DH)
                st_scr[ib, p, rs, o * DH:(o + 1) * DH] = c0_ref[ib, h]
                n_col = jnp.sum(jnp.where(eye64, n0_ref[ib, h:h + 1, :], 0.0), axis=1, keepdims=True)
                st_scr[ib, p, rs, 2 * DH + o * DH:2 * DH + (o + 1) * DH] = jnp.broadcast_to(n_col, (DH, DH))
        m_scr[...] = m0_ref[...]

    row = lax.broadcasted_iota(jnp.int32, (L, L), 0)
    col = lax.broadcasted_iota(jnp.int32, (L, L), 1)
    causal = row >= col
    tri = jnp.where(causal, 1.0, 0.0).astype(BF)
    tri_u = jnp.where(row <= col, 1.0, 0.0).astype(BF)
    even = lax.broadcasted_iota(jnp.int32, (L, 2 * DH), 1) < DH
    row_s = lax.broadcasted_iota(jnp.int32, (2 * DH, 4 * DH), 0)
    lane_s = lax.broadcasted_iota(jnp.int32, (2 * DH, 4 * DH), 1)
    top = row_s < DH
    same_head = jnp.where(top, 0, DH) == jnp.bitwise_and(lane_s, DH)
    ones_l = jnp.ones((L, 2 * DH), BF)
    ibs = range(nb)
    g = [g_ref[ib] + bif_ref[...] for ib in ibs]
    gt = [gt_ref[ib] + bift_ref[...] for ib in ibs]
    bcum = [_mm_mask_l(tri, _log_sigmoid(g[ib])) for ib in ibs]
    bcum_t = [_mm_mask_r(_log_sigmoid(gt[ib]), tri_u) for ib in ibs]
    m_prev = [m_scr[ib] for ib in ibs]
    ch = [(ib, h) for ib in ibs for h in range(H)]
    prs = [(ib, p) for ib in ibs for p in range(NP)]
    pair_of = lambda u: (u[0], u[1] // 2)
    wide = lambda x: jnp.broadcast_to(x, (L, 2 * DH))
    lanes = {u: slice(u[1] * 2 * DH, (u[1] + 1) * 2 * DH) for u in prs}
    q_f = {u: q_ref[u[0], :, lanes[u]] for u in prs}
    q_b = {u: q_f[u].astype(BF) for u in prs}
    k_f = {u: k_ref[u[0], :, lanes[u]] * (DH ** -0.5) for u in prs}
    k_b = {u: k_f[u].astype(BF) for u in prs}
    rhs = {u: jnp.concatenate([v_ref[u[0], :, lanes[u]].astype(BF), ones_l], axis=1) for u in prs}
    odd = lax.broadcasted_iota(jnp.int32, (L, 2 * DH), 1) >= DH
    qk = {u: _mm_nt(jnp.where(odd if u[1] % 2 else even, q_f[pair_of(u)], 0.0).astype(BF), k_b[pair_of(u)])
          for u in ch}
    st = {u: st_scr[u[0], u[1]] for u in prs}
    qst = {u: _mm(q_b[u], st[u]) for u in prs}
    bc = {u: wide(bcum[u[0]][:, H + u[1]:H + u[1] + 1]) for u in ch}
    ic = {u: wide(g[u[0]][:, u[1]:u[1] + 1]) for u in ch}
    mp = {u: m_prev[u[0]][:, u[1]:u[1] + 1] for u in ch}
    dmat = {u: jnp.where(causal, bc[u][:, 0:L] + (gt[u[0]][u[1]:u[1] + 1, :] - bcum_t[u[0]][H + u[1]:H + u[1] + 1, :]),
                         -jnp.inf) for u in ch}
    g_inter = {u: bc[u] + mp[u] for u in ch}
    m_t = {u: jnp.maximum(g_inter[u], jnp.max(dmat[u], axis=1, keepdims=True)) for u in ch}
    s = {u: (qk[u] * jnp.exp(dmat[u] - m_t[u][:, 0:L])).astype(BF) for u in ch}
    w_inter = {u: jnp.exp(g_inter[u] - m_t[u]) for u in ch}
    sv = {u: _mm(s[u], rhs[pair_of(u)]) for u in ch}
    m_new = {u: m_t[u][L - 1:L, 0:1] for u in ch}
    b_last = {u: bcum[u[0]][L - 1:L, H + u[1]:H + u[1] + 1] for u in ch}
    w_s = {u: jnp.exp(b_last[u] - bc[u] + ic[u] - m_new[u]) for u in ch}
    decay = {u: jnp.exp(b_last[u] + mp[u] - m_new[u]) for u in ch}
    pick = lambda d, u: jnp.where(even, d[u[0], 2 * u[1]], d[u[0], 2 * u[1] + 1])
    hh = {}
    for u in prs:
        e, o = (u[0], 2 * u[1]), (u[0], 2 * u[1] + 1)
        w_pair = pick(w_inter, u)
        num = jnp.where(even, sv[e][:, 0:2 * DH], sv[o][:, 0:2 * DH]) + w_pair * qst[u][:, 0:2 * DH]
        den = jnp.where(even, sv[e][:, 2 * DH:], sv[o][:, 2 * DH:]) + w_pair * qst[u][:, 2 * DH:]
        hh[u] = num / jnp.maximum(jnp.abs(den), jnp.exp(-pick(m_t, u)))
        inc = _mm_tn((k_f[u] * pick(w_s, u)).astype(BF), rhs[u])
        st_scr[u[0], u[1]] = jnp.where(top, decay[e], decay[o]) * st[u] + jnp.where(same_head, inc, 0.0)
    for ib in ibs:
        h_ref[ib] = jnp.concatenate([hh[ib, p] for p in range(NP)], axis=1)
        m_scr[ib] = jnp.concatenate([m_new[ib, h] for h in range(H)], axis=1)

    @pl.when(c == nc - 1)
    def _():
        for ib in range(nb):
            for h in range(H):
                p, o = divmod(h, 2)
                rs = slice(o * DH, (o + 1) * DH)
                c1_ref[ib, h] = st_scr[ib, p, rs, o * DH:(o + 1) * DH]
                n_rep = st_scr[ib, p, rs, 2 * DH + o * DH:2 * DH + (o + 1) * DH]
                n1_ref[ib, h:h + 1, :] = jnp.sum(jnp.where(eye64, n_rep, 0.0), axis=0, keepdims=True)
        m1_ref[...] = m_scr[...]


def _mlstm(p3, gif_t, bif, c0, n0, m0, L, nb):
    b, t, _ = p3.shape
    nc = t // L
    col = lambda j: pl.BlockSpec((nb, L, W), lambda i, c, j=j: (i, c, j))
    bif_pad = jnp.zeros((1, 128), F32).at[0, :2 * H].set(bif)
    return pl.pallas_call(
        functools.partial(_mlstm_kernel, L=L, nb=nb, nc=nc),
        grid=(b // nb, nc),
        in_specs=[col(0), col(1), col(2),
                  pl.BlockSpec((nb, L, 128), lambda i, c: (i, c, 6 * W // 128)),
                  pl.BlockSpec((nb, 2 * H, L), lambda i, c: (i, 0, c)),
                  pl.BlockSpec((1, 128), lambda i, c: (0, 0)),
                  pl.BlockSpec((2 * H, 1), lambda i, c: (0, 0)),
                  pl.BlockSpec((nb, H, DH, DH), lambda i, c: (i, 0, 0, 0)),
                  pl.BlockSpec((nb, H, DH), lambda i, c: (i, 0, 0)),
                  pl.BlockSpec((nb, 1, H), lambda i, c: (i, 0, 0))],
        out_specs=[pl.BlockSpec((nb, L, W), lambda i, c: (i, c, 0)),
                   pl.BlockSpec((nb, H, DH, DH), lambda i, c: (i, 0, 0, 0)),
                   pl.BlockSpec((nb, H, DH), lambda i, c: (i, 0, 0)),
                   pl.BlockSpec((nb, 1, H), lambda i, c: (i, 0, 0))],
        out_shape=[jax.ShapeDtypeStruct((b, t, W), F32),
                   jax.ShapeDtypeStruct((b, H, DH, DH), F32),
                   jax.ShapeDtypeStruct((b, H, DH), F32),
                   jax.ShapeDtypeStruct((b, 1, H), F32)],
        scratch_shapes=[pltpu.VMEM((nb, H // 2, 2 * DH, 4 * DH), F32), pltpu.VMEM((nb, 1, H), F32)],
        compiler_params=_params(("parallel", "arbitrary")),
        name="mlstm",
    )(p3, p3, p3, p3, gif_t, bif_pad, bif.reshape(2 * H, 1), c0, n0, m0.reshape(b, 1, H))


def _gelu_tanh(x):
    return 0.5 * x * (1.0 + jnp.tanh(0.7978845608028654 * (x + 0.044715 * x * x * x)))


def _lru_kernel(xr_ref, xg_ref, cache_ref, h0_ref, wc_ref, bc_ref, wg_ref, bg_ref, lam_ref,
                o_ref, h1_ref, ext_scr, a_scr, u_scr, hs_scr, hc_scr, *, Tc, nc):
    c = pl.program_id(1)
    K1 = LRU_CONV - 1

    @pl.when(c == 0)
    def _():
        ext_scr[8 - K1:8, :] = cache_ref[0]
        hc_scr[...] = h0_ref[0]

    x = xr_ref[0]
    ext_scr[8:8 + Tc, :] = x
    xc = bc_ref[...] + wc_ref[K1:K1 + 1, :] * x
    for d in range(1, LRU_CONV):
        xc = xc + wc_ref[K1 - d:K1 - d + 1, :] * ext_scr[8 - d:8 - d + Tc, :]
    ext_scr[8 - K1:8, :] = ext_scr[8 + Tc - K1:8 + Tc, :]
    gates = _mm(xc, wg_ref[...]) + bg_ref[...]
    r = _sigmoid(gates[:, 0:W])
    ig = _sigmoid(gates[:, W:2 * W])
    lam = lam_ref[...]
    softplus_neg = jnp.maximum(-lam, 0.0) + jnp.log(1.0 + jnp.exp(-jnp.abs(lam)))
    log_a = -LRU_C * r * softplus_neg
    a_scr[...] = jnp.exp(log_a)
    th = jnp.tanh(log_a)
    one_minus_a2 = -2.0 * th / (1.0 - th)
    u_scr[...] = jnp.sqrt(one_minus_a2) * (ig * xc)

    def body(t, h):
        h = a_scr[pl.ds(t, 1), :] * h + u_scr[pl.ds(t, 1), :]
        hs_scr[pl.ds(t, 1), :] = h
        return h

    h_fin = lax.fori_loop(0, Tc, body, hc_scr[...], unroll=8)
    hc_scr[...] = h_fin
    o_ref[0] = (hs_scr[...] * _gelu_tanh(xg_ref[0])).astype(BF)

    @pl.when(c == nc - 1)
    def _():
        h1_ref[0] = h_fin


def _lru(p3, cache, h0, w_conv, b_conv, wg_bf, bg, lam, Tc):
    b, t, _ = p3.shape
    nc = t // Tc
    return pl.pallas_call(
        functools.partial(_lru_kernel, Tc=Tc, nc=nc),
        grid=(b, nc),
        in_specs=[pl.BlockSpec((1, Tc, W), lambda i, c: (i, c, 4)),
                  pl.BlockSpec((1, Tc, W), lambda i, c: (i, c, 5)),
                  pl.BlockSpec((1, LRU_CONV - 1, W), lambda i, c: (i, 0, 0)),
                  pl.BlockSpec((1, 1, W), lambda i, c: (i, 0, 0)),
                  pl.BlockSpec((LRU_CONV, W), lambda i, c: (0, 0)),
                  pl.BlockSpec((1, W), lambda i, c: (0, 0)),
                  pl.BlockSpec((W, 2 * W), lambda i, c: (0, 0)),
                  pl.BlockSpec((1, 2 * W), lambda i, c: (0, 0)),
                  pl.BlockSpec((1, W), lambda i, c: (0, 0))],
        out_specs=[pl.BlockSpec((1, Tc, W), lambda i, c: (i, c, 0)),
                   pl.BlockSpec((1, 1, W), lambda i, c: (i, 0, 0))],
        out_shape=[jax.ShapeDtypeStruct((b, t, W), BF), jax.ShapeDtypeStruct((b, 1, W), F32)],
        scratch_shapes=[pltpu.VMEM((8 + Tc, W), F32), pltpu.VMEM((Tc, W), F32), pltpu.VMEM((Tc, W), F32),
                        pltpu.VMEM((Tc, W), F32), pltpu.VMEM((1, W), F32)],
        compiler_params=_params(("parallel", "arbitrary")),
        name="lru",
    )(p3, p3, cache, h0.reshape(b, 1, W), w_conv, b_conv.reshape(1, W), wg_bf, bg.reshape(1, 2 * W),
      lam.reshape(1, W))


CONF_ROWS = 32


def _conf_kernel(u_ref, gte_ref, cache_ref, bu_ref, bg_ref, wdw_ref, bdw_ref, gln_ref, bln_ref,
                 o_ref, cache1_ref, ext_scr, *, Tc, nc):
    c = pl.program_id(1)
    K1 = CONV_C - 1
    base = 32 - K1

    @pl.when(c == 0)
    def _():
        ext_scr[base:32, :] = cache_ref[0]

    u = u_ref[0] + bu_ref[...]
    gte = gte_ref[0] + bg_ref[...]
    ext_scr[32:32 + Tc, :] = u * _sigmoid(gte)
    rb = min(CONF_ROWS, Tc)
    for r0 in range(0, Tc, rb):
        acc = bdw_ref[...] + wdw_ref[0:1, :] * ext_scr[base + r0:base + r0 + rb, :]
        for j in range(1, CONV_C):
            acc = acc + wdw_ref[j:j + 1, :] * ext_scr[base + r0 + j:base + r0 + j + rb, :]
        mu = jnp.mean(acc, axis=1, keepdims=True)
        var = jnp.mean(jnp.square(acc - mu), axis=1, keepdims=True)
        y = (acc - mu) * lax.rsqrt(var + LN_EPS) * gln_ref[...] + bln_ref[...]
        o_ref[0, r0:r0 + rb, :] = (y * _sigmoid(y)).astype(BF)
    tail = ext_scr[base + Tc:32 + Tc, :]
    ext_scr[base:32, :] = tail

    @pl.when(c == nc - 1)
    def _():
        cache1_ref[0] = tail


def _conf(p3, cache, b_glu, w_dw, b_dw, g_ln, b_ln, Tc):
    b, t, _ = p3.shape
    nc = t // Tc
    vec = lambda: pl.BlockSpec((1, W), lambda i, c: (0, 0))
    return pl.pallas_call(
        functools.partial(_conf_kernel, Tc=Tc, nc=nc),
        grid=(b, nc),
        in_specs=[pl.BlockSpec((1, Tc, W), lambda i, c: (i, c, 0)),
                  pl.BlockSpec((1, Tc, W), lambda i, c: (i, c, 1)),
                  pl.BlockSpec((1, CONV_C - 1, W), lambda i, c: (i, 0, 0)),
                  vec(), vec(),
                  pl.BlockSpec((CONV_C, W), lambda i, c: (0, 0)),
                  vec(), vec(), vec()],
        out_specs=[pl.BlockSpec((1, Tc, W), lambda i, c: (i, c, 0)),
                   pl.BlockSpec((1, CONV_C - 1, W), lambda i, c: (i, 0, 0))],
        out_shape=[jax.ShapeDtypeStruct((b, t, W), BF), jax.ShapeDtypeStruct((b, CONV_C - 1, W), F32)],
        scratch_shapes=[pltpu.VMEM((32 + Tc, W), F32)],
        compiler_params=_params(("parallel", "arbitrary")),
        name="conformer",
    )(p3, p3, cache, b_glu[:W].reshape(1, W), b_glu[W:].reshape(1, W), w_dw, b_dw.reshape(1, W),
      g_ln.reshape(1, W), b_ln.reshape(1, W))


def _rprep_kernel(pr_ref, pk_ref, pv_ref, pz_ref, sr_ref, sk_ref, sv_ref, sz_ref,
                  mr_ref, mk_ref, mv_ref, mz_ref, w0_ref, a0_ref, wba_ref, gb_ref, kkw_ref, kaw_ref, rk_ref,
                  ones_ref, r_ref, lw_ref, k_ref, v_ref, kk_ref, b_ref, bonus_ref, g_ref,
                  qr_scr, qk_scr, qv_scr, qz_scr, *, Tc):
    c = pl.program_id(1)

    @pl.when(c == 0)
    def _():
        qr_scr[...] = sr_ref[0]
        qk_scr[...] = sk_ref[0]
        qv_scr[...] = sv_ref[0]
        qz_scr[...] = sz_ref[0]

    def shift_mix(x_ref, prev_scr, mu_ref):
        x = x_ref[0]
        first = lax.broadcasted_iota(jnp.int32, x.shape, 0) == 0
        prev = jnp.where(first, prev_scr[...], pltpu.roll(x, 1, 0))
        prev_scr[...] = x[Tc - 1:Tc, :]
        return x + (prev - x) * mu_ref[...]

    r = shift_mix(pr_ref, qr_scr, mr_ref)
    k = shift_mix(pk_ref, qk_scr, mk_ref)
    v = shift_mix(pv_ref, qv_scr, mv_ref)
    z = shift_mix(pz_ref, qz_scr, mz_ref)
    zwa = z[:, 0:128]
    lane = lax.broadcasted_iota(jnp.int32, zwa.shape, 1)
    wa = _mm(jnp.where(lane < LORA_W, jnp.tanh(zwa), zwa), wba_ref[...])
    lw = -RWKV_DECAY * _sigmoid(w0_ref[...] + wa[:, 0:W])
    a = _sigmoid(a0_ref[...] + wa[:, W:2 * W])
    g = _mm(_sigmoid(z[:, 128:256]), gb_ref[...])
    ones_bd = ones_ref[...]
    kk = k * kkw_ref[...]
    kk = kk * lax.rsqrt(_seg_sum(kk * kk, ones_bd) + 1e-12)
    k2 = k * (1.0 + (a - 1.0) * kaw_ref[...])
    r_ref[0] = r
    lw_ref[0] = lw
    k_ref[0] = k2
    v_ref[0] = v
    kk_ref[0] = kk
    b_ref[0] = kk * a
    bonus_ref[0] = _seg_sum(r * k2 * rk_ref[...], ones_bd) * v
    g_ref[0] = g


def _rprep(p3, shift, mu, w0, a0, wba_bf, gb_bf, kkw, kaw, rk, ones_bd, Tc):
    b, t, _ = p3.shape
    nc = t // Tc
    vec = lambda: pl.BlockSpec((1, W), lambda i, c: (0, 0))
    st = lambda width: pl.BlockSpec((1, 1, width), lambda i, c: (i, 0, 0))
    out = pl.BlockSpec((1, Tc, W), lambda i, c: (i, c, 0))
    sh3 = shift.reshape(b, 1, -1)
    mu2 = mu.reshape(1, -1)
    return pl.pallas_call(
        functools.partial(_rprep_kernel, Tc=Tc),
        grid=(b, nc),
        in_specs=[pl.BlockSpec((1, Tc, W), lambda i, c: (i, c, 2)),
                  pl.BlockSpec((1, Tc, W), lambda i, c: (i, c, 3)),
                  pl.BlockSpec((1, Tc, W), lambda i, c: (i, c, 4)),
                  pl.BlockSpec((1, Tc, 256), lambda i, c: (i, c, 10)),
                  st(W), st(W), st(W), st(256),
                  vec(), vec(), vec(), pl.BlockSpec((1, 256), lambda i, c: (0, 0)),
                  vec(), vec(),
                  pl.BlockSpec((128, 2 * W), lambda i, c: (0, 0)),
                  pl.BlockSpec((LORA_G, W), lambda i, c: (0, 0)),
                  vec(), vec(), vec(),
                  pl.BlockSpec((W, W), lambda i, c: (0, 0))],
        out_specs=[out] * 8,
        out_shape=[jax.ShapeDtypeStruct((b, t, W), F32)] * 8,
        scratch_shapes=[pltpu.VMEM((1, W), F32), pltpu.VMEM((1, W), F32), pltpu.VMEM((1, W), F32),
                        pltpu.VMEM((1, 256), F32)],
        compiler_params=_params(("parallel", "arbitrary")),
        name="rwkv_prep",
    )(p3, p3, p3, p3,
      sh3[:, :, 0:W], sh3[:, :, W:2 * W], sh3[:, :, 2 * W:3 * W], sh3[:, :, 3 * W:],
      mu2[:, 0:W], mu2[:, W:2 * W], mu2[:, 2 * W:3 * W], mu2[:, 3 * W:],
      w0.reshape(1, W), a0.reshape(1, W), wba_bf, gb_bf, kkw.reshape(1, W), kaw.reshape(1, W),
      rk.reshape(1, W), ones_bd)


def _rscan_kernel(r_ref, lw_ref, k_ref, v_ref, kk_ref, b_ref, s0_ref, y_ref, s1_ref, s_scr, *, L, nb, nc):
    c = pl.program_id(1)

    @pl.when(c == 0)
    def _():
        s_scr[...] = s0_ref[...]

    row = lax.broadcasted_iota(jnp.int32, (L, L), 0)
    col = lax.broadcasted_iota(jnp.int32, (L, L), 1)
    eye = jnp.where(row == col, 1.0, 0.0)
    tri = jnp.where(row >= col, 1.0, 0.0).astype(BF)
    row2 = lax.broadcasted_iota(jnp.int32, (2 * L, 2 * L), 0)
    col2 = lax.broadcasted_iota(jnp.int32, (2 * L, 2 * L), 1)
    cc = jnp.where(col2 >= L, col2 - L, col2)
    keep = jnp.where(row2 < L, row2 - 1, row2 - L) >= cc
    sls = [slice(h * DH, (h + 1) * DH) for h in range(H)]
    chains = [(ib, h) for ib in range(nb) for h in range(H)]
    pre = {}
    for ib in range(nb):
        lw = lw_ref[ib]
        cum = _mm_mask_l(tri, lw)
        p_in = jnp.exp(cum)
        p_inv = jnp.exp(-cum)
        p_last = p_in[L - 1:L, :]
        kh = k_ref[ib] * p_inv
        bh = b_ref[ib] * p_inv
        pre[ib] = dict(
            kr=jnp.concatenate([(kk_ref[ib] * jnp.exp(cum - lw)).astype(BF), (r_ref[ib] * p_in).astype(BF)], axis=0),
            kb=jnp.concatenate([kh.astype(BF), bh.astype(BF)], axis=0),
            kbl=jnp.concatenate([(kh * p_last).astype(BF), (bh * p_last).astype(BF)], axis=0),
            v=v_ref[ib].astype(BF), p_last=p_last)
    op = lambda u, name: pre[u[0]][name][:, sls[u[1]]]
    gm = {u: jnp.where(keep, _mm_nt(op(u, "kr"), op(u, "kb")), 0.0) for u in chains}
    akr = {u: gm[u][:, 0:L].astype(BF) for u in chains}
    r_b = {u: gm[u][L:2 * L, L:2 * L].astype(BF) for u in chains}
    xs = {u: -gm[u][0:L, L:2 * L] for u in chains}
    invs = {u: eye + xs[u] for u in chains}
    if L > 2:
        xs = {u: _mm(xs[u], xs[u]) for u in chains}
    n = 4
    while n < L:
        st = {u: _mm(jnp.concatenate([invs[u], xs[u]], axis=0), xs[u]) for u in chains}
        invs = {u: invs[u] + st[u][0:L] for u in chains}
        xs = {u: st[u][L:2 * L] for u in chains}
        n *= 2
    if L > 2:
        invs = {u: invs[u] + _mm(invs[u], xs[u]) for u in chains}
    akrv = {u: _mm(akr[u], op(u, "v")) for u in chains}
    ss = {u: s_scr[u] for u in chains}
    krs = {u: _mm_nt(op(u, "kr"), ss[u]) for u in chains}
    us = {u: _mm(invs[u], krs[u][0:L] + akrv[u][0:L]) for u in chains}
    ys = {u: krs[u][L:2 * L] + akrv[u][L:2 * L] - _mm(r_b[u], us[u]) for u in chains}
    for u in chains:
        vu = jnp.concatenate([op(u, "v"), (-us[u]).astype(BF)], axis=0)
        s_scr[u] = ss[u] * pre[u[0]]["p_last"][:, sls[u[1]]] + _mm_tn(vu, op(u, "kbl"))
    for ib in range(nb):
        y_ref[ib] = jnp.concatenate([ys[ib, h] for h in range(H)], axis=1)

    @pl.when(c == nc - 1)
    def _():
        s1_ref[...] = s_scr[...]


def _rscan(r, lw, k, v, kk, bb, s0, L, nb):
    b, t, _ = r.shape
    nc = t // L
    blk = lambda: pl.BlockSpec((nb, L, W), lambda i, c: (i, c, 0))
    st = pl.BlockSpec((nb, H, DH, DH), lambda i, c: (i, 0, 0, 0))
    return pl.pallas_call(
        functools.partial(_rscan_kernel, L=L, nb=nb, nc=nc),
        grid=(b // nb, nc),
        in_specs=[blk() for _ in range(6)] + [st],
        out_specs=[blk(), st],
        out_shape=[jax.ShapeDtypeStruct((b, t, W), F32), jax.ShapeDtypeStruct((b, H, DH, DH), F32)],
        scratch_shapes=[pltpu.VMEM((nb, H, DH, DH), F32)],
        compiler_params=_params(("parallel", "arbitrary")),
        name="rwkv_scan",
    )(r, lw, k, v, kk, bb, s0)


def _block_diag(w):
    nb, bw, _ = w.shape
    return (jnp.eye(nb, dtype=w.dtype)[:, None, :, None] * w[:, :, None, :]).reshape(nb * bw, nb * bw)


def _chunk(t, target):
    return target if t % target == 0 else t


def _scan_blocking(b, t, target):
    L = _chunk(t, target)
    return L, min(b, 4 if L >= 64 else 8)


def _run_group(x, mods, st, wts):
    b, t, d = x.shape
    x2 = x.reshape(b * t, d)
    (mc, mn, mm, lh, lconv, ccb, rs, rsh) = st

    sh1, sc1, gt1, sh2, sc2, gt2 = mods[0]
    e = wts["even"]
    p = _in_proj(x2, wts["g_mix"][0], sh1, sc1, e["w_in"], t, 640)
    p3 = p.reshape(b, t, IN_EVEN_PAD)
    gif_t = jnp.transpose(p3[:, :, 6 * W:6 * W + 2 * H], (0, 2, 1))
    hm, c1, n1, m1 = _mlstm(p3, gif_t, e["b_if"], mc[0], mn[0], mm[0], *_scan_blocking(b, t, 128))
    hl, lh1 = _lru(p3, lconv[0], lh[0], e["w_conv"], e["b_conv"], e["w_gate"], e["b_gate"], e["lam"],
                   _chunk(t, 256))
    xr = p3[:, :, 4 * W:5 * W]
    conv1 = jnp.concatenate([lconv[0], xr], axis=1)[:, -(LRU_CONV - 1):]
    x2 = _out_proj(x2, hl.reshape(b * t, W), hm.reshape(b * t, W), p, 3, None, e["g_head"], jnp.zeros((W,), F32),
                   wts["ones_bd"], e["w_out"], gt1, t, True, RMS_EPS, True)
    m = wts["moe"][0]
    x2 = _moe(x2, wts["g_ffn"][0], sh2, sc2, gt2, m["wr_hi"], m["wr_lo"], m["br"], m["w1"], m["w3"], m["w2"],
              wts["g_final"], t, False)

    sh1, sc1, gt1, sh2, sc2, gt2 = mods[1]
    o = wts["odd"]
    p = _in_proj(x2, wts["g_mix"][1], sh1, sc1, o["w_in"], t, 1408)
    p3 = p.reshape(b, t, IN_ODD)
    cc, cc1 = _conf(p3, ccb[0], o["b_glu"], o["w_dw"], o["b_dw"], o["g_ln"], o["b_ln"], _chunk(t, 256))
    r, lw, k2, v, kk, bb, bonus, g = _rprep(p3, rsh[0], o["mu"], o["w0"], o["a0"], o["wba"], o["gb"],
                                            o["kkw"], o["kaw"], o["rk"], wts["ones_bd"], _chunk(t, 256))
    y, s1 = _rscan(r, lw, k2, v, kk, bb, rs[0], *_scan_blocking(b, t, 64))
    sh_out = p3[:, t - 1, 2 * W:]
    x2 = _out_proj(x2, cc.reshape(b * t, W), y.reshape(b * t, W), g.reshape(b * t, W), 0, bonus.reshape(b * t, W),
                   o["g_gn"], o["b_gn"], wts["ones_bd"], o["w_out"], gt1, t, False, RWKV_GN_EPS, False)
    m = wts["moe"][1]
    y2 = _moe(x2, wts["g_ffn"][1], sh2, sc2, gt2, m["wr_hi"], m["wr_lo"], m["br"], m["w1"], m["w3"], m["w2"],
              wts["g_final"], t, True)
    states = (c1[None], n1[None], m1.reshape(1, b, H), lh1.reshape(1, b, W), conv1[None], cc1[None], s1[None],
              sh_out[None])
    return y2.reshape(b, t, d), states


def kernel(x_prompt, x_sample, c_prompt, c_sample, state_mlstm_C, state_mlstm_n, state_mlstm_m, state_lru_h,
           cache_lru_conv, cache_conformer_conv, state_rwkv_S, cache_rwkv_shift, w_ada, b_ada, g_norm_mix,
           g_norm_ffn, w_in_even, b_mlstm_if, g_mlstm_head, w_lru_conv, b_lru_conv, w_lru_r, b_lru_r, w_lru_i,
           b_lru_i, lru_lambda, w_out_even, w_in_odd, b_glu, w_cc_dw, b_cc_dw, g_cc_ln, b_cc_ln, rwkv_mu,
           rwkv_w0, rwkv_wB, rwkv_a0, rwkv_aB, rwkv_gB, rwkv_kk, rwkv_ka, rwkv_rk, g_rwkv_gn, b_rwkv_gn,
           w_out_odd, w_router_g, b_router_g, w_router_e, b_router_e, w_exp_gate, w_exp_up, w_exp_down, g_final):
    bp, bs = x_prompt.shape[0], x_sample.shape[0]

    wi = w_in_even[0]
    gcol = 4 * W
    w_in_e = jnp.concatenate([wi[:, :gcol], wi[:, gcol + 2 * H:], wi[:, gcol:gcol + 2 * H],
                              jnp.zeros((D_MODEL, 128 - 2 * H), F32)], axis=1).astype(BF)
    even = dict(
        w_in=w_in_e, b_if=b_mlstm_if[0], g_head=g_mlstm_head[0], w_conv=w_lru_conv[0], b_conv=b_lru_conv[0],
        w_gate=jnp.concatenate([_block_diag(w_lru_r[0]), _block_diag(w_lru_i[0])], axis=1).astype(BF),
        b_gate=jnp.concatenate([b_lru_r[0], b_lru_i[0]]), lam=lru_lambda[0], w_out=w_out_even[0].astype(BF))
    zl = jnp.zeros((LORA_W, W), F32)
    odd = dict(
        w_in=w_in_odd[0].astype(BF), b_glu=b_glu[0], w_dw=w_cc_dw[0], b_dw=b_cc_dw[0], g_ln=g_cc_ln[0],
        b_ln=b_cc_ln[0], mu=rwkv_mu[0], w0=rwkv_w0[0], a0=rwkv_a0[0],
        wba=jnp.concatenate([jnp.concatenate([rwkv_wB[0], zl], axis=1),
                             jnp.concatenate([zl, rwkv_aB[0]], axis=1)], axis=0).astype(BF),
        gb=rwkv_gB[0].astype(BF), kkw=rwkv_kk[0], kaw=rwkv_ka[0], rk=rwkv_rk[0], g_gn=g_rwkv_gn[0],
        b_gn=b_rwkv_gn[0], w_out=w_out_odd[0].astype(BF))
    moe = []
    for l in range(DEPTH):
        wr = jnp.concatenate([w_router_g[l], w_router_e[l],
                              jnp.zeros((D_MODEL, 128 - N_GROUPS - N_EXPERTS), F32)], axis=1)
        wr_hi = wr.astype(BF)
        wr_lo = (wr - wr_hi.astype(F32)).astype(BF)
        br = jnp.concatenate([b_router_g[l], b_router_e[l],
                              jnp.zeros((128 - N_GROUPS - N_EXPERTS,), F32)]).reshape(1, 128)
        moe.append(dict(wr_hi=wr_hi, wr_lo=wr_lo, br=br, w1=w_exp_gate[l].astype(BF),
                        w3=w_exp_up[l].astype(BF), w2=w_exp_down[l].astype(BF)))
    ones_bd = _block_diag(jnp.ones((H, DH, DH), F32)).astype(BF)
    wts = dict(even=even, odd=odd, moe=moe, g_mix=g_norm_mix, g_ffn=g_norm_ffn, g_final=g_final, ones_bd=ones_bd)

    mod = _ada(jnp.concatenate([c_prompt, c_sample], axis=0), w_ada.astype(BF), b_ada)

    def mods_of(lo, hi):
        return [tuple(mod[l, lo:hi, j * D_MODEL:(j + 1) * D_MODEL].reshape(hi - lo, 1, D_MODEL) for j in range(6))
                for l in range(DEPTH)]

    z = lambda *s: jnp.zeros(s, F32)
    st_p = (z(1, bp, H, DH, DH), z(1, bp, H, DH), z(1, bp, H), z(1, bp, W), z(1, bp, LRU_CONV - 1, W),
            z(1, bp, CONV_C - 1, W), z(1, bp, H, DH, DH), z(1, bp, 3 * W + LORA_W + LORA_A + LORA_G))
    st_s = (state_mlstm_C, state_mlstm_n, state_mlstm_m, state_lru_h, cache_lru_conv, cache_conformer_conv,
            state_rwkv_S, cache_rwkv_shift)
    y_p, out_p = _run_group(x_prompt, mods_of(0, bp), st_p, wts)
    y_s, out_s = _run_group(x_sample, mods_of(bp, bp + bs), st_s, wts)
    return (y_p, y_s) + tuple(out_p) + tuple(out_s)
```

```python
import functools

import jax
import jax.numpy as jnp
from jax import lax
from jax.experimental import pallas as pl
from jax.experimental.pallas import tpu as pltpu

F32 = jnp.float32
BF = jnp.bfloat16

D_MODEL = 1024
DEPTH = 2
H = 8
DH = 64
W = 512
LRU_CONV = 4
LRU_C = 8.0
CONV_C = 31
LORA_W = 64
LORA_A = 64
LORA_G = 128
RWKV_DECAY = 0.606531
RWKV_GN_EPS = 64e-5
N_GROUPS = 4
E_PER_GROUP = 4
N_EXPERTS = 16
D_EXPERT = 256
RMS_EPS = 1e-6
LN_EPS = 1e-5
IN_EVEN_PAD = 6 * W + 128
IN_ODD = 2 * W + 3 * W + LORA_W + LORA_A + LORA_G

ROW_TILE = 1024
MOE_EXPERTS_PER_STEP = 4
VMEM_LIMIT = 48 * 1024 * 1024


def _mm(a, b):
    return jnp.dot(a.astype(BF), b.astype(BF), preferred_element_type=F32)


def _mm_nt(a, b):
    return lax.dot_general(a.astype(BF), b.astype(BF), (((1,), (1,)), ((), ())),
                           preferred_element_type=F32)


def _mm_tn(a, b):
    return lax.dot_general(a.astype(BF), b.astype(BF), (((0,), (0,)), ((), ())),
                           preferred_element_type=F32)


def _split3(x):
    hi = x.astype(BF)
    r = x - hi.astype(F32)
    mid = r.astype(BF)
    lo = (r - mid.astype(F32)).astype(BF)
    return hi, mid, lo


def _mm_mask_l(mask, x):
    return sum(jnp.dot(mask, p, preferred_element_type=F32) for p in _split3(x))


def _mm_mask_r(x, mask):
    return sum(jnp.dot(p, mask, preferred_element_type=F32) for p in _split3(x))


def _sigmoid(x):
    return 1.0 / (1.0 + jnp.exp(-x))


def _rows(v, tm):
    nb, _, c = v.shape
    if nb == 1:
        return v[0]
    return jnp.broadcast_to(v, (nb, tm // nb, c)).reshape(tm, c)


def _mod_spec(T, tm, c):
    if tm <= T:
        per = T // tm
        return pl.BlockSpec((1, 1, c), lambda i, *_: (i // per, 0, 0))
    return pl.BlockSpec((tm // T, 1, c), lambda i, *_: (i, 0, 0))


def _norm_mod(x, g, sh, sc):
    y = x * lax.rsqrt(jnp.mean(x * x, axis=-1, keepdims=True) + RMS_EPS) * g
    return y * (1.0 + sc) + sh


def _params(sem):
    return pltpu.CompilerParams(dimension_semantics=sem, vmem_limit_bytes=VMEM_LIMIT)


def _ada_kernel(c_ref, w_ref, b_ref, o_ref):
    o_ref[0] = _mm(c_ref[...], w_ref[0]) + b_ref[0]


def _ada(c_all, w_bf, b):
    nb = c_all.shape[0]
    tn = 1536
    return pl.pallas_call(
        _ada_kernel,
        grid=(DEPTH, 6 * D_MODEL // tn),
        in_specs=[pl.BlockSpec((nb, D_MODEL), lambda l, j: (0, 0)),
                  pl.BlockSpec((1, D_MODEL, tn), lambda l, j: (l, 0, j)),
                  pl.BlockSpec((1, 1, tn), lambda l, j: (l, 0, j))],
        out_specs=pl.BlockSpec((1, nb, tn), lambda l, j: (l, 0, j)),
        out_shape=jax.ShapeDtypeStruct((DEPTH, nb, 6 * D_MODEL), F32),
        compiler_params=_params(("parallel", "parallel")),
        name="ada",
    )(c_all, w_bf, b.reshape(DEPTH, 1, 6 * D_MODEL))


def _inproj_kernel(x_ref, g_ref, sh_ref, sc_ref, w_ref, o_ref, h_scr):
    tm = x_ref.shape[0]

    @pl.when(pl.program_id(1) == 0)
    def _():
        h = _norm_mod(x_ref[...], g_ref[...], _rows(sh_ref[...], tm), _rows(sc_ref[...], tm))
        h_scr[...] = h.astype(BF)

    o_ref[...] = jnp.dot(h_scr[...], w_ref[...], preferred_element_type=F32)


def _in_proj(x2, g, sh, sc, w_bf, T, tn):
    n, d = x2.shape
    cols = w_bf.shape[1]
    tm = ROW_TILE
    return pl.pallas_call(
        _inproj_kernel,
        grid=(n // tm, cols // tn),
        in_specs=[pl.BlockSpec((tm, d), lambda i, j: (i, 0)),
                  pl.BlockSpec((1, d), lambda i, j: (0, 0)),
                  _mod_spec(T, tm, d), _mod_spec(T, tm, d),
                  pl.BlockSpec((d, tn), lambda i, j: (0, j))],
        out_specs=pl.BlockSpec((tm, tn), lambda i, j: (i, j)),
        out_shape=jax.ShapeDtypeStruct((n, cols), F32),
        scratch_shapes=[pltpu.VMEM((tm, d), BF)],
        compiler_params=_params(("parallel", "arbitrary")),
        name="in_proj",
    )(x2, g.reshape(1, d), sh, sc, w_bf)


def _seg_sum(x, ones_bd):
    hi = x.astype(BF)
    lo = (x - hi.astype(F32)).astype(BF)
    return (jnp.dot(hi, ones_bd, preferred_element_type=F32)
            + jnp.dot(lo, ones_bd, preferred_element_type=F32))


def _outproj_kernel(*refs, pre_first, eps, has_add, sigmoid_mul):
    if has_add:
        x_ref, a_ref, pre_ref, mul_ref, add_ref, gain_ref, bias_ref, ones_ref, w_ref, gt_ref, o_ref = refs
    else:
        x_ref, a_ref, pre_ref, mul_ref, gain_ref, bias_ref, ones_ref, w_ref, gt_ref, o_ref = refs
    tm = x_ref.shape[0]
    pre = pre_ref[...]
    ones_bd = ones_ref[...]
    dev = pre - _seg_sum(pre, ones_bd) * (1.0 / DH)
    var = _seg_sum(dev * dev, ones_bd) * (1.0 / DH)
    y = dev * lax.rsqrt(var + eps) * gain_ref[...] + bias_ref[...]
    if has_add:
        y = y + add_ref[...]
    m = mul_ref[...]
    y = (y * (_sigmoid(m) if sigmoid_mul else m)).astype(BF)
    first, second = (y, a_ref[...]) if pre_first else (a_ref[...], y)
    mix = (jnp.dot(first, w_ref[0:W, :], preferred_element_type=F32)
           + jnp.dot(second, w_ref[W:2 * W, :], preferred_element_type=F32))
    o_ref[...] = x_ref[...] + _rows(gt_ref[...], tm) * mix


def _out_proj(x2, a2, pre2, mul2, mul_col, add2, gain, bias, ones_bd, w_bf, gt, T, pre_first, eps, sigmoid_mul):
    n, d = x2.shape
    tm = ROW_TILE
    has_add = add2 is not None
    row = lambda c=0: pl.BlockSpec((tm, W), lambda i, c=c: (i, c))
    vec = lambda: pl.BlockSpec((1, W), lambda i: (0, 0))
    in_specs = [pl.BlockSpec((tm, d), lambda i: (i, 0)), row(), row(), row(mul_col)]
    args = [x2, a2, pre2, mul2]
    if has_add:
        in_specs.append(row())
        args.append(add2)
    in_specs += [vec(), vec(), pl.BlockSpec((W, W), lambda i: (0, 0)), pl.BlockSpec((2 * W, d), lambda i: (0, 0)),
                 _mod_spec(T, tm, d)]
    args += [gain.reshape(1, W), bias.reshape(1, W), ones_bd, w_bf, gt]
    return pl.pallas_call(
        functools.partial(_outproj_kernel, pre_first=pre_first, eps=eps, has_add=has_add, sigmoid_mul=sigmoid_mul),
        grid=(n // tm,),
        in_specs=in_specs,
        out_specs=pl.BlockSpec((tm, d), lambda i: (i, 0)),
        out_shape=jax.ShapeDtypeStruct((n, d), F32),
        compiler_params=_params(("parallel",)),
        name="out_proj",
    )(*args)


def _route(logits):
    lane = lax.broadcasted_iota(jnp.int32, logits.shape, 1).astype(F32)
    neg = -jnp.inf
    is_g = lane < N_GROUPS
    lg = jnp.where(is_g, logits, neg)
    mg = jnp.max(lg, axis=1, keepdims=True)
    gsel = jnp.min(jnp.where(lg == mg, lane, 128.0), axis=1, keepdims=True)
    psum = jnp.sum(jnp.where(is_g, jnp.exp(lg - mg), 0.0), axis=1, keepdims=True)
    pg_sel = 1.0 / psum
    lo = N_GROUPS + E_PER_GROUP * gsel
    le = jnp.where((lane >= lo) & (lane < lo + E_PER_GROUP), logits, neg)
    v1 = jnp.max(le, axis=1, keepdims=True)
    i1 = jnp.min(jnp.where(le == v1, lane, 128.0), axis=1, keepdims=True)
    le2 = jnp.where(lane == i1, neg, le)
    v2 = jnp.max(le2, axis=1, keepdims=True)
    i2 = jnp.min(jnp.where(le2 == v2, lane, 128.0), axis=1, keepdims=True)
    e2 = jnp.exp(v2 - v1)
    p1 = 1.0 / (1.0 + e2)
    p2 = e2 / (1.0 + e2)
    return pg_sel * jnp.where(lane == i1, p1, jnp.where(lane == i2, p2, 0.0))


def _moe_kernel(x_ref, g_ref, sh_ref, sc_ref, gt_ref, wrh_ref, wrl_ref, br_ref, w1_ref, w3_ref, w2_ref,
                gf_ref, o_ref, h_scr, gate_scr, acc_scr, *, final_norm):
    tm = x_ref.shape[0]
    e = pl.program_id(1)

    @pl.when(e == 0)
    def _():
        h = _norm_mod(x_ref[...], g_ref[...], _rows(sh_ref[...], tm), _rows(sc_ref[...], tm))
        hb = h.astype(BF)
        h_scr[...] = hb
        hl = (h - hb.astype(F32)).astype(BF)
        logits = (jnp.dot(hb, wrh_ref[...], preferred_element_type=F32)
                  + jnp.dot(hl, wrh_ref[...], preferred_element_type=F32)
                  + jnp.dot(hb, wrl_ref[...], preferred_element_type=F32)) + br_ref[...]
        gate_scr[...] = _route(logits)
        acc_scr[...] = jnp.zeros_like(acc_scr)

    hb = h_scr[...]
    lane = lax.broadcasted_iota(jnp.int32, gate_scr.shape, 1)
    acc = None
    for i in range(MOE_EXPERTS_PER_STEP):
        hg = jnp.dot(hb, w1_ref[i], preferred_element_type=F32)
        hu = jnp.dot(hb, w3_ref[i], preferred_element_type=F32)
        ge = jnp.sum(jnp.where(lane == e * MOE_EXPERTS_PER_STEP + i + N_GROUPS, gate_scr[...], 0.0),
                     axis=1, keepdims=True)
        hh = hg * _sigmoid(hg) * hu * ge
        part = jnp.dot(hh.astype(BF), w2_ref[i], preferred_element_type=F32)
        acc = part if acc is None else acc + part
    acc_scr[...] += acc

    @pl.when(e == N_EXPERTS // MOE_EXPERTS_PER_STEP - 1)
    def _():
        y = x_ref[...] + _rows(gt_ref[...], tm) * acc_scr[...]
        if final_norm:
            y = y * lax.rsqrt(jnp.mean(y * y, axis=-1, keepdims=True) + RMS_EPS) * gf_ref[...]
        o_ref[...] = y


def _moe(x2, g, sh, sc, gt, wr_hi, wr_lo, br, w1, w3, w2, g_final, T, final_norm):
    n, d = x2.shape
    tm = ROW_TILE
    return pl.pallas_call(
        functools.partial(_moe_kernel, final_norm=final_norm),
        grid=(n // tm, N_EXPERTS // MOE_EXPERTS_PER_STEP),
        in_specs=[pl.BlockSpec((tm, d), lambda i, e: (i, 0)),
                  pl.BlockSpec((1, d), lambda i, e: (0, 0)),
                  _mod_spec(T, tm, d), _mod_spec(T, tm, d), _mod_spec(T, tm, d),
                  pl.BlockSpec((d, 128), lambda i, e: (0, 0)),
                  pl.BlockSpec((d, 128), lambda i, e: (0, 0)),
                  pl.BlockSpec((1, 128), lambda i, e: (0, 0)),
                  pl.BlockSpec((MOE_EXPERTS_PER_STEP, d, D_EXPERT), lambda i, e: (e, 0, 0)),
                  pl.BlockSpec((MOE_EXPERTS_PER_STEP, d, D_EXPERT), lambda i, e: (e, 0, 0)),
                  pl.BlockSpec((MOE_EXPERTS_PER_STEP, D_EXPERT, d), lambda i, e: (e, 0, 0)),
                  pl.BlockSpec((1, d), lambda i, e: (0, 0))],
        out_specs=pl.BlockSpec((tm, d), lambda i, e: (i, 0)),
        out_shape=jax.ShapeDtypeStruct((n, d), F32),
        scratch_shapes=[pltpu.VMEM((tm, d), BF), pltpu.VMEM((tm, 128), F32), pltpu.VMEM((tm, d), F32)],
        compiler_params=_params(("parallel", "arbitrary")),
        name="moe",
    )(x2, g.reshape(1, d), sh, sc, gt, wr_hi, wr_lo, br, w1, w3, w2, g_final.reshape(1, d))


def _log_sigmoid(x):
    return jnp.minimum(x, 0.0) - jnp.log(1.0 + jnp.exp(-jnp.abs(x)))


def _mlstm_kernel(q_ref, k_ref, v_ref, g_ref, gt_ref, bif_ref, bift_ref, c0_ref, n0_ref, m0_ref,
                  h_ref, c1_ref, n1_ref, m1_ref, st_scr, m_scr, *, L, nb, nc):
    c = pl.program_id(1)
    NP = H // 2
    r64 = lax.broadcasted_iota(jnp.int32, (DH, DH), 0)
    c64 = lax.broadcasted_iota(jnp.int32, (DH, DH), 1)
    eye64 = r64 == c64

    @pl.when(c == 0)
    def _():
        st_scr[...] = jnp.zeros_like(st_scr)
        for ib in range(nb):
            for h in range(H):
                p, o = divmod(h, 2)
                rs = slice(o * DH, (o + 1) * DH)
                st_scr[ib, p, rs, o * DH:(o + 1) * DH] = c0_ref[ib, h]
                n_col = jnp.sum(jnp.where(eye64, n0_ref[ib, h:h + 1, :], 0.0), axis=1, keepdims=True)
                st_scr[ib, p, rs, 2 * DH + o * DH:2 * DH + (o + 1) * DH] = jnp.broadcast_to(n_col, (DH, DH))
        m_scr[...] = m0_ref[...]

    row = lax.broadcasted_iota(jnp.int32, (L, L), 0)
    col = lax.broadcasted_iota(jnp.int32, (L, L), 1)
    causal = row >= col
    tri = jnp.where(causal, 1.0, 0.0).astype(BF)
    tri_u = jnp.where(row <= col, 1.0, 0.0).astype(BF)
    even = lax.broadcasted_iota(jnp.int32, (L, 2 * DH), 1) < DH
    row_s = lax.broadcasted_iota(jnp.int32, (2 * DH, 4 * DH), 0)
    lane_s = lax.broadcasted_iota(jnp.int32, (2 * DH, 4 * DH), 1)
    top = row_s < DH
    same_head = jnp.where(top, 0, DH) == jnp.bitwise_and(lane_s, DH)
    ones_l = jnp.ones((L, 2 * DH), BF)
    ibs = range(nb)
    g = [g_ref[ib] + bif_ref[...] for ib in ibs]
    gt = [gt_ref[ib] + bift_ref[...] for ib in ibs]
    bcum = [_mm_mask_l(tri, _log_sigmoid(g[ib])) for ib in ibs]
    bcum_t = [_mm_mask_r(_log_sigmoid(gt[ib]), tri_u) for ib in ibs]
    m_prev = [m_scr[ib] for ib in ibs]
    ch = [(ib, h) for ib in ibs for h in range(H)]
    prs = [(ib, p) for ib in ibs for p in range(NP)]
    pair_of = lambda u: (u[0], u[1] // 2)
    wide = lambda x: jnp.broadcast_to(x, (L, 2 * DH))
    lanes = {u: slice(u[1] * 2 * DH, (u[1] + 1) * 2 * DH) for u in prs}
    q_f = {u: q_ref[u[0], :, lanes[u]] for u in prs}
    q_b = {u: q_f[u].astype(BF) for u in prs}
    k_f = {u: k_ref[u[0], :, lanes[u]] * (DH ** -0.5) for u in prs}
    k_b = {u: k_f[u].astype(BF) for u in prs}
    rhs = {u: jnp.concatenate([v_ref[u[0], :, lanes[u]].astype(BF), ones_l], axis=1) for u in prs}
    odd = lax.broadcasted_iota(jnp.int32, (L, 2 * DH), 1) >= DH
    qk = {u: _mm_nt(jnp.where(odd if u[1] % 2 else even, q_f[pair_of(u)], 0.0).astype(BF), k_b[pair_of(u)])
          for u in ch}
    st = {u: st_scr[u[0], u[1]] for u in prs}
    qst = {u: _mm(q_b[u], st[u]) for u in prs}
    bc = {u: wide(bcum[u[0]][:, H + u[1]:H + u[1] + 1]) for u in ch}
    ic = {u: wide(g[u[0]][:, u[1]:u[1] + 1]) for u in ch}
    mp = {u: m_prev[u[0]][:, u[1]:u[1] + 1] for u in ch}
    dmat = {u: jnp.where(causal, bc[u][:, 0:L] + (gt[u[0]][u[1]:u[1] + 1, :] - bcum_t[u[0]][H + u[1]:H + u[1] + 1, :]),
                         -jnp.inf) for u in ch}
    g_inter = {u: bc[u] + mp[u] for u in ch}
    m_t = {u: jnp.maximum(g_inter[u], jnp.max(dmat[u], axis=1, keepdims=True)) for u in ch}
    s = {u: (qk[u] * jnp.exp(dmat[u] - m_t[u][:, 0:L])).astype(BF) for u in ch}
    w_inter = {u: jnp.exp(g_inter[u] - m_t[u]) for u in ch}
    sv = {u: _mm(s[u], rhs[pair_of(u)]) for u in ch}
    m_new = {u: m_t[u][L - 1:L, 0:1] for u in ch}
    b_last = {u: bcum[u[0]][L - 1:L, H + u[1]:H + u[1] + 1] for u in ch}
    w_s = {u: jnp.exp(b_last[u] - bc[u] + ic[u] - m_new[u]) for u in ch}
    decay = {u: jnp.exp(b_last[u] + mp[u] - m_new[u]) for u in ch}
    pick = lambda d, u: jnp.where(even, d[u[0], 2 * u[1]], d[u[0], 2 * u[1] + 1])
    hh = {}
    for u in prs:
        e, o = (u[0], 2 * u[1]), (u[0], 2 * u[1] + 1)
        w_pair = pick(w_inter, u)
        num = jnp.where(even, sv[e][:, 0:2 * DH], sv[o][:, 0:2 * DH]) + w_pair * qst[u][:, 0:2 * DH]
        den = jnp.where(even, sv[e][:, 2 * DH:], sv[o][:, 2 * DH:]) + w_pair * qst[u][:, 2 * DH:]
        hh[u] = num / jnp.maximum(jnp.abs(den), jnp.exp(-pick(m_t, u)))
        inc = _mm_tn((k_f[u] * pick(w_s, u)).astype(BF), rhs[u])
        st_scr[u[0], u[1]] = jnp.where(top, decay[e], decay[o]) * st[u] + jnp.where(same_head, inc, 0.0)
    for ib in ibs:
        h_ref[ib] = jnp.concatenate([hh[ib, p] for p in range(NP)], axis=1)
        m_scr[ib] = jnp.concatenate([m_new[ib, h] for h in range(H)], axis=1)

    @pl.when(c == nc - 1)
    def _():
        for ib in range(nb):
            for h in range(H):
                p, o = divmod(h, 2)
                rs = slice(o * DH, (o + 1) * DH)
                c1_ref[ib, h] = st_scr[ib, p, rs, o * DH:(o + 1) * DH]
                n_rep = st_scr[ib, p, rs, 2 * DH + o * DH:2 * DH + (o + 1) * DH]
                n1_ref[ib, h:h + 1, :] = jnp.sum(jnp.where(eye64, n_rep, 0.0), axis=0, keepdims=True)
        m1_ref[...] = m_scr[...]


def _mlstm(p3, gif_t, bif, c0, n0, m0, L, nb):
    b, t, _ = p3.shape
    nc = t // L
    col = lambda j: pl.BlockSpec((nb, L, W), lambda i, c, j=j: (i, c, j))
    bif_pad = jnp.zeros((1, 128), F32).at[0, :2 * H].set(bif)
    return pl.pallas_call(
        functools.partial(_mlstm_kernel, L=L, nb=nb, nc=nc),
        grid=(b // nb, nc),
        in_specs=[col(0), col(1), col(2),
                  pl.BlockSpec((nb, L, 128), lambda i, c: (i, c, 6 * W // 128)),
                  pl.BlockSpec((nb, 2 * H, L), lambda i, c: (i, 0, c)),
                  pl.BlockSpec((1, 128), lambda i, c: (0, 0)),
                  pl.BlockSpec((2 * H, 1), lambda i, c: (0, 0)),
                  pl.BlockSpec((nb, H, DH, DH), lambda i, c: (i, 0, 0, 0)),
                  pl.BlockSpec((nb, H, DH), lambda i, c: (i, 0, 0)),
                  pl.BlockSpec((nb, 1, H), lambda i, c: (i, 0, 0))],
        out_specs=[pl.BlockSpec((nb, L, W), lambda i, c: (i, c, 0)),
                   pl.BlockSpec((nb, H, DH, DH), lambda i, c: (i, 0, 0, 0)),
                   pl.BlockSpec((nb, H, DH), lambda i, c: (i, 0, 0)),
                   pl.BlockSpec((nb, 1, H), lambda i, c: (i, 0, 0))],
        out_shape=[jax.ShapeDtypeStruct((b, t, W), F32),
                   jax.ShapeDtypeStruct((b, H, DH, DH), F32),
                   jax.ShapeDtypeStruct((b, H, DH), F32),
                   jax.ShapeDtypeStruct((b, 1, H), F32)],
        scratch_shapes=[pltpu.VMEM((nb, H // 2, 2 * DH, 4 * DH), F32), pltpu.VMEM((nb, 1, H), F32)],
        compiler_params=_params(("parallel", "arbitrary")),
        name="mlstm",
    )(p3, p3, p3, p3, gif_t, bif_pad, bif.reshape(2 * H, 1), c0, n0, m0.reshape(b, 1, H))


def _gelu_tanh(x):
    return 0.5 * x * (1.0 + jnp.tanh(0.7978845608028654 * (x + 0.044715 * x * x * x)))


def _lru_kernel(xr_ref, xg_ref, cache_ref, h0_ref, wc_ref, bc_ref, wg_ref, bg_ref, lam_ref,
                o_ref, h1_ref, ext_scr, a_scr, u_scr, hs_scr, hc_scr, *, Tc, nc):
    c = pl.program_id(1)
    K1 = LRU_CONV - 1

    @pl.when(c == 0)
    def _():
        ext_scr[8 - K1:8, :] = cache_ref[0]
        hc_scr[...] = h0_ref[0]

    x = xr_ref[0]
    ext_scr[8:8 + Tc, :] = x
    xc = bc_ref[...] + wc_ref[K1:K1 + 1, :] * x
    for d in range(1, LRU_CONV):
        xc = xc + wc_ref[K1 - d:K1 - d + 1, :] * ext_scr[8 - d:8 - d + Tc, :]
    ext_scr[8 - K1:8, :] = ext_scr[8 + Tc - K1:8 + Tc, :]
    gates = _mm(xc, wg_ref[...]) + bg_ref[...]
    r = _sigmoid(gates[:, 0:W])
    ig = _sigmoid(gates[:, W:2 * W])
    lam = lam_ref[...]
    softplus_neg = jnp.maximum(-lam, 0.0) + jnp.log(1.0 + jnp.exp(-jnp.abs(lam)))
    log_a = -LRU_C * r * softplus_neg
    a_scr[...] = jnp.exp(log_a)
    th = jnp.tanh(log_a)
    one_minus_a2 = -2.0 * th / (1.0 - th)
    u_scr[...] = jnp.sqrt(one_minus_a2) * (ig * xc)

    def body(t, h):
        h = a_scr[pl.ds(t, 1), :] * h + u_scr[pl.ds(t, 1), :]
        hs_scr[pl.ds(t, 1), :] = h
        return h

    h_fin = lax.fori_loop(0, Tc, body, hc_scr[...], unroll=8)
    hc_scr[...] = h_fin
    o_ref[0] = (hs_scr[...] * _gelu_tanh(xg_ref[0])).astype(BF)

    @pl.when(c == nc - 1)
    def _():
        h1_ref[0] = h_fin


def _lru(p3, cache, h0, w_conv, b_conv, wg_bf, bg, lam, Tc):
    b, t, _ = p3.shape
    nc = t // Tc
    return pl.pallas_call(
        functools.partial(_lru_kernel, Tc=Tc, nc=nc),
        grid=(b, nc),
        in_specs=[pl.BlockSpec((1, Tc, W), lambda i, c: (i, c, 4)),
                  pl.BlockSpec((1, Tc, W), lambda i, c: (i, c, 5)),
                  pl.BlockSpec((1, LRU_CONV - 1, W), lambda i, c: (i, 0, 0)),
                  pl.BlockSpec((1, 1, W), lambda i, c: (i, 0, 0)),
                  pl.BlockSpec((LRU_CONV, W), lambda i, c: (0, 0)),
                  pl.BlockSpec((1, W), lambda i, c: (0, 0)),
                  pl.BlockSpec((W, 2 * W), lambda i, c: (0, 0)),
                  pl.BlockSpec((1, 2 * W), lambda i, c: (0, 0)),
                  pl.BlockSpec((1, W), lambda i, c: (0, 0))],
        out_specs=[pl.BlockSpec((1, Tc, W), lambda i, c: (i, c, 0)),
                   pl.BlockSpec((1, 1, W), lambda i, c: (i, 0, 0))],
        out_shape=[jax.ShapeDtypeStruct((b, t, W), BF), jax.ShapeDtypeStruct((b, 1, W), F32)],
        scratch_shapes=[pltpu.VMEM((8 + Tc, W), F32), pltpu.VMEM((Tc, W), F32), pltpu.VMEM((Tc, W), F32),
                        pltpu.VMEM((Tc, W), F32), pltpu.VMEM((1, W), F32)],
        compiler_params=_params(("parallel", "arbitrary")),
        name="lru",
    )(p3, p3, cache, h0.reshape(b, 1, W), w_conv, b_conv.reshape(1, W), wg_bf, bg.reshape(1, 2 * W),
      lam.reshape(1, W))


CONF_ROWS = 32


def _conf_kernel(u_ref, gte_ref, cache_ref, bu_ref, bg_ref, wdw_ref, bdw_ref, gln_ref, bln_ref,
                 o_ref, cache1_ref, ext_scr, sh_scr, *, Tc, nc):
    c = pl.program_id(1)
    K1 = CONV_C - 1
    base = 32 - K1

    @pl.when(c == 0)
    def _():
        ext_scr[base:32, :] = cache_ref[0]

    u = u_ref[0] + bu_ref[...]
    gte = gte_ref[0] + bg_ref[...]
    ext_scr[32:32 + Tc, :] = u * _sigmoid(gte)
    for b in range(1, 8):
        sh_scr[b - 1] = ext_scr[b:b + Tc + 24, :]

    def window(off, r0, rb):
        a, b = divmod(off, 8)
        if b == 0:
            return ext_scr[8 * a + r0:8 * a + r0 + rb, :]
        return sh_scr[b - 1, 8 * a + r0:8 * a + r0 + rb, :]

    rb = min(CONF_ROWS, Tc)
    for r0 in range(0, Tc, rb):
        acc = bdw_ref[...] + wdw_ref[0:1, :] * window(base, r0, rb)
        for j in range(1, CONV_C):
            acc = acc + wdw_ref[j:j + 1, :] * window(base + j, r0, rb)
        mu = jnp.mean(acc, axis=1, keepdims=True)
        var = jnp.mean(jnp.square(acc - mu), axis=1, keepdims=True)
        y = (acc - mu) * lax.rsqrt(var + LN_EPS) * gln_ref[...] + bln_ref[...]
        o_ref[0, r0:r0 + rb, :] = (y * _sigmoid(y)).astype(BF)
    tail = ext_scr[base + Tc:32 + Tc, :]
    ext_scr[base:32, :] = tail

    @pl.when(c == nc - 1)
    def _():
        cache1_ref[0] = tail


def _conf(p3, cache, b_glu, w_dw, b_dw, g_ln, b_ln, Tc):
    b, t, _ = p3.shape
    nc = t // Tc
    vec = lambda: pl.BlockSpec((1, W), lambda i, c: (0, 0))
    return pl.pallas_call(
        functools.partial(_conf_kernel, Tc=Tc, nc=nc),
        grid=(b, nc),
        in_specs=[pl.BlockSpec((1, Tc, W), lambda i, c: (i, c, 0)),
                  pl.BlockSpec((1, Tc, W), lambda i, c: (i, c, 1)),
                  pl.BlockSpec((1, CONV_C - 1, W), lambda i, c: (i, 0, 0)),
                  vec(), vec(),
                  pl.BlockSpec((CONV_C, W), lambda i, c: (0, 0)),
                  vec(), vec(), vec()],
        out_specs=[pl.BlockSpec((1, Tc, W), lambda i, c: (i, c, 0)),
                   pl.BlockSpec((1, CONV_C - 1, W), lambda i, c: (i, 0, 0))],
        out_shape=[jax.ShapeDtypeStruct((b, t, W), BF), jax.ShapeDtypeStruct((b, CONV_C - 1, W), F32)],
        scratch_shapes=[pltpu.VMEM((32 + Tc, W), F32), pltpu.VMEM((7, Tc + 24, W), F32)],
        compiler_params=_params(("parallel", "arbitrary")),
        name="conformer",
    )(p3, p3, cache, b_glu[:W].reshape(1, W), b_glu[W:].reshape(1, W), w_dw, b_dw.reshape(1, W),
      g_ln.reshape(1, W), b_ln.reshape(1, W))


def _rwkv_prep(x_refs, prev_scrs, mu_refs, w0_ref, a0_ref, wba_ref, gb_ref, kkw_ref, kaw_ref, rk_ref, ones_ref,
               L, nb):
    def shift_mix(x_ref, prev_scr, mu_ref):
        x = x_ref[...].reshape(nb * L, x_ref.shape[-1])
        first = jnp.bitwise_and(lax.broadcasted_iota(jnp.int32, x.shape, 0), L - 1) == 0
        carried = jnp.concatenate([jnp.broadcast_to(prev_scr[ib], (L, x.shape[1])) for ib in range(nb)], axis=0)
        prev = jnp.where(first, carried, pltpu.roll(x, 1, 0))
        for ib in range(nb):
            prev_scr[ib] = x[(ib + 1) * L - 1:(ib + 1) * L, :]
        return x + (prev - x) * mu_ref[...]

    r, k, v, z = (shift_mix(x, p, m) for x, p, m in zip(x_refs, prev_scrs, mu_refs))
    zwa = z[:, 0:128]
    lane = lax.broadcasted_iota(jnp.int32, zwa.shape, 1)
    wa = _mm(jnp.where(lane < LORA_W, jnp.tanh(zwa), zwa), wba_ref[...])
    lw = -RWKV_DECAY * _sigmoid(w0_ref[...] + wa[:, 0:W])
    a = _sigmoid(a0_ref[...] + wa[:, W:2 * W])
    g = _mm(_sigmoid(z[:, 128:256]), gb_ref[...])
    ones_bd = ones_ref[...]
    kk = k * kkw_ref[...]
    kk = kk * lax.rsqrt(_seg_sum(kk * kk, ones_bd) + 1e-12)
    k2 = k * (1.0 + (a - 1.0) * kaw_ref[...])
    bonus = _seg_sum(r * k2 * rk_ref[...], ones_bd) * v
    return dict(r=r, lw=lw, k=k2, v=v, kk=kk, b=kk * a), bonus, g


def _rwkv_kernel(pr_ref, pk_ref, pv_ref, pz_ref, sr_ref, sk_ref, sv_ref, sz_ref, mr_ref, mk_ref, mv_ref, mz_ref,
                 w0_ref, a0_ref, wba_ref, gb_ref, kkw_ref, kaw_ref, rk_ref, ones_ref, s0_ref,
                 y_ref, bonus_ref, g_ref, s1_ref, s_scr, qr_scr, qk_scr, qv_scr, qz_scr, *, L, nb, nc):
    c = pl.program_id(1)

    @pl.when(c == 0)
    def _():
        s_scr[...] = s0_ref[...]
        qr_scr[...] = sr_ref[...]
        qk_scr[...] = sk_ref[...]
        qv_scr[...] = sv_ref[...]
        qz_scr[...] = sz_ref[...]

    vals, bonus, g = _rwkv_prep((pr_ref, pk_ref, pv_ref, pz_ref), (qr_scr, qk_scr, qv_scr, qz_scr),
                                (mr_ref, mk_ref, mv_ref, mz_ref), w0_ref, a0_ref, wba_ref, gb_ref, kkw_ref,
                                kaw_ref, rk_ref, ones_ref, L, nb)
    bonus_ref[...] = bonus.reshape(nb, L, W)
    g_ref[...] = g.reshape(nb, L, W)
    chunk = lambda name, ib: vals[name][ib * L:(ib + 1) * L, :]
    row = lax.broadcasted_iota(jnp.int32, (L, L), 0)
    col = lax.broadcasted_iota(jnp.int32, (L, L), 1)
    eye = jnp.where(row == col, 1.0, 0.0)
    tri = jnp.where(row >= col, 1.0, 0.0).astype(BF)
    row2 = lax.broadcasted_iota(jnp.int32, (2 * L, 2 * L), 0)
    col2 = lax.broadcasted_iota(jnp.int32, (2 * L, 2 * L), 1)
    cc = jnp.where(col2 >= L, col2 - L, col2)
    keep = jnp.where(row2 < L, row2 - 1, row2 - L) >= cc
    sls = [slice(h * DH, (h + 1) * DH) for h in range(H)]
    chains = [(ib, h) for ib in range(nb) for h in range(H)]
    pre = {}
    for ib in range(nb):
        lw = chunk("lw", ib)
        cum = _mm_mask_l(tri, lw)
        p_in = jnp.exp(cum)
        p_inv = jnp.exp(-cum)
        p_last = p_in[L - 1:L, :]
        kh = chunk("k", ib) * p_inv
        bh = chunk("b", ib) * p_inv
        pre[ib] = dict(
            kr=jnp.concatenate([(chunk("kk", ib) * jnp.exp(cum - lw)).astype(BF),
                                (chunk("r", ib) * p_in).astype(BF)], axis=0),
            kb=jnp.concatenate([kh.astype(BF), bh.astype(BF)], axis=0),
            kbl=jnp.concatenate([(kh * p_last).astype(BF), (bh * p_last).astype(BF)], axis=0),
            v=chunk("v", ib).astype(BF), p_last=p_last)
    op = lambda u, name: pre[u[0]][name][:, sls[u[1]]]
    gm = {u: jnp.where(keep, _mm_nt(op(u, "kr"), op(u, "kb")), 0.0) for u in chains}
    akr = {u: gm[u][:, 0:L].astype(BF) for u in chains}
    r_b = {u: gm[u][L:2 * L, L:2 * L].astype(BF) for u in chains}
    xs = {u: -gm[u][0:L, L:2 * L] for u in chains}
    invs = {u: eye + xs[u] for u in chains}
    if L > 2:
        xs = {u: _mm(xs[u], xs[u]) for u in chains}
    n = 4
    while n < L:
        st = {u: _mm(jnp.concatenate([invs[u], xs[u]], axis=0), xs[u]) for u in chains}
        invs = {u: invs[u] + st[u][0:L] for u in chains}
        xs = {u: st[u][L:2 * L] for u in chains}
        n *= 2
    if L > 2:
        invs = {u: invs[u] + _mm(invs[u], xs[u]) for u in chains}
    akrv = {u: _mm(akr[u], op(u, "v")) for u in chains}
    ss = {u: s_scr[u] for u in chains}
    krs = {u: _mm_nt(op(u, "kr"), ss[u]) for u in chains}
    us = {u: _mm(invs[u], krs[u][0:L] + akrv[u][0:L]) for u in chains}
    ys = {u: krs[u][L:2 * L] + akrv[u][L:2 * L] - _mm(r_b[u], us[u]) for u in chains}
    for u in chains:
        vu = jnp.concatenate([op(u, "v"), (-us[u]).astype(BF)], axis=0)
        s_scr[u] = ss[u] * pre[u[0]]["p_last"][:, sls[u[1]]] + _mm_tn(vu, op(u, "kbl"))
    for ib in range(nb):
        y_ref[ib] = jnp.concatenate([ys[ib, h] for h in range(H)], axis=1)

    @pl.when(c == nc - 1)
    def _():
        s1_ref[...] = s_scr[...]


def _rwkv(p3, shift, mu, w0, a0, wba_bf, gb_bf, kkw, kaw, rk, ones_bd, s0, L, nb):
    b, t, _ = p3.shape
    nc = t // L
    vec = lambda: pl.BlockSpec((1, W), lambda i, c: (0, 0))
    carry = lambda width: pl.BlockSpec((nb, 1, width), lambda i, c: (i, 0, 0))
    blk = lambda: pl.BlockSpec((nb, L, W), lambda i, c: (i, c, 0))
    st = pl.BlockSpec((nb, H, DH, DH), lambda i, c: (i, 0, 0, 0))
    sh3 = shift.reshape(b, 1, -1)
    mu2 = mu.reshape(1, -1)
    return pl.pallas_call(
        functools.partial(_rwkv_kernel, L=L, nb=nb, nc=nc),
        grid=(b // nb, nc),
        in_specs=[pl.BlockSpec((nb, L, W), lambda i, c: (i, c, 2)),
                  pl.BlockSpec((nb, L, W), lambda i, c: (i, c, 3)),
                  pl.BlockSpec((nb, L, W), lambda i, c: (i, c, 4)),
                  pl.BlockSpec((nb, L, 256), lambda i, c: (i, c, 10)),
                  carry(W), carry(W), carry(W), carry(256),
                  vec(), vec(), vec(), pl.BlockSpec((1, 256), lambda i, c: (0, 0)),
                  vec(), vec(),
                  pl.BlockSpec((128, 2 * W), lambda i, c: (0, 0)),
                  pl.BlockSpec((LORA_G, W), lambda i, c: (0, 0)),
                  vec(), vec(), vec(),
                  pl.BlockSpec((W, W), lambda i, c: (0, 0)),
                  st],
        out_specs=[blk(), blk(), blk(), st],
        out_shape=[jax.ShapeDtypeStruct((b, t, W), F32)] * 3 + [jax.ShapeDtypeStruct((b, H, DH, DH), F32)],
        scratch_shapes=[pltpu.VMEM((nb, H, DH, DH), F32), pltpu.VMEM((nb, 1, W), F32), pltpu.VMEM((nb, 1, W), F32),
                        pltpu.VMEM((nb, 1, W), F32), pltpu.VMEM((nb, 1, 256), F32)],
        compiler_params=_params(("parallel", "arbitrary")),
        name="rwkv",
    )(p3, p3, p3, p3,
      sh3[:, :, 0:W], sh3[:, :, W:2 * W], sh3[:, :, 2 * W:3 * W], sh3[:, :, 3 * W:],
      mu2[:, 0:W], mu2[:, W:2 * W], mu2[:, 2 * W:3 * W], mu2[:, 3 * W:],
      w0.reshape(1, W), a0.reshape(1, W), wba_bf, gb_bf, kkw.reshape(1, W), kaw.reshape(1, W),
      rk.reshape(1, W), ones_bd, s0)


def _block_diag(w):
    nb, bw, _ = w.shape
    return (jnp.eye(nb, dtype=w.dtype)[:, None, :, None] * w[:, :, None, :]).reshape(nb * bw, nb * bw)


def _chunk(t, target):
    return target if t % target == 0 else t


def _scan_blocking(b, t, target):
    L = _chunk(t, target)
    return L, min(b, 4 if L >= 64 else 8)


def _run_group(x, mods, st, wts):
    b, t, d = x.shape
    x2 = x.reshape(b * t, d)
    (mc, mn, mm, lh, lconv, ccb, rs, rsh) = st

    sh1, sc1, gt1, sh2, sc2, gt2 = mods[0]
    e = wts["even"]
    p = _in_proj(x2, wts["g_mix"][0], sh1, sc1, e["w_in"], t, 640)
    p3 = p.reshape(b, t, IN_EVEN_PAD)
    gif_t = jnp.transpose(p3[:, :, 6 * W:6 * W + 2 * H], (0, 2, 1))
    hm, c1, n1, m1 = _mlstm(p3, gif_t, e["b_if"], mc[0], mn[0], mm[0], *_scan_blocking(b, t, 128))
    hl, lh1 = _lru(p3, lconv[0], lh[0], e["w_conv"], e["b_conv"], e["w_gate"], e["b_gate"], e["lam"],
                   _chunk(t, 256))
    xr = p3[:, :, 4 * W:5 * W]
    conv1 = jnp.concatenate([lconv[0], xr], axis=1)[:, -(LRU_CONV - 1):]
    x2 = _out_proj(x2, hl.reshape(b * t, W), hm.reshape(b * t, W), p, 3, None, e["g_head"], jnp.zeros((W,), F32),
                   wts["ones_bd"], e["w_out"], gt1, t, True, RMS_EPS, True)
    m = wts["moe"][0]
    x2 = _moe(x2, wts["g_ffn"][0], sh2, sc2, gt2, m["wr_hi"], m["wr_lo"], m["br"], m["w1"], m["w3"], m["w2"],
              wts["g_final"], t, False)

    sh1, sc1, gt1, sh2, sc2, gt2 = mods[1]
    o = wts["odd"]
    p = _in_proj(x2, wts["g_mix"][1], sh1, sc1, o["w_in"], t, 1408)
    p3 = p.reshape(b, t, IN_ODD)
    cc, cc1 = _conf(p3, ccb[0], o["b_glu"], o["w_dw"], o["b_dw"], o["g_ln"], o["b_ln"], _chunk(t, 256))
    y, bonus, g, s1 = _rwkv(p3, rsh[0], o["mu"], o["w0"], o["a0"], o["wba"], o["gb"], o["kkw"], o["kaw"], o["rk"],
                            wts["ones_bd"], rs[0], *_scan_blocking(b, t, 64))
    sh_out = p3[:, t - 1, 2 * W:]
    x2 = _out_proj(x2, cc.reshape(b * t, W), y.reshape(b * t, W), g.reshape(b * t, W), 0, bonus.reshape(b * t, W),
                   o["g_gn"], o["b_gn"], wts["ones_bd"], o["w_out"], gt1, t, False, RWKV_GN_EPS, False)
    m = wts["moe"][1]
    y2 = _moe(x2, wts["g_ffn"][1], sh2, sc2, gt2, m["wr_hi"], m["wr_lo"], m["br"], m["w1"], m["w3"], m["w2"],
              wts["g_final"], t, True)
    states = (c1[None], n1[None], m1.reshape(1, b, H), lh1.reshape(1, b, W), conv1[None], cc1[None], s1[None],
              sh_out[None])
    return y2.reshape(b, t, d), states


def kernel(x_prompt, x_sample, c_prompt, c_sample, state_mlstm_C, state_mlstm_n, state_mlstm_m, state_lru_h,
           cache_lru_conv, cache_conformer_conv, state_rwkv_S, cache_rwkv_shift, w_ada, b_ada, g_norm_mix,
           g_norm_ffn, w_in_even, b_mlstm_if, g_mlstm_head, w_lru_conv, b_lru_conv, w_lru_r, b_lru_r, w_lru_i,
           b_lru_i, lru_lambda, w_out_even, w_in_odd, b_glu, w_cc_dw, b_cc_dw, g_cc_ln, b_cc_ln, rwkv_mu,
           rwkv_w0, rwkv_wB, rwkv_a0, rwkv_aB, rwkv_gB, rwkv_kk, rwkv_ka, rwkv_rk, g_rwkv_gn, b_rwkv_gn,
           w_out_odd, w_router_g, b_router_g, w_router_e, b_router_e, w_exp_gate, w_exp_up, w_exp_down, g_final):
    bp, bs = x_prompt.shape[0], x_sample.shape[0]

    wi = w_in_even[0]
    gcol = 4 * W
    w_in_e = jnp.concatenate([wi[:, :gcol], wi[:, gcol + 2 * H:], wi[:, gcol:gcol + 2 * H],
                              jnp.zeros((D_MODEL, 128 - 2 * H), F32)], axis=1).astype(BF)
    even = dict(
        w_in=w_in_e, b_if=b_mlstm_if[0], g_head=g_mlstm_head[0], w_conv=w_lru_conv[0], b_conv=b_lru_conv[0],
        w_gate=jnp.concatenate([_block_diag(w_lru_r[0]), _block_diag(w_lru_i[0])], axis=1).astype(BF),
        b_gate=jnp.concatenate([b_lru_r[0], b_lru_i[0]]), lam=lru_lambda[0], w_out=w_out_even[0].astype(BF))
    zl = jnp.zeros((LORA_W, W), F32)
    odd = dict(
        w_in=w_in_odd[0].astype(BF), b_glu=b_glu[0], w_dw=w_cc_dw[0], b_dw=b_cc_dw[0], g_ln=g_cc_ln[0],
        b_ln=b_cc_ln[0], mu=rwkv_mu[0], w0=rwkv_w0[0], a0=rwkv_a0[0],
        wba=jnp.concatenate([jnp.concatenate([rwkv_wB[0], zl], axis=1),
                             jnp.concatenate([zl, rwkv_aB[0]], axis=1)], axis=0).astype(BF),
        gb=rwkv_gB[0].astype(BF), kkw=rwkv_kk[0], kaw=rwkv_ka[0], rk=rwkv_rk[0], g_gn=g_rwkv_gn[0],
        b_gn=b_rwkv_gn[0], w_out=w_out_odd[0].astype(BF))
    moe = []
    for l in range(DEPTH):
        wr = jnp.concatenate([w_router_g[l], w_router_e[l],
                              jnp.zeros((D_MODEL, 128 - N_GROUPS - N_EXPERTS), F32)], axis=1)
        wr_hi = wr.astype(BF)
        wr_lo = (wr - wr_hi.astype(F32)).astype(BF)
        br = jnp.concatenate([b_router_g[l], b_router_e[l],
                              jnp.zeros((128 - N_GROUPS - N_EXPERTS,), F32)]).reshape(1, 128)
        moe.append(dict(wr_hi=wr_hi, wr_lo=wr_lo, br=br, w1=w_exp_gate[l].astype(BF),
                        w3=w_exp_up[l].astype(BF), w2=w_exp_down[l].astype(BF)))
    ones_bd = _block_diag(jnp.ones((H, DH, DH), F32)).astype(BF)
    wts = dict(even=even, odd=odd, moe=moe, g_mix=g_norm_mix, g_ffn=g_norm_ffn, g_final=g_final, ones_bd=ones_bd)

    mod = _ada(jnp.concatenate([c_prompt, c_sample], axis=0), w_ada.astype(BF), b_ada)

    def mods_of(lo, hi):
        return [tuple(mod[l, lo:hi, j * D_MODEL:(j + 1) * D_MODEL].reshape(hi - lo, 1, D_MODEL) for j in range(6))
                for l in range(DEPTH)]

    z = lambda *s: jnp.zeros(s, F32)
    st_p = (z(1, bp, H, DH, DH), z(1, bp, H, DH), z(1, bp, H), z(1, bp, W), z(1, bp, LRU_CONV - 1, W),
            z(1, bp, CONV_C - 1, W), z(1, bp, H, DH, DH), z(1, bp, 3 * W + LORA_W + LORA_A + LORA_G))
    st_s = (state_mlstm_C, state_mlstm_n, state_mlstm_m, state_lru_h, cache_lru_conv, cache_conformer_conv,
            state_rwkv_S, cache_rwkv_shift)
    y_p, out_p = _run_group(x_prompt, mods_of(0, bp), st_p, wts)
    y_s, out_s = _run_group(x_sample, mods_of(bp, bp + bs), st_s, wts)
    return (y_p, y_s) + tuple(out_p) + tuple(out_s)
```

```python
import functools

import jax
import jax.numpy as jnp
from jax import lax
from jax.experimental import pallas as pl
from jax.experimental.pallas import tpu as pltpu

F32 = jnp.float32
BF = jnp.bfloat16

D_MODEL = 1024
DEPTH = 2
H = 8
DH = 64
W = 512
LRU_CONV = 4
LRU_C = 8.0
CONV_C = 31
LORA_W = 64
LORA_A = 64
LORA_G = 128
RWKV_DECAY = 0.606531
RWKV_GN_EPS = 64e-5
N_GROUPS = 4
E_PER_GROUP = 4
N_EXPERTS = 16
D_EXPERT = 256
RMS_EPS = 1e-6
LN_EPS = 1e-5
IN_EVEN_PAD = 6 * W + 128
IN_ODD = 2 * W + 3 * W + LORA_W + LORA_A + LORA_G

ROW_TILE = 1024
MOE_EXPERTS_PER_STEP = 4
VMEM_LIMIT = 48 * 1024 * 1024


def _mm(a, b):
    return jnp.dot(a.astype(BF), b.astype(BF), preferred_element_type=F32)


def _mm_nt(a, b):
    return lax.dot_general(a.astype(BF), b.astype(BF), (((1,), (1,)), ((), ())),
                           preferred_element_type=F32)


def _mm_tn(a, b):
    return lax.dot_general(a.astype(BF), b.astype(BF), (((0,), (0,)), ((), ())),
                           preferred_element_type=F32)


def _split3(x):
    hi = x.astype(BF)
    r = x - hi.astype(F32)
    mid = r.astype(BF)
    lo = (r - mid.astype(F32)).astype(BF)
    return hi, mid, lo


def _mm_mask_l(mask, x):
    return sum(jnp.dot(mask, p, preferred_element_type=F32) for p in _split3(x))


def _mm_mask_r(x, mask):
    return sum(jnp.dot(p, mask, preferred_element_type=F32) for p in _split3(x))


def _sigmoid(x):
    return 1.0 / (1.0 + jnp.exp(-x))


def _rows(v, tm):
    nb, _, c = v.shape
    if nb == 1:
        return v[0]
    return jnp.broadcast_to(v, (nb, tm // nb, c)).reshape(tm, c)


def _mod_spec(T, tm, c):
    if tm <= T:
        per = T // tm
        return pl.BlockSpec((1, 1, c), lambda i, *_: (i // per, 0, 0))
    return pl.BlockSpec((tm // T, 1, c), lambda i, *_: (i, 0, 0))


def _norm_mod(x, g, sh, sc):
    y = x * lax.rsqrt(jnp.mean(x * x, axis=-1, keepdims=True) + RMS_EPS) * g
    return y * (1.0 + sc) + sh


def _params(sem):
    return pltpu.CompilerParams(dimension_semantics=sem, vmem_limit_bytes=VMEM_LIMIT)


def _ada_kernel(c_ref, w_ref, b_ref, o_ref):
    o_ref[0] = _mm(c_ref[...], w_ref[0]) + b_ref[0]


def _ada(c_all, w, b):
    nb = c_all.shape[0]
    tn = 1536
    return pl.pallas_call(
        _ada_kernel,
        grid=(DEPTH, 6 * D_MODEL // tn),
        in_specs=[pl.BlockSpec((nb, D_MODEL), lambda l, j: (0, 0)),
                  pl.BlockSpec((1, D_MODEL, tn), lambda l, j: (l, 0, j)),
                  pl.BlockSpec((1, 1, tn), lambda l, j: (l, 0, j))],
        out_specs=pl.BlockSpec((1, nb, tn), lambda l, j: (l, 0, j)),
        out_shape=jax.ShapeDtypeStruct((DEPTH, nb, 6 * D_MODEL), F32),
        compiler_params=_params(("parallel", "parallel")),
        name="ada",
    )(c_all, w, b.reshape(DEPTH, 1, 6 * D_MODEL))


def _inproj_kernel(x_ref, g_ref, sh_ref, sc_ref, w_ref, o_ref, h_scr):
    tm = x_ref.shape[0]

    @pl.when(pl.program_id(1) == 0)
    def _():
        h = _norm_mod(x_ref[...], g_ref[...], _rows(sh_ref[...], tm), _rows(sc_ref[...], tm))
        h_scr[...] = h.astype(BF)

    o_ref[...] = jnp.dot(h_scr[...], w_ref[...], preferred_element_type=F32)


def _in_proj(x2, g, sh, sc, w_bf, T, tn):
    n, d = x2.shape
    cols = w_bf.shape[1]
    tm = ROW_TILE
    return pl.pallas_call(
        _inproj_kernel,
        grid=(n // tm, cols // tn),
        in_specs=[pl.BlockSpec((tm, d), lambda i, j: (i, 0)),
                  pl.BlockSpec((1, d), lambda i, j: (0, 0)),
                  _mod_spec(T, tm, d), _mod_spec(T, tm, d),
                  pl.BlockSpec((d, tn), lambda i, j: (0, j))],
        out_specs=pl.BlockSpec((tm, tn), lambda i, j: (i, j)),
        out_shape=jax.ShapeDtypeStruct((n, cols), F32),
        scratch_shapes=[pltpu.VMEM((tm, d), BF)],
        compiler_params=_params(("parallel", "arbitrary")),
        name="in_proj",
    )(x2, g.reshape(1, d), sh, sc, w_bf)


def _seg_sum(x, ones_bd):
    hi = x.astype(BF)
    lo = (x - hi.astype(F32)).astype(BF)
    return (jnp.dot(hi, ones_bd, preferred_element_type=F32)
            + jnp.dot(lo, ones_bd, preferred_element_type=F32))


def _outproj_kernel(*refs, pre_first, eps, has_add, sigmoid_mul):
    if has_add:
        x_ref, a_ref, pre_ref, mul_ref, add_ref, gain_ref, bias_ref, ones_ref, w_ref, gt_ref, o_ref = refs
    else:
        x_ref, a_ref, pre_ref, mul_ref, gain_ref, bias_ref, ones_ref, w_ref, gt_ref, o_ref = refs
    tm = x_ref.shape[0]
    pre = pre_ref[...]
    ones_bd = ones_ref[...]
    dev = pre - _seg_sum(pre, ones_bd) * (1.0 / DH)
    var = _seg_sum(dev * dev, ones_bd) * (1.0 / DH)
    y = dev * lax.rsqrt(var + eps) * gain_ref[...] + bias_ref[...]
    if has_add:
        y = y + add_ref[...]
    m = mul_ref[...]
    y = (y * (_sigmoid(m) if sigmoid_mul else m)).astype(BF)
    first, second = (y, a_ref[...]) if pre_first else (a_ref[...], y)
    mix = (jnp.dot(first, w_ref[0:W, :], preferred_element_type=F32)
           + jnp.dot(second, w_ref[W:2 * W, :], preferred_element_type=F32))
    o_ref[...] = x_ref[...] + _rows(gt_ref[...], tm) * mix


def _out_proj(x2, a2, pre2, mul2, mul_col, add2, gain, bias, ones_bd, w_bf, gt, T, pre_first, eps, sigmoid_mul):
    n, d = x2.shape
    tm = ROW_TILE
    has_add = add2 is not None
    row = lambda c=0: pl.BlockSpec((tm, W), lambda i, c=c: (i, c))
    vec = lambda: pl.BlockSpec((1, W), lambda i: (0, 0))
    in_specs = [pl.BlockSpec((tm, d), lambda i: (i, 0)), row(), row(), row(mul_col)]
    args = [x2, a2, pre2, mul2]
    if has_add:
        in_specs.append(row())
        args.append(add2)
    in_specs += [vec(), vec(), pl.BlockSpec((W, W), lambda i: (0, 0)), pl.BlockSpec((2 * W, d), lambda i: (0, 0)),
                 _mod_spec(T, tm, d)]
    args += [gain.reshape(1, W), bias.reshape(1, W), ones_bd, w_bf, gt]
    return pl.pallas_call(
        functools.partial(_outproj_kernel, pre_first=pre_first, eps=eps, has_add=has_add, sigmoid_mul=sigmoid_mul),
        grid=(n // tm,),
        in_specs=in_specs,
        out_specs=pl.BlockSpec((tm, d), lambda i: (i, 0)),
        out_shape=jax.ShapeDtypeStruct((n, d), F32),
        compiler_params=_params(("parallel",)),
        name="out_proj",
    )(*args)


def _route(logits):
    lane = lax.broadcasted_iota(jnp.int32, logits.shape, 1).astype(F32)
    neg = -jnp.inf
    is_g = lane < N_GROUPS
    lg = jnp.where(is_g, logits, neg)
    mg = jnp.max(lg, axis=1, keepdims=True)
    gsel = jnp.min(jnp.where(lg == mg, lane, 128.0), axis=1, keepdims=True)
    psum = jnp.sum(jnp.where(is_g, jnp.exp(lg - mg), 0.0), axis=1, keepdims=True)
    pg_sel = 1.0 / psum
    lo = N_GROUPS + E_PER_GROUP * gsel
    le = jnp.where((lane >= lo) & (lane < lo + E_PER_GROUP), logits, neg)
    v1 = jnp.max(le, axis=1, keepdims=True)
    i1 = jnp.min(jnp.where(le == v1, lane, 128.0), axis=1, keepdims=True)
    le2 = jnp.where(lane == i1, neg, le)
    v2 = jnp.max(le2, axis=1, keepdims=True)
    i2 = jnp.min(jnp.where(le2 == v2, lane, 128.0), axis=1, keepdims=True)
    e2 = jnp.exp(v2 - v1)
    p1 = 1.0 / (1.0 + e2)
    p2 = e2 / (1.0 + e2)
    return pg_sel * jnp.where(lane == i1, p1, jnp.where(lane == i2, p2, 0.0))


def _moe_kernel(x_ref, g_ref, sh_ref, sc_ref, gt_ref, wrh_ref, wrl_ref, br_ref, w1_ref, w3_ref, w2_ref,
                gf_ref, o_ref, h_scr, gate_scr, acc_scr, *, final_norm):
    tm = x_ref.shape[0]
    e = pl.program_id(1)

    @pl.when(e == 0)
    def _():
        h = _norm_mod(x_ref[...], g_ref[...], _rows(sh_ref[...], tm), _rows(sc_ref[...], tm))
        hb = h.astype(BF)
        h_scr[...] = hb
        hl = (h - hb.astype(F32)).astype(BF)
        logits = (jnp.dot(hb, wrh_ref[...], preferred_element_type=F32)
                  + jnp.dot(hl, wrh_ref[...], preferred_element_type=F32)
                  + jnp.dot(hb, wrl_ref[...], preferred_element_type=F32)) + br_ref[...]
        gate_scr[...] = _route(logits)
        acc_scr[...] = jnp.zeros_like(acc_scr)

    hb = h_scr[...]
    lane = lax.broadcasted_iota(jnp.int32, gate_scr.shape, 1)
    acc = None
    for i in range(MOE_EXPERTS_PER_STEP):
        hg = jnp.dot(hb, w1_ref[i], preferred_element_type=F32)
        hu = jnp.dot(hb, w3_ref[i], preferred_element_type=F32)
        ge = jnp.sum(jnp.where(lane == e * MOE_EXPERTS_PER_STEP + i + N_GROUPS, gate_scr[...], 0.0),
                     axis=1, keepdims=True)
        hh = hg * _sigmoid(hg) * hu * ge
        part = jnp.dot(hh.astype(BF), w2_ref[i], preferred_element_type=F32)
        acc = part if acc is None else acc + part
    acc_scr[...] += acc

    @pl.when(e == N_EXPERTS // MOE_EXPERTS_PER_STEP - 1)
    def _():
        y = x_ref[...] + _rows(gt_ref[...], tm) * acc_scr[...]
        if final_norm:
            y = y * lax.rsqrt(jnp.mean(y * y, axis=-1, keepdims=True) + RMS_EPS) * gf_ref[...]
        o_ref[...] = y


def _moe(x2, g, sh, sc, gt, wr_hi, wr_lo, br, w1, w3, w2, layer, g_final, T, final_norm):
    n, d = x2.shape
    tm = ROW_TILE
    e0 = layer * (N_EXPERTS // MOE_EXPERTS_PER_STEP)
    return pl.pallas_call(
        functools.partial(_moe_kernel, final_norm=final_norm),
        grid=(n // tm, N_EXPERTS // MOE_EXPERTS_PER_STEP),
        in_specs=[pl.BlockSpec((tm, d), lambda i, e: (i, 0)),
                  pl.BlockSpec((1, d), lambda i, e: (0, 0)),
                  _mod_spec(T, tm, d), _mod_spec(T, tm, d), _mod_spec(T, tm, d),
                  pl.BlockSpec((d, 128), lambda i, e: (0, 0)),
                  pl.BlockSpec((d, 128), lambda i, e: (0, 0)),
                  pl.BlockSpec((1, 128), lambda i, e: (0, 0)),
                  pl.BlockSpec((MOE_EXPERTS_PER_STEP, d, D_EXPERT), lambda i, e: (e0 + e, 0, 0)),
                  pl.BlockSpec((MOE_EXPERTS_PER_STEP, d, D_EXPERT), lambda i, e: (e0 + e, 0, 0)),
                  pl.BlockSpec((MOE_EXPERTS_PER_STEP, D_EXPERT, d), lambda i, e: (e0 + e, 0, 0)),
                  pl.BlockSpec((1, d), lambda i, e: (0, 0))],
        out_specs=pl.BlockSpec((tm, d), lambda i, e: (i, 0)),
        out_shape=jax.ShapeDtypeStruct((n, d), F32),
        scratch_shapes=[pltpu.VMEM((tm, d), BF), pltpu.VMEM((tm, 128), F32), pltpu.VMEM((tm, d), F32)],
        compiler_params=_params(("parallel", "arbitrary")),
        name="moe",
    )(x2, g.reshape(1, d), sh, sc, gt, wr_hi, wr_lo, br, w1, w3, w2, g_final.reshape(1, d))


def _log_sigmoid(x):
    return jnp.minimum(x, 0.0) - jnp.log(1.0 + jnp.exp(-jnp.abs(x)))


def _mlstm_kernel(q_ref, k_ref, v_ref, g_ref, gt_ref, bif_ref, bift_ref, c0_ref, n0_ref, m0_ref,
                  h_ref, c1_ref, n1_ref, m1_ref, st_scr, m_scr, *, L, nb, nc):
    c = pl.program_id(1)
    NP = H // 2
    r64 = lax.broadcasted_iota(jnp.int32, (DH, DH), 0)
    c64 = lax.broadcasted_iota(jnp.int32, (DH, DH), 1)
    eye64 = r64 == c64

    @pl.when(c == 0)
    def _():
        st_scr[...] = jnp.zeros_like(st_scr)
        for ib in range(nb):
            for h in range(H):
                p, o = divmod(h, 2)
                rs = slice(o * DH, (o + 1) * DH)
                st_scr[ib, p, rs, o * DH:(o + 1) * DH] = c0_ref[ib, h]
                n_col = jnp.sum(jnp.where(eye64, n0_ref[ib, h:h + 1, :], 0.0), axis=1, keepdims=True)
                st_scr[ib, p, rs, 2 * DH + o * DH:2 * DH + (o + 1) * DH] = jnp.broadcast_to(n_col, (DH, DH))
        m_scr[...] = m0_ref[...]

    row = lax.broadcasted_iota(jnp.int32, (L, L), 0)
    col = lax.broadcasted_iota(jnp.int32, (L, L), 1)
    causal = row >= col
    tri = jnp.where(causal, 1.0, 0.0).astype(BF)
    tri_u = jnp.where(row <= col, 1.0, 0.0).astype(BF)
    even = lax.broadcasted_iota(jnp.int32, (L, 2 * DH), 1) < DH
    row_s = lax.broadcasted_iota(jnp.int32, (2 * DH, 4 * DH), 0)
    lane_s = lax.broadcasted_iota(jnp.int32, (2 * DH, 4 * DH), 1)
    top = row_s < DH
    same_head = jnp.where(top, 0, DH) == jnp.bitwise_and(lane_s, DH)
    ones_l = jnp.ones((L, 2 * DH), BF)
    ibs = range(nb)
    g = [g_ref[ib] + bif_ref[...] for ib in ibs]
    gt = [gt_ref[ib] + bift_ref[...] for ib in ibs]
    bcum = [_mm_mask_l(tri, _log_sigmoid(g[ib])) for ib in ibs]
    bcum_t = [_mm_mask_r(_log_sigmoid(gt[ib]), tri_u) for ib in ibs]
    m_prev = [m_scr[ib] for ib in ibs]
    ch = [(ib, h) for ib in ibs for h in range(H)]
    prs = [(ib, p) for ib in ibs for p in range(NP)]
    pair_of = lambda u: (u[0], u[1] // 2)
    wide = lambda x: jnp.broadcast_to(x, (L, 2 * DH))
    lanes = {u: slice(u[1] * 2 * DH, (u[1] + 1) * 2 * DH) for u in prs}
    q_f = {u: q_ref[u[0], :, lanes[u]] for u in prs}
    q_b = {u: q_f[u].astype(BF) for u in prs}
    k_f = {u: k_ref[u[0], :, lanes[u]] * (DH ** -0.5) for u in prs}
    k_b = {u: k_f[u].astype(BF) for u in prs}
    rhs = {u: jnp.concatenate([v_ref[u[0], :, lanes[u]].astype(BF), ones_l], axis=1) for u in prs}
    odd = lax.broadcasted_iota(jnp.int32, (L, 2 * DH), 1) >= DH
    qk = {u: _mm_nt(jnp.where(odd if u[1] % 2 else even, q_f[pair_of(u)], 0.0).astype(BF), k_b[pair_of(u)])
          for u in ch}
    st = {u: st_scr[u[0], u[1]] for u in prs}
    qst = {u: _mm(q_b[u], st[u]) for u in prs}
    bc = {u: wide(bcum[u[0]][:, H + u[1]:H + u[1] + 1]) for u in ch}
    ic = {u: wide(g[u[0]][:, u[1]:u[1] + 1]) for u in ch}
    mp = {u: m_prev[u[0]][:, u[1]:u[1] + 1] for u in ch}
    dmat = {u: jnp.where(causal, bc[u][:, 0:L] + (gt[u[0]][u[1]:u[1] + 1, :] - bcum_t[u[0]][H + u[1]:H + u[1] + 1, :]),
                         -jnp.inf) for u in ch}
    g_inter = {u: bc[u] + mp[u] for u in ch}
    m_t = {u: jnp.maximum(g_inter[u], jnp.max(dmat[u], axis=1, keepdims=True)) for u in ch}
    s = {u: (qk[u] * jnp.exp(dmat[u] - m_t[u][:, 0:L])).astype(BF) for u in ch}
    w_inter = {u: jnp.exp(g_inter[u] - m_t[u]) for u in ch}
    sv = {u: _mm(s[u], rhs[pair_of(u)]) for u in ch}
    m_new = {u: m_t[u][L - 1:L, 0:1] for u in ch}
    b_last = {u: bcum[u[0]][L - 1:L, H + u[1]:H + u[1] + 1] for u in ch}
    w_s = {u: jnp.exp(b_last[u] - bc[u] + ic[u] - m_new[u]) for u in ch}
    decay = {u: jnp.exp(b_last[u] + mp[u] - m_new[u]) for u in ch}
    pick = lambda d, u: jnp.where(even, d[u[0], 2 * u[1]], d[u[0], 2 * u[1] + 1])
    hh = {}
    for u in prs:
        e, o = (u[0], 2 * u[1]), (u[0], 2 * u[1] + 1)
        w_pair = pick(w_inter, u)
        num = jnp.where(even, sv[e][:, 0:2 * DH], sv[o][:, 0:2 * DH]) + w_pair * qst[u][:, 0:2 * DH]
        den = jnp.where(even, sv[e][:, 2 * DH:], sv[o][:, 2 * DH:]) + w_pair * qst[u][:, 2 * DH:]
        hh[u] = num / jnp.maximum(jnp.abs(den), jnp.exp(-pick(m_t, u)))
        inc = _mm_tn((k_f[u] * pick(w_s, u)).astype(BF), rhs[u])
        st_scr[u[0], u[1]] = jnp.where(top, decay[e], decay[o]) * st[u] + jnp.where(same_head, inc, 0.0)
    for ib in ibs:
        h_ref[ib] = jnp.concatenate([hh[ib, p] for p in range(NP)], axis=1)
        m_scr[ib] = jnp.concatenate([m_new[ib, h] for h in range(H)], axis=1)

    @pl.when(c == nc - 1)
    def _():
        for ib in range(nb):
            for h in range(H):
                p, o = divmod(h, 2)
                rs = slice(o * DH, (o + 1) * DH)
                c1_ref[ib, h] = st_scr[ib, p, rs, o * DH:(o + 1) * DH]
                n_rep = st_scr[ib, p, rs, 2 * DH + o * DH:2 * DH + (o + 1) * DH]
                n1_ref[ib, h:h + 1, :] = jnp.sum(jnp.where(eye64, n_rep, 0.0), axis=0, keepdims=True)
        m1_ref[...] = m_scr[...]


def _mlstm(p3, gif_t, bif, c0, n0, m0, L, nb):
    b, t, _ = p3.shape
    nc = t // L
    col = lambda j: pl.BlockSpec((nb, L, W), lambda i, c, j=j: (i, c, j))
    bif_pad = jnp.zeros((1, 128), F32).at[0, :2 * H].set(bif)
    return pl.pallas_call(
        functools.partial(_mlstm_kernel, L=L, nb=nb, nc=nc),
        grid=(b // nb, nc),
        in_specs=[col(0), col(1), col(2),
                  pl.BlockSpec((nb, L, 128), lambda i, c: (i, c, 6 * W // 128)),
                  pl.BlockSpec((nb, 2 * H, L), lambda i, c: (i, 0, c)),
                  pl.BlockSpec((1, 128), lambda i, c: (0, 0)),
                  pl.BlockSpec((2 * H, 1), lambda i, c: (0, 0)),
                  pl.BlockSpec((nb, H, DH, DH), lambda i, c: (i, 0, 0, 0)),
                  pl.BlockSpec((nb, H, DH), lambda i, c: (i, 0, 0)),
                  pl.BlockSpec((nb, 1, H), lambda i, c: (i, 0, 0))],
        out_specs=[pl.BlockSpec((nb, L, W), lambda i, c: (i, c, 0)),
                   pl.BlockSpec((nb, H, DH, DH), lambda i, c: (i, 0, 0, 0)),
                   pl.BlockSpec((nb, H, DH), lambda i, c: (i, 0, 0)),
                   pl.BlockSpec((nb, 1, H), lambda i, c: (i, 0, 0))],
        out_shape=[jax.ShapeDtypeStruct((b, t, W), F32),
                   jax.ShapeDtypeStruct((b, H, DH, DH), F32),
                   jax.ShapeDtypeStruct((b, H, DH), F32),
                   jax.ShapeDtypeStruct((b, 1, H), F32)],
        scratch_shapes=[pltpu.VMEM((nb, H // 2, 2 * DH, 4 * DH), F32), pltpu.VMEM((nb, 1, H), F32)],
        compiler_params=_params(("parallel", "arbitrary")),
        name="mlstm",
    )(p3, p3, p3, p3, gif_t, bif_pad, bif.reshape(2 * H, 1), c0, n0, m0.reshape(b, 1, H))


def _gelu_tanh(x):
    return 0.5 * x * (1.0 + jnp.tanh(0.7978845608028654 * (x + 0.044715 * x * x * x)))


def _lru_kernel(xr_ref, xg_ref, cache_ref, h0_ref, wc_ref, bc_ref, wg_ref, bg_ref, lam_ref,
                o_ref, h1_ref, ext_scr, a_scr, u_scr, hs_scr, hc_scr, *, Tc, nb, nc):
    c = pl.program_id(1)
    K1 = LRU_CONV - 1

    @pl.when(c == 0)
    def _():
        ext_scr[:, 8 - K1:8, :] = cache_ref[...]
        hc_scr[...] = h0_ref[...]

    xcs = []
    for ib in range(nb):
        x = xr_ref[ib]
        ext_scr[ib, 8:8 + Tc, :] = x
        xc = bc_ref[...] + wc_ref[K1:K1 + 1, :] * x
        for d in range(1, LRU_CONV):
            xc = xc + wc_ref[K1 - d:K1 - d + 1, :] * ext_scr[ib, 8 - d:8 - d + Tc, :]
        ext_scr[ib, 8 - K1:8, :] = ext_scr[ib, 8 + Tc - K1:8 + Tc, :]
        xcs.append(xc)
    xc = jnp.concatenate(xcs, axis=0)
    gates = _mm(xc, wg_ref[...]) + bg_ref[...]
    r = _sigmoid(gates[:, 0:W])
    ig = _sigmoid(gates[:, W:2 * W])
    lam = lam_ref[...]
    softplus_neg = jnp.maximum(-lam, 0.0) + jnp.log(1.0 + jnp.exp(-jnp.abs(lam)))
    log_a = -LRU_C * r * softplus_neg
    a_scr[...] = jnp.exp(log_a).reshape(nb, Tc, W)
    th = jnp.tanh(log_a)
    one_minus_a2 = -2.0 * th / (1.0 - th)
    u_scr[...] = (jnp.sqrt(one_minus_a2) * (ig * xc)).reshape(nb, Tc, W)

    def body(t, hs):
        new = []
        for ib in range(nb):
            h = a_scr[ib, pl.ds(t, 1), :] * hs[ib] + u_scr[ib, pl.ds(t, 1), :]
            hs_scr[ib, pl.ds(t, 1), :] = h
            new.append(h)
        return tuple(new)

    h_fin = lax.fori_loop(0, Tc, body, tuple(hc_scr[ib] for ib in range(nb)), unroll=8)
    for ib in range(nb):
        hc_scr[ib] = h_fin[ib]
        o_ref[ib] = (hs_scr[ib] * _gelu_tanh(xg_ref[ib])).astype(BF)

    @pl.when(c == nc - 1)
    def _():
        h1_ref[...] = hc_scr[...]


def _lru(p3, cache, h0, w_conv, b_conv, wg_bf, bg, lam, Tc, nb):
    b, t, _ = p3.shape
    nc = t // Tc
    return pl.pallas_call(
        functools.partial(_lru_kernel, Tc=Tc, nb=nb, nc=nc),
        grid=(b // nb, nc),
        in_specs=[pl.BlockSpec((nb, Tc, W), lambda i, c: (i, c, 4)),
                  pl.BlockSpec((nb, Tc, W), lambda i, c: (i, c, 5)),
                  pl.BlockSpec((nb, LRU_CONV - 1, W), lambda i, c: (i, 0, 0)),
                  pl.BlockSpec((nb, 1, W), lambda i, c: (i, 0, 0)),
                  pl.BlockSpec((LRU_CONV, W), lambda i, c: (0, 0)),
                  pl.BlockSpec((1, W), lambda i, c: (0, 0)),
                  pl.BlockSpec((W, 2 * W), lambda i, c: (0, 0)),
                  pl.BlockSpec((1, 2 * W), lambda i, c: (0, 0)),
                  pl.BlockSpec((1, W), lambda i, c: (0, 0))],
        out_specs=[pl.BlockSpec((nb, Tc, W), lambda i, c: (i, c, 0)),
                   pl.BlockSpec((nb, 1, W), lambda i, c: (i, 0, 0))],
        out_shape=[jax.ShapeDtypeStruct((b, t, W), BF), jax.ShapeDtypeStruct((b, 1, W), F32)],
        scratch_shapes=[pltpu.VMEM((nb, 8 + Tc, W), F32), pltpu.VMEM((nb, Tc, W), F32), pltpu.VMEM((nb, Tc, W), F32),
                        pltpu.VMEM((nb, Tc, W), F32), pltpu.VMEM((nb, 1, W), F32)],
        compiler_params=_params(("parallel", "arbitrary")),
        name="lru",
    )(p3, p3, cache, h0.reshape(b, 1, W), w_conv, b_conv.reshape(1, W), wg_bf, bg.reshape(1, 2 * W),
      lam.reshape(1, W))


CONF_ROWS = 32


def _conf_kernel(u_ref, gte_ref, cache_ref, bu_ref, bg_ref, wdw_ref, bdw_ref, gln_ref, bln_ref,
                 o_ref, cache1_ref, ext_scr, sh_scr, *, Tc, nb, nc):
    c = pl.program_id(1)
    K1 = CONV_C - 1
    base = 32 - K1

    @pl.when(c == 0)
    def _():
        ext_scr[:, base:32, :] = cache_ref[...]

    for ib in range(nb):
        u = u_ref[ib] + bu_ref[...]
        gte = gte_ref[ib] + bg_ref[...]
        ext_scr[ib, 32:32 + Tc, :] = u * _sigmoid(gte)
        for b in range(1, 8):
            sh_scr[b - 1] = ext_scr[ib, b:b + Tc + 24, :]

        def window(off, r0, rb):
            a, b = divmod(off, 8)
            if b == 0:
                return ext_scr[ib, 8 * a + r0:8 * a + r0 + rb, :]
            return sh_scr[b - 1, 8 * a + r0:8 * a + r0 + rb, :]

        rb = min(CONF_ROWS, Tc)
        for r0 in range(0, Tc, rb):
            acc = bdw_ref[...] + wdw_ref[0:1, :] * window(base, r0, rb)
            for j in range(1, CONV_C):
                acc = acc + wdw_ref[j:j + 1, :] * window(base + j, r0, rb)
            mu = jnp.mean(acc, axis=1, keepdims=True)
            var = jnp.mean(jnp.square(acc - mu), axis=1, keepdims=True)
            y = (acc - mu) * lax.rsqrt(var + LN_EPS) * gln_ref[...] + bln_ref[...]
            o_ref[ib, r0:r0 + rb, :] = (y * _sigmoid(y)).astype(BF)
        ext_scr[ib, base:32, :] = ext_scr[ib, base + Tc:32 + Tc, :]

    @pl.when(c == nc - 1)
    def _():
        cache1_ref[...] = ext_scr[:, base:32, :]


def _conf(p3, cache, b_glu, w_dw, b_dw, g_ln, b_ln, Tc, nb):
    b, t, _ = p3.shape
    nc = t // Tc
    vec = lambda: pl.BlockSpec((1, W), lambda i, c: (0, 0))
    return pl.pallas_call(
        functools.partial(_conf_kernel, Tc=Tc, nb=nb, nc=nc),
        grid=(b // nb, nc),
        in_specs=[pl.BlockSpec((nb, Tc, W), lambda i, c: (i, c, 0)),
                  pl.BlockSpec((nb, Tc, W), lambda i, c: (i, c, 1)),
                  pl.BlockSpec((nb, CONV_C - 1, W), lambda i, c: (i, 0, 0)),
                  vec(), vec(),
                  pl.BlockSpec((CONV_C, W), lambda i, c: (0, 0)),
                  vec(), vec(), vec()],
        out_specs=[pl.BlockSpec((nb, Tc, W), lambda i, c: (i, c, 0)),
                   pl.BlockSpec((nb, CONV_C - 1, W), lambda i, c: (i, 0, 0))],
        out_shape=[jax.ShapeDtypeStruct((b, t, W), BF), jax.ShapeDtypeStruct((b, CONV_C - 1, W), F32)],
        scratch_shapes=[pltpu.VMEM((nb, 32 + Tc, W), F32), pltpu.VMEM((7, Tc + 24, W), F32)],
        compiler_params=_params(("parallel", "arbitrary")),
        name="conformer",
    )(p3, p3, cache, b_glu[:W].reshape(1, W), b_glu[W:].reshape(1, W), w_dw, b_dw.reshape(1, W),
      g_ln.reshape(1, W), b_ln.reshape(1, W))


def _rwkv_prep(x_refs, prev_scrs, mu_refs, w0_ref, a0_ref, wba_ref, gb_ref, kkw_ref, kaw_ref, rk_ref, ones_ref,
               L, nb):
    def shift_mix(x_ref, prev_scr, mu_ref):
        x = x_ref[...].reshape(nb * L, x_ref.shape[-1])
        first = jnp.bitwise_and(lax.broadcasted_iota(jnp.int32, x.shape, 0), L - 1) == 0
        carried = jnp.concatenate([jnp.broadcast_to(prev_scr[ib], (L, x.shape[1])) for ib in range(nb)], axis=0)
        prev = jnp.where(first, carried, pltpu.roll(x, 1, 0))
        for ib in range(nb):
            prev_scr[ib] = x[(ib + 1) * L - 1:(ib + 1) * L, :]
        return x + (prev - x) * mu_ref[...]

    r, k, v, z = (shift_mix(x, p, m) for x, p, m in zip(x_refs, prev_scrs, mu_refs))
    zwa = z[:, 0:128]
    lane = lax.broadcasted_iota(jnp.int32, zwa.shape, 1)
    wa = _mm(jnp.where(lane < LORA_W, jnp.tanh(zwa), zwa), wba_ref[...])
    lw = -RWKV_DECAY * _sigmoid(w0_ref[...] + wa[:, 0:W])
    a = _sigmoid(a0_ref[...] + wa[:, W:2 * W])
    g = _mm(_sigmoid(z[:, 128:256]), gb_ref[...])
    ones_bd = ones_ref[...]
    kk = k * kkw_ref[...]
    kk = kk * lax.rsqrt(_seg_sum(kk * kk, ones_bd) + 1e-12)
    k2 = k * (1.0 + (a - 1.0) * kaw_ref[...])
    bonus = _seg_sum(r * k2 * rk_ref[...], ones_bd) * v
    return dict(r=r, lw=lw, k=k2, v=v, kk=kk, b=kk * a), bonus, g


def _rwkv_kernel(pr_ref, pk_ref, pv_ref, pz_ref, sr_ref, sk_ref, sv_ref, sz_ref, mr_ref, mk_ref, mv_ref, mz_ref,
                 w0_ref, a0_ref, wba_ref, gb_ref, kkw_ref, kaw_ref, rk_ref, ones_ref, s0_ref,
                 y_ref, bonus_ref, g_ref, s1_ref, s_scr, qr_scr, qk_scr, qv_scr, qz_scr, *, L, nb, nc):
    c = pl.program_id(1)

    @pl.when(c == 0)
    def _():
        s_scr[...] = s0_ref[...]
        qr_scr[...] = sr_ref[...]
        qk_scr[...] = sk_ref[...]
        qv_scr[...] = sv_ref[...]
        qz_scr[...] = sz_ref[...]

    vals, bonus, g = _rwkv_prep((pr_ref, pk_ref, pv_ref, pz_ref), (qr_scr, qk_scr, qv_scr, qz_scr),
                                (mr_ref, mk_ref, mv_ref, mz_ref), w0_ref, a0_ref, wba_ref, gb_ref, kkw_ref,
                                kaw_ref, rk_ref, ones_ref, L, nb)
    bonus_ref[...] = bonus.reshape(nb, L, W)
    g_ref[...] = g.reshape(nb, L, W)
    chunk = lambda name, ib: vals[name][ib * L:(ib + 1) * L, :]
    row = lax.broadcasted_iota(jnp.int32, (L, L), 0)
    col = lax.broadcasted_iota(jnp.int32, (L, L), 1)
    eye = jnp.where(row == col, 1.0, 0.0)
    tri = jnp.where(row >= col, 1.0, 0.0).astype(BF)
    row2 = lax.broadcasted_iota(jnp.int32, (2 * L, 2 * L), 0)
    col2 = lax.broadcasted_iota(jnp.int32, (2 * L, 2 * L), 1)
    cc = jnp.where(col2 >= L, col2 - L, col2)
    keep = jnp.where(row2 < L, row2 - 1, row2 - L) >= cc
    sls = [slice(h * DH, (h + 1) * DH) for h in range(H)]
    chains = [(ib, h) for ib in range(nb) for h in range(H)]
    pre = {}
    for ib in range(nb):
        lw = chunk("lw", ib)
        cum = _mm_mask_l(tri, lw)
        p_in = jnp.exp(cum)
        p_inv = jnp.exp(-cum)
        p_last = p_in[L - 1:L, :]
        kh = chunk("k", ib) * p_inv
        bh = chunk("b", ib) * p_inv
        pre[ib] = dict(
            kr=jnp.concatenate([(chunk("kk", ib) * jnp.exp(cum - lw)).astype(BF),
                                (chunk("r", ib) * p_in).astype(BF)], axis=0),
            kb=jnp.concatenate([kh.astype(BF), bh.astype(BF)], axis=0),
            kbl=jnp.concatenate([(kh * p_last).astype(BF), (bh * p_last).astype(BF)], axis=0),
            v=chunk("v", ib).astype(BF), p_last=p_last)
    op = lambda u, name: pre[u[0]][name][:, sls[u[1]]]
    gm = {u: jnp.where(keep, _mm_nt(op(u, "kr"), op(u, "kb")), 0.0) for u in chains}
    akr = {u: gm[u][:, 0:L].astype(BF) for u in chains}
    r_b = {u: gm[u][L:2 * L, L:2 * L].astype(BF) for u in chains}
    xs = {u: -gm[u][0:L, L:2 * L] for u in chains}
    invs = {u: eye + xs[u] for u in chains}
    if L > 2:
        xs = {u: _mm(xs[u], xs[u]) for u in chains}
    n = 4
    while n < L:
        st = {u: _mm(jnp.concatenate([invs[u], xs[u]], axis=0), xs[u]) for u in chains}
        invs = {u: invs[u] + st[u][0:L] for u in chains}
        xs = {u: st[u][L:2 * L] for u in chains}
        n *= 2
    if L > 2:
        invs = {u: invs[u] + _mm(invs[u], xs[u]) for u in chains}
    akrv = {u: _mm(akr[u], op(u, "v")) for u in chains}
    ss = {u: s_scr[u] for u in chains}
    krs = {u: _mm_nt(op(u, "kr"), ss[u]) for u in chains}
    us = {u: _mm(invs[u], krs[u][0:L] + akrv[u][0:L]) for u in chains}
    ys = {u: krs[u][L:2 * L] + akrv[u][L:2 * L] - _mm(r_b[u], us[u]) for u in chains}
    for u in chains:
        vu = jnp.concatenate([op(u, "v"), (-us[u]).astype(BF)], axis=0)
        s_scr[u] = ss[u] * pre[u[0]]["p_last"][:, sls[u[1]]] + _mm_tn(vu, op(u, "kbl"))
    for ib in range(nb):
        y_ref[ib] = jnp.concatenate([ys[ib, h] for h in range(H)], axis=1)

    @pl.when(c == nc - 1)
    def _():
        s1_ref[...] = s_scr[...]


def _rwkv(p3, shift, mu, w0, a0, wba_bf, gb_bf, kkw, kaw, rk, ones_bd, s0, L, nb):
    b, t, _ = p3.shape
    nc = t // L
    vec = lambda: pl.BlockSpec((1, W), lambda i, c: (0, 0))
    carry = lambda width: pl.BlockSpec((nb, 1, width), lambda i, c: (i, 0, 0))
    blk = lambda: pl.BlockSpec((nb, L, W), lambda i, c: (i, c, 0))
    st = pl.BlockSpec((nb, H, DH, DH), lambda i, c: (i, 0, 0, 0))
    sh3 = shift.reshape(b, 1, -1)
    mu2 = mu.reshape(1, -1)
    return pl.pallas_call(
        functools.partial(_rwkv_kernel, L=L, nb=nb, nc=nc),
        grid=(b // nb, nc),
        in_specs=[pl.BlockSpec((nb, L, W), lambda i, c: (i, c, 2)),
                  pl.BlockSpec((nb, L, W), lambda i, c: (i, c, 3)),
                  pl.BlockSpec((nb, L, W), lambda i, c: (i, c, 4)),
                  pl.BlockSpec((nb, L, 256), lambda i, c: (i, c, 10)),
                  carry(W), carry(W), carry(W), carry(256),
                  vec(), vec(), vec(), pl.BlockSpec((1, 256), lambda i, c: (0, 0)),
                  vec(), vec(),
                  pl.BlockSpec((128, 2 * W), lambda i, c: (0, 0)),
                  pl.BlockSpec((LORA_G, W), lambda i, c: (0, 0)),
                  vec(), vec(), vec(),
                  pl.BlockSpec((W, W), lambda i, c: (0, 0)),
                  st],
        out_specs=[blk(), blk(), blk(), st],
        out_shape=[jax.ShapeDtypeStruct((b, t, W), F32)] * 3 + [jax.ShapeDtypeStruct((b, H, DH, DH), F32)],
        scratch_shapes=[pltpu.VMEM((nb, H, DH, DH), F32), pltpu.VMEM((nb, 1, W), F32), pltpu.VMEM((nb, 1, W), F32),
                        pltpu.VMEM((nb, 1, W), F32), pltpu.VMEM((nb, 1, 256), F32)],
        compiler_params=_params(("parallel", "arbitrary")),
        name="rwkv",
    )(p3, p3, p3, p3,
      sh3[:, :, 0:W], sh3[:, :, W:2 * W], sh3[:, :, 2 * W:3 * W], sh3[:, :, 3 * W:],
      mu2[:, 0:W], mu2[:, W:2 * W], mu2[:, 2 * W:3 * W], mu2[:, 3 * W:],
      w0.reshape(1, W), a0.reshape(1, W), wba_bf, gb_bf, kkw.reshape(1, W), kaw.reshape(1, W),
      rk.reshape(1, W), ones_bd, s0)


def _block_diag(w):
    nb, bw, _ = w.shape
    return (jnp.eye(nb, dtype=w.dtype)[:, None, :, None] * w[:, :, None, :]).reshape(nb * bw, nb * bw)


def _chunk(t, target):
    return target if t % target == 0 else t


def _scan_blocking(b, t, target):
    L = _chunk(t, target)
    return L, min(b, 4 if L >= 64 else 8)


def _run_group(x, mods, st, wts):
    b, t, d = x.shape
    x2 = x.reshape(b * t, d)
    (mc, mn, mm, lh, lconv, ccb, rs, rsh) = st

    sh1, sc1, gt1, sh2, sc2, gt2 = mods[0]
    e = wts["even"]
    p = _in_proj(x2, wts["g_mix"][0], sh1, sc1, e["w_in"], t, 640)
    p3 = p.reshape(b, t, IN_EVEN_PAD)
    gif_t = jnp.transpose(p3[:, :, 6 * W:6 * W + 2 * H], (0, 2, 1))
    hm, c1, n1, m1 = _mlstm(p3, gif_t, e["b_if"], mc[0], mn[0], mm[0], *_scan_blocking(b, t, 128))
    hl, lh1 = _lru(p3, lconv[0], lh[0], e["w_conv"], e["b_conv"], e["w_gate"], e["b_gate"], e["lam"],
                   _chunk(t, 256), min(b, 4 if t >= 256 else 8))
    assert t >= LRU_CONV - 1
    conv1 = p3[:, t - (LRU_CONV - 1):, 4 * W:5 * W]
    x2 = _out_proj(x2, hl.reshape(b * t, W), hm.reshape(b * t, W), p, 3, None, e["g_head"], jnp.zeros((W,), F32),
                   wts["ones_bd"], e["w_out"], gt1, t, True, RMS_EPS, True)
    m = wts["moe"][0]
    x2 = _moe(x2, wts["g_ffn"][0], sh2, sc2, gt2, m["wr_hi"], m["wr_lo"], m["br"], wts["w1"], wts["w3"], wts["w2"],
              0, wts["g_final"], t, False)

    sh1, sc1, gt1, sh2, sc2, gt2 = mods[1]
    o = wts["odd"]
    p = _in_proj(x2, wts["g_mix"][1], sh1, sc1, o["w_in"], t, 1408)
    p3 = p.reshape(b, t, IN_ODD)
    cc, cc1 = _conf(p3, ccb[0], o["b_glu"], o["w_dw"], o["b_dw"], o["g_ln"], o["b_ln"], _chunk(t, 256),
                    min(b, 1 if t >= 256 else 8))
    y, bonus, g, s1 = _rwkv(p3, rsh[0], o["mu"], o["w0"], o["a0"], o["wba"], o["gb"], o["kkw"], o["kaw"], o["rk"],
                            wts["ones_bd"], rs[0], *_scan_blocking(b, t, 64))
    sh_out = p3[:, t - 1, 2 * W:]
    x2 = _out_proj(x2, cc.reshape(b * t, W), y.reshape(b * t, W), g.reshape(b * t, W), 0, bonus.reshape(b * t, W),
                   o["g_gn"], o["b_gn"], wts["ones_bd"], o["w_out"], gt1, t, False, RWKV_GN_EPS, False)
    m = wts["moe"][1]
    y2 = _moe(x2, wts["g_ffn"][1], sh2, sc2, gt2, m["wr_hi"], m["wr_lo"], m["br"], wts["w1"], wts["w3"], wts["w2"],
              1, wts["g_final"], t, True)
    states = (c1[None], n1[None], m1.reshape(1, b, H), lh1.reshape(1, b, W), conv1[None], cc1[None], s1[None],
              sh_out[None])
    return y2.reshape(b, t, d), states


def kernel(x_prompt, x_sample, c_prompt, c_sample, state_mlstm_C, state_mlstm_n, state_mlstm_m, state_lru_h,
           cache_lru_conv, cache_conformer_conv, state_rwkv_S, cache_rwkv_shift, w_ada, b_ada, g_norm_mix,
           g_norm_ffn, w_in_even, b_mlstm_if, g_mlstm_head, w_lru_conv, b_lru_conv, w_lru_r, b_lru_r, w_lru_i,
           b_lru_i, lru_lambda, w_out_even, w_in_odd, b_glu, w_cc_dw, b_cc_dw, g_cc_ln, b_cc_ln, rwkv_mu,
           rwkv_w0, rwkv_wB, rwkv_a0, rwkv_aB, rwkv_gB, rwkv_kk, rwkv_ka, rwkv_rk, g_rwkv_gn, b_rwkv_gn,
           w_out_odd, w_router_g, b_router_g, w_router_e, b_router_e, w_exp_gate, w_exp_up, w_exp_down, g_final):
    bp, bs = x_prompt.shape[0], x_sample.shape[0]

    wi = w_in_even[0]
    gcol = 4 * W
    w_in_e = jnp.concatenate([wi[:, :gcol], wi[:, gcol + 2 * H:], wi[:, gcol:gcol + 2 * H],
                              jnp.zeros((D_MODEL, 128 - 2 * H), F32)], axis=1).astype(BF)
    even = dict(
        w_in=w_in_e, b_if=b_mlstm_if[0], g_head=g_mlstm_head[0], w_conv=w_lru_conv[0], b_conv=b_lru_conv[0],
        w_gate=jnp.concatenate([_block_diag(w_lru_r[0]), _block_diag(w_lru_i[0])], axis=1).astype(BF),
        b_gate=jnp.concatenate([b_lru_r[0], b_lru_i[0]]), lam=lru_lambda[0], w_out=w_out_even[0].astype(BF))
    zl = jnp.zeros((LORA_W, W), F32)
    odd = dict(
        w_in=w_in_odd[0].astype(BF), b_glu=b_glu[0], w_dw=w_cc_dw[0], b_dw=b_cc_dw[0], g_ln=g_cc_ln[0],
        b_ln=b_cc_ln[0], mu=rwkv_mu[0], w0=rwkv_w0[0], a0=rwkv_a0[0],
        wba=jnp.concatenate([jnp.concatenate([rwkv_wB[0], zl], axis=1),
                             jnp.concatenate([zl, rwkv_aB[0]], axis=1)], axis=0).astype(BF),
        gb=rwkv_gB[0].astype(BF), kkw=rwkv_kk[0], kaw=rwkv_ka[0], rk=rwkv_rk[0], g_gn=g_rwkv_gn[0],
        b_gn=b_rwkv_gn[0], w_out=w_out_odd[0].astype(BF))
    moe = []
    for l in range(DEPTH):
        wr = jnp.concatenate([w_router_g[l], w_router_e[l],
                              jnp.zeros((D_MODEL, 128 - N_GROUPS - N_EXPERTS), F32)], axis=1)
        wr_hi = wr.astype(BF)
        wr_lo = (wr - wr_hi.astype(F32)).astype(BF)
        br = jnp.concatenate([b_router_g[l], b_router_e[l],
                              jnp.zeros((128 - N_GROUPS - N_EXPERTS,), F32)]).reshape(1, 128)
        moe.append(dict(wr_hi=wr_hi, wr_lo=wr_lo, br=br))
    ones_bd = _block_diag(jnp.ones((H, DH, DH), F32)).astype(BF)
    wts = dict(even=even, odd=odd, moe=moe, g_mix=g_norm_mix, g_ffn=g_norm_ffn, g_final=g_final, ones_bd=ones_bd,
               w1=w_exp_gate.astype(BF).reshape(DEPTH * N_EXPERTS, D_MODEL, D_EXPERT),
               w3=w_exp_up.astype(BF).reshape(DEPTH * N_EXPERTS, D_MODEL, D_EXPERT),
               w2=w_exp_down.astype(BF).reshape(DEPTH * N_EXPERTS, D_EXPERT, D_MODEL))

    mod = _ada(jnp.concatenate([c_prompt, c_sample], axis=0), w_ada, b_ada)

    def mods_of(lo, hi):
        return [tuple(mod[l, lo:hi, j * D_MODEL:(j + 1) * D_MODEL].reshape(hi - lo, 1, D_MODEL) for j in range(6))
                for l in range(DEPTH)]

    z = lambda *s: jnp.zeros(s, F32)
    st_p = (z(1, bp, H, DH, DH), z(1, bp, H, DH), z(1, bp, H), z(1, bp, W), z(1, bp, LRU_CONV - 1, W),
            z(1, bp, CONV_C - 1, W), z(1, bp, H, DH, DH), z(1, bp, 3 * W + LORA_W + LORA_A + LORA_G))
    st_s = (state_mlstm_C, state_mlstm_n, state_mlstm_m, state_lru_h, cache_lru_conv, cache_conformer_conv,
            state_rwkv_S, cache_rwkv_shift)
    y_p, out_p = _run_group(x_prompt, mods_of(0, bp), st_p, wts)
    y_s, out_s = _run_group(x_sample, mods_of(bp, bp + bs), st_s, wts)
    return (y_p, y_s) + tuple(out_p) + tuple(out_s)
```

```python
import functools

import jax
import jax.numpy as jnp
from jax import lax
from jax.experimental import pallas as pl
from jax.experimental.pallas import tpu as pltpu

F32 = jnp.float32
BF = jnp.bfloat16

D_MODEL = 1024
DEPTH = 2
H = 8
DH = 64
W = 512
LRU_CONV = 4
LRU_C = 8.0
CONV_C = 31
LORA_W = 64
LORA_A = 64
LORA_G = 128
RWKV_DECAY = 0.606531
RWKV_GN_EPS = 64e-5
N_GROUPS = 4
E_PER_GROUP = 4
N_EXPERTS = 16
D_EXPERT = 256
RMS_EPS = 1e-6
LN_EPS = 1e-5
IN_EVEN_PAD = 6 * W + 128
IN_ODD = 2 * W + 3 * W + LORA_W + LORA_A + LORA_G

ROW_TILE = 1024
MOE_EXPERTS_PER_STEP = 4
VMEM_LIMIT = 48 * 1024 * 1024


def _mm(a, b):
    return jnp.dot(a.astype(BF), b.astype(BF), preferred_element_type=F32)


def _mm_nt(a, b):
    return lax.dot_general(a.astype(BF), b.astype(BF), (((1,), (1,)), ((), ())),
                           preferred_element_type=F32)


def _mm_tn(a, b):
    return lax.dot_general(a.astype(BF), b.astype(BF), (((0,), (0,)), ((), ())),
                           preferred_element_type=F32)


def _split3(x):
    hi = x.astype(BF)
    r = x - hi.astype(F32)
    mid = r.astype(BF)
    lo = (r - mid.astype(F32)).astype(BF)
    return hi, mid, lo


def _mm_mask_l(mask, x):
    return sum(jnp.dot(mask, p, preferred_element_type=F32) for p in _split3(x))


def _mm_mask_r(x, mask):
    return sum(jnp.dot(p, mask, preferred_element_type=F32) for p in _split3(x))


def _sigmoid(x):
    return 1.0 / (1.0 + jnp.exp(-x))


def _rows(v, tm):
    nb, _, c = v.shape
    if nb == 1:
        return v[0]
    return jnp.broadcast_to(v, (nb, tm // nb, c)).reshape(tm, c)


def _mod_spec(T, tm, c):
    if tm <= T:
        per = T // tm
        return pl.BlockSpec((1, 1, c), lambda i, *_: (i // per, 0, 0))
    return pl.BlockSpec((tm // T, 1, c), lambda i, *_: (i, 0, 0))


def _norm_mod(x, g, sh, sc):
    y = x * lax.rsqrt(jnp.mean(x * x, axis=-1, keepdims=True) + RMS_EPS) * g
    return y * (1.0 + sc) + sh


def _params(sem):
    return pltpu.CompilerParams(dimension_semantics=sem, vmem_limit_bytes=VMEM_LIMIT)


def _ada_kernel(c_ref, w_ref, b_ref, o_ref):
    o_ref[0] = _mm(c_ref[...], w_ref[0]) + b_ref[0]


def _ada(c_all, w, b):
    nb = c_all.shape[0]
    tn = 1536
    return pl.pallas_call(
        _ada_kernel,
        grid=(DEPTH, 6 * D_MODEL // tn),
        in_specs=[pl.BlockSpec((nb, D_MODEL), lambda l, j: (0, 0)),
                  pl.BlockSpec((1, D_MODEL, tn), lambda l, j: (l, 0, j)),
                  pl.BlockSpec((1, 1, tn), lambda l, j: (l, 0, j))],
        out_specs=pl.BlockSpec((1, nb, tn), lambda l, j: (l, 0, j)),
        out_shape=jax.ShapeDtypeStruct((DEPTH, nb, 6 * D_MODEL), F32),
        compiler_params=_params(("parallel", "parallel")),
        name="ada",
    )(c_all, w, b.reshape(DEPTH, 1, 6 * D_MODEL))


def _inproj_kernel(x_ref, g_ref, sh_ref, sc_ref, w_ref, o_ref, h_scr):
    tm = x_ref.shape[0]

    @pl.when(pl.program_id(1) == 0)
    def _():
        h = _norm_mod(x_ref[...], g_ref[...], _rows(sh_ref[...], tm), _rows(sc_ref[...], tm))
        h_scr[...] = h.astype(BF)

    o_ref[...] = jnp.dot(h_scr[...], w_ref[...], preferred_element_type=F32)


INPROJ_ROW_TILE = 512


def _in_proj(x2, g, sh, sc, w_bf, T):
    n, d = x2.shape
    cols = w_bf.shape[1]
    tm = INPROJ_ROW_TILE
    tn = cols
    return pl.pallas_call(
        _inproj_kernel,
        grid=(n // tm, cols // tn),
        in_specs=[pl.BlockSpec((tm, d), lambda i, j: (i, 0)),
                  pl.BlockSpec((1, d), lambda i, j: (0, 0)),
                  _mod_spec(T, tm, d), _mod_spec(T, tm, d),
                  pl.BlockSpec((d, tn), lambda i, j: (0, j))],
        out_specs=pl.BlockSpec((tm, tn), lambda i, j: (i, j)),
        out_shape=jax.ShapeDtypeStruct((n, cols), F32),
        scratch_shapes=[pltpu.VMEM((tm, d), BF)],
        compiler_params=_params(("parallel", "arbitrary")),
        name="in_proj",
    )(x2, g.reshape(1, d), sh, sc, w_bf)


def _seg_sum(x, ones_bd):
    hi = x.astype(BF)
    lo = (x - hi.astype(F32)).astype(BF)
    return (jnp.dot(hi, ones_bd, preferred_element_type=F32)
            + jnp.dot(lo, ones_bd, preferred_element_type=F32))


def _outproj_kernel(*refs, pre_first, eps, has_add, sigmoid_mul):
    if has_add:
        x_ref, a_ref, pre_ref, mul_ref, add_ref, gain_ref, bias_ref, ones_ref, w_ref, gt_ref, o_ref = refs
    else:
        x_ref, a_ref, pre_ref, mul_ref, gain_ref, bias_ref, ones_ref, w_ref, gt_ref, o_ref = refs
    tm = x_ref.shape[0]
    pre = pre_ref[...]
    ones_bd = ones_ref[...]
    dev = pre - _seg_sum(pre, ones_bd) * (1.0 / DH)
    var = _seg_sum(dev * dev, ones_bd) * (1.0 / DH)
    y = dev * lax.rsqrt(var + eps) * gain_ref[...] + bias_ref[...]
    if has_add:
        y = y + add_ref[...]
    m = mul_ref[...]
    y = (y * (_sigmoid(m) if sigmoid_mul else m)).astype(BF)
    first, second = (y, a_ref[...]) if pre_first else (a_ref[...], y)
    mix = (jnp.dot(first, w_ref[0:W, :], preferred_element_type=F32)
           + jnp.dot(second, w_ref[W:2 * W, :], preferred_element_type=F32))
    o_ref[...] = x_ref[...] + _rows(gt_ref[...], tm) * mix


def _out_proj(x2, a2, pre2, mul2, mul_col, add2, gain, bias, ones_bd, w_bf, gt, T, pre_first, eps, sigmoid_mul):
    n, d = x2.shape
    tm = ROW_TILE
    has_add = add2 is not None
    row = lambda c=0: pl.BlockSpec((tm, W), lambda i, c=c: (i, c))
    vec = lambda: pl.BlockSpec((1, W), lambda i: (0, 0))
    in_specs = [pl.BlockSpec((tm, d), lambda i: (i, 0)), row(), row(), row(mul_col)]
    args = [x2, a2, pre2, mul2]
    if has_add:
        in_specs.append(row())
        args.append(add2)
    in_specs += [vec(), vec(), pl.BlockSpec((W, W), lambda i: (0, 0)), pl.BlockSpec((2 * W, d), lambda i: (0, 0)),
                 _mod_spec(T, tm, d)]
    args += [gain.reshape(1, W), bias.reshape(1, W), ones_bd, w_bf, gt]
    return pl.pallas_call(
        functools.partial(_outproj_kernel, pre_first=pre_first, eps=eps, has_add=has_add, sigmoid_mul=sigmoid_mul),
        grid=(n // tm,),
        in_specs=in_specs,
        out_specs=pl.BlockSpec((tm, d), lambda i: (i, 0)),
        out_shape=jax.ShapeDtypeStruct((n, d), F32),
        compiler_params=_params(("parallel",)),
        name="out_proj",
    )(*args)


def _route(logits):
    lane = lax.broadcasted_iota(jnp.int32, logits.shape, 1).astype(F32)
    neg = -jnp.inf
    is_g = lane < N_GROUPS
    lg = jnp.where(is_g, logits, neg)
    mg = jnp.max(lg, axis=1, keepdims=True)
    gsel = jnp.min(jnp.where(lg == mg, lane, 128.0), axis=1, keepdims=True)
    psum = jnp.sum(jnp.where(is_g, jnp.exp(lg - mg), 0.0), axis=1, keepdims=True)
    pg_sel = 1.0 / psum
    lo = N_GROUPS + E_PER_GROUP * gsel
    le = jnp.where((lane >= lo) & (lane < lo + E_PER_GROUP), logits, neg)
    v1 = jnp.max(le, axis=1, keepdims=True)
    i1 = jnp.min(jnp.where(le == v1, lane, 128.0), axis=1, keepdims=True)
    le2 = jnp.where(lane == i1, neg, le)
    v2 = jnp.max(le2, axis=1, keepdims=True)
    i2 = jnp.min(jnp.where(le2 == v2, lane, 128.0), axis=1, keepdims=True)
    e2 = jnp.exp(v2 - v1)
    p1 = 1.0 / (1.0 + e2)
    p2 = e2 / (1.0 + e2)
    return pg_sel * jnp.where(lane == i1, p1, jnp.where(lane == i2, p2, 0.0))


def _moe_kernel(x_ref, g_ref, sh_ref, sc_ref, gt_ref, wrh_ref, wrl_ref, br_ref, w1_ref, w3_ref, w2_ref,
                gf_ref, o_ref, h_scr, gate_scr, acc_scr, *, final_norm):
    tm = x_ref.shape[0]
    e = pl.program_id(1)

    @pl.when(e == 0)
    def _():
        h = _norm_mod(x_ref[...], g_ref[...], _rows(sh_ref[...], tm), _rows(sc_ref[...], tm))
        hb = h.astype(BF)
        h_scr[...] = hb
        hl = (h - hb.astype(F32)).astype(BF)
        logits = (jnp.dot(hb, wrh_ref[...], preferred_element_type=F32)
                  + jnp.dot(hl, wrh_ref[...], preferred_element_type=F32)
                  + jnp.dot(hb, wrl_ref[...], preferred_element_type=F32)) + br_ref[...]
        gate_scr[...] = _route(logits)
        acc_scr[...] = jnp.zeros_like(acc_scr)

    hb = h_scr[...]
    lane = lax.broadcasted_iota(jnp.int32, gate_scr.shape, 1)
    acc = None
    for i in range(MOE_EXPERTS_PER_STEP):
        hg = jnp.dot(hb, w1_ref[i], preferred_element_type=F32)
        hu = jnp.dot(hb, w3_ref[i], preferred_element_type=F32)
        ge = jnp.sum(jnp.where(lane == e * MOE_EXPERTS_PER_STEP + i + N_GROUPS, gate_scr[...], 0.0),
                     axis=1, keepdims=True)
        hh = hg * _sigmoid(hg) * hu * ge
        part = jnp.dot(hh.astype(BF), w2_ref[i], preferred_element_type=F32)
        acc = part if acc is None else acc + part
    acc_scr[...] += acc

    @pl.when(e == N_EXPERTS // MOE_EXPERTS_PER_STEP - 1)
    def _():
        y = x_ref[...] + _rows(gt_ref[...], tm) * acc_scr[...]
        if final_norm:
            y = y * lax.rsqrt(jnp.mean(y * y, axis=-1, keepdims=True) + RMS_EPS) * gf_ref[...]
        o_ref[...] = y


def _moe(x2, g, sh, sc, gt, wr_hi, wr_lo, br, w1, w3, w2, layer, g_final, T, final_norm):
    n, d = x2.shape
    tm = ROW_TILE
    e0 = layer * (N_EXPERTS // MOE_EXPERTS_PER_STEP)
    return pl.pallas_call(
        functools.partial(_moe_kernel, final_norm=final_norm),
        grid=(n // tm, N_EXPERTS // MOE_EXPERTS_PER_STEP),
        in_specs=[pl.BlockSpec((tm, d), lambda i, e: (i, 0)),
                  pl.BlockSpec((1, d), lambda i, e: (0, 0)),
                  _mod_spec(T, tm, d), _mod_spec(T, tm, d), _mod_spec(T, tm, d),
                  pl.BlockSpec((d, 128), lambda i, e: (0, 0)),
                  pl.BlockSpec((d, 128), lambda i, e: (0, 0)),
                  pl.BlockSpec((1, 128), lambda i, e: (0, 0)),
                  pl.BlockSpec((MOE_EXPERTS_PER_STEP, d, D_EXPERT), lambda i, e: (e0 + e, 0, 0)),
                  pl.BlockSpec((MOE_EXPERTS_PER_STEP, d, D_EXPERT), lambda i, e: (e0 + e, 0, 0)),
                  pl.BlockSpec((MOE_EXPERTS_PER_STEP, D_EXPERT, d), lambda i, e: (e0 + e, 0, 0)),
                  pl.BlockSpec((1, d), lambda i, e: (0, 0))],
        out_specs=pl.BlockSpec((tm, d), lambda i, e: (i, 0)),
        out_shape=jax.ShapeDtypeStruct((n, d), F32),
        scratch_shapes=[pltpu.VMEM((tm, d), BF), pltpu.VMEM((tm, 128), F32), pltpu.VMEM((tm, d), F32)],
        compiler_params=_params(("parallel", "arbitrary")),
        name="moe",
    )(x2, g.reshape(1, d), sh, sc, gt, wr_hi, wr_lo, br, w1, w3, w2, g_final.reshape(1, d))


def _log_sigmoid(x):
    return jnp.minimum(x, 0.0) - jnp.log(1.0 + jnp.exp(-jnp.abs(x)))


def _mlstm_kernel(q_ref, k_ref, v_ref, g_ref, gt_ref, bif_ref, bift_ref, c0_ref, n0_ref, m0_ref,
                  h_ref, c1_ref, n1_ref, m1_ref, st_scr, m_scr, *, L, nb, nc):
    c = pl.program_id(1)
    NP = H // 2
    r64 = lax.broadcasted_iota(jnp.int32, (DH, DH), 0)
    c64 = lax.broadcasted_iota(jnp.int32, (DH, DH), 1)
    eye64 = r64 == c64

    @pl.when(c == 0)
    def _():
        st_scr[...] = jnp.zeros_like(st_scr)
        for ib in range(nb):
            for h in range(H):
                p, o = divmod(h, 2)
                rs = slice(o * DH, (o + 1) * DH)
                st_scr[ib, p, rs, o * DH:(o + 1) * DH] = c0_ref[ib, h]
                n_col = jnp.sum(jnp.where(eye64, n0_ref[ib, h:h + 1, :], 0.0), axis=1, keepdims=True)
                st_scr[ib, p, rs, 2 * DH + o * DH:2 * DH + (o + 1) * DH] = jnp.broadcast_to(n_col, (DH, DH))
        m_scr[...] = m0_ref[...]

    row = lax.broadcasted_iota(jnp.int32, (L, L), 0)
    col = lax.broadcasted_iota(jnp.int32, (L, L), 1)
    causal = row >= col
    tri = jnp.where(causal, 1.0, 0.0).astype(BF)
    tri_u = jnp.where(row <= col, 1.0, 0.0).astype(BF)
    even = lax.broadcasted_iota(jnp.int32, (L, 2 * DH), 1) < DH
    row_s = lax.broadcasted_iota(jnp.int32, (2 * DH, 4 * DH), 0)
    lane_s = lax.broadcasted_iota(jnp.int32, (2 * DH, 4 * DH), 1)
    top = row_s < DH
    same_head = jnp.where(top, 0, DH) == jnp.bitwise_and(lane_s, DH)
    ones_l = jnp.ones((L, 2 * DH), BF)
    ibs = range(nb)
    g = [g_ref[ib] + bif_ref[...] for ib in ibs]
    gt = [gt_ref[ib] + bift_ref[...] for ib in ibs]
    bcum = [_mm_mask_l(tri, _log_sigmoid(g[ib])) for ib in ibs]
    bcum_t = [_mm_mask_r(_log_sigmoid(gt[ib]), tri_u) for ib in ibs]
    m_prev = [m_scr[ib] for ib in ibs]
    ch = [(ib, h) for ib in ibs for h in range(H)]
    prs = [(ib, p) for ib in ibs for p in range(NP)]
    pair_of = lambda u: (u[0], u[1] // 2)
    wide = lambda x: jnp.broadcast_to(x, (L, 2 * DH))
    lanes = {u: slice(u[1] * 2 * DH, (u[1] + 1) * 2 * DH) for u in prs}
    q_f = {u: q_ref[u[0], :, lanes[u]] for u in prs}
    q_b = {u: q_f[u].astype(BF) for u in prs}
    k_f = {u: k_ref[u[0], :, lanes[u]] * (DH ** -0.5) for u in prs}
    k_b = {u: k_f[u].astype(BF) for u in prs}
    rhs = {u: jnp.concatenate([v_ref[u[0], :, lanes[u]].astype(BF), ones_l], axis=1) for u in prs}
    odd = lax.broadcasted_iota(jnp.int32, (L, 2 * DH), 1) >= DH
    qk = {u: _mm_nt(jnp.where(odd if u[1] % 2 else even, q_f[pair_of(u)], 0.0).astype(BF), k_b[pair_of(u)])
          for u in ch}
    st = {u: st_scr[u[0], u[1]] for u in prs}
    qst = {u: _mm(q_b[u], st[u]) for u in prs}
    bc = {u: wide(bcum[u[0]][:, H + u[1]:H + u[1] + 1]) for u in ch}
    ic = {u: wide(g[u[0]][:, u[1]:u[1] + 1]) for u in ch}
    mp = {u: m_prev[u[0]][:, u[1]:u[1] + 1] for u in ch}
    dmat = {u: jnp.where(causal, bc[u][:, 0:L] + (gt[u[0]][u[1]:u[1] + 1, :] - bcum_t[u[0]][H + u[1]:H + u[1] + 1, :]),
                         -jnp.inf) for u in ch}
    g_inter = {u: bc[u] + mp[u] for u in ch}
    m_t = {u: jnp.maximum(g_inter[u], jnp.max(dmat[u], axis=1, keepdims=True)) for u in ch}
    s = {u: (qk[u] * jnp.exp(dmat[u] - m_t[u][:, 0:L])).astype(BF) for u in ch}
    w_inter = {u: jnp.exp(g_inter[u] - m_t[u]) for u in ch}
    sv = {u: _mm(s[u], rhs[pair_of(u)]) for u in ch}
    m_new = {u: m_t[u][L - 1:L, 0:1] for u in ch}
    b_last = {u: bcum[u[0]][L - 1:L, H + u[1]:H + u[1] + 1] for u in ch}
    w_s = {u: jnp.exp(b_last[u] - bc[u] + ic[u] - m_new[u]) for u in ch}
    decay = {u: jnp.exp(b_last[u] + mp[u] - m_new[u]) for u in ch}
    pick = lambda d, u: jnp.where(even, d[u[0], 2 * u[1]], d[u[0], 2 * u[1] + 1])
    hh = {}
    for u in prs:
        e, o = (u[0], 2 * u[1]), (u[0], 2 * u[1] + 1)
        w_pair = pick(w_inter, u)
        num = jnp.where(even, sv[e][:, 0:2 * DH], sv[o][:, 0:2 * DH]) + w_pair * qst[u][:, 0:2 * DH]
        den = jnp.where(even, sv[e][:, 2 * DH:], sv[o][:, 2 * DH:]) + w_pair * qst[u][:, 2 * DH:]
        hh[u] = num / jnp.maximum(jnp.abs(den), jnp.exp(-pick(m_t, u)))
        inc = _mm_tn((k_f[u] * pick(w_s, u)).astype(BF), rhs[u])
        st_scr[u[0], u[1]] = jnp.where(top, decay[e], decay[o]) * st[u] + jnp.where(same_head, inc, 0.0)
    for ib in ibs:
        h_ref[ib] = jnp.concatenate([hh[ib, p] for p in range(NP)], axis=1)
        m_scr[ib] = jnp.concatenate([m_new[ib, h] for h in range(H)], axis=1)

    @pl.when(c == nc - 1)
    def _():
        for ib in range(nb):
            for h in range(H):
                p, o = divmod(h, 2)
                rs = slice(o * DH, (o + 1) * DH)
                c1_ref[ib, h] = st_scr[ib, p, rs, o * DH:(o + 1) * DH]
                n_rep = st_scr[ib, p, rs, 2 * DH + o * DH:2 * DH + (o + 1) * DH]
                n1_ref[ib, h:h + 1, :] = jnp.sum(jnp.where(eye64, n_rep, 0.0), axis=0, keepdims=True)
        m1_ref[...] = m_scr[...]


def _mlstm(p3, gif_t, bif, c0, n0, m0, L, nb):
    b, t, _ = p3.shape
    nc = t // L
    col = lambda j: pl.BlockSpec((nb, L, W), lambda i, c, j=j: (i, c, j))
    bif_pad = jnp.zeros((1, 128), F32).at[0, :2 * H].set(bif)
    return pl.pallas_call(
        functools.partial(_mlstm_kernel, L=L, nb=nb, nc=nc),
        grid=(b // nb, nc),
        in_specs=[col(0), col(1), col(2),
                  pl.BlockSpec((nb, L, 128), lambda i, c: (i, c, 6 * W // 128)),
                  pl.BlockSpec((nb, 2 * H, L), lambda i, c: (i, 0, c)),
                  pl.BlockSpec((1, 128), lambda i, c: (0, 0)),
                  pl.BlockSpec((2 * H, 1), lambda i, c: (0, 0)),
                  pl.BlockSpec((nb, H, DH, DH), lambda i, c: (i, 0, 0, 0)),
                  pl.BlockSpec((nb, H, DH), lambda i, c: (i, 0, 0)),
                  pl.BlockSpec((nb, 1, H), lambda i, c: (i, 0, 0))],
        out_specs=[pl.BlockSpec((nb, L, W), lambda i, c: (i, c, 0)),
                   pl.BlockSpec((nb, H, DH, DH), lambda i, c: (i, 0, 0, 0)),
                   pl.BlockSpec((nb, H, DH), lambda i, c: (i, 0, 0)),
                   pl.BlockSpec((nb, 1, H), lambda i, c: (i, 0, 0))],
        out_shape=[jax.ShapeDtypeStruct((b, t, W), F32),
                   jax.ShapeDtypeStruct((b, H, DH, DH), F32),
                   jax.ShapeDtypeStruct((b, H, DH), F32),
                   jax.ShapeDtypeStruct((b, 1, H), F32)],
        scratch_shapes=[pltpu.VMEM((nb, H // 2, 2 * DH, 4 * DH), F32), pltpu.VMEM((nb, 1, H), F32)],
        compiler_params=_params(("parallel", "arbitrary")),
        name="mlstm",
    )(p3, p3, p3, p3, gif_t, bif_pad, bif.reshape(2 * H, 1), c0, n0, m0.reshape(b, 1, H))


def _gelu_tanh(x):
    return 0.5 * x * (1.0 + jnp.tanh(0.7978845608028654 * (x + 0.044715 * x * x * x)))


def _lru_kernel(xr_ref, xg_ref, cache_ref, h0_ref, wc_ref, bc_ref, wg_ref, bg_ref, lam_ref,
                o_ref, h1_ref, ext_scr, a_scr, u_scr, hs_scr, hc_scr, *, Tc, nb, nc):
    c = pl.program_id(1)
    K1 = LRU_CONV - 1

    @pl.when(c == 0)
    def _():
        ext_scr[:, 8 - K1:8, :] = cache_ref[...]
        hc_scr[...] = h0_ref[...]

    xcs = []
    for ib in range(nb):
        x = xr_ref[ib]
        ext_scr[ib, 8:8 + Tc, :] = x
        xc = bc_ref[...] + wc_ref[K1:K1 + 1, :] * x
        for d in range(1, LRU_CONV):
            xc = xc + wc_ref[K1 - d:K1 - d + 1, :] * ext_scr[ib, 8 - d:8 - d + Tc, :]
        ext_scr[ib, 8 - K1:8, :] = ext_scr[ib, 8 + Tc - K1:8 + Tc, :]
        xcs.append(xc)
    xc = jnp.concatenate(xcs, axis=0)
    gates = _mm(xc, wg_ref[...]) + bg_ref[...]
    r = _sigmoid(gates[:, 0:W])
    ig = _sigmoid(gates[:, W:2 * W])
    lam = lam_ref[...]
    softplus_neg = jnp.maximum(-lam, 0.0) + jnp.log(1.0 + jnp.exp(-jnp.abs(lam)))
    log_a = -LRU_C * r * softplus_neg
    a_scr[...] = jnp.exp(log_a).reshape(nb, Tc, W)
    th = jnp.tanh(log_a)
    one_minus_a2 = -2.0 * th / (1.0 - th)
    u_scr[...] = (jnp.sqrt(one_minus_a2) * (ig * xc)).reshape(nb, Tc, W)

    def body(t, hs):
        new = []
        for ib in range(nb):
            h = a_scr[ib, pl.ds(t, 1), :] * hs[ib] + u_scr[ib, pl.ds(t, 1), :]
            hs_scr[ib, pl.ds(t, 1), :] = h
            new.append(h)
        return tuple(new)

    h_fin = lax.fori_loop(0, Tc, body, tuple(hc_scr[ib] for ib in range(nb)), unroll=8)
    for ib in range(nb):
        hc_scr[ib] = h_fin[ib]
        o_ref[ib] = (hs_scr[ib] * _gelu_tanh(xg_ref[ib])).astype(BF)

    @pl.when(c == nc - 1)
    def _():
        h1_ref[...] = hc_scr[...]


def _lru(p3, cache, h0, w_conv, b_conv, wg_bf, bg, lam, Tc, nb):
    b, t, _ = p3.shape
    nc = t // Tc
    return pl.pallas_call(
        functools.partial(_lru_kernel, Tc=Tc, nb=nb, nc=nc),
        grid=(b // nb, nc),
        in_specs=[pl.BlockSpec((nb, Tc, W), lambda i, c: (i, c, 4)),
                  pl.BlockSpec((nb, Tc, W), lambda i, c: (i, c, 5)),
                  pl.BlockSpec((nb, LRU_CONV - 1, W), lambda i, c: (i, 0, 0)),
                  pl.BlockSpec((nb, 1, W), lambda i, c: (i, 0, 0)),
                  pl.BlockSpec((LRU_CONV, W), lambda i, c: (0, 0)),
                  pl.BlockSpec((1, W), lambda i, c: (0, 0)),
                  pl.BlockSpec((W, 2 * W), lambda i, c: (0, 0)),
                  pl.BlockSpec((1, 2 * W), lambda i, c: (0, 0)),
                  pl.BlockSpec((1, W), lambda i, c: (0, 0))],
        out_specs=[pl.BlockSpec((nb, Tc, W), lambda i, c: (i, c, 0)),
                   pl.BlockSpec((nb, 1, W), lambda i, c: (i, 0, 0))],
        out_shape=[jax.ShapeDtypeStruct((b, t, W), BF), jax.ShapeDtypeStruct((b, 1, W), F32)],
        scratch_shapes=[pltpu.VMEM((nb, 8 + Tc, W), F32), pltpu.VMEM((nb, Tc, W), F32), pltpu.VMEM((nb, Tc, W), F32),
                        pltpu.VMEM((nb, Tc, W), F32), pltpu.VMEM((nb, 1, W), F32)],
        compiler_params=_params(("parallel", "arbitrary")),
        name="lru",
    )(p3, p3, cache, h0.reshape(b, 1, W), w_conv, b_conv.reshape(1, W), wg_bf, bg.reshape(1, 2 * W),
      lam.reshape(1, W))


CONF_ROWS = 32


def _conf_kernel(u_ref, gte_ref, cache_ref, bu_ref, bg_ref, wdw_ref, bdw_ref, gln_ref, bln_ref,
                 o_ref, cache1_ref, ext_scr, sh_scr, *, Tc, nb, nc):
    c = pl.program_id(1)
    K1 = CONV_C - 1
    base = 32 - K1

    @pl.when(c == 0)
    def _():
        ext_scr[:, base:32, :] = cache_ref[...]

    for ib in range(nb):
        u = u_ref[ib] + bu_ref[...]
        gte = gte_ref[ib] + bg_ref[...]
        ext_scr[ib, 32:32 + Tc, :] = u * _sigmoid(gte)
        for b in range(1, 8):
            sh_scr[b - 1] = ext_scr[ib, b:b + Tc + 24, :]

        def window(off, r0, rb):
            a, b = divmod(off, 8)
            if b == 0:
                return ext_scr[ib, 8 * a + r0:8 * a + r0 + rb, :]
            return sh_scr[b - 1, 8 * a + r0:8 * a + r0 + rb, :]

        rb = min(CONF_ROWS, Tc)
        for r0 in range(0, Tc, rb):
            acc = bdw_ref[...] + wdw_ref[0:1, :] * window(base, r0, rb)
            for j in range(1, CONV_C):
                acc = acc + wdw_ref[j:j + 1, :] * window(base + j, r0, rb)
            mu = jnp.mean(acc, axis=1, keepdims=True)
            var = jnp.mean(jnp.square(acc - mu), axis=1, keepdims=True)
            y = (acc - mu) * lax.rsqrt(var + LN_EPS) * gln_ref[...] + bln_ref[...]
            o_ref[ib, r0:r0 + rb, :] = (y * _sigmoid(y)).astype(BF)
        ext_scr[ib, base:32, :] = ext_scr[ib, base + Tc:32 + Tc, :]

    @pl.when(c == nc - 1)
    def _():
        cache1_ref[...] = ext_scr[:, base:32, :]


def _conf(p3, cache, b_glu, w_dw, b_dw, g_ln, b_ln, Tc, nb):
    b, t, _ = p3.shape
    nc = t // Tc
    vec = lambda: pl.BlockSpec((1, W), lambda i, c: (0, 0))
    return pl.pallas_call(
        functools.partial(_conf_kernel, Tc=Tc, nb=nb, nc=nc),
        grid=(b // nb, nc),
        in_specs=[pl.BlockSpec((nb, Tc, W), lambda i, c: (i, c, 0)),
                  pl.BlockSpec((nb, Tc, W), lambda i, c: (i, c, 1)),
                  pl.BlockSpec((nb, CONV_C - 1, W), lambda i, c: (i, 0, 0)),
                  vec(), vec(),
                  pl.BlockSpec((CONV_C, W), lambda i, c: (0, 0)),
                  vec(), vec(), vec()],
        out_specs=[pl.BlockSpec((nb, Tc, W), lambda i, c: (i, c, 0)),
                   pl.BlockSpec((nb, CONV_C - 1, W), lambda i, c: (i, 0, 0))],
        out_shape=[jax.ShapeDtypeStruct((b, t, W), BF), jax.ShapeDtypeStruct((b, CONV_C - 1, W), F32)],
        scratch_shapes=[pltpu.VMEM((nb, 32 + Tc, W), F32), pltpu.VMEM((7, Tc + 24, W), F32)],
        compiler_params=_params(("parallel", "arbitrary")),
        name="conformer",
    )(p3, p3, cache, b_glu[:W].reshape(1, W), b_glu[W:].reshape(1, W), w_dw, b_dw.reshape(1, W),
      g_ln.reshape(1, W), b_ln.reshape(1, W))


def _rwkv_prep(x_refs, prev_scrs, mu_refs, w0_ref, a0_ref, wba_ref, gb_ref, kkw_ref, kaw_ref, rk_ref, ones_ref,
               L, nb):
    def shift_mix(x_ref, prev_scr, mu_ref):
        x = x_ref[...].reshape(nb * L, x_ref.shape[-1])
        first = jnp.bitwise_and(lax.broadcasted_iota(jnp.int32, x.shape, 0), L - 1) == 0
        carried = jnp.concatenate([jnp.broadcast_to(prev_scr[ib], (L, x.shape[1])) for ib in range(nb)], axis=0)
        prev = jnp.where(first, carried, pltpu.roll(x, 1, 0))
        for ib in range(nb):
            prev_scr[ib] = x[(ib + 1) * L - 1:(ib + 1) * L, :]
        return x + (prev - x) * mu_ref[...]

    r, k, v, z = (shift_mix(x, p, m) for x, p, m in zip(x_refs, prev_scrs, mu_refs))
    zwa = z[:, 0:128]
    lane = lax.broadcasted_iota(jnp.int32, zwa.shape, 1)
    wa = _mm(jnp.where(lane < LORA_W, jnp.tanh(zwa), zwa), wba_ref[...])
    lw = -RWKV_DECAY * _sigmoid(w0_ref[...] + wa[:, 0:W])
    a = _sigmoid(a0_ref[...] + wa[:, W:2 * W])
    g = _mm(_sigmoid(z[:, 128:256]), gb_ref[...])
    ones_bd = ones_ref[...]
    kk = k * kkw_ref[...]
    kk = kk * lax.rsqrt(_seg_sum(kk * kk, ones_bd) + 1e-12)
    k2 = k * (1.0 + (a - 1.0) * kaw_ref[...])
    bonus = _seg_sum(r * k2 * rk_ref[...], ones_bd) * v
    return dict(r=r, lw=lw, k=k2, v=v, kk=kk, b=kk * a), bonus, g


def _rwkv_kernel(pr_ref, pk_ref, pv_ref, pz_ref, sr_ref, sk_ref, sv_ref, sz_ref, mr_ref, mk_ref, mv_ref, mz_ref,
                 w0_ref, a0_ref, wba_ref, gb_ref, kkw_ref, kaw_ref, rk_ref, ones_ref, s0_ref,
                 y_ref, bonus_ref, g_ref, s1_ref, s_scr, qr_scr, qk_scr, qv_scr, qz_scr, *, L, nb, nc):
    c = pl.program_id(1)

    @pl.when(c == 0)
    def _():
        s_scr[...] = s0_ref[...]
        qr_scr[...] = sr_ref[...]
        qk_scr[...] = sk_ref[...]
        qv_scr[...] = sv_ref[...]
        qz_scr[...] = sz_ref[...]

    vals, bonus, g = _rwkv_prep((pr_ref, pk_ref, pv_ref, pz_ref), (qr_scr, qk_scr, qv_scr, qz_scr),
                                (mr_ref, mk_ref, mv_ref, mz_ref), w0_ref, a0_ref, wba_ref, gb_ref, kkw_ref,
                                kaw_ref, rk_ref, ones_ref, L, nb)
    bonus_ref[...] = bonus.reshape(nb, L, W)
    g_ref[...] = g.reshape(nb, L, W)
    chunk = lambda name, ib: vals[name][ib * L:(ib + 1) * L, :]
    row = lax.broadcasted_iota(jnp.int32, (L, L), 0)
    col = lax.broadcasted_iota(jnp.int32, (L, L), 1)
    eye = jnp.where(row == col, 1.0, 0.0)
    tri = jnp.where(row >= col, 1.0, 0.0).astype(BF)
    row2 = lax.broadcasted_iota(jnp.int32, (2 * L, 2 * L), 0)
    col2 = lax.broadcasted_iota(jnp.int32, (2 * L, 2 * L), 1)
    cc = jnp.where(col2 >= L, col2 - L, col2)
    keep = jnp.where(row2 < L, row2 - 1, row2 - L) >= cc
    sls = [slice(h * DH, (h + 1) * DH) for h in range(H)]
    chains = [(ib, h) for ib in range(nb) for h in range(H)]
    pre = {}
    for ib in range(nb):
        lw = chunk("lw", ib)
        cum = _mm_mask_l(tri, lw)
        p_in = jnp.exp(cum)
        p_inv = jnp.exp(-cum)
        p_last = p_in[L - 1:L, :]
        kh = chunk("k", ib) * p_inv
        bh = chunk("b", ib) * p_inv
        pre[ib] = dict(
            kr=jnp.concatenate([(chunk("kk", ib) * jnp.exp(cum - lw)).astype(BF),
                                (chunk("r", ib) * p_in).astype(BF)], axis=0),
            kb=jnp.concatenate([kh.astype(BF), bh.astype(BF)], axis=0),
            kbl=jnp.concatenate([(kh * p_last).astype(BF), (bh * p_last).astype(BF)], axis=0),
            v=chunk("v", ib).astype(BF), p_last=p_last)
    op = lambda u, name: pre[u[0]][name][:, sls[u[1]]]
    gm = {u: jnp.where(keep, _mm_nt(op(u, "kr"), op(u, "kb")), 0.0) for u in chains}
    akr = {u: gm[u][:, 0:L].astype(BF) for u in chains}
    r_b = {u: gm[u][L:2 * L, L:2 * L].astype(BF) for u in chains}
    xs = {u: -gm[u][0:L, L:2 * L] for u in chains}
    invs = {u: eye + xs[u] for u in chains}
    if L > 2:
        xs = {u: _mm(xs[u], xs[u]) for u in chains}
    n = 4
    while n < L:
        st = {u: _mm(jnp.concatenate([invs[u], xs[u]], axis=0), xs[u]) for u in chains}
        invs = {u: invs[u] + st[u][0:L] for u in chains}
        xs = {u: st[u][L:2 * L] for u in chains}
        n *= 2
    if L > 2:
        invs = {u: invs[u] + _mm(invs[u], xs[u]) for u in chains}
    akrv = {u: _mm(akr[u], op(u, "v")) for u in chains}
    ss = {u: s_scr[u] for u in chains}
    krs = {u: _mm_nt(op(u, "kr"), ss[u]) for u in chains}
    us = {u: _mm(invs[u], krs[u][0:L] + akrv[u][0:L]) for u in chains}
    ys = {u: krs[u][L:2 * L] + akrv[u][L:2 * L] - _mm(r_b[u], us[u]) for u in chains}
    for u in chains:
        vu = jnp.concatenate([op(u, "v"), (-us[u]).astype(BF)], axis=0)
        s_scr[u] = ss[u] * pre[u[0]]["p_last"][:, sls[u[1]]] + _mm_tn(vu, op(u, "kbl"))
    for ib in range(nb):
        y_ref[ib] = jnp.concatenate([ys[ib, h] for h in range(H)], axis=1)

    @pl.when(c == nc - 1)
    def _():
        s1_ref[...] = s_scr[...]


def _rwkv(p3, shift, mu, w0, a0, wba_bf, gb_bf, kkw, kaw, rk, ones_bd, s0, L, nb):
    b, t, _ = p3.shape
    nc = t // L
    vec = lambda: pl.BlockSpec((1, W), lambda i, c: (0, 0))
    carry = lambda width: pl.BlockSpec((nb, 1, width), lambda i, c: (i, 0, 0))
    blk = lambda: pl.BlockSpec((nb, L, W), lambda i, c: (i, c, 0))
    st = pl.BlockSpec((nb, H, DH, DH), lambda i, c: (i, 0, 0, 0))
    sh3 = shift.reshape(b, 1, -1)
    mu2 = mu.reshape(1, -1)
    return pl.pallas_call(
        functools.partial(_rwkv_kernel, L=L, nb=nb, nc=nc),
        grid=(b // nb, nc),
        in_specs=[pl.BlockSpec((nb, L, W), lambda i, c: (i, c, 2)),
                  pl.BlockSpec((nb, L, W), lambda i, c: (i, c, 3)),
                  pl.BlockSpec((nb, L, W), lambda i, c: (i, c, 4)),
                  pl.BlockSpec((nb, L, 256), lambda i, c: (i, c, 10)),
                  carry(W), carry(W), carry(W), carry(256),
                  vec(), vec(), vec(), pl.BlockSpec((1, 256), lambda i, c: (0, 0)),
                  vec(), vec(),
                  pl.BlockSpec((128, 2 * W), lambda i, c: (0, 0)),
                  pl.BlockSpec((LORA_G, W), lambda i, c: (0, 0)),
                  vec(), vec(), vec(),
                  pl.BlockSpec((W, W), lambda i, c: (0, 0)),
                  st],
        out_specs=[blk(), blk(), blk(), st],
        out_shape=[jax.ShapeDtypeStruct((b, t, W), F32)] * 3 + [jax.ShapeDtypeStruct((b, H, DH, DH), F32)],
        scratch_shapes=[pltpu.VMEM((nb, H, DH, DH), F32), pltpu.VMEM((nb, 1, W), F32), pltpu.VMEM((nb, 1, W), F32),
                        pltpu.VMEM((nb, 1, W), F32), pltpu.VMEM((nb, 1, 256), F32)],
        compiler_params=_params(("parallel", "arbitrary")),
        name="rwkv",
    )(p3, p3, p3, p3,
      sh3[:, :, 0:W], sh3[:, :, W:2 * W], sh3[:, :, 2 * W:3 * W], sh3[:, :, 3 * W:],
      mu2[:, 0:W], mu2[:, W:2 * W], mu2[:, 2 * W:3 * W], mu2[:, 3 * W:],
      w0.reshape(1, W), a0.reshape(1, W), wba_bf, gb_bf, kkw.reshape(1, W), kaw.reshape(1, W),
      rk.reshape(1, W), ones_bd, s0)


def _block_diag(w):
    nb, bw, _ = w.shape
    return (jnp.eye(nb, dtype=w.dtype)[:, None, :, None] * w[:, :, None, :]).reshape(nb * bw, nb * bw)


def _chunk(t, target):
    return target if t % target == 0 else t


def _scan_blocking(b, t, target):
    L = _chunk(t, target)
    return L, min(b, 8)


def _run_group(x, mods, st, wts):
    b, t, d = x.shape
    x2 = x.reshape(b * t, d)
    (mc, mn, mm, lh, lconv, ccb, rs, rsh) = st

    sh1, sc1, gt1, sh2, sc2, gt2 = mods[0]
    e = wts["even"]
    p = _in_proj(x2, wts["g_mix"][0], sh1, sc1, e["w_in"], t)
    p3 = p.reshape(b, t, IN_EVEN_PAD)
    gif_t = jnp.transpose(p3[:, :, 6 * W:6 * W + 2 * H], (0, 2, 1))
    hm, c1, n1, m1 = _mlstm(p3, gif_t, e["b_if"], mc[0], mn[0], mm[0], *_scan_blocking(b, t, 128))
    hl, lh1 = _lru(p3, lconv[0], lh[0], e["w_conv"], e["b_conv"], e["w_gate"], e["b_gate"], e["lam"],
                   _chunk(t, 256), min(b, 4 if t >= 256 else 8))
    assert t >= LRU_CONV - 1
    conv1 = p3[:, t - (LRU_CONV - 1):, 4 * W:5 * W]
    x2 = _out_proj(x2, hl.reshape(b * t, W), hm.reshape(b * t, W), p, 3, None, e["g_head"], jnp.zeros((W,), F32),
                   wts["ones_bd"], e["w_out"], gt1, t, True, RMS_EPS, True)
    m = wts["moe"][0]
    x2 = _moe(x2, wts["g_ffn"][0], sh2, sc2, gt2, m["wr_hi"], m["wr_lo"], m["br"], wts["w1"], wts["w3"], wts["w2"],
              0, wts["g_final"], t, False)

    sh1, sc1, gt1, sh2, sc2, gt2 = mods[1]
    o = wts["odd"]
    p = _in_proj(x2, wts["g_mix"][1], sh1, sc1, o["w_in"], t)
    p3 = p.reshape(b, t, IN_ODD)
    cc, cc1 = _conf(p3, ccb[0], o["b_glu"], o["w_dw"], o["b_dw"], o["g_ln"], o["b_ln"], _chunk(t, 256),
                    min(b, 1 if t >= 256 else 8))
    y, bonus, g, s1 = _rwkv(p3, rsh[0], o["mu"], o["w0"], o["a0"], o["wba"], o["gb"], o["kkw"], o["kaw"], o["rk"],
                            wts["ones_bd"], rs[0], *_scan_blocking(b, t, 64))
    sh_out = p3[:, t - 1, 2 * W:]
    x2 = _out_proj(x2, cc.reshape(b * t, W), y.reshape(b * t, W), g.reshape(b * t, W), 0, bonus.reshape(b * t, W),
                   o["g_gn"], o["b_gn"], wts["ones_bd"], o["w_out"], gt1, t, False, RWKV_GN_EPS, False)
    m = wts["moe"][1]
    y2 = _moe(x2, wts["g_ffn"][1], sh2, sc2, gt2, m["wr_hi"], m["wr_lo"], m["br"], wts["w1"], wts["w3"], wts["w2"],
              1, wts["g_final"], t, True)
    states = (c1[None], n1[None], m1.reshape(1, b, H), lh1.reshape(1, b, W), conv1[None], cc1[None], s1[None],
              sh_out[None])
    return y2.reshape(b, t, d), states


def kernel(x_prompt, x_sample, c_prompt, c_sample, state_mlstm_C, state_mlstm_n, state_mlstm_m, state_lru_h,
           cache_lru_conv, cache_conformer_conv, state_rwkv_S, cache_rwkv_shift, w_ada, b_ada, g_norm_mix,
           g_norm_ffn, w_in_even, b_mlstm_if, g_mlstm_head, w_lru_conv, b_lru_conv, w_lru_r, b_lru_r, w_lru_i,
           b_lru_i, lru_lambda, w_out_even, w_in_odd, b_glu, w_cc_dw, b_cc_dw, g_cc_ln, b_cc_ln, rwkv_mu,
           rwkv_w0, rwkv_wB, rwkv_a0, rwkv_aB, rwkv_gB, rwkv_kk, rwkv_ka, rwkv_rk, g_rwkv_gn, b_rwkv_gn,
           w_out_odd, w_router_g, b_router_g, w_router_e, b_router_e, w_exp_gate, w_exp_up, w_exp_down, g_final):
    bp, bs = x_prompt.shape[0], x_sample.shape[0]

    wi = w_in_even[0]
    gcol = 4 * W
    w_in_e = jnp.concatenate([wi[:, :gcol], wi[:, gcol + 2 * H:], wi[:, gcol:gcol + 2 * H],
                              jnp.zeros((D_MODEL, 128 - 2 * H), F32)], axis=1).astype(BF)
    even = dict(
        w_in=w_in_e, b_if=b_mlstm_if[0], g_head=g_mlstm_head[0], w_conv=w_lru_conv[0], b_conv=b_lru_conv[0],
        w_gate=jnp.concatenate([_block_diag(w_lru_r[0]), _block_diag(w_lru_i[0])], axis=1).astype(BF),
        b_gate=jnp.concatenate([b_lru_r[0], b_lru_i[0]]), lam=lru_lambda[0], w_out=w_out_even[0].astype(BF))
    zl = jnp.zeros((LORA_W, W), F32)
    odd = dict(
        w_in=w_in_odd[0].astype(BF), b_glu=b_glu[0], w_dw=w_cc_dw[0], b_dw=b_cc_dw[0], g_ln=g_cc_ln[0],
        b_ln=b_cc_ln[0], mu=rwkv_mu[0], w0=rwkv_w0[0], a0=rwkv_a0[0],
        wba=jnp.concatenate([jnp.concatenate([rwkv_wB[0], zl], axis=1),
                             jnp.concatenate([zl, rwkv_aB[0]], axis=1)], axis=0).astype(BF),
        gb=rwkv_gB[0].astype(BF), kkw=rwkv_kk[0], kaw=rwkv_ka[0], rk=rwkv_rk[0], g_gn=g_rwkv_gn[0],
        b_gn=b_rwkv_gn[0], w_out=w_out_odd[0].astype(BF))
    moe = []
    for l in range(DEPTH):
        wr = jnp.concatenate([w_router_g[l], w_router_e[l],
                              jnp.zeros((D_MODEL, 128 - N_GROUPS - N_EXPERTS), F32)], axis=1)
        wr_hi = wr.astype(BF)
        wr_lo = (wr - wr_hi.astype(F32)).astype(BF)
        br = jnp.concatenate([b_router_g[l], b_router_e[l],
                              jnp.zeros((128 - N_GROUPS - N_EXPERTS,), F32)]).reshape(1, 128)
        moe.append(dict(wr_hi=wr_hi, wr_lo=wr_lo, br=br))
    ones_bd = _block_diag(jnp.ones((H, DH, DH), F32)).astype(BF)
    wts = dict(even=even, odd=odd, moe=moe, g_mix=g_norm_mix, g_ffn=g_norm_ffn, g_final=g_final, ones_bd=ones_bd,
               w1=w_exp_gate.astype(BF).reshape(DEPTH * N_EXPERTS, D_MODEL, D_EXPERT),
               w3=w_exp_up.astype(BF).reshape(DEPTH * N_EXPERTS, D_MODEL, D_EXPERT),
               w2=w_exp_down.astype(BF).reshape(DEPTH * N_EXPERTS, D_EXPERT, D_MODEL))

    mod = _ada(jnp.concatenate([c_prompt, c_sample], axis=0), w_ada, b_ada)

    def mods_of(lo, hi):
        return [tuple(mod[l, lo:hi, j * D_MODEL:(j + 1) * D_MODEL].reshape(hi - lo, 1, D_MODEL) for j in range(6))
                for l in range(DEPTH)]

    z = lambda *s: jnp.zeros(s, F32)
    st_p = (z(1, bp, H, DH, DH), z(1, bp, H, DH), z(1, bp, H), z(1, bp, W), z(1, bp, LRU_CONV - 1, W),
            z(1, bp, CONV_C - 1, W), z(1, bp, H, DH, DH), z(1, bp, 3 * W + LORA_W + LORA_A + LORA_G))
    st_s = (state_mlstm_C, state_mlstm_n, state_mlstm_m, state_lru_h, cache_lru_conv, cache_conformer_conv,
            state_rwkv_S, cache_rwkv_shift)
    y_p, out_p = _run_group(x_prompt, mods_of(0, bp), st_p, wts)
    y_s, out_s = _run_group(x_sample, mods_of(bp, bp + bs), st_s, wts)
    return (y_p, y_s) + tuple(out_p) + tuple(out_s)
```

```python
import functools

import jax
import jax.numpy as jnp
from jax import lax
from jax.experimental import pallas as pl
from jax.experimental.pallas import tpu as pltpu

F32 = jnp.float32
BF = jnp.bfloat16

D_MODEL = 1024
DEPTH = 2
H = 8
DH = 64
W = 512
LRU_CONV = 4
LRU_C = 8.0
CONV_C = 31
LORA_W = 64
LORA_A = 64
LORA_G = 128
RWKV_DECAY = 0.606531
RWKV_GN_EPS = 64e-5
N_GROUPS = 4
E_PER_GROUP = 4
N_EXPERTS = 16
D_EXPERT = 256
RMS_EPS = 1e-6
LN_EPS = 1e-5
IN_EVEN_PAD = 6 * W + 128
IN_ODD = 2 * W + 3 * W + LORA_W + LORA_A + LORA_G

ROW_TILE = 1024
MOE_ROW_TILE = 512
VMEM_LIMIT = 48 * 1024 * 1024


def _mm(a, b):
    return jnp.dot(a.astype(BF), b.astype(BF), preferred_element_type=F32)


def _mm_nt(a, b):
    return lax.dot_general(a.astype(BF), b.astype(BF), (((1,), (1,)), ((), ())),
                           preferred_element_type=F32)


def _mm_tn(a, b):
    return lax.dot_general(a.astype(BF), b.astype(BF), (((0,), (0,)), ((), ())),
                           preferred_element_type=F32)


def _split3(x):
    hi = x.astype(BF)
    r = x - hi.astype(F32)
    mid = r.astype(BF)
    lo = (r - mid.astype(F32)).astype(BF)
    return hi, mid, lo


def _mm_mask_l(mask, x):
    return sum(jnp.dot(mask, p, preferred_element_type=F32) for p in _split3(x))


def _mm_mask_r(x, mask):
    return sum(jnp.dot(p, mask, preferred_element_type=F32) for p in _split3(x))


def _sigmoid(x):
    return 1.0 / (1.0 + jnp.exp(-x))


def _rows(v, tm):
    nb, _, c = v.shape
    if nb == 1:
        return v[0]
    return jnp.broadcast_to(v, (nb, tm // nb, c)).reshape(tm, c)


def _mod_spec(T, tm, c):
    if tm <= T:
        per = T // tm
        return pl.BlockSpec((1, 1, c), lambda i, *_: (i // per, 0, 0))
    return pl.BlockSpec((tm // T, 1, c), lambda i, *_: (i, 0, 0))


def _norm_mod(x, g, sh, sc):
    y = x * lax.rsqrt(jnp.mean(x * x, axis=-1, keepdims=True) + RMS_EPS) * g
    return y * (1.0 + sc) + sh


def _params(sem):
    return pltpu.CompilerParams(dimension_semantics=sem, vmem_limit_bytes=VMEM_LIMIT)


def _ada_kernel(c_ref, w_ref, b_ref, o_ref):
    o_ref[0] = _mm(c_ref[...], w_ref[0]) + b_ref[0]


def _ada(c_all, w, b):
    nb = c_all.shape[0]
    tn = 1536
    return pl.pallas_call(
        _ada_kernel,
        grid=(DEPTH, 6 * D_MODEL // tn),
        in_specs=[pl.BlockSpec((nb, D_MODEL), lambda l, j: (0, 0)),
                  pl.BlockSpec((1, D_MODEL, tn), lambda l, j: (l, 0, j)),
                  pl.BlockSpec((1, 1, tn), lambda l, j: (l, 0, j))],
        out_specs=pl.BlockSpec((1, nb, tn), lambda l, j: (l, 0, j)),
        out_shape=jax.ShapeDtypeStruct((DEPTH, nb, 6 * D_MODEL), F32),
        compiler_params=_params(("parallel", "parallel")),
        name="ada",
    )(c_all, w, b.reshape(DEPTH, 1, 6 * D_MODEL))


def _inproj_kernel(x_ref, g_ref, sh_ref, sc_ref, w_ref, o_ref, h_scr):
    tm = x_ref.shape[0]

    @pl.when(pl.program_id(1) == 0)
    def _():
        h = _norm_mod(x_ref[...], g_ref[...], _rows(sh_ref[...], tm), _rows(sc_ref[...], tm))
        h_scr[...] = h.astype(BF)

    o_ref[...] = jnp.dot(h_scr[...], w_ref[...], preferred_element_type=F32)


INPROJ_ROW_TILE = 512


def _in_proj(x2, g, sh, sc, w_bf, T):
    n, d = x2.shape
    cols = w_bf.shape[1]
    tm = INPROJ_ROW_TILE
    tn = cols
    return pl.pallas_call(
        _inproj_kernel,
        grid=(n // tm, cols // tn),
        in_specs=[pl.BlockSpec((tm, d), lambda i, j: (i, 0)),
                  pl.BlockSpec((1, d), lambda i, j: (0, 0)),
                  _mod_spec(T, tm, d), _mod_spec(T, tm, d),
                  pl.BlockSpec((d, tn), lambda i, j: (0, j))],
        out_specs=pl.BlockSpec((tm, tn), lambda i, j: (i, j)),
        out_shape=jax.ShapeDtypeStruct((n, cols), F32),
        scratch_shapes=[pltpu.VMEM((tm, d), BF)],
        compiler_params=_params(("parallel", "arbitrary")),
        name="in_proj",
    )(x2, g.reshape(1, d), sh, sc, w_bf)


def _seg_sum(x, ones_bd):
    hi = x.astype(BF)
    lo = (x - hi.astype(F32)).astype(BF)
    return (jnp.dot(hi, ones_bd, preferred_element_type=F32)
            + jnp.dot(lo, ones_bd, preferred_element_type=F32))


def _outproj_kernel(*refs, pre_first, eps, has_add, sigmoid_mul):
    if has_add:
        x_ref, a_ref, pre_ref, mul_ref, add_ref, gain_ref, bias_ref, ones_ref, w_ref, gt_ref, o_ref = refs
    else:
        x_ref, a_ref, pre_ref, mul_ref, gain_ref, bias_ref, ones_ref, w_ref, gt_ref, o_ref = refs
    tm = x_ref.shape[0]
    pre = pre_ref[...]
    ones_bd = ones_ref[...]
    dev = pre - _seg_sum(pre, ones_bd) * (1.0 / DH)
    var = _seg_sum(dev * dev, ones_bd) * (1.0 / DH)
    y = dev * lax.rsqrt(var + eps) * gain_ref[...] + bias_ref[...]
    if has_add:
        y = y + add_ref[...]
    m = mul_ref[...]
    y = (y * (_sigmoid(m) if sigmoid_mul else m)).astype(BF)
    first, second = (y, a_ref[...]) if pre_first else (a_ref[...], y)
    mix = (jnp.dot(first, w_ref[0:W, :], preferred_element_type=F32)
           + jnp.dot(second, w_ref[W:2 * W, :], preferred_element_type=F32))
    o_ref[...] = x_ref[...] + _rows(gt_ref[...], tm) * mix


def _out_proj(x2, a2, pre2, mul2, mul_col, add2, gain, bias, ones_bd, w_bf, gt, T, pre_first, eps, sigmoid_mul):
    n, d = x2.shape
    tm = ROW_TILE
    has_add = add2 is not None
    row = lambda c=0: pl.BlockSpec((tm, W), lambda i, c=c: (i, c))
    vec = lambda: pl.BlockSpec((1, W), lambda i: (0, 0))
    in_specs = [pl.BlockSpec((tm, d), lambda i: (i, 0)), row(), row(), row(mul_col)]
    args = [x2, a2, pre2, mul2]
    if has_add:
        in_specs.append(row())
        args.append(add2)
    in_specs += [vec(), vec(), pl.BlockSpec((W, W), lambda i: (0, 0)), pl.BlockSpec((2 * W, d), lambda i: (0, 0)),
                 _mod_spec(T, tm, d)]
    args += [gain.reshape(1, W), bias.reshape(1, W), ones_bd, w_bf, gt]
    return pl.pallas_call(
        functools.partial(_outproj_kernel, pre_first=pre_first, eps=eps, has_add=has_add, sigmoid_mul=sigmoid_mul),
        grid=(n // tm,),
        in_specs=in_specs,
        out_specs=pl.BlockSpec((tm, d), lambda i: (i, 0)),
        out_shape=jax.ShapeDtypeStruct((n, d), F32),
        compiler_params=_params(("parallel",)),
        name="out_proj",
    )(*args)


def _route(logits):
    lane = lax.broadcasted_iota(jnp.int32, logits.shape, 1).astype(F32)
    neg = -jnp.inf
    is_g = lane < N_GROUPS
    lg = jnp.where(is_g, logits, neg)
    mg = jnp.max(lg, axis=1, keepdims=True)
    gsel = jnp.min(jnp.where(lg == mg, lane, 128.0), axis=1, keepdims=True)
    psum = jnp.sum(jnp.where(is_g, jnp.exp(lg - mg), 0.0), axis=1, keepdims=True)
    pg_sel = 1.0 / psum
    lo = N_GROUPS + E_PER_GROUP * gsel
    le = jnp.where((lane >= lo) & (lane < lo + E_PER_GROUP), logits, neg)
    v1 = jnp.max(le, axis=1, keepdims=True)
    i1 = jnp.min(jnp.where(le == v1, lane, 128.0), axis=1, keepdims=True)
    le2 = jnp.where(lane == i1, neg, le)
    v2 = jnp.max(le2, axis=1, keepdims=True)
    i2 = jnp.min(jnp.where(le2 == v2, lane, 128.0), axis=1, keepdims=True)
    e2 = jnp.exp(v2 - v1)
    p1 = 1.0 / (1.0 + e2)
    p2 = e2 / (1.0 + e2)
    return pg_sel * jnp.where(lane == i1, p1, jnp.where(lane == i2, p2, 0.0))


def _moe_kernel(x_ref, g_ref, sh_ref, sc_ref, gt_ref, wrh_ref, wrl_ref, br_ref, w1_ref, w3_ref, w2_ref,
                gf_ref, o_ref, *, final_norm):
    tm = x_ref.shape[0]
    h = _norm_mod(x_ref[...], g_ref[...], _rows(sh_ref[...], tm), _rows(sc_ref[...], tm))
    hb = h.astype(BF)
    hl = (h - hb.astype(F32)).astype(BF)
    logits = (jnp.dot(hb, wrh_ref[...], preferred_element_type=F32)
              + jnp.dot(hl, wrh_ref[...], preferred_element_type=F32)
              + jnp.dot(hb, wrl_ref[...], preferred_element_type=F32)) + br_ref[...]
    gate = _route(logits)
    lane = lax.broadcasted_iota(jnp.int32, gate.shape, 1)
    acc = None
    for e in range(N_EXPERTS):
        hg = jnp.dot(hb, w1_ref[e], preferred_element_type=F32)
        hu = jnp.dot(hb, w3_ref[e], preferred_element_type=F32)
        ge = jnp.sum(jnp.where(lane == e + N_GROUPS, gate, 0.0), axis=1, keepdims=True)
        hh = hg * _sigmoid(hg) * hu * ge
        part = jnp.dot(hh.astype(BF), w2_ref[e], preferred_element_type=F32)
        acc = part if acc is None else acc + part
    y = x_ref[...] + _rows(gt_ref[...], tm) * acc
    if final_norm:
        y = y * lax.rsqrt(jnp.mean(y * y, axis=-1, keepdims=True) + RMS_EPS) * gf_ref[...]
    o_ref[...] = y


def _moe(x2, g, sh, sc, gt, wr_hi, wr_lo, br, w1, w3, w2, layer, g_final, T, final_norm):
    n, d = x2.shape
    tm = MOE_ROW_TILE
    resident = lambda shape: pl.BlockSpec(shape, lambda i: (layer, 0, 0), pipeline_mode=pl.Buffered(1))
    return pl.pallas_call(
        functools.partial(_moe_kernel, final_norm=final_norm),
        grid=(n // tm,),
        in_specs=[pl.BlockSpec((tm, d), lambda i: (i, 0)),
                  pl.BlockSpec((1, d), lambda i: (0, 0)),
                  _mod_spec(T, tm, d), _mod_spec(T, tm, d), _mod_spec(T, tm, d),
                  pl.BlockSpec((d, 128), lambda i: (0, 0)),
                  pl.BlockSpec((d, 128), lambda i: (0, 0)),
                  pl.BlockSpec((1, 128), lambda i: (0, 0)),
                  resident((N_EXPERTS, d, D_EXPERT)), resident((N_EXPERTS, d, D_EXPERT)),
                  resident((N_EXPERTS, D_EXPERT, d)),
                  pl.BlockSpec((1, d), lambda i: (0, 0))],
        out_specs=pl.BlockSpec((tm, d), lambda i: (i, 0)),
        out_shape=jax.ShapeDtypeStruct((n, d), F32),
        compiler_params=_params(("parallel",)),
        name="moe",
    )(x2, g.reshape(1, d), sh, sc, gt, wr_hi, wr_lo, br, w1, w3, w2, g_final.reshape(1, d))


def _log_sigmoid(x):
    return jnp.minimum(x, 0.0) - jnp.log(1.0 + jnp.exp(-jnp.abs(x)))


def _mlstm_kernel(q_ref, k_ref, v_ref, g_ref, gt_ref, bif_ref, bift_ref, c0_ref, n0_ref, m0_ref,
                  h_ref, c1_ref, n1_ref, m1_ref, st_scr, m_scr, *, L, nb, nc):
    c = pl.program_id(1)
    NP = H // 2
    r64 = lax.broadcasted_iota(jnp.int32, (DH, DH), 0)
    c64 = lax.broadcasted_iota(jnp.int32, (DH, DH), 1)
    eye64 = r64 == c64

    @pl.when(c == 0)
    def _():
        st_scr[...] = jnp.zeros_like(st_scr)
        for ib in range(nb):
            for h in range(H):
                p, o = divmod(h, 2)
                rs = slice(o * DH, (o + 1) * DH)
                st_scr[ib, p, rs, o * DH:(o + 1) * DH] = c0_ref[ib, h]
                n_col = jnp.sum(jnp.where(eye64, n0_ref[ib, h:h + 1, :], 0.0), axis=1, keepdims=True)
                st_scr[ib, p, rs, 2 * DH + o * DH:2 * DH + (o + 1) * DH] = jnp.broadcast_to(n_col, (DH, DH))
        m_scr[...] = m0_ref[...]

    row = lax.broadcasted_iota(jnp.int32, (L, L), 0)
    col = lax.broadcasted_iota(jnp.int32, (L, L), 1)
    causal = row >= col
    tri = jnp.where(causal, 1.0, 0.0).astype(BF)
    tri_u = jnp.where(row <= col, 1.0, 0.0).astype(BF)
    even = lax.broadcasted_iota(jnp.int32, (L, 2 * DH), 1) < DH
    row_s = lax.broadcasted_iota(jnp.int32, (2 * DH, 4 * DH), 0)
    lane_s = lax.broadcasted_iota(jnp.int32, (2 * DH, 4 * DH), 1)
    top = row_s < DH
    same_head = jnp.where(top, 0, DH) == jnp.bitwise_and(lane_s, DH)
    ones_l = jnp.ones((L, 2 * DH), BF)
    ibs = range(nb)
    g = [g_ref[ib] + bif_ref[...] for ib in ibs]
    gt = [gt_ref[ib] + bift_ref[...] for ib in ibs]
    bcum = [_mm_mask_l(tri, _log_sigmoid(g[ib])) for ib in ibs]
    bcum_t = [_mm_mask_r(_log_sigmoid(gt[ib]), tri_u) for ib in ibs]
    m_prev = [m_scr[ib] for ib in ibs]
    ch = [(ib, h) for ib in ibs for h in range(H)]
    prs = [(ib, p) for ib in ibs for p in range(NP)]
    pair_of = lambda u: (u[0], u[1] // 2)
    wide = lambda x: jnp.broadcast_to(x, (L, 2 * DH))
    lanes = {u: slice(u[1] * 2 * DH, (u[1] + 1) * 2 * DH) for u in prs}
    q_f = {u: q_ref[u[0], :, lanes[u]] for u in prs}
    q_b = {u: q_f[u].astype(BF) for u in prs}
    k_f = {u: k_ref[u[0], :, lanes[u]] * (DH ** -0.5) for u in prs}
    k_b = {u: k_f[u].astype(BF) for u in prs}
    rhs = {u: jnp.concatenate([v_ref[u[0], :, lanes[u]].astype(BF), ones_l], axis=1) for u in prs}
    odd = lax.broadcasted_iota(jnp.int32, (L, 2 * DH), 1) >= DH
    qk = {u: _mm_nt(jnp.where(odd if u[1] % 2 else even, q_f[pair_of(u)], 0.0).astype(BF), k_b[pair_of(u)])
          for u in ch}
    st = {u: st_scr[u[0], u[1]] for u in prs}
    qst = {u: _mm(q_b[u], st[u]) for u in prs}
    bc = {u: wide(bcum[u[0]][:, H + u[1]:H + u[1] + 1]) for u in ch}
    ic = {u: wide(g[u[0]][:, u[1]:u[1] + 1]) for u in ch}
    mp = {u: m_prev[u[0]][:, u[1]:u[1] + 1] for u in ch}
    dmat = {u: jnp.where(causal, bc[u][:, 0:L] + (gt[u[0]][u[1]:u[1] + 1, :] - bcum_t[u[0]][H + u[1]:H + u[1] + 1, :]),
                         -jnp.inf) for u in ch}
    g_inter = {u: bc[u] + mp[u] for u in ch}
    m_t = {u: jnp.maximum(g_inter[u], jnp.max(dmat[u], axis=1, keepdims=True)) for u in ch}
    s = {u: (qk[u] * jnp.exp(dmat[u] - m_t[u][:, 0:L])).astype(BF) for u in ch}
    w_inter = {u: jnp.exp(g_inter[u] - m_t[u]) for u in ch}
    sv = {u: _mm(s[u], rhs[pair_of(u)]) for u in ch}
    m_new = {u: m_t[u][L - 1:L, 0:1] for u in ch}
    b_last = {u: bcum[u[0]][L - 1:L, H + u[1]:H + u[1] + 1] for u in ch}
    w_s = {u: jnp.exp(b_last[u] - bc[u] + ic[u] - m_new[u]) for u in ch}
    decay = {u: jnp.exp(b_last[u] + mp[u] - m_new[u]) for u in ch}
    pick = lambda d, u: jnp.where(even, d[u[0], 2 * u[1]], d[u[0], 2 * u[1] + 1])
    hh = {}
    for u in prs:
        e, o = (u[0], 2 * u[1]), (u[0], 2 * u[1] + 1)
        w_pair = pick(w_inter, u)
        num = jnp.where(even, sv[e][:, 0:2 * DH], sv[o][:, 0:2 * DH]) + w_pair * qst[u][:, 0:2 * DH]
        den = jnp.where(even, sv[e][:, 2 * DH:], sv[o][:, 2 * DH:]) + w_pair * qst[u][:, 2 * DH:]
        hh[u] = num / jnp.maximum(jnp.abs(den), jnp.exp(-pick(m_t, u)))
        inc = _mm_tn((k_f[u] * pick(w_s, u)).astype(BF), rhs[u])
        st_scr[u[0], u[1]] = jnp.where(top, decay[e], decay[o]) * st[u] + jnp.where(same_head, inc, 0.0)
    for ib in ibs:
        h_ref[ib] = jnp.concatenate([hh[ib, p] for p in range(NP)], axis=1)
        m_scr[ib] = jnp.concatenate([m_new[ib, h] for h in range(H)], axis=1)

    @pl.when(c == nc - 1)
    def _():
        for ib in range(nb):
            for h in range(H):
                p, o = divmod(h, 2)
                rs = slice(o * DH, (o + 1) * DH)
                c1_ref[ib, h] = st_scr[ib, p, rs, o * DH:(o + 1) * DH]
                n_rep = st_scr[ib, p, rs, 2 * DH + o * DH:2 * DH + (o + 1) * DH]
                n1_ref[ib, h:h + 1, :] = jnp.sum(jnp.where(eye64, n_rep, 0.0), axis=0, keepdims=True)
        m1_ref[...] = m_scr[...]


def _mlstm(p3, gif_t, bif, c0, n0, m0, L, nb):
    b, t, _ = p3.shape
    nc = t // L
    col = lambda j: pl.BlockSpec((nb, L, W), lambda i, c, j=j: (i, c, j))
    bif_pad = jnp.zeros((1, 128), F32).at[0, :2 * H].set(bif)
    return pl.pallas_call(
        functools.partial(_mlstm_kernel, L=L, nb=nb, nc=nc),
        grid=(b // nb, nc),
        in_specs=[col(0), col(1), col(2),
                  pl.BlockSpec((nb, L, 128), lambda i, c: (i, c, 6 * W // 128)),
                  pl.BlockSpec((nb, 2 * H, L), lambda i, c: (i, 0, c)),
                  pl.BlockSpec((1, 128), lambda i, c: (0, 0)),
                  pl.BlockSpec((2 * H, 1), lambda i, c: (0, 0)),
                  pl.BlockSpec((nb, H, DH, DH), lambda i, c: (i, 0, 0, 0)),
                  pl.BlockSpec((nb, H, DH), lambda i, c: (i, 0, 0)),
                  pl.BlockSpec((nb, 1, H), lambda i, c: (i, 0, 0))],
        out_specs=[pl.BlockSpec((nb, L, W), lambda i, c: (i, c, 0)),
                   pl.BlockSpec((nb, H, DH, DH), lambda i, c: (i, 0, 0, 0)),
                   pl.BlockSpec((nb, H, DH), lambda i, c: (i, 0, 0)),
                   pl.BlockSpec((nb, 1, H), lambda i, c: (i, 0, 0))],
        out_shape=[jax.ShapeDtypeStruct((b, t, W), F32),
                   jax.ShapeDtypeStruct((b, H, DH, DH), F32),
                   jax.ShapeDtypeStruct((b, H, DH), F32),
                   jax.ShapeDtypeStruct((b, 1, H), F32)],
        scratch_shapes=[pltpu.VMEM((nb, H // 2, 2 * DH, 4 * DH), F32), pltpu.VMEM((nb, 1, H), F32)],
        compiler_params=_params(("parallel", "arbitrary")),
        name="mlstm",
    )(p3, p3, p3, p3, gif_t, bif_pad, bif.reshape(2 * H, 1), c0, n0, m0.reshape(b, 1, H))


def _gelu_tanh(x):
    return 0.5 * x * (1.0 + jnp.tanh(0.7978845608028654 * (x + 0.044715 * x * x * x)))


def _lru_kernel(xr_ref, xg_ref, cache_ref, h0_ref, wc_ref, bc_ref, wg_ref, bg_ref, lam_ref,
                o_ref, h1_ref, ext_scr, a_scr, u_scr, hs_scr, hc_scr, *, Tc, nb, nc):
    c = pl.program_id(1)
    K1 = LRU_CONV - 1

    @pl.when(c == 0)
    def _():
        ext_scr[:, 8 - K1:8, :] = cache_ref[...]
        hc_scr[...] = h0_ref[...]

    xcs = []
    for ib in range(nb):
        x = xr_ref[ib]
        ext_scr[ib, 8:8 + Tc, :] = x
        xc = bc_ref[...] + wc_ref[K1:K1 + 1, :] * x
        for d in range(1, LRU_CONV):
            xc = xc + wc_ref[K1 - d:K1 - d + 1, :] * ext_scr[ib, 8 - d:8 - d + Tc, :]
        ext_scr[ib, 8 - K1:8, :] = ext_scr[ib, 8 + Tc - K1:8 + Tc, :]
        xcs.append(xc)
    xc = jnp.concatenate(xcs, axis=0)
    gates = _mm(xc, wg_ref[...]) + bg_ref[...]
    r = _sigmoid(gates[:, 0:W])
    ig = _sigmoid(gates[:, W:2 * W])
    lam = lam_ref[...]
    softplus_neg = jnp.maximum(-lam, 0.0) + jnp.log(1.0 + jnp.exp(-jnp.abs(lam)))
    log_a = -LRU_C * r * softplus_neg
    a_scr[...] = jnp.exp(log_a).reshape(nb, Tc, W)
    th = jnp.tanh(log_a)
    one_minus_a2 = -2.0 * th / (1.0 - th)
    u_scr[...] = (jnp.sqrt(one_minus_a2) * (ig * xc)).reshape(nb, Tc, W)

    def body(t, hs):
        new = []
        for ib in range(nb):
            h = a_scr[ib, pl.ds(t, 1), :] * hs[ib] + u_scr[ib, pl.ds(t, 1), :]
            hs_scr[ib, pl.ds(t, 1), :] = h
            new.append(h)
        return tuple(new)

    h_fin = lax.fori_loop(0, Tc, body, tuple(hc_scr[ib] for ib in range(nb)), unroll=8)
    for ib in range(nb):
        hc_scr[ib] = h_fin[ib]
        o_ref[ib] = (hs_scr[ib] * _gelu_tanh(xg_ref[ib])).astype(BF)

    @pl.when(c == nc - 1)
    def _():
        h1_ref[...] = hc_scr[...]


def _lru(p3, cache, h0, w_conv, b_conv, wg_bf, bg, lam, Tc, nb):
    b, t, _ = p3.shape
    nc = t // Tc
    return pl.pallas_call(
        functools.partial(_lru_kernel, Tc=Tc, nb=nb, nc=nc),
        grid=(b // nb, nc),
        in_specs=[pl.BlockSpec((nb, Tc, W), lambda i, c: (i, c, 4)),
                  pl.BlockSpec((nb, Tc, W), lambda i, c: (i, c, 5)),
                  pl.BlockSpec((nb, LRU_CONV - 1, W), lambda i, c: (i, 0, 0)),
                  pl.BlockSpec((nb, 1, W), lambda i, c: (i, 0, 0)),
                  pl.BlockSpec((LRU_CONV, W), lambda i, c: (0, 0)),
                  pl.BlockSpec((1, W), lambda i, c: (0, 0)),
                  pl.BlockSpec((W, 2 * W), lambda i, c: (0, 0)),
                  pl.BlockSpec((1, 2 * W), lambda i, c: (0, 0)),
                  pl.BlockSpec((1, W), lambda i, c: (0, 0))],
        out_specs=[pl.BlockSpec((nb, Tc, W), lambda i, c: (i, c, 0)),
                   pl.BlockSpec((nb, 1, W), lambda i, c: (i, 0, 0))],
        out_shape=[jax.ShapeDtypeStruct((b, t, W), BF), jax.ShapeDtypeStruct((b, 1, W), F32)],
        scratch_shapes=[pltpu.VMEM((nb, 8 + Tc, W), F32), pltpu.VMEM((nb, Tc, W), F32), pltpu.VMEM((nb, Tc, W), F32),
                        pltpu.VMEM((nb, Tc, W), F32), pltpu.VMEM((nb, 1, W), F32)],
        compiler_params=_params(("parallel", "arbitrary")),
        name="lru",
    )(p3, p3, cache, h0.reshape(b, 1, W), w_conv, b_conv.reshape(1, W), wg_bf, bg.reshape(1, 2 * W),
      lam.reshape(1, W))


CONF_ROWS = 32


def _conf_kernel(u_ref, gte_ref, cache_ref, bu_ref, bg_ref, wdw_ref, bdw_ref, gln_ref, bln_ref,
                 o_ref, cache1_ref, ext_scr, sh_scr, *, Tc, nb, nc):
    c = pl.program_id(1)
    K1 = CONV_C - 1
    base = 32 - K1

    @pl.when(c == 0)
    def _():
        ext_scr[:, base:32, :] = cache_ref[...]

    for ib in range(nb):
        u = u_ref[ib] + bu_ref[...]
        gte = gte_ref[ib] + bg_ref[...]
        ext_scr[ib, 32:32 + Tc, :] = u * _sigmoid(gte)
        for b in range(1, 8):
            sh_scr[b - 1] = ext_scr[ib, b:b + Tc + 24, :]

        def window(off, r0, rb):
            a, b = divmod(off, 8)
            if b == 0:
                return ext_scr[ib, 8 * a + r0:8 * a + r0 + rb, :]
            return sh_scr[b - 1, 8 * a + r0:8 * a + r0 + rb, :]

        rb = min(CONF_ROWS, Tc)
        for r0 in range(0, Tc, rb):
            acc = bdw_ref[...] + wdw_ref[0:1, :] * window(base, r0, rb)
            for j in range(1, CONV_C):
                acc = acc + wdw_ref[j:j + 1, :] * window(base + j, r0, rb)
            mu = jnp.mean(acc, axis=1, keepdims=True)
            var = jnp.mean(jnp.square(acc - mu), axis=1, keepdims=True)
            y = (acc - mu) * lax.rsqrt(var + LN_EPS) * gln_ref[...] + bln_ref[...]
            o_ref[ib, r0:r0 + rb, :] = (y * _sigmoid(y)).astype(BF)
        ext_scr[ib, base:32, :] = ext_scr[ib, base + Tc:32 + Tc, :]

    @pl.when(c == nc - 1)
    def _():
        cache1_ref[...] = ext_scr[:, base:32, :]


def _conf(p3, cache, b_glu, w_dw, b_dw, g_ln, b_ln, Tc, nb):
    b, t, _ = p3.shape
    nc = t // Tc
    vec = lambda: pl.BlockSpec((1, W), lambda i, c: (0, 0))
    return pl.pallas_call(
        functools.partial(_conf_kernel, Tc=Tc, nb=nb, nc=nc),
        grid=(b // nb, nc),
        in_specs=[pl.BlockSpec((nb, Tc, W), lambda i, c: (i, c, 0)),
                  pl.BlockSpec((nb, Tc, W), lambda i, c: (i, c, 1)),
                  pl.BlockSpec((nb, CONV_C - 1, W), lambda i, c: (i, 0, 0)),
                  vec(), vec(),
                  pl.BlockSpec((CONV_C, W), lambda i, c: (0, 0)),
                  vec(), vec(), vec()],
        out_specs=[pl.BlockSpec((nb, Tc, W), lambda i, c: (i, c, 0)),
                   pl.BlockSpec((nb, CONV_C - 1, W), lambda i, c: (i, 0, 0))],
        out_shape=[jax.ShapeDtypeStruct((b, t, W), BF), jax.ShapeDtypeStruct((b, CONV_C - 1, W), F32)],
        scratch_shapes=[pltpu.VMEM((nb, 32 + Tc, W), F32), pltpu.VMEM((7, Tc + 24, W), F32)],
        compiler_params=_params(("parallel", "arbitrary")),
        name="conformer",
    )(p3, p3, cache, b_glu[:W].reshape(1, W), b_glu[W:].reshape(1, W), w_dw, b_dw.reshape(1, W),
      g_ln.reshape(1, W), b_ln.reshape(1, W))


def _rwkv_prep(x_refs, prev_scrs, mu_refs, w0_ref, a0_ref, wba_ref, gb_ref, kkw_ref, kaw_ref, rk_ref, ones_ref,
               L, nb):
    def shift_mix(x_ref, prev_scr, mu_ref):
        x = x_ref[...].reshape(nb * L, x_ref.shape[-1])
        first = jnp.bitwise_and(lax.broadcasted_iota(jnp.int32, x.shape, 0), L - 1) == 0
        carried = jnp.concatenate([jnp.broadcast_to(prev_scr[ib], (L, x.shape[1])) for ib in range(nb)], axis=0)
        prev = jnp.where(first, carried, pltpu.roll(x, 1, 0))
        for ib in range(nb):
            prev_scr[ib] = x[(ib + 1) * L - 1:(ib + 1) * L, :]
        return x + (prev - x) * mu_ref[...]

    r, k, v, z = (shift_mix(x, p, m) for x, p, m in zip(x_refs, prev_scrs, mu_refs))
    zwa = z[:, 0:128]
    lane = lax.broadcasted_iota(jnp.int32, zwa.shape, 1)
    wa = _mm(jnp.where(lane < LORA_W, jnp.tanh(zwa), zwa), wba_ref[...])
    lw = -RWKV_DECAY * _sigmoid(w0_ref[...] + wa[:, 0:W])
    a = _sigmoid(a0_ref[...] + wa[:, W:2 * W])
    g = _mm(_sigmoid(z[:, 128:256]), gb_ref[...])
    ones_bd = ones_ref[...]
    kk = k * kkw_ref[...]
    kk = kk * lax.rsqrt(_seg_sum(kk * kk, ones_bd) + 1e-12)
    k2 = k * (1.0 + (a - 1.0) * kaw_ref[...])
    bonus = _seg_sum(r * k2 * rk_ref[...], ones_bd) * v
    return dict(r=r, lw=lw, k=k2, v=v, kk=kk, b=kk * a), bonus, g


def _rwkv_kernel(pr_ref, pk_ref, pv_ref, pz_ref, sr_ref, sk_ref, sv_ref, sz_ref, mr_ref, mk_ref, mv_ref, mz_ref,
                 w0_ref, a0_ref, wba_ref, gb_ref, kkw_ref, kaw_ref, rk_ref, ones_ref, s0_ref,
                 y_ref, bonus_ref, g_ref, s1_ref, s_scr, qr_scr, qk_scr, qv_scr, qz_scr, *, L, nb, nc):
    c = pl.program_id(1)

    @pl.when(c == 0)
    def _():
        s_scr[...] = s0_ref[...]
        qr_scr[...] = sr_ref[...]
        qk_scr[...] = sk_ref[...]
        qv_scr[...] = sv_ref[...]
        qz_scr[...] = sz_ref[...]

    vals, bonus, g = _rwkv_prep((pr_ref, pk_ref, pv_ref, pz_ref), (qr_scr, qk_scr, qv_scr, qz_scr),
                                (mr_ref, mk_ref, mv_ref, mz_ref), w0_ref, a0_ref, wba_ref, gb_ref, kkw_ref,
                                kaw_ref, rk_ref, ones_ref, L, nb)
    bonus_ref[...] = bonus.reshape(nb, L, W)
    g_ref[...] = g.reshape(nb, L, W)
    chunk = lambda name, ib: vals[name][ib * L:(ib + 1) * L, :]
    row = lax.broadcasted_iota(jnp.int32, (L, L), 0)
    col = lax.broadcasted_iota(jnp.int32, (L, L), 1)
    eye = jnp.where(row == col, 1.0, 0.0)
    tri = jnp.where(row >= col, 1.0, 0.0).astype(BF)
    row2 = lax.broadcasted_iota(jnp.int32, (2 * L, 2 * L), 0)
    col2 = lax.broadcasted_iota(jnp.int32, (2 * L, 2 * L), 1)
    cc = jnp.where(col2 >= L, col2 - L, col2)
    keep = jnp.where(row2 < L, row2 - 1, row2 - L) >= cc
    sls = [slice(h * DH, (h + 1) * DH) for h in range(H)]
    chains = [(ib, h) for ib in range(nb) for h in range(H)]
    pre = {}
    for ib in range(nb):
        lw = chunk("lw", ib)
        cum = _mm_mask_l(tri, lw)
        p_in = jnp.exp(cum)
        p_inv = jnp.exp(-cum)
        p_last = p_in[L - 1:L, :]
        kh = chunk("k", ib) * p_inv
        bh = chunk("b", ib) * p_inv
        pre[ib] = dict(
            kr=jnp.concatenate([(chunk("kk", ib) * jnp.exp(cum - lw)).astype(BF),
                                (chunk("r", ib) * p_in).astype(BF)], axis=0),
            kb=jnp.concatenate([kh.astype(BF), bh.astype(BF)], axis=0),
            kbl=jnp.concatenate([(kh * p_last).astype(BF), (bh * p_last).astype(BF)], axis=0),
            v=chunk("v", ib).astype(BF), p_last=p_last)
    op = lambda u, name: pre[u[0]][name][:, sls[u[1]]]
    gm = {u: jnp.where(keep, _mm_nt(op(u, "kr"), op(u, "kb")), 0.0) for u in chains}
    akr = {u: gm[u][:, 0:L].astype(BF) for u in chains}
    r_b = {u: gm[u][L:2 * L, L:2 * L].astype(BF) for u in chains}
    xs = {u: -gm[u][0:L, L:2 * L] for u in chains}
    invs = {u: eye + xs[u] for u in chains}
    if L > 2:
        xs = {u: _mm(xs[u], xs[u]) for u in chains}
    n = 4
    while n < L:
        st = {u: _mm(jnp.concatenate([invs[u], xs[u]], axis=0), xs[u]) for u in chains}
        invs = {u: invs[u] + st[u][0:L] for u in chains}
        xs = {u: st[u][L:2 * L] for u in chains}
        n *= 2
    if L > 2:
        invs = {u: invs[u] + _mm(invs[u], xs[u]) for u in chains}
    akrv = {u: _mm(akr[u], op(u, "v")) for u in chains}
    ss = {u: s_scr[u] for u in chains}
    krs = {u: _mm_nt(op(u, "kr"), ss[u]) for u in chains}
    us = {u: _mm(invs[u], krs[u][0:L] + akrv[u][0:L]) for u in chains}
    ys = {u: krs[u][L:2 * L] + akrv[u][L:2 * L] - _mm(r_b[u], us[u]) for u in chains}
    for u in chains:
        vu = jnp.concatenate([op(u, "v"), (-us[u]).astype(BF)], axis=0)
        s_scr[u] = ss[u] * pre[u[0]]["p_last"][:, sls[u[1]]] + _mm_tn(vu, op(u, "kbl"))
    for ib in range(nb):
        y_ref[ib] = jnp.concatenate([ys[ib, h] for h in range(H)], axis=1)

    @pl.when(c == nc - 1)
    def _():
        s1_ref[...] = s_scr[...]


def _rwkv(p3, shift, mu, w0, a0, wba_bf, gb_bf, kkw, kaw, rk, ones_bd, s0, L, nb):
    b, t, _ = p3.shape
    nc = t // L
    vec = lambda: pl.BlockSpec((1, W), lambda i, c: (0, 0))
    carry = lambda width: pl.BlockSpec((nb, 1, width), lambda i, c: (i, 0, 0))
    blk = lambda: pl.BlockSpec((nb, L, W), lambda i, c: (i, c, 0))
    st = pl.BlockSpec((nb, H, DH, DH), lambda i, c: (i, 0, 0, 0))
    sh3 = shift.reshape(b, 1, -1)
    mu2 = mu.reshape(1, -1)
    return pl.pallas_call(
        functools.partial(_rwkv_kernel, L=L, nb=nb, nc=nc),
        grid=(b // nb, nc),
        in_specs=[pl.BlockSpec((nb, L, W), lambda i, c: (i, c, 2)),
                  pl.BlockSpec((nb, L, W), lambda i, c: (i, c, 3)),
                  pl.BlockSpec((nb, L, W), lambda i, c: (i, c, 4)),
                  pl.BlockSpec((nb, L, 256), lambda i, c: (i, c, 10)),
                  carry(W), carry(W), carry(W), carry(256),
                  vec(), vec(), vec(), pl.BlockSpec((1, 256), lambda i, c: (0, 0)),
                  vec(), vec(),
                  pl.BlockSpec((128, 2 * W), lambda i, c: (0, 0)),
                  pl.BlockSpec((LORA_G, W), lambda i, c: (0, 0)),
                  vec(), vec(), vec(),
                  pl.BlockSpec((W, W), lambda i, c: (0, 0)),
                  st],
        out_specs=[blk(), blk(), blk(), st],
        out_shape=[jax.ShapeDtypeStruct((b, t, W), F32)] * 3 + [jax.ShapeDtypeStruct((b, H, DH, DH), F32)],
        scratch_shapes=[pltpu.VMEM((nb, H, DH, DH), F32), pltpu.VMEM((nb, 1, W), F32), pltpu.VMEM((nb, 1, W), F32),
                        pltpu.VMEM((nb, 1, W), F32), pltpu.VMEM((nb, 1, 256), F32)],
        compiler_params=_params(("parallel", "arbitrary")),
        name="rwkv",
    )(p3, p3, p3, p3,
      sh3[:, :, 0:W], sh3[:, :, W:2 * W], sh3[:, :, 2 * W:3 * W], sh3[:, :, 3 * W:],
      mu2[:, 0:W], mu2[:, W:2 * W], mu2[:, 2 * W:3 * W], mu2[:, 3 * W:],
      w0.reshape(1, W), a0.reshape(1, W), wba_bf, gb_bf, kkw.reshape(1, W), kaw.reshape(1, W),
      rk.reshape(1, W), ones_bd, s0)


def _block_diag(w):
    nb, bw, _ = w.shape
    return (jnp.eye(nb, dtype=w.dtype)[:, None, :, None] * w[:, :, None, :]).reshape(nb * bw, nb * bw)


def _chunk(t, target):
    return target if t % target == 0 else t


def _scan_blocking(b, t, target):
    L = _chunk(t, target)
    return L, min(b, 8)


def _run_group(x, mods, st, wts):
    b, t, d = x.shape
    x2 = x.reshape(b * t, d)
    (mc, mn, mm, lh, lconv, ccb, rs, rsh) = st

    sh1, sc1, gt1, sh2, sc2, gt2 = mods[0]
    e = wts["even"]
    p = _in_proj(x2, wts["g_mix"][0], sh1, sc1, e["w_in"], t)
    p3 = p.reshape(b, t, IN_EVEN_PAD)
    gif_t = jnp.transpose(p3[:, :, 6 * W:6 * W + 2 * H], (0, 2, 1))
    hm, c1, n1, m1 = _mlstm(p3, gif_t, e["b_if"], mc[0], mn[0], mm[0], *_scan_blocking(b, t, 128))
    hl, lh1 = _lru(p3, lconv[0], lh[0], e["w_conv"], e["b_conv"], e["w_gate"], e["b_gate"], e["lam"],
                   _chunk(t, 256), min(b, 4 if t >= 256 else 8))
    assert t >= LRU_CONV - 1
    conv1 = p3[:, t - (LRU_CONV - 1):, 4 * W:5 * W]
    x2 = _out_proj(x2, hl.reshape(b * t, W), hm.reshape(b * t, W), p, 3, None, e["g_head"], jnp.zeros((W,), F32),
                   wts["ones_bd"], e["w_out"], gt1, t, True, RMS_EPS, True)
    m = wts["moe"][0]
    x2 = _moe(x2, wts["g_ffn"][0], sh2, sc2, gt2, m["wr_hi"], m["wr_lo"], m["br"], wts["w1"], wts["w3"], wts["w2"],
              0, wts["g_final"], t, False)

    sh1, sc1, gt1, sh2, sc2, gt2 = mods[1]
    o = wts["odd"]
    p = _in_proj(x2, wts["g_mix"][1], sh1, sc1, o["w_in"], t)
    p3 = p.reshape(b, t, IN_ODD)
    cc, cc1 = _conf(p3, ccb[0], o["b_glu"], o["w_dw"], o["b_dw"], o["g_ln"], o["b_ln"], _chunk(t, 256),
                    min(b, 1 if t >= 256 else 8))
    y, bonus, g, s1 = _rwkv(p3, rsh[0], o["mu"], o["w0"], o["a0"], o["wba"], o["gb"], o["kkw"], o["kaw"], o["rk"],
                            wts["ones_bd"], rs[0], *_scan_blocking(b, t, 64))
    sh_out = p3[:, t - 1, 2 * W:]
    x2 = _out_proj(x2, cc.reshape(b * t, W), y.reshape(b * t, W), g.reshape(b * t, W), 0, bonus.reshape(b * t, W),
                   o["g_gn"], o["b_gn"], wts["ones_bd"], o["w_out"], gt1, t, False, RWKV_GN_EPS, False)
    m = wts["moe"][1]
    y2 = _moe(x2, wts["g_ffn"][1], sh2, sc2, gt2, m["wr_hi"], m["wr_lo"], m["br"], wts["w1"], wts["w3"], wts["w2"],
              1, wts["g_final"], t, True)
    states = (c1[None], n1[None], m1.reshape(1, b, H), lh1.reshape(1, b, W), conv1[None], cc1[None], s1[None],
              sh_out[None])
    return y2.reshape(b, t, d), states


def kernel(x_prompt, x_sample, c_prompt, c_sample, state_mlstm_C, state_mlstm_n, state_mlstm_m, state_lru_h,
           cache_lru_conv, cache_conformer_conv, state_rwkv_S, cache_rwkv_shift, w_ada, b_ada, g_norm_mix,
           g_norm_ffn, w_in_even, b_mlstm_if, g_mlstm_head, w_lru_conv, b_lru_conv, w_lru_r, b_lru_r, w_lru_i,
           b_lru_i, lru_lambda, w_out_even, w_in_odd, b_glu, w_cc_dw, b_cc_dw, g_cc_ln, b_cc_ln, rwkv_mu,
           rwkv_w0, rwkv_wB, rwkv_a0, rwkv_aB, rwkv_gB, rwkv_kk, rwkv_ka, rwkv_rk, g_rwkv_gn, b_rwkv_gn,
           w_out_odd, w_router_g, b_router_g, w_router_e, b_router_e, w_exp_gate, w_exp_up, w_exp_down, g_final):
    bp, bs = x_prompt.shape[0], x_sample.shape[0]

    wi = w_in_even[0]
    gcol = 4 * W
    w_in_e = jnp.concatenate([wi[:, :gcol], wi[:, gcol + 2 * H:], wi[:, gcol:gcol + 2 * H],
                              jnp.zeros((D_MODEL, 128 - 2 * H), F32)], axis=1).astype(BF)
    even = dict(
        w_in=w_in_e, b_if=b_mlstm_if[0], g_head=g_mlstm_head[0], w_conv=w_lru_conv[0], b_conv=b_lru_conv[0],
        w_gate=jnp.concatenate([_block_diag(w_lru_r[0]), _block_diag(w_lru_i[0])], axis=1).astype(BF),
        b_gate=jnp.concatenate([b_lru_r[0], b_lru_i[0]]), lam=lru_lambda[0], w_out=w_out_even[0].astype(BF))
    zl = jnp.zeros((LORA_W, W), F32)
    odd = dict(
        w_in=w_in_odd[0].astype(BF), b_glu=b_glu[0], w_dw=w_cc_dw[0], b_dw=b_cc_dw[0], g_ln=g_cc_ln[0],
        b_ln=b_cc_ln[0], mu=rwkv_mu[0], w0=rwkv_w0[0], a0=rwkv_a0[0],
        wba=jnp.concatenate([jnp.concatenate([rwkv_wB[0], zl], axis=1),
                             jnp.concatenate([zl, rwkv_aB[0]], axis=1)], axis=0).astype(BF),
        gb=rwkv_gB[0].astype(BF), kkw=rwkv_kk[0], kaw=rwkv_ka[0], rk=rwkv_rk[0], g_gn=g_rwkv_gn[0],
        b_gn=b_rwkv_gn[0], w_out=w_out_odd[0].astype(BF))
    moe = []
    for l in range(DEPTH):
        wr = jnp.concatenate([w_router_g[l], w_router_e[l],
                              jnp.zeros((D_MODEL, 128 - N_GROUPS - N_EXPERTS), F32)], axis=1)
        wr_hi = wr.astype(BF)
        wr_lo = (wr - wr_hi.astype(F32)).astype(BF)
        br = jnp.concatenate([b_router_g[l], b_router_e[l],
                              jnp.zeros((128 - N_GROUPS - N_EXPERTS,), F32)]).reshape(1, 128)
        moe.append(dict(wr_hi=wr_hi, wr_lo=wr_lo, br=br))
    ones_bd = _block_diag(jnp.ones((H, DH, DH), F32)).astype(BF)
    wts = dict(even=even, odd=odd, moe=moe, g_mix=g_norm_mix, g_ffn=g_norm_ffn, g_final=g_final, ones_bd=ones_bd,
               w1=w_exp_gate.astype(BF).reshape(DEPTH * N_EXPERTS, D_MODEL, D_EXPERT),
               w3=w_exp_up.astype(BF).reshape(DEPTH * N_EXPERTS, D_MODEL, D_EXPERT),
               w2=w_exp_down.astype(BF).reshape(DEPTH * N_EXPERTS, D_EXPERT, D_MODEL))

    mod = _ada(jnp.concatenate([c_prompt, c_sample], axis=0), w_ada, b_ada)

    def mods_of(lo, hi):
        return [tuple(mod[l, lo:hi, j * D_MODEL:(j + 1) * D_MODEL].reshape(hi - lo, 1, D_MODEL) for j in range(6))
                for l in range(DEPTH)]

    z = lambda *s: jnp.zeros(s, F32)
    st_p = (z(1, bp, H, DH, DH), z(1, bp, H, DH), z(1, bp, H), z(1, bp, W), z(1, bp, LRU_CONV - 1, W),
            z(1, bp, CONV_C - 1, W), z(1, bp, H, DH, DH), z(1, bp, 3 * W + LORA_W + LORA_A + LORA_G))
    st_s = (state_mlstm_C, state_mlstm_n, state_mlstm_m, state_lru_h, cache_lru_conv, cache_conformer_conv,
            state_rwkv_S, cache_rwkv_shift)
    y_p, out_p = _run_group(x_prompt, mods_of(0, bp), st_p, wts)
    y_s, out_s = _run_group(x_sample, mods_of(bp, bp + bs), st_s, wts)
    return (y_p, y_s) + tuple(out_p) + tuple(out_s)
```

```python
import functools
from typing import NamedTuple

import jax
import jax.numpy as jnp
from jax import lax
from jax.experimental import pallas as pl
from jax.experimental.pallas import tpu as pltpu

F32 = jnp.float32
BF = jnp.bfloat16

D_MODEL = 1024
DEPTH = 2
H = 8
DH = 64
W = 512
LRU_CONV = 4
LRU_C = 8.0
CONV_C = 31
LORA_W = 64
LORA_A = 64
LORA_G = 128
RWKV_DECAY = 0.606531
RWKV_GN_EPS = 64e-5
N_GROUPS = 4
E_PER_GROUP = 4
N_EXPERTS = 16
D_EXPERT = 256
RMS_EPS = 1e-6
LN_EPS = 1e-5
LANES = 128
IN_EVEN_PAD = 6 * W + LANES
IN_ODD = 2 * W + 3 * W + LORA_W + LORA_A + LORA_G

ROW_TILE = 1024
MOE_ROW_TILE = 512
VMEM_LIMIT = 48 * 1024 * 1024


def _mm(a, b):
    return jnp.dot(a.astype(BF), b.astype(BF), preferred_element_type=F32)


def _mm_nt(a, b):
    return lax.dot_general(a.astype(BF), b.astype(BF), (((1,), (1,)), ((), ())),
                           preferred_element_type=F32)


def _mm_tn(a, b):
    return lax.dot_general(a.astype(BF), b.astype(BF), (((0,), (0,)), ((), ())),
                           preferred_element_type=F32)


def _split3(x):
    hi = x.astype(BF)
    r = x - hi.astype(F32)
    mid = r.astype(BF)
    lo = (r - mid.astype(F32)).astype(BF)
    return hi, mid, lo


def _mm_mask_l(mask, x):
    return sum(jnp.dot(mask, p, preferred_element_type=F32) for p in _split3(x))


def _mm_mask_r(x, mask):
    return sum(jnp.dot(p, mask, preferred_element_type=F32) for p in _split3(x))


def _sigmoid(x):
    return 1.0 / (1.0 + jnp.exp(-x))


def _rows(v, tm):
    nb, _, c = v.shape
    if nb == 1:
        return v[0]
    return jnp.broadcast_to(v, (nb, tm // nb, c)).reshape(tm, c)


class _Mod(NamedTuple):
    arr: jax.Array
    row0: int
    col: int


def _mod_spec(m, T, tm):
    if tm <= T:
        per = T // tm
        return pl.BlockSpec((1, 1, D_MODEL), lambda i, *_: (m.row0 + i // per, 0, m.col))
    nbk = tm // T
    assert m.row0 % nbk == 0
    return pl.BlockSpec((nbk, 1, D_MODEL), lambda i, *_: (m.row0 // nbk + i, 0, m.col))


def _norm_mod(x, g, sh, sc):
    y = x * lax.rsqrt(jnp.mean(x * x, axis=-1, keepdims=True) + RMS_EPS) * g
    return y * (1.0 + sc) + sh


def _params(sem):
    return pltpu.CompilerParams(dimension_semantics=sem, vmem_limit_bytes=VMEM_LIMIT)


def _ada_kernel(c_ref, w_ref, b_ref, o_ref):
    o_ref[0] = _mm(c_ref[...], w_ref[0]) + b_ref[0]


def _ada(c_all, w, b):
    nb = c_all.shape[0]
    tn = 1536
    return pl.pallas_call(
        _ada_kernel,
        grid=(DEPTH, 6 * D_MODEL // tn),
        in_specs=[pl.BlockSpec((nb, D_MODEL), lambda l, j: (0, 0)),
                  pl.BlockSpec((1, D_MODEL, tn), lambda l, j: (l, 0, j)),
                  pl.BlockSpec((1, 1, tn), lambda l, j: (l, 0, j))],
        out_specs=pl.BlockSpec((1, nb, tn), lambda l, j: (l, 0, j)),
        out_shape=jax.ShapeDtypeStruct((DEPTH, nb, 6 * D_MODEL), F32),
        compiler_params=_params(("parallel", "parallel")),
        name="ada",
    )(c_all, w, b.reshape(DEPTH, 1, 6 * D_MODEL))


def _inproj_kernel(x_ref, g_ref, sh_ref, sc_ref, w_ref, o_ref, h_scr):
    tm = x_ref.shape[0]

    @pl.when(pl.program_id(1) == 0)
    def _():
        h = _norm_mod(x_ref[...], g_ref[...], _rows(sh_ref[...], tm), _rows(sc_ref[...], tm))
        h_scr[...] = h.astype(BF)

    o_ref[...] = jnp.dot(h_scr[...], w_ref[...], preferred_element_type=F32)


INPROJ_ROW_TILE = 512


def _in_proj(x2, g, sh, sc, w_bf, T):
    n, d = x2.shape
    cols = w_bf.shape[1]
    tm = INPROJ_ROW_TILE
    tn = cols
    return pl.pallas_call(
        _inproj_kernel,
        grid=(n // tm, cols // tn),
        in_specs=[pl.BlockSpec((tm, d), lambda i, j: (i, 0)),
                  pl.BlockSpec((1, d), lambda i, j: (0, 0)),
                  _mod_spec(sh, T, tm), _mod_spec(sc, T, tm),
                  pl.BlockSpec((d, tn), lambda i, j: (0, j))],
        out_specs=pl.BlockSpec((tm, tn), lambda i, j: (i, j)),
        out_shape=jax.ShapeDtypeStruct((n, cols), F32),
        scratch_shapes=[pltpu.VMEM((tm, d), BF)],
        compiler_params=_params(("parallel", "arbitrary")),
        name="in_proj",
    )(x2, g.reshape(1, d), sh.arr, sc.arr, w_bf)


def _seg_sum(x, ones_bd):
    hi = x.astype(BF)
    lo = (x - hi.astype(F32)).astype(BF)
    return (jnp.dot(hi, ones_bd, preferred_element_type=F32)
            + jnp.dot(lo, ones_bd, preferred_element_type=F32))


def _outproj_kernel(*refs, pre_first, eps, has_add, sigmoid_mul):
    if has_add:
        x_ref, a_ref, pre_ref, mul_ref, add_ref, gain_ref, bias_ref, ones_ref, w_ref, gt_ref, o_ref = refs
    else:
        x_ref, a_ref, pre_ref, mul_ref, gain_ref, bias_ref, ones_ref, w_ref, gt_ref, o_ref = refs
    tm = x_ref.shape[0]
    pre = pre_ref[...]
    ones_bd = ones_ref[...]
    dev = pre - _seg_sum(pre, ones_bd) * (1.0 / DH)
    var = _seg_sum(dev * dev, ones_bd) * (1.0 / DH)
    y = dev * lax.rsqrt(var + eps) * gain_ref[...] + bias_ref[...]
    if has_add:
        y = y + add_ref[...]
    m = mul_ref[...]
    y = (y * (_sigmoid(m) if sigmoid_mul else m)).astype(BF)
    first, second = (y, a_ref[...]) if pre_first else (a_ref[...], y)
    mix = (jnp.dot(first, w_ref[0:W, :], preferred_element_type=F32)
           + jnp.dot(second, w_ref[W:2 * W, :], preferred_element_type=F32))
    o_ref[...] = x_ref[...] + _rows(gt_ref[...], tm) * mix


def _out_proj(x2, a2, pre2, mul2, mul_col, add2, gain, bias, ones_bd, w_bf, gt, T, pre_first, eps, sigmoid_mul):
    n, d = x2.shape
    tm = ROW_TILE
    has_add = add2 is not None
    row = lambda c=0: pl.BlockSpec((tm, W), lambda i, c=c: (i, c))
    vec = lambda: pl.BlockSpec((1, W), lambda i: (0, 0))
    in_specs = [pl.BlockSpec((tm, d), lambda i: (i, 0)), row(), row(), row(mul_col)]
    args = [x2, a2, pre2, mul2]
    if has_add:
        in_specs.append(row())
        args.append(add2)
    in_specs += [vec(), vec(), pl.BlockSpec((W, W), lambda i: (0, 0)), pl.BlockSpec((2 * W, d), lambda i: (0, 0)),
                 _mod_spec(gt, T, tm)]
    args += [gain.reshape(1, W), bias.reshape(1, W), ones_bd, w_bf, gt.arr]
    return pl.pallas_call(
        functools.partial(_outproj_kernel, pre_first=pre_first, eps=eps, has_add=has_add, sigmoid_mul=sigmoid_mul),
        grid=(n // tm,),
        in_specs=in_specs,
        out_specs=pl.BlockSpec((tm, d), lambda i: (i, 0)),
        out_shape=jax.ShapeDtypeStruct((n, d), F32),
        compiler_params=_params(("parallel",)),
        name="out_proj",
    )(*args)


def _route(logits):
    lane = lax.broadcasted_iota(jnp.int32, logits.shape, 1).astype(F32)
    neg = -jnp.inf
    is_g = lane < N_GROUPS
    lg = jnp.where(is_g, logits, neg)
    mg = jnp.max(lg, axis=1, keepdims=True)
    gsel = jnp.min(jnp.where(lg == mg, lane, float(LANES)), axis=1, keepdims=True)
    psum = jnp.sum(jnp.where(is_g, jnp.exp(lg - mg), 0.0), axis=1, keepdims=True)
    pg_sel = 1.0 / psum
    lo = N_GROUPS + E_PER_GROUP * gsel
    le = jnp.where((lane >= lo) & (lane < lo + E_PER_GROUP), logits, neg)
    v1 = jnp.max(le, axis=1, keepdims=True)
    i1 = jnp.min(jnp.where(le == v1, lane, float(LANES)), axis=1, keepdims=True)
    le2 = jnp.where(lane == i1, neg, le)
    v2 = jnp.max(le2, axis=1, keepdims=True)
    i2 = jnp.min(jnp.where(le2 == v2, lane, float(LANES)), axis=1, keepdims=True)
    e2 = jnp.exp(v2 - v1)
    p1 = 1.0 / (1.0 + e2)
    p2 = e2 / (1.0 + e2)
    return pg_sel * jnp.where(lane == i1, p1, jnp.where(lane == i2, p2, 0.0))


def _moe_kernel(x_ref, g_ref, sh_ref, sc_ref, gt_ref, wrh_ref, wrl_ref, br_ref, w1_ref, w3_ref, w2_ref,
                gf_ref, o_ref, *, final_norm):
    tm = x_ref.shape[0]
    h = _norm_mod(x_ref[...], g_ref[...], _rows(sh_ref[...], tm), _rows(sc_ref[...], tm))
    hb = h.astype(BF)
    hl = (h - hb.astype(F32)).astype(BF)
    logits = (jnp.dot(hb, wrh_ref[...], preferred_element_type=F32)
              + jnp.dot(hl, wrh_ref[...], preferred_element_type=F32)
              + jnp.dot(hb, wrl_ref[...], preferred_element_type=F32)) + br_ref[...]
    gate = _route(logits)
    lane = lax.broadcasted_iota(jnp.int32, gate.shape, 1)
    acc = None
    for e in range(N_EXPERTS):
        hg = jnp.dot(hb, w1_ref[e], preferred_element_type=F32)
        hu = jnp.dot(hb, w3_ref[e], preferred_element_type=F32)
        ge = jnp.sum(jnp.where(lane == e + N_GROUPS, gate, 0.0), axis=1, keepdims=True)
        hh = hg * _sigmoid(hg) * hu * ge
        part = jnp.dot(hh.astype(BF), w2_ref[e], preferred_element_type=F32)
        acc = part if acc is None else acc + part
    y = x_ref[...] + _rows(gt_ref[...], tm) * acc
    if final_norm:
        y = y * lax.rsqrt(jnp.mean(y * y, axis=-1, keepdims=True) + RMS_EPS) * gf_ref[...]
    o_ref[...] = y


def _moe(x2, g, sh, sc, gt, wr_hi, wr_lo, br, w1, w3, w2, layer, g_final, T, final_norm):
    n, d = x2.shape
    tm = MOE_ROW_TILE
    resident = lambda shape: pl.BlockSpec(shape, lambda i: (layer, 0, 0), pipeline_mode=pl.Buffered(1))
    return pl.pallas_call(
        functools.partial(_moe_kernel, final_norm=final_norm),
        grid=(n // tm,),
        in_specs=[pl.BlockSpec((tm, d), lambda i: (i, 0)),
                  pl.BlockSpec((1, d), lambda i: (0, 0)),
                  _mod_spec(sh, T, tm), _mod_spec(sc, T, tm), _mod_spec(gt, T, tm),
                  pl.BlockSpec((d, LANES), lambda i: (0, 0)),
                  pl.BlockSpec((d, LANES), lambda i: (0, 0)),
                  pl.BlockSpec((1, LANES), lambda i: (0, 0)),
                  resident((N_EXPERTS, d, D_EXPERT)), resident((N_EXPERTS, d, D_EXPERT)),
                  resident((N_EXPERTS, D_EXPERT, d)),
                  pl.BlockSpec((1, d), lambda i: (0, 0))],
        out_specs=pl.BlockSpec((tm, d), lambda i: (i, 0)),
        out_shape=jax.ShapeDtypeStruct((n, d), F32),
        compiler_params=_params(("parallel",)),
        name="moe",
    )(x2, g.reshape(1, d), sh.arr, sc.arr, gt.arr, wr_hi, wr_lo, br, w1, w3, w2, g_final.reshape(1, d))


def _log_sigmoid(x):
    return jnp.minimum(x, 0.0) - jnp.log(1.0 + jnp.exp(-jnp.abs(x)))


def _mlstm_kernel(q_ref, k_ref, v_ref, g_ref, gt_ref, bif_ref, bift_ref, c0_ref, n0_ref, m0_ref,
                  h_ref, c1_ref, n1_ref, m1_ref, st_scr, m_scr, *, L, nb, nc):
    c = pl.program_id(1)
    NP = H // 2
    r64 = lax.broadcasted_iota(jnp.int32, (DH, DH), 0)
    c64 = lax.broadcasted_iota(jnp.int32, (DH, DH), 1)
    eye64 = r64 == c64

    @pl.when(c == 0)
    def _():
        st_scr[...] = jnp.zeros_like(st_scr)
        for ib in range(nb):
            for h in range(H):
                p, o = divmod(h, 2)
                rs = slice(o * DH, (o + 1) * DH)
                st_scr[ib, p, rs, o * DH:(o + 1) * DH] = c0_ref[ib, h]
                n_col = jnp.sum(jnp.where(eye64, n0_ref[ib, h:h + 1, :], 0.0), axis=1, keepdims=True)
                st_scr[ib, p, rs, 2 * DH + o * DH:2 * DH + (o + 1) * DH] = jnp.broadcast_to(n_col, (DH, DH))
        m_scr[...] = m0_ref[...]

    row = lax.broadcasted_iota(jnp.int32, (L, L), 0)
    col = lax.broadcasted_iota(jnp.int32, (L, L), 1)
    causal = row >= col
    tri = jnp.where(causal, 1.0, 0.0).astype(BF)
    tri_u = jnp.where(row <= col, 1.0, 0.0).astype(BF)
    even = lax.broadcasted_iota(jnp.int32, (L, 2 * DH), 1) < DH
    row_s = lax.broadcasted_iota(jnp.int32, (2 * DH, 4 * DH), 0)
    lane_s = lax.broadcasted_iota(jnp.int32, (2 * DH, 4 * DH), 1)
    top = row_s < DH
    same_head = jnp.where(top, 0, DH) == jnp.bitwise_and(lane_s, DH)
    ones_l = jnp.ones((L, 2 * DH), BF)
    ibs = range(nb)
    g = [g_ref[ib] + bif_ref[...] for ib in ibs]
    gt = [gt_ref[ib] + bift_ref[...] for ib in ibs]
    bcum = [_mm_mask_l(tri, _log_sigmoid(g[ib])) for ib in ibs]
    bcum_t = [_mm_mask_r(_log_sigmoid(gt[ib]), tri_u) for ib in ibs]
    m_prev = [m_scr[ib] for ib in ibs]
    ch = [(ib, h) for ib in ibs for h in range(H)]
    prs = [(ib, p) for ib in ibs for p in range(NP)]
    pair_of = lambda u: (u[0], u[1] // 2)
    wide = lambda x: jnp.broadcast_to(x, (L, 2 * DH))
    lanes = {u: slice(u[1] * 2 * DH, (u[1] + 1) * 2 * DH) for u in prs}
    q_f = {u: q_ref[u[0], :, lanes[u]] for u in prs}
    q_b = {u: q_f[u].astype(BF) for u in prs}
    k_f = {u: k_ref[u[0], :, lanes[u]] * (DH ** -0.5) for u in prs}
    k_b = {u: k_f[u].astype(BF) for u in prs}
    rhs = {u: jnp.concatenate([v_ref[u[0], :, lanes[u]].astype(BF), ones_l], axis=1) for u in prs}
    odd = lax.broadcasted_iota(jnp.int32, (L, 2 * DH), 1) >= DH
    qk = {u: _mm_nt(jnp.where(odd if u[1] % 2 else even, q_f[pair_of(u)], 0.0).astype(BF), k_b[pair_of(u)])
          for u in ch}
    st = {u: st_scr[u[0], u[1]] for u in prs}
    qst = {u: _mm(q_b[u], st[u]) for u in prs}
    bc = {u: wide(bcum[u[0]][:, H + u[1]:H + u[1] + 1]) for u in ch}
    ic = {u: wide(g[u[0]][:, u[1]:u[1] + 1]) for u in ch}
    mp = {u: m_prev[u[0]][:, u[1]:u[1] + 1] for u in ch}
    dmat = {u: jnp.where(causal, bc[u][:, 0:L] + (gt[u[0]][u[1]:u[1] + 1, :] - bcum_t[u[0]][H + u[1]:H + u[1] + 1, :]),
                         -jnp.inf) for u in ch}
    g_inter = {u: bc[u] + mp[u] for u in ch}
    m_t = {u: jnp.maximum(g_inter[u], jnp.max(dmat[u], axis=1, keepdims=True)) for u in ch}
    s = {u: (qk[u] * jnp.exp(dmat[u] - m_t[u][:, 0:L])).astype(BF) for u in ch}
    w_inter = {u: jnp.exp(g_inter[u] - m_t[u]) for u in ch}
    sv = {u: _mm(s[u], rhs[pair_of(u)]) for u in ch}
    m_new = {u: m_t[u][L - 1:L, 0:1] for u in ch}
    b_last = {u: bcum[u[0]][L - 1:L, H + u[1]:H + u[1] + 1] for u in ch}
    w_s = {u: jnp.exp(b_last[u] - bc[u] + ic[u] - m_new[u]) for u in ch}
    decay = {u: jnp.exp(b_last[u] + mp[u] - m_new[u]) for u in ch}
    pick = lambda d, u: jnp.where(even, d[u[0], 2 * u[1]], d[u[0], 2 * u[1] + 1])
    hh = {}
    for u in prs:
        e, o = (u[0], 2 * u[1]), (u[0], 2 * u[1] + 1)
        w_pair = pick(w_inter, u)
        num = jnp.where(even, sv[e][:, 0:2 * DH], sv[o][:, 0:2 * DH]) + w_pair * qst[u][:, 0:2 * DH]
        den = jnp.where(even, sv[e][:, 2 * DH:], sv[o][:, 2 * DH:]) + w_pair * qst[u][:, 2 * DH:]
        hh[u] = num / jnp.maximum(jnp.abs(den), jnp.exp(-pick(m_t, u)))
        inc = _mm_tn((k_f[u] * pick(w_s, u)).astype(BF), rhs[u])
        st_scr[u[0], u[1]] = jnp.where(top, decay[e], decay[o]) * st[u] + jnp.where(same_head, inc, 0.0)
    for ib in ibs:
        h_ref[ib] = jnp.concatenate([hh[ib, p] for p in range(NP)], axis=1)
        m_scr[ib] = jnp.concatenate([m_new[ib, h] for h in range(H)], axis=1)

    @pl.when(c == nc - 1)
    def _():
        for ib in range(nb):
            for h in range(H):
                p, o = divmod(h, 2)
                rs = slice(o * DH, (o + 1) * DH)
                c1_ref[ib, h] = st_scr[ib, p, rs, o * DH:(o + 1) * DH]
                n_rep = st_scr[ib, p, rs, 2 * DH + o * DH:2 * DH + (o + 1) * DH]
                n1_ref[ib, h:h + 1, :] = jnp.sum(jnp.where(eye64, n_rep, 0.0), axis=0, keepdims=True)
        m1_ref[...] = m_scr[...]


def _mlstm(p3, gif_t, bif, c0, n0, m0, L, nb):
    b, t, _ = p3.shape
    nc = t // L
    col = lambda j: pl.BlockSpec((nb, L, W), lambda i, c, j=j: (i, c, j))
    bif_pad = jnp.zeros((1, LANES), F32).at[0, :2 * H].set(bif)
    return pl.pallas_call(
        functools.partial(_mlstm_kernel, L=L, nb=nb, nc=nc),
        grid=(b // nb, nc),
        in_specs=[col(0), col(1), col(2),
                  pl.BlockSpec((nb, L, LANES), lambda i, c: (i, c, 6 * W // LANES)),
                  pl.BlockSpec((nb, 2 * H, L), lambda i, c: (i, 0, c)),
                  pl.BlockSpec((1, LANES), lambda i, c: (0, 0)),
                  pl.BlockSpec((2 * H, 1), lambda i, c: (0, 0)),
                  pl.BlockSpec((nb, H, DH, DH), lambda i, c: (i, 0, 0, 0)),
                  pl.BlockSpec((nb, H, DH), lambda i, c: (i, 0, 0)),
                  pl.BlockSpec((nb, 1, H), lambda i, c: (i, 0, 0))],
        out_specs=[pl.BlockSpec((nb, L, W), lambda i, c: (i, c, 0)),
                   pl.BlockSpec((nb, H, DH, DH), lambda i, c: (i, 0, 0, 0)),
                   pl.BlockSpec((nb, H, DH), lambda i, c: (i, 0, 0)),
                   pl.BlockSpec((nb, 1, H), lambda i, c: (i, 0, 0))],
        out_shape=[jax.ShapeDtypeStruct((b, t, W), F32),
                   jax.ShapeDtypeStruct((b, H, DH, DH), F32),
                   jax.ShapeDtypeStruct((b, H, DH), F32),
                   jax.ShapeDtypeStruct((b, 1, H), F32)],
        scratch_shapes=[pltpu.VMEM((nb, H // 2, 2 * DH, 4 * DH), F32), pltpu.VMEM((nb, 1, H), F32)],
        compiler_params=_params(("parallel", "arbitrary")),
        name="mlstm",
    )(p3, p3, p3, p3, gif_t, bif_pad, bif.reshape(2 * H, 1), c0, n0, m0.reshape(b, 1, H))


def _gelu_tanh(x):
    return 0.5 * x * (1.0 + jnp.tanh(0.7978845608028654 * (x + 0.044715 * x * x * x)))


def _lru_kernel(xr_ref, xg_ref, cache_ref, h0_ref, wc_ref, bc_ref, wg_ref, bg_ref, lam_ref,
                o_ref, h1_ref, ext_scr, a_scr, u_scr, hs_scr, hc_scr, *, Tc, nb, nc):
    c = pl.program_id(1)
    K1 = LRU_CONV - 1

    @pl.when(c == 0)
    def _():
        ext_scr[:, 8 - K1:8, :] = cache_ref[...]
        hc_scr[...] = h0_ref[...]

    xcs = []
    for ib in range(nb):
        x = xr_ref[ib]
        ext_scr[ib, 8:8 + Tc, :] = x
        xc = bc_ref[...] + wc_ref[K1:K1 + 1, :] * x
        for d in range(1, LRU_CONV):
            xc = xc + wc_ref[K1 - d:K1 - d + 1, :] * ext_scr[ib, 8 - d:8 - d + Tc, :]
        ext_scr[ib, 8 - K1:8, :] = ext_scr[ib, 8 + Tc - K1:8 + Tc, :]
        xcs.append(xc)
    xc = jnp.concatenate(xcs, axis=0)
    gates = _mm(xc, wg_ref[...]) + bg_ref[...]
    r = _sigmoid(gates[:, 0:W])
    ig = _sigmoid(gates[:, W:2 * W])
    lam = lam_ref[...]
    softplus_neg = jnp.maximum(-lam, 0.0) + jnp.log(1.0 + jnp.exp(-jnp.abs(lam)))
    log_a = -LRU_C * r * softplus_neg
    a_scr[...] = jnp.exp(log_a).reshape(nb, Tc, W)
    th = jnp.tanh(log_a)
    one_minus_a2 = -2.0 * th / (1.0 - th)
    u_scr[...] = (jnp.sqrt(one_minus_a2) * (ig * xc)).reshape(nb, Tc, W)

    def body(t, hs):
        new = []
        for ib in range(nb):
            h = a_scr[ib, pl.ds(t, 1), :] * hs[ib] + u_scr[ib, pl.ds(t, 1), :]
            hs_scr[ib, pl.ds(t, 1), :] = h
            new.append(h)
        return tuple(new)

    h_fin = lax.fori_loop(0, Tc, body, tuple(hc_scr[ib] for ib in range(nb)), unroll=8)
    for ib in range(nb):
        hc_scr[ib] = h_fin[ib]
        o_ref[ib] = (hs_scr[ib] * _gelu_tanh(xg_ref[ib])).astype(BF)

    @pl.when(c == nc - 1)
    def _():
        h1_ref[...] = hc_scr[...]


def _lru(p3, cache, h0, w_conv, b_conv, wg_bf, bg, lam, Tc, nb):
    b, t, _ = p3.shape
    nc = t // Tc
    return pl.pallas_call(
        functools.partial(_lru_kernel, Tc=Tc, nb=nb, nc=nc),
        grid=(b // nb, nc),
        in_specs=[pl.BlockSpec((nb, Tc, W), lambda i, c: (i, c, 4)),
                  pl.BlockSpec((nb, Tc, W), lambda i, c: (i, c, 5)),
                  pl.BlockSpec((nb, LRU_CONV - 1, W), lambda i, c: (i, 0, 0)),
                  pl.BlockSpec((nb, 1, W), lambda i, c: (i, 0, 0)),
                  pl.BlockSpec((LRU_CONV, W), lambda i, c: (0, 0)),
                  pl.BlockSpec((1, W), lambda i, c: (0, 0)),
                  pl.BlockSpec((W, 2 * W), lambda i, c: (0, 0)),
                  pl.BlockSpec((1, 2 * W), lambda i, c: (0, 0)),
                  pl.BlockSpec((1, W), lambda i, c: (0, 0))],
        out_specs=[pl.BlockSpec((nb, Tc, W), lambda i, c: (i, c, 0)),
                   pl.BlockSpec((nb, 1, W), lambda i, c: (i, 0, 0))],
        out_shape=[jax.ShapeDtypeStruct((b, t, W), BF), jax.ShapeDtypeStruct((b, 1, W), F32)],
        scratch_shapes=[pltpu.VMEM((nb, 8 + Tc, W), F32), pltpu.VMEM((nb, Tc, W), F32), pltpu.VMEM((nb, Tc, W), F32),
                        pltpu.VMEM((nb, Tc, W), F32), pltpu.VMEM((nb, 1, W), F32)],
        compiler_params=_params(("parallel", "arbitrary")),
        name="lru",
    )(p3, p3, cache, h0.reshape(b, 1, W), w_conv, b_conv.reshape(1, W), wg_bf, bg.reshape(1, 2 * W),
      lam.reshape(1, W))


CONF_ROWS = 32


def _conf_kernel(u_ref, gte_ref, cache_ref, bu_ref, bg_ref, wdw_ref, bdw_ref, gln_ref, bln_ref,
                 o_ref, cache1_ref, ext_scr, sh_scr, *, Tc, nb, nc):
    c = pl.program_id(1)
    K1 = CONV_C - 1
    base = 32 - K1

    @pl.when(c == 0)
    def _():
        ext_scr[:, base:32, :] = cache_ref[...]

    for ib in range(nb):
        u = u_ref[ib] + bu_ref[...]
        gte = gte_ref[ib] + bg_ref[...]
        ext_scr[ib, 32:32 + Tc, :] = u * _sigmoid(gte)
        for b in range(1, 8):
            sh_scr[b - 1] = ext_scr[ib, b:b + Tc + 24, :]

        def window(off, r0, rb):
            a, b = divmod(off, 8)
            if b == 0:
                return ext_scr[ib, 8 * a + r0:8 * a + r0 + rb, :]
            return sh_scr[b - 1, 8 * a + r0:8 * a + r0 + rb, :]

        rb = min(CONF_ROWS, Tc)
        for r0 in range(0, Tc, rb):
            acc = bdw_ref[...] + wdw_ref[0:1, :] * window(base, r0, rb)
            for j in range(1, CONV_C):
                acc = acc + wdw_ref[j:j + 1, :] * window(base + j, r0, rb)
            mu = jnp.mean(acc, axis=1, keepdims=True)
            var = jnp.mean(jnp.square(acc - mu), axis=1, keepdims=True)
            y = (acc - mu) * lax.rsqrt(var + LN_EPS) * gln_ref[...] + bln_ref[...]
            o_ref[ib, r0:r0 + rb, :] = (y * _sigmoid(y)).astype(BF)
        ext_scr[ib, base:32, :] = ext_scr[ib, base + Tc:32 + Tc, :]

    @pl.when(c == nc - 1)
    def _():
        cache1_ref[...] = ext_scr[:, base:32, :]


def _conf(p3, cache, b_glu, w_dw, b_dw, g_ln, b_ln, Tc, nb):
    b, t, _ = p3.shape
    nc = t // Tc
    vec = lambda: pl.BlockSpec((1, W), lambda i, c: (0, 0))
    return pl.pallas_call(
        functools.partial(_conf_kernel, Tc=Tc, nb=nb, nc=nc),
        grid=(b // nb, nc),
        in_specs=[pl.BlockSpec((nb, Tc, W), lambda i, c: (i, c, 0)),
                  pl.BlockSpec((nb, Tc, W), lambda i, c: (i, c, 1)),
                  pl.BlockSpec((nb, CONV_C - 1, W), lambda i, c: (i, 0, 0)),
                  vec(), pl.BlockSpec((1, W), lambda i, c: (0, 1)),
                  pl.BlockSpec((CONV_C, W), lambda i, c: (0, 0)),
                  vec(), vec(), vec()],
        out_specs=[pl.BlockSpec((nb, Tc, W), lambda i, c: (i, c, 0)),
                   pl.BlockSpec((nb, CONV_C - 1, W), lambda i, c: (i, 0, 0))],
        out_shape=[jax.ShapeDtypeStruct((b, t, W), BF), jax.ShapeDtypeStruct((b, CONV_C - 1, W), F32)],
        scratch_shapes=[pltpu.VMEM((nb, 32 + Tc, W), F32), pltpu.VMEM((7, Tc + 24, W), F32)],
        compiler_params=_params(("parallel", "arbitrary")),
        name="conformer",
    )(p3, p3, cache, b_glu.reshape(1, 2 * W), b_glu.reshape(1, 2 * W), w_dw, b_dw.reshape(1, W),
      g_ln.reshape(1, W), b_ln.reshape(1, W))


def _rwkv_prep(x_refs, prev_scrs, mu_refs, w0_ref, a0_ref, wba_ref, gb_ref, kkw_ref, kaw_ref, rk_ref, ones_ref,
               L, nb):
    def shift_mix(x_ref, prev_scr, mu_ref):
        x = x_ref[...].reshape(nb * L, x_ref.shape[-1])
        first = jnp.bitwise_and(lax.broadcasted_iota(jnp.int32, x.shape, 0), L - 1) == 0
        carried = jnp.concatenate([jnp.broadcast_to(prev_scr[ib], (L, x.shape[1])) for ib in range(nb)], axis=0)
        prev = jnp.where(first, carried, pltpu.roll(x, 1, 0))
        for ib in range(nb):
            prev_scr[ib] = x[(ib + 1) * L - 1:(ib + 1) * L, :]
        return x + (prev - x) * mu_ref[...]

    r, k, v, z = (shift_mix(x, p, m) for x, p, m in zip(x_refs, prev_scrs, mu_refs))
    zwa = z[:, 0:128]
    lane = lax.broadcasted_iota(jnp.int32, zwa.shape, 1)
    wa = _mm(jnp.where(lane < LORA_W, jnp.tanh(zwa), zwa), wba_ref[...])
    lw = -RWKV_DECAY * _sigmoid(w0_ref[...] + wa[:, 0:W])
    a = _sigmoid(a0_ref[...] + wa[:, W:2 * W])
    g = _mm(_sigmoid(z[:, 128:256]), gb_ref[...])
    ones_bd = ones_ref[...]
    kk = k * kkw_ref[...]
    kk = kk * lax.rsqrt(_seg_sum(kk * kk, ones_bd) + 1e-12)
    k2 = k * (1.0 + (a - 1.0) * kaw_ref[...])
    bonus = _seg_sum(r * k2 * rk_ref[...], ones_bd) * v
    return dict(r=r, lw=lw, k=k2, v=v, kk=kk, b=kk * a), bonus, g


def _rwkv_kernel(pr_ref, pk_ref, pv_ref, pz_ref, sr_ref, sk_ref, sv_ref, sz_ref, mr_ref, mk_ref, mv_ref, mz_ref,
                 w0_ref, a0_ref, wba_ref, gb_ref, kkw_ref, kaw_ref, rk_ref, ones_ref, s0_ref,
                 y_ref, bonus_ref, g_ref, s1_ref, s_scr, qr_scr, qk_scr, qv_scr, qz_scr, *, L, nb, nc):
    c = pl.program_id(1)

    @pl.when(c == 0)
    def _():
        s_scr[...] = s0_ref[...]
        qr_scr[...] = sr_ref[...]
        qk_scr[...] = sk_ref[...]
        qv_scr[...] = sv_ref[...]
        qz_scr[...] = sz_ref[...]

    vals, bonus, g = _rwkv_prep((pr_ref, pk_ref, pv_ref, pz_ref), (qr_scr, qk_scr, qv_scr, qz_scr),
                                (mr_ref, mk_ref, mv_ref, mz_ref), w0_ref, a0_ref, wba_ref, gb_ref, kkw_ref,
                                kaw_ref, rk_ref, ones_ref, L, nb)
    bonus_ref[...] = bonus.reshape(nb, L, W)
    g_ref[...] = g.reshape(nb, L, W)
    chunk = lambda name, ib: vals[name][ib * L:(ib + 1) * L, :]
    row = lax.broadcasted_iota(jnp.int32, (L, L), 0)
    col = lax.broadcasted_iota(jnp.int32, (L, L), 1)
    eye = jnp.where(row == col, 1.0, 0.0)
    tri = jnp.where(row >= col, 1.0, 0.0).astype(BF)
    row2 = lax.broadcasted_iota(jnp.int32, (2 * L, 2 * L), 0)
    col2 = lax.broadcasted_iota(jnp.int32, (2 * L, 2 * L), 1)
    cc = jnp.where(col2 >= L, col2 - L, col2)
    keep = jnp.where(row2 < L, row2 - 1, row2 - L) >= cc
    sls = [slice(h * DH, (h + 1) * DH) for h in range(H)]
    chains = [(ib, h) for ib in range(nb) for h in range(H)]
    pre = {}
    for ib in range(nb):
        lw = chunk("lw", ib)
        cum = _mm_mask_l(tri, lw)
        p_in = jnp.exp(cum)
        p_inv = jnp.exp(-cum)
        p_last = p_in[L - 1:L, :]
        kh = chunk("k", ib) * p_inv
        bh = chunk("b", ib) * p_inv
        pre[ib] = dict(
            kr=jnp.concatenate([(chunk("kk", ib) * jnp.exp(cum - lw)).astype(BF),
                                (chunk("r", ib) * p_in).astype(BF)], axis=0),
            kb=jnp.concatenate([kh.astype(BF), bh.astype(BF)], axis=0),
            kbl=jnp.concatenate([(kh * p_last).astype(BF), (bh * p_last).astype(BF)], axis=0),
            v=chunk("v", ib).astype(BF), p_last=p_last)
    op = lambda u, name: pre[u[0]][name][:, sls[u[1]]]
    gm = {u: jnp.where(keep, _mm_nt(op(u, "kr"), op(u, "kb")), 0.0) for u in chains}
    akr = {u: gm[u][:, 0:L].astype(BF) for u in chains}
    r_b = {u: gm[u][L:2 * L, L:2 * L].astype(BF) for u in chains}
    xs = {u: -gm[u][0:L, L:2 * L] for u in chains}
    invs = {u: eye + xs[u] for u in chains}
    if L > 2:
        xs = {u: _mm(xs[u], xs[u]) for u in chains}
    n = 4
    while n < L:
        st = {u: _mm(jnp.concatenate([invs[u], xs[u]], axis=0), xs[u]) for u in chains}
        invs = {u: invs[u] + st[u][0:L] for u in chains}
        xs = {u: st[u][L:2 * L] for u in chains}
        n *= 2
    if L > 2:
        invs = {u: invs[u] + _mm(invs[u], xs[u]) for u in chains}
    akrv = {u: _mm(akr[u], op(u, "v")) for u in chains}
    ss = {u: s_scr[u] for u in chains}
    krs = {u: _mm_nt(op(u, "kr"), ss[u]) for u in chains}
    us = {u: _mm(invs[u], krs[u][0:L] + akrv[u][0:L]) for u in chains}
    ys = {u: krs[u][L:2 * L] + akrv[u][L:2 * L] - _mm(r_b[u], us[u]) for u in chains}
    for u in chains:
        vu = jnp.concatenate([op(u, "v"), (-us[u]).astype(BF)], axis=0)
        s_scr[u] = ss[u] * pre[u[0]]["p_last"][:, sls[u[1]]] + _mm_tn(vu, op(u, "kbl"))
    for ib in range(nb):
        y_ref[ib] = jnp.concatenate([ys[ib, h] for h in range(H)], axis=1)

    @pl.when(c == nc - 1)
    def _():
        s1_ref[...] = s_scr[...]


def _rwkv(p3, shift, mu, w0, a0, wba_bf, gb_bf, kkw, kaw, rk, ones_bd, s0, L, nb):
    b, t, _ = p3.shape
    nc = t // L
    vec = lambda: pl.BlockSpec((1, W), lambda i, c: (0, 0))
    carry = lambda width, j: pl.BlockSpec((nb, 1, width), lambda i, c: (i, 0, j))
    mix = lambda width, j: pl.BlockSpec((1, width), lambda i, c: (0, j))
    blk = lambda: pl.BlockSpec((nb, L, W), lambda i, c: (i, c, 0))
    st = pl.BlockSpec((nb, H, DH, DH), lambda i, c: (i, 0, 0, 0))
    sh3 = shift.reshape(b, 1, -1)
    mu2 = mu.reshape(1, -1)
    return pl.pallas_call(
        functools.partial(_rwkv_kernel, L=L, nb=nb, nc=nc),
        grid=(b // nb, nc),
        in_specs=[pl.BlockSpec((nb, L, W), lambda i, c: (i, c, 2)),
                  pl.BlockSpec((nb, L, W), lambda i, c: (i, c, 3)),
                  pl.BlockSpec((nb, L, W), lambda i, c: (i, c, 4)),
                  pl.BlockSpec((nb, L, 256), lambda i, c: (i, c, 10)),
                  carry(W, 0), carry(W, 1), carry(W, 2), carry(256, 3 * W // 256),
                  mix(W, 0), mix(W, 1), mix(W, 2), mix(256, 3 * W // 256),
                  vec(), vec(),
                  pl.BlockSpec((128, 2 * W), lambda i, c: (0, 0)),
                  pl.BlockSpec((LORA_G, W), lambda i, c: (0, 0)),
                  vec(), vec(), vec(),
                  pl.BlockSpec((W, W), lambda i, c: (0, 0)),
                  st],
        out_specs=[blk(), blk(), blk(), st],
        out_shape=[jax.ShapeDtypeStruct((b, t, W), F32)] * 3 + [jax.ShapeDtypeStruct((b, H, DH, DH), F32)],
        scratch_shapes=[pltpu.VMEM((nb, H, DH, DH), F32), pltpu.VMEM((nb, 1, W), F32), pltpu.VMEM((nb, 1, W), F32),
                        pltpu.VMEM((nb, 1, W), F32), pltpu.VMEM((nb, 1, 256), F32)],
        compiler_params=_params(("parallel", "arbitrary")),
        name="rwkv",
    )(p3, p3, p3, p3,
      sh3, sh3, sh3, sh3, mu2, mu2, mu2, mu2,
      w0.reshape(1, W), a0.reshape(1, W), wba_bf, gb_bf, kkw.reshape(1, W), kaw.reshape(1, W),
      rk.reshape(1, W), ones_bd, s0)


def _block_diag(w):
    nb, bw, _ = w.shape
    return (jnp.eye(nb, dtype=w.dtype)[:, None, :, None] * w[:, :, None, :]).reshape(nb * bw, nb * bw)


def _chunk(t, target):
    return target if t % target == 0 else t


def _scan_blocking(b, t, target):
    L = _chunk(t, target)
    return L, min(b, 8)


def _run_group(x, mods, st, wts):
    b, t, d = x.shape
    x2 = x.reshape(b * t, d)
    (mc, mn, mm, lh, lconv, ccb, rs, rsh) = st

    sh1, sc1, gt1, sh2, sc2, gt2 = mods[0]
    e = wts["even"]
    p = _in_proj(x2, wts["g_mix"][0], sh1, sc1, e["w_in"], t)
    p3 = p.reshape(b, t, IN_EVEN_PAD)
    gif_t = jnp.transpose(p3[:, :, 6 * W:6 * W + 2 * H], (0, 2, 1))
    hm, c1, n1, m1 = _mlstm(p3, gif_t, e["b_if"], mc[0], mn[0], mm[0], *_scan_blocking(b, t, 128))
    hl, lh1 = _lru(p3, lconv[0], lh[0], e["w_conv"], e["b_conv"], e["w_gate"], e["b_gate"], e["lam"],
                   _chunk(t, 256), min(b, 4 if t >= 256 else 8))
    assert t >= LRU_CONV - 1
    conv1 = p3[:, t - (LRU_CONV - 1):, 4 * W:5 * W]
    x2 = _out_proj(x2, hl.reshape(b * t, W), hm.reshape(b * t, W), p, 3, None, e["g_head"], jnp.zeros((W,), F32),
                   wts["ones_bd"], e["w_out"], gt1, t, True, RMS_EPS, True)
    m = wts["moe"][0]
    x2 = _moe(x2, wts["g_ffn"][0], sh2, sc2, gt2, m["wr_hi"], m["wr_lo"], m["br"], wts["w1"], wts["w3"], wts["w2"],
              0, wts["g_final"], t, False)

    sh1, sc1, gt1, sh2, sc2, gt2 = mods[1]
    o = wts["odd"]
    p = _in_proj(x2, wts["g_mix"][1], sh1, sc1, o["w_in"], t)
    p3 = p.reshape(b, t, IN_ODD)
    cc, cc1 = _conf(p3, ccb[0], o["b_glu"], o["w_dw"], o["b_dw"], o["g_ln"], o["b_ln"], _chunk(t, 256),
                    min(b, 1 if t >= 256 else 8))
    y, bonus, g, s1 = _rwkv(p3, rsh[0], o["mu"], o["w0"], o["a0"], o["wba"], o["gb"], o["kkw"], o["kaw"], o["rk"],
                            wts["ones_bd"], rs[0], *_scan_blocking(b, t, 64))
    sh_out = p3[:, t - 1, 2 * W:]
    x2 = _out_proj(x2, cc.reshape(b * t, W), y.reshape(b * t, W), g.reshape(b * t, W), 0, bonus.reshape(b * t, W),
                   o["g_gn"], o["b_gn"], wts["ones_bd"], o["w_out"], gt1, t, False, RWKV_GN_EPS, False)
    m = wts["moe"][1]
    y2 = _moe(x2, wts["g_ffn"][1], sh2, sc2, gt2, m["wr_hi"], m["wr_lo"], m["br"], wts["w1"], wts["w3"], wts["w2"],
              1, wts["g_final"], t, True)
    states = (c1[None], n1[None], m1.reshape(1, b, H), lh1.reshape(1, b, W), conv1[None], cc1[None], s1[None],
              sh_out[None])
    return y2.reshape(b, t, d), states


def kernel(x_prompt, x_sample, c_prompt, c_sample, state_mlstm_C, state_mlstm_n, state_mlstm_m, state_lru_h,
           cache_lru_conv, cache_conformer_conv, state_rwkv_S, cache_rwkv_shift, w_ada, b_ada, g_norm_mix,
           g_norm_ffn, w_in_even, b_mlstm_if, g_mlstm_head, w_lru_conv, b_lru_conv, w_lru_r, b_lru_r, w_lru_i,
           b_lru_i, lru_lambda, w_out_even, w_in_odd, b_glu, w_cc_dw, b_cc_dw, g_cc_ln, b_cc_ln, rwkv_mu,
           rwkv_w0, rwkv_wB, rwkv_a0, rwkv_aB, rwkv_gB, rwkv_kk, rwkv_ka, rwkv_rk, g_rwkv_gn, b_rwkv_gn,
           w_out_odd, w_router_g, b_router_g, w_router_e, b_router_e, w_exp_gate, w_exp_up, w_exp_down, g_final):
    bp, bs = x_prompt.shape[0], x_sample.shape[0]

    wi = w_in_even[0]
    gcol = 4 * W
    w_in_e = jnp.concatenate([wi[:, :gcol], wi[:, gcol + 2 * H:], wi[:, gcol:gcol + 2 * H],
                              jnp.zeros((D_MODEL, LANES - 2 * H), F32)], axis=1).astype(BF)
    even = dict(
        w_in=w_in_e, b_if=b_mlstm_if[0], g_head=g_mlstm_head[0], w_conv=w_lru_conv[0], b_conv=b_lru_conv[0],
        w_gate=jnp.concatenate([_block_diag(w_lru_r[0]), _block_diag(w_lru_i[0])], axis=1).astype(BF),
        b_gate=jnp.concatenate([b_lru_r[0], b_lru_i[0]]), lam=lru_lambda[0], w_out=w_out_even[0].astype(BF))
    zl = jnp.zeros((LORA_W, W), F32)
    odd = dict(
        w_in=w_in_odd[0].astype(BF), b_glu=b_glu[0], w_dw=w_cc_dw[0], b_dw=b_cc_dw[0], g_ln=g_cc_ln[0],
        b_ln=b_cc_ln[0], mu=rwkv_mu[0], w0=rwkv_w0[0], a0=rwkv_a0[0],
        wba=jnp.concatenate([jnp.concatenate([rwkv_wB[0], zl], axis=1),
                             jnp.concatenate([zl, rwkv_aB[0]], axis=1)], axis=0).astype(BF),
        gb=rwkv_gB[0].astype(BF), kkw=rwkv_kk[0], kaw=rwkv_ka[0], rk=rwkv_rk[0], g_gn=g_rwkv_gn[0],
        b_gn=b_rwkv_gn[0], w_out=w_out_odd[0].astype(BF))
    moe = []
    for l in range(DEPTH):
        wr = jnp.concatenate([w_router_g[l], w_router_e[l],
                              jnp.zeros((D_MODEL, LANES - N_GROUPS - N_EXPERTS), F32)], axis=1)
        wr_hi = wr.astype(BF)
        wr_lo = (wr - wr_hi.astype(F32)).astype(BF)
        br = jnp.concatenate([b_router_g[l], b_router_e[l],
                              jnp.zeros((LANES - N_GROUPS - N_EXPERTS,), F32)]).reshape(1, LANES)
        moe.append(dict(wr_hi=wr_hi, wr_lo=wr_lo, br=br))
    ones_bd = _block_diag(jnp.ones((H, DH, DH), F32)).astype(BF)
    wts = dict(even=even, odd=odd, moe=moe, g_mix=g_norm_mix, g_ffn=g_norm_ffn, g_final=g_final, ones_bd=ones_bd,
               w1=w_exp_gate.astype(BF).reshape(DEPTH * N_EXPERTS, D_MODEL, D_EXPERT),
               w3=w_exp_up.astype(BF).reshape(DEPTH * N_EXPERTS, D_MODEL, D_EXPERT),
               w2=w_exp_down.astype(BF).reshape(DEPTH * N_EXPERTS, D_EXPERT, D_MODEL))

    rows = 2 * bs
    assert bp <= bs
    c_all = jnp.concatenate([c_sample, c_prompt, jnp.zeros((rows - bs - bp, D_MODEL), F32)], axis=0)
    mod = _ada(c_all, w_ada, b_ada).reshape(DEPTH * rows, 1, 6 * D_MODEL)

    def mods_of(lo):
        return [tuple(_Mod(mod, l * rows + lo, j) for j in range(6)) for l in range(DEPTH)]

    z = lambda *s: jnp.zeros(s, F32)
    st_p = (z(1, bp, H, DH, DH), z(1, bp, H, DH), z(1, bp, H), z(1, bp, W), z(1, bp, LRU_CONV - 1, W),
            z(1, bp, CONV_C - 1, W), z(1, bp, H, DH, DH), z(1, bp, 3 * W + LORA_W + LORA_A + LORA_G))
    st_s = (state_mlstm_C, state_mlstm_n, state_mlstm_m, state_lru_h, cache_lru_conv, cache_conformer_conv,
            state_rwkv_S, cache_rwkv_shift)
    y_p, out_p = _run_group(x_prompt, mods_of(bs), st_p, wts)
    y_s, out_s = _run_group(x_sample, mods_of(0), st_s, wts)
    return (y_p, y_s) + tuple(out_p) + tuple(out_s)
```

```python
import functools
from typing import NamedTuple

import jax
import jax.numpy as jnp
from jax import lax
from jax.experimental import pallas as pl
from jax.experimental.pallas import tpu as pltpu

F32 = jnp.float32
BF = jnp.bfloat16

D_MODEL = 1024
DEPTH = 2
H = 8
DH = 64
W = 512
LRU_CONV = 4
LRU_C = 8.0
CONV_C = 31
LORA_W = 64
LORA_A = 64
LORA_G = 128
RWKV_DECAY = 0.606531
RWKV_GN_EPS = 64e-5
N_GROUPS = 4
E_PER_GROUP = 4
N_EXPERTS = 16
D_EXPERT = 256
RMS_EPS = 1e-6
LN_EPS = 1e-5
LANES = 128
IN_EVEN_PAD = 6 * W + LANES
IN_ODD = 2 * W + 3 * W + LORA_W + LORA_A + LORA_G

ROW_TILE = 1024
MOE_ROW_TILE = 512
VMEM_LIMIT = 48 * 1024 * 1024


def _mm(a, b):
    return jnp.dot(a.astype(BF), b.astype(BF), preferred_element_type=F32)


def _mm_nt(a, b):
    return lax.dot_general(a.astype(BF), b.astype(BF), (((1,), (1,)), ((), ())),
                           preferred_element_type=F32)


def _mm_tn(a, b):
    return lax.dot_general(a.astype(BF), b.astype(BF), (((0,), (0,)), ((), ())),
                           preferred_element_type=F32)


def _split3(x):
    hi = x.astype(BF)
    r = x - hi.astype(F32)
    mid = r.astype(BF)
    lo = (r - mid.astype(F32)).astype(BF)
    return hi, mid, lo


def _mm_mask_l(mask, x):
    return sum(jnp.dot(mask, p, preferred_element_type=F32) for p in _split3(x))


def _mm_mask_r(x, mask):
    return sum(jnp.dot(p, mask, preferred_element_type=F32) for p in _split3(x))


def _sigmoid(x):
    return 1.0 / (1.0 + jnp.exp(-x))


def _rows(v, tm):
    nb, _, c = v.shape
    if nb == 1:
        return v[0]
    return jnp.broadcast_to(v, (nb, tm // nb, c)).reshape(tm, c)


class _Mod(NamedTuple):
    arr: jax.Array
    row0: int
    col: int


def _mod_spec(m, T, tm):
    if tm <= T:
        per = T // tm
        return pl.BlockSpec((1, 1, D_MODEL), lambda i, *_: (m.row0 + i // per, 0, m.col))
    nbk = tm // T
    assert m.row0 % nbk == 0
    return pl.BlockSpec((nbk, 1, D_MODEL), lambda i, *_: (m.row0 // nbk + i, 0, m.col))


def _norm_mod(x, g, sh, sc):
    y = x * lax.rsqrt(jnp.mean(x * x, axis=-1, keepdims=True) + RMS_EPS) * g
    return y * (1.0 + sc) + sh


def _params(sem):
    return pltpu.CompilerParams(dimension_semantics=sem, vmem_limit_bytes=VMEM_LIMIT)


def _ada_kernel(c_ref, w_ref, b_ref, o_ref):
    y = _mm(c_ref[...], w_ref[0]) + b_ref[0]
    o_ref[...] = y.reshape(o_ref.shape)


def _ada(c_all, w, b):
    nb = c_all.shape[0]
    tn = 1536
    return pl.pallas_call(
        _ada_kernel,
        grid=(DEPTH, 6 * D_MODEL // tn),
        in_specs=[pl.BlockSpec((nb, D_MODEL), lambda l, j: (0, 0)),
                  pl.BlockSpec((1, D_MODEL, tn), lambda l, j: (l, 0, j)),
                  pl.BlockSpec((1, 1, tn), lambda l, j: (l, 0, j))],
        out_specs=pl.BlockSpec((nb, 1, tn), lambda l, j: (l, 0, j)),
        out_shape=jax.ShapeDtypeStruct((DEPTH * nb, 1, 6 * D_MODEL), F32),
        compiler_params=_params(("parallel", "parallel")),
        name="ada",
    )(c_all, w, b.reshape(DEPTH, 1, 6 * D_MODEL))


def _inproj_kernel(x_ref, g_ref, sh_ref, sc_ref, w_ref, o_ref, h_scr):
    tm = x_ref.shape[0]

    @pl.when(pl.program_id(1) == 0)
    def _():
        h = _norm_mod(x_ref[...], g_ref[...], _rows(sh_ref[...], tm), _rows(sc_ref[...], tm))
        h_scr[...] = h.astype(BF)

    o_ref[...] = jnp.dot(h_scr[...], w_ref[...], preferred_element_type=F32)


INPROJ_ROW_TILE = 512


def _in_proj(x2, g, sh, sc, w_bf, T):
    n, d = x2.shape
    cols = w_bf.shape[1]
    tm = INPROJ_ROW_TILE
    tn = cols
    return pl.pallas_call(
        _inproj_kernel,
        grid=(n // tm, cols // tn),
        in_specs=[pl.BlockSpec((tm, d), lambda i, j: (i, 0)),
                  pl.BlockSpec((1, d), lambda i, j: (0, 0)),
                  _mod_spec(sh, T, tm), _mod_spec(sc, T, tm),
                  pl.BlockSpec((d, tn), lambda i, j: (0, j))],
        out_specs=pl.BlockSpec((tm, tn), lambda i, j: (i, j)),
        out_shape=jax.ShapeDtypeStruct((n, cols), F32),
        scratch_shapes=[pltpu.VMEM((tm, d), BF)],
        compiler_params=_params(("parallel", "arbitrary")),
        name="in_proj",
    )(x2, g.reshape(1, d), sh.arr, sc.arr, w_bf)


def _seg_sum(x, ones_bd):
    hi = x.astype(BF)
    lo = (x - hi.astype(F32)).astype(BF)
    return (jnp.dot(hi, ones_bd, preferred_element_type=F32)
            + jnp.dot(lo, ones_bd, preferred_element_type=F32))


def _outproj_kernel(*refs, pre_first, eps, has_add, sigmoid_mul):
    if has_add:
        x_ref, a_ref, pre_ref, mul_ref, add_ref, gain_ref, bias_ref, ones_ref, w_ref, gt_ref, o_ref = refs
    else:
        x_ref, a_ref, pre_ref, mul_ref, gain_ref, bias_ref, ones_ref, w_ref, gt_ref, o_ref = refs
    tm = x_ref.shape[0]
    pre = pre_ref[...]
    ones_bd = ones_ref[...]
    dev = pre - _seg_sum(pre, ones_bd) * (1.0 / DH)
    var = _seg_sum(dev * dev, ones_bd) * (1.0 / DH)
    y = dev * lax.rsqrt(var + eps) * gain_ref[...] + bias_ref[...]
    if has_add:
        y = y + add_ref[...]
    m = mul_ref[...]
    y = (y * (_sigmoid(m) if sigmoid_mul else m)).astype(BF)
    first, second = (y, a_ref[...]) if pre_first else (a_ref[...], y)
    mix = (jnp.dot(first, w_ref[0:W, :], preferred_element_type=F32)
           + jnp.dot(second, w_ref[W:2 * W, :], preferred_element_type=F32))
    o_ref[...] = x_ref[...] + _rows(gt_ref[...], tm) * mix


def _out_proj(x2, a2, pre2, mul2, mul_col, add2, gain, bias, ones_bd, w_bf, gt, T, pre_first, eps, sigmoid_mul):
    n, d = x2.shape
    tm = ROW_TILE
    has_add = add2 is not None
    row = lambda c=0: pl.BlockSpec((tm, W), lambda i, c=c: (i, c))
    vec = lambda: pl.BlockSpec((1, W), lambda i: (0, 0))
    in_specs = [pl.BlockSpec((tm, d), lambda i: (i, 0)), row(), row(), row(mul_col)]
    args = [x2, a2, pre2, mul2]
    if has_add:
        in_specs.append(row())
        args.append(add2)
    in_specs += [vec(), vec(), pl.BlockSpec((W, W), lambda i: (0, 0)), pl.BlockSpec((2 * W, d), lambda i: (0, 0)),
                 _mod_spec(gt, T, tm)]
    args += [gain.reshape(1, W), bias.reshape(1, W), ones_bd, w_bf, gt.arr]
    return pl.pallas_call(
        functools.partial(_outproj_kernel, pre_first=pre_first, eps=eps, has_add=has_add, sigmoid_mul=sigmoid_mul),
        grid=(n // tm,),
        in_specs=in_specs,
        out_specs=pl.BlockSpec((tm, d), lambda i: (i, 0)),
        out_shape=jax.ShapeDtypeStruct((n, d), F32),
        compiler_params=_params(("parallel",)),
        name="out_proj",
    )(*args)


def _route(logits):
    lane = lax.broadcasted_iota(jnp.int32, logits.shape, 1).astype(F32)
    neg = -jnp.inf
    is_g = lane < N_GROUPS
    lg = jnp.where(is_g, logits, neg)
    mg = jnp.max(lg, axis=1, keepdims=True)
    gsel = jnp.min(jnp.where(lg == mg, lane, float(LANES)), axis=1, keepdims=True)
    psum = jnp.sum(jnp.where(is_g, jnp.exp(lg - mg), 0.0), axis=1, keepdims=True)
    pg_sel = 1.0 / psum
    lo = N_GROUPS + E_PER_GROUP * gsel
    le = jnp.where((lane >= lo) & (lane < lo + E_PER_GROUP), logits, neg)
    v1 = jnp.max(le, axis=1, keepdims=True)
    i1 = jnp.min(jnp.where(le == v1, lane, float(LANES)), axis=1, keepdims=True)
    le2 = jnp.where(lane == i1, neg, le)
    v2 = jnp.max(le2, axis=1, keepdims=True)
    i2 = jnp.min(jnp.where(le2 == v2, lane, float(LANES)), axis=1, keepdims=True)
    e2 = jnp.exp(v2 - v1)
    p1 = 1.0 / (1.0 + e2)
    p2 = e2 / (1.0 + e2)
    return pg_sel * jnp.where(lane == i1, p1, jnp.where(lane == i2, p2, 0.0))


def _moe_kernel(x_ref, g_ref, sh_ref, sc_ref, gt_ref, wrh_ref, wrl_ref, br_ref, w1_ref, w3_ref, w2_ref,
                gf_ref, o_ref, *, final_norm):
    tm = x_ref.shape[0]
    h = _norm_mod(x_ref[...], g_ref[...], _rows(sh_ref[...], tm), _rows(sc_ref[...], tm))
    hb = h.astype(BF)
    hl = (h - hb.astype(F32)).astype(BF)
    logits = (jnp.dot(hb, wrh_ref[...], preferred_element_type=F32)
              + jnp.dot(hl, wrh_ref[...], preferred_element_type=F32)
              + jnp.dot(hb, wrl_ref[...], preferred_element_type=F32)) + br_ref[...]
    gate = _route(logits)
    lane = lax.broadcasted_iota(jnp.int32, gate.shape, 1)
    acc = None
    for e in range(N_EXPERTS):
        hg = jnp.dot(hb, w1_ref[e], preferred_element_type=F32)
        hu = jnp.dot(hb, w3_ref[e], preferred_element_type=F32)
        ge = jnp.sum(jnp.where(lane == e + N_GROUPS, gate, 0.0), axis=1, keepdims=True)
        hh = hg * _sigmoid(hg) * hu * ge
        part = jnp.dot(hh.astype(BF), w2_ref[e], preferred_element_type=F32)
        acc = part if acc is None else acc + part
    y = x_ref[...] + _rows(gt_ref[...], tm) * acc
    if final_norm:
        y = y * lax.rsqrt(jnp.mean(y * y, axis=-1, keepdims=True) + RMS_EPS) * gf_ref[...]
    o_ref[...] = y


def _moe(x2, g, sh, sc, gt, wr_hi, wr_lo, br, w1, w3, w2, layer, g_final, T, final_norm):
    n, d = x2.shape
    tm = MOE_ROW_TILE
    resident = lambda shape: pl.BlockSpec(shape, lambda i: (layer, 0, 0), pipeline_mode=pl.Buffered(1))
    return pl.pallas_call(
        functools.partial(_moe_kernel, final_norm=final_norm),
        grid=(n // tm,),
        in_specs=[pl.BlockSpec((tm, d), lambda i: (i, 0)),
                  pl.BlockSpec((1, d), lambda i: (0, 0)),
                  _mod_spec(sh, T, tm), _mod_spec(sc, T, tm), _mod_spec(gt, T, tm),
                  pl.BlockSpec((d, LANES), lambda i: (0, 0)),
                  pl.BlockSpec((d, LANES), lambda i: (0, 0)),
                  pl.BlockSpec((1, LANES), lambda i: (0, 0)),
                  resident((N_EXPERTS, d, D_EXPERT)), resident((N_EXPERTS, d, D_EXPERT)),
                  resident((N_EXPERTS, D_EXPERT, d)),
                  pl.BlockSpec((1, d), lambda i: (0, 0))],
        out_specs=pl.BlockSpec((tm, d), lambda i: (i, 0)),
        out_shape=jax.ShapeDtypeStruct((n, d), F32),
        compiler_params=_params(("parallel",)),
        name="moe",
    )(x2, g.reshape(1, d), sh.arr, sc.arr, gt.arr, wr_hi, wr_lo, br, w1, w3, w2, g_final.reshape(1, d))


def _log_sigmoid(x):
    return jnp.minimum(x, 0.0) - jnp.log(1.0 + jnp.exp(-jnp.abs(x)))


def _mlstm_kernel(q_ref, k_ref, v_ref, g_ref, gt_ref, bif_ref, bift_ref, c0_ref, n0_ref, m0_ref,
                  h_ref, c1_ref, n1_ref, m1_ref, st_scr, m_scr, *, L, nb, nc):
    c = pl.program_id(1)
    NP = H // 2
    r64 = lax.broadcasted_iota(jnp.int32, (DH, DH), 0)
    c64 = lax.broadcasted_iota(jnp.int32, (DH, DH), 1)
    eye64 = r64 == c64

    @pl.when(c == 0)
    def _():
        st_scr[...] = jnp.zeros_like(st_scr)
        for ib in range(nb):
            for h in range(H):
                p, o = divmod(h, 2)
                rs = slice(o * DH, (o + 1) * DH)
                st_scr[ib, p, rs, o * DH:(o + 1) * DH] = c0_ref[ib, h]
                n_col = jnp.sum(jnp.where(eye64, n0_ref[ib, h:h + 1, :], 0.0), axis=1, keepdims=True)
                st_scr[ib, p, rs, 2 * DH + o * DH:2 * DH + (o + 1) * DH] = jnp.broadcast_to(n_col, (DH, DH))
        m_scr[...] = m0_ref[...]

    row = lax.broadcasted_iota(jnp.int32, (L, L), 0)
    col = lax.broadcasted_iota(jnp.int32, (L, L), 1)
    causal = row >= col
    tri = jnp.where(causal, 1.0, 0.0).astype(BF)
    tri_u = jnp.where(row <= col, 1.0, 0.0).astype(BF)
    even = lax.broadcasted_iota(jnp.int32, (L, 2 * DH), 1) < DH
    row_s = lax.broadcasted_iota(jnp.int32, (2 * DH, 4 * DH), 0)
    lane_s = lax.broadcasted_iota(jnp.int32, (2 * DH, 4 * DH), 1)
    top = row_s < DH
    same_head = jnp.where(top, 0, DH) == jnp.bitwise_and(lane_s, DH)
    ones_l = jnp.ones((L, 2 * DH), BF)
    ibs = range(nb)
    g = [g_ref[ib] + bif_ref[...] for ib in ibs]
    gt = [gt_ref[ib] + bift_ref[...] for ib in ibs]
    bcum = [_mm_mask_l(tri, _log_sigmoid(g[ib])) for ib in ibs]
    bcum_t = [_mm_mask_r(_log_sigmoid(gt[ib]), tri_u) for ib in ibs]
    m_prev = [m_scr[ib] for ib in ibs]
    ch = [(ib, h) for ib in ibs for h in range(H)]
    prs = [(ib, p) for ib in ibs for p in range(NP)]
    pair_of = lambda u: (u[0], u[1] // 2)
    wide = lambda x: jnp.broadcast_to(x, (L, 2 * DH))
    lanes = {u: slice(u[1] * 2 * DH, (u[1] + 1) * 2 * DH) for u in prs}
    q_f = {u: q_ref[u[0], :, lanes[u]] for u in prs}
    q_b = {u: q_f[u].astype(BF) for u in prs}
    k_f = {u: k_ref[u[0], :, lanes[u]] * (DH ** -0.5) for u in prs}
    k_b = {u: k_f[u].astype(BF) for u in prs}
    rhs = {u: jnp.concatenate([v_ref[u[0], :, lanes[u]].astype(BF), ones_l], axis=1) for u in prs}
    odd = lax.broadcasted_iota(jnp.int32, (L, 2 * DH), 1) >= DH
    qk = {u: _mm_nt(jnp.where(odd if u[1] % 2 else even, q_f[pair_of(u)], 0.0).astype(BF), k_b[pair_of(u)])
          for u in ch}
    st = {u: st_scr[u[0], u[1]] for u in prs}
    qst = {u: _mm(q_b[u], st[u]) for u in prs}
    bc = {u: wide(bcum[u[0]][:, H + u[1]:H + u[1] + 1]) for u in ch}
    ic = {u: wide(g[u[0]][:, u[1]:u[1] + 1]) for u in ch}
    mp = {u: m_prev[u[0]][:, u[1]:u[1] + 1] for u in ch}
    dmat = {u: jnp.where(causal, bc[u][:, 0:L] + (gt[u[0]][u[1]:u[1] + 1, :] - bcum_t[u[0]][H + u[1]:H + u[1] + 1, :]),
                         -jnp.inf) for u in ch}
    g_inter = {u: bc[u] + mp[u] for u in ch}
    m_t = {u: jnp.maximum(g_inter[u], jnp.max(dmat[u], axis=1, keepdims=True)) for u in ch}
    s = {u: (qk[u] * jnp.exp(dmat[u] - m_t[u][:, 0:L])).astype(BF) for u in ch}
    w_inter = {u: jnp.exp(g_inter[u] - m_t[u]) for u in ch}
    sv = {u: _mm(s[u], rhs[pair_of(u)]) for u in ch}
    m_new = {u: m_t[u][L - 1:L, 0:1] for u in ch}
    b_last = {u: bcum[u[0]][L - 1:L, H + u[1]:H + u[1] + 1] for u in ch}
    w_s = {u: jnp.exp(b_last[u] - bc[u] + ic[u] - m_new[u]) for u in ch}
    decay = {u: jnp.exp(b_last[u] + mp[u] - m_new[u]) for u in ch}
    pick = lambda d, u: jnp.where(even, d[u[0], 2 * u[1]], d[u[0], 2 * u[1] + 1])
    hh = {}
    for u in prs:
        e, o = (u[0], 2 * u[1]), (u[0], 2 * u[1] + 1)
        w_pair = pick(w_inter, u)
        num = jnp.where(even, sv[e][:, 0:2 * DH], sv[o][:, 0:2 * DH]) + w_pair * qst[u][:, 0:2 * DH]
        den = jnp.where(even, sv[e][:, 2 * DH:], sv[o][:, 2 * DH:]) + w_pair * qst[u][:, 2 * DH:]
        hh[u] = num / jnp.maximum(jnp.abs(den), jnp.exp(-pick(m_t, u)))
        inc = _mm_tn((k_f[u] * pick(w_s, u)).astype(BF), rhs[u])
        st_scr[u[0], u[1]] = jnp.where(top, decay[e], decay[o]) * st[u] + jnp.where(same_head, inc, 0.0)
    for ib in ibs:
        h_ref[ib] = jnp.concatenate([hh[ib, p] for p in range(NP)], axis=1)
        m_scr[ib] = jnp.concatenate([m_new[ib, h] for h in range(H)], axis=1)

    @pl.when(c == nc - 1)
    def _():
        for ib in range(nb):
            for h in range(H):
                p, o = divmod(h, 2)
                rs = slice(o * DH, (o + 1) * DH)
                c1_ref[ib, h] = st_scr[ib, p, rs, o * DH:(o + 1) * DH]
                n_rep = st_scr[ib, p, rs, 2 * DH + o * DH:2 * DH + (o + 1) * DH]
                n1_ref[ib, h:h + 1, :] = jnp.sum(jnp.where(eye64, n_rep, 0.0), axis=0, keepdims=True)
        m1_ref[...] = m_scr[...]


def _mlstm(p3, gif_t, bif, c0, n0, m0, L, nb):
    b, t, _ = p3.shape
    nc = t // L
    col = lambda j: pl.BlockSpec((nb, L, W), lambda i, c, j=j: (i, c, j))
    bif_pad = jnp.zeros((1, LANES), F32).at[0, :2 * H].set(bif)
    return pl.pallas_call(
        functools.partial(_mlstm_kernel, L=L, nb=nb, nc=nc),
        grid=(b // nb, nc),
        in_specs=[col(0), col(1), col(2),
                  pl.BlockSpec((nb, L, LANES), lambda i, c: (i, c, 6 * W // LANES)),
                  pl.BlockSpec((nb, 2 * H, L), lambda i, c: (i, 0, c)),
                  pl.BlockSpec((1, LANES), lambda i, c: (0, 0)),
                  pl.BlockSpec((2 * H, 1), lambda i, c: (0, 0)),
                  pl.BlockSpec((nb, H, DH, DH), lambda i, c: (i, 0, 0, 0)),
                  pl.BlockSpec((nb, H, DH), lambda i, c: (i, 0, 0)),
                  pl.BlockSpec((nb, 1, H), lambda i, c: (i, 0, 0))],
        out_specs=[pl.BlockSpec((nb, L, W), lambda i, c: (i, c, 0)),
                   pl.BlockSpec((nb, H, DH, DH), lambda i, c: (i, 0, 0, 0)),
                   pl.BlockSpec((nb, H, DH), lambda i, c: (i, 0, 0)),
                   pl.BlockSpec((nb, 1, H), lambda i, c: (i, 0, 0))],
        out_shape=[jax.ShapeDtypeStruct((b, t, W), F32),
                   jax.ShapeDtypeStruct((b, H, DH, DH), F32),
                   jax.ShapeDtypeStruct((b, H, DH), F32),
                   jax.ShapeDtypeStruct((b, 1, H), F32)],
        scratch_shapes=[pltpu.VMEM((nb, H // 2, 2 * DH, 4 * DH), F32), pltpu.VMEM((nb, 1, H), F32)],
        compiler_params=_params(("parallel", "arbitrary")),
        name="mlstm",
    )(p3, p3, p3, p3, gif_t, bif_pad, bif.reshape(2 * H, 1), c0, n0, m0.reshape(b, 1, H))


def _gelu_tanh(x):
    return 0.5 * x * (1.0 + jnp.tanh(0.7978845608028654 * (x + 0.044715 * x * x * x)))


def _lru_kernel(xr_ref, xg_ref, cache_ref, h0_ref, wc_ref, bc_ref, wg_ref, bg_ref, lam_ref,
                o_ref, h1_ref, ext_scr, a_scr, u_scr, hs_scr, hc_scr, *, Tc, nb, nc):
    c = pl.program_id(1)
    K1 = LRU_CONV - 1

    @pl.when(c == 0)
    def _():
        ext_scr[:, 8 - K1:8, :] = cache_ref[...]
        hc_scr[...] = h0_ref[...]

    xcs = []
    for ib in range(nb):
        x = xr_ref[ib]
        ext_scr[ib, 8:8 + Tc, :] = x
        xc = bc_ref[...] + wc_ref[K1:K1 + 1, :] * x
        for d in range(1, LRU_CONV):
            xc = xc + wc_ref[K1 - d:K1 - d + 1, :] * ext_scr[ib, 8 - d:8 - d + Tc, :]
        ext_scr[ib, 8 - K1:8, :] = ext_scr[ib, 8 + Tc - K1:8 + Tc, :]
        xcs.append(xc)
    xc = jnp.concatenate(xcs, axis=0)
    gates = _mm(xc, wg_ref[...]) + bg_ref[...]
    r = _sigmoid(gates[:, 0:W])
    ig = _sigmoid(gates[:, W:2 * W])
    lam = lam_ref[...]
    softplus_neg = jnp.maximum(-lam, 0.0) + jnp.log(1.0 + jnp.exp(-jnp.abs(lam)))
    log_a = -LRU_C * r * softplus_neg
    a_scr[...] = jnp.exp(log_a).reshape(nb, Tc, W)
    th = jnp.tanh(log_a)
    one_minus_a2 = -2.0 * th / (1.0 - th)
    u_scr[...] = (jnp.sqrt(one_minus_a2) * (ig * xc)).reshape(nb, Tc, W)

    def body(t, hs):
        new = []
        for ib in range(nb):
            h = a_scr[ib, pl.ds(t, 1), :] * hs[ib] + u_scr[ib, pl.ds(t, 1), :]
            hs_scr[ib, pl.ds(t, 1), :] = h
            new.append(h)
        return tuple(new)

    h_fin = lax.fori_loop(0, Tc, body, tuple(hc_scr[ib] for ib in range(nb)), unroll=8)
    for ib in range(nb):
        hc_scr[ib] = h_fin[ib]
        o_ref[ib] = (hs_scr[ib] * _gelu_tanh(xg_ref[ib])).astype(BF)

    @pl.when(c == nc - 1)
    def _():
        h1_ref[...] = hc_scr[...]


def _lru(p3, cache, h0, w_conv, b_conv, wg_bf, bg, lam, Tc, nb):
    b, t, _ = p3.shape
    nc = t // Tc
    return pl.pallas_call(
        functools.partial(_lru_kernel, Tc=Tc, nb=nb, nc=nc),
        grid=(b // nb, nc),
        in_specs=[pl.BlockSpec((nb, Tc, W), lambda i, c: (i, c, 4)),
                  pl.BlockSpec((nb, Tc, W), lambda i, c: (i, c, 5)),
                  pl.BlockSpec((nb, LRU_CONV - 1, W), lambda i, c: (i, 0, 0)),
                  pl.BlockSpec((nb, 1, W), lambda i, c: (i, 0, 0)),
                  pl.BlockSpec((LRU_CONV, W), lambda i, c: (0, 0)),
                  pl.BlockSpec((1, W), lambda i, c: (0, 0)),
                  pl.BlockSpec((W, 2 * W), lambda i, c: (0, 0)),
                  pl.BlockSpec((1, 2 * W), lambda i, c: (0, 0)),
                  pl.BlockSpec((1, W), lambda i, c: (0, 0))],
        out_specs=[pl.BlockSpec((nb, Tc, W), lambda i, c: (i, c, 0)),
                   pl.BlockSpec((nb, 1, W), lambda i, c: (i, 0, 0))],
        out_shape=[jax.ShapeDtypeStruct((b, t, W), BF), jax.ShapeDtypeStruct((b, 1, W), F32)],
        scratch_shapes=[pltpu.VMEM((nb, 8 + Tc, W), F32), pltpu.VMEM((nb, Tc, W), F32), pltpu.VMEM((nb, Tc, W), F32),
                        pltpu.VMEM((nb, Tc, W), F32), pltpu.VMEM((nb, 1, W), F32)],
        compiler_params=_params(("parallel", "arbitrary")),
        name="lru",
    )(p3, p3, cache, h0.reshape(b, 1, W), w_conv, b_conv.reshape(1, W), wg_bf, bg.reshape(1, 2 * W),
      lam.reshape(1, W))


CONF_ROWS = 32


def _conf_kernel(u_ref, gte_ref, cache_ref, bu_ref, bg_ref, wdw_ref, bdw_ref, gln_ref, bln_ref,
                 o_ref, cache1_ref, ext_scr, sh_scr, *, Tc, nb, nc):
    c = pl.program_id(1)
    K1 = CONV_C - 1
    base = 32 - K1

    @pl.when(c == 0)
    def _():
        ext_scr[:, base:32, :] = cache_ref[...]

    for ib in range(nb):
        u = u_ref[ib] + bu_ref[...]
        gte = gte_ref[ib] + bg_ref[...]
        ext_scr[ib, 32:32 + Tc, :] = u * _sigmoid(gte)
        for b in range(1, 8):
            sh_scr[b - 1] = ext_scr[ib, b:b + Tc + 24, :]

        def window(off, r0, rb):
            a, b = divmod(off, 8)
            if b == 0:
                return ext_scr[ib, 8 * a + r0:8 * a + r0 + rb, :]
            return sh_scr[b - 1, 8 * a + r0:8 * a + r0 + rb, :]

        rb = min(CONF_ROWS, Tc)
        for r0 in range(0, Tc, rb):
            acc = bdw_ref[...] + wdw_ref[0:1, :] * window(base, r0, rb)
            for j in range(1, CONV_C):
                acc = acc + wdw_ref[j:j + 1, :] * window(base + j, r0, rb)
            mu = jnp.mean(acc, axis=1, keepdims=True)
            var = jnp.mean(jnp.square(acc - mu), axis=1, keepdims=True)
            y = (acc - mu) * lax.rsqrt(var + LN_EPS) * gln_ref[...] + bln_ref[...]
            o_ref[ib, r0:r0 + rb, :] = (y * _sigmoid(y)).astype(BF)
        ext_scr[ib, base:32, :] = ext_scr[ib, base + Tc:32 + Tc, :]

    @pl.when(c == nc - 1)
    def _():
        cache1_ref[...] = ext_scr[:, base:32, :]


def _conf(p3, cache, b_glu, w_dw, b_dw, g_ln, b_ln, Tc, nb):
    b, t, _ = p3.shape
    nc = t // Tc
    vec = lambda: pl.BlockSpec((1, W), lambda i, c: (0, 0))
    return pl.pallas_call(
        functools.partial(_conf_kernel, Tc=Tc, nb=nb, nc=nc),
        grid=(b // nb, nc),
        in_specs=[pl.BlockSpec((nb, Tc, W), lambda i, c: (i, c, 0)),
                  pl.BlockSpec((nb, Tc, W), lambda i, c: (i, c, 1)),
                  pl.BlockSpec((nb, CONV_C - 1, W), lambda i, c: (i, 0, 0)),
                  vec(), pl.BlockSpec((1, W), lambda i, c: (0, 1)),
                  pl.BlockSpec((CONV_C, W), lambda i, c: (0, 0)),
                  vec(), vec(), vec()],
        out_specs=[pl.BlockSpec((nb, Tc, W), lambda i, c: (i, c, 0)),
                   pl.BlockSpec((nb, CONV_C - 1, W), lambda i, c: (i, 0, 0))],
        out_shape=[jax.ShapeDtypeStruct((b, t, W), BF), jax.ShapeDtypeStruct((b, CONV_C - 1, W), F32)],
        scratch_shapes=[pltpu.VMEM((nb, 32 + Tc, W), F32), pltpu.VMEM((7, Tc + 24, W), F32)],
        compiler_params=_params(("parallel", "arbitrary")),
        name="conformer",
    )(p3, p3, cache, b_glu.reshape(1, 2 * W), b_glu.reshape(1, 2 * W), w_dw, b_dw.reshape(1, W),
      g_ln.reshape(1, W), b_ln.reshape(1, W))


def _rwkv_prep(x_refs, prev_scrs, mu_refs, w0_ref, a0_ref, wba_ref, gb_ref, kkw_ref, kaw_ref, rk_ref, ones_ref,
               L, nb):
    def shift_mix(x_ref, prev_scr, mu_ref):
        x = x_ref[...].reshape(nb * L, x_ref.shape[-1])
        first = jnp.bitwise_and(lax.broadcasted_iota(jnp.int32, x.shape, 0), L - 1) == 0
        carried = jnp.concatenate([jnp.broadcast_to(prev_scr[ib], (L, x.shape[1])) for ib in range(nb)], axis=0)
        prev = jnp.where(first, carried, pltpu.roll(x, 1, 0))
        for ib in range(nb):
            prev_scr[ib] = x[(ib + 1) * L - 1:(ib + 1) * L, :]
        return x + (prev - x) * mu_ref[...]

    r, k, v, z = (shift_mix(x, p, m) for x, p, m in zip(x_refs, prev_scrs, mu_refs))
    zwa = z[:, 0:128]
    lane = lax.broadcasted_iota(jnp.int32, zwa.shape, 1)
    wa = _mm(jnp.where(lane < LORA_W, jnp.tanh(zwa), zwa), wba_ref[...])
    lw = -RWKV_DECAY * _sigmoid(w0_ref[...] + wa[:, 0:W])
    a = _sigmoid(a0_ref[...] + wa[:, W:2 * W])
    g = _mm(_sigmoid(z[:, 128:256]), gb_ref[...])
    ones_bd = ones_ref[...]
    kk = k * kkw_ref[...]
    kk = kk * lax.rsqrt(_seg_sum(kk * kk, ones_bd) + 1e-12)
    k2 = k * (1.0 + (a - 1.0) * kaw_ref[...])
    bonus = _seg_sum(r * k2 * rk_ref[...], ones_bd) * v
    return dict(r=r, lw=lw, k=k2, v=v, kk=kk, b=kk * a), bonus, g


def _rwkv_kernel(pr_ref, pk_ref, pv_ref, pz_ref, sr_ref, sk_ref, sv_ref, sz_ref, mr_ref, mk_ref, mv_ref, mz_ref,
                 w0_ref, a0_ref, wba_ref, gb_ref, kkw_ref, kaw_ref, rk_ref, ones_ref, s0_ref,
                 y_ref, bonus_ref, g_ref, s1_ref, s_scr, qr_scr, qk_scr, qv_scr, qz_scr, *, L, nb, nc):
    c = pl.program_id(1)

    @pl.when(c == 0)
    def _():
        s_scr[...] = s0_ref[...]
        qr_scr[...] = sr_ref[...]
        qk_scr[...] = sk_ref[...]
        qv_scr[...] = sv_ref[...]
        qz_scr[...] = sz_ref[...]

    vals, bonus, g = _rwkv_prep((pr_ref, pk_ref, pv_ref, pz_ref), (qr_scr, qk_scr, qv_scr, qz_scr),
                                (mr_ref, mk_ref, mv_ref, mz_ref), w0_ref, a0_ref, wba_ref, gb_ref, kkw_ref,
                                kaw_ref, rk_ref, ones_ref, L, nb)
    bonus_ref[...] = bonus.reshape(nb, L, W)
    g_ref[...] = g.reshape(nb, L, W)
    chunk = lambda name, ib: vals[name][ib * L:(ib + 1) * L, :]
    row = lax.broadcasted_iota(jnp.int32, (L, L), 0)
    col = lax.broadcasted_iota(jnp.int32, (L, L), 1)
    eye = jnp.where(row == col, 1.0, 0.0)
    tri = jnp.where(row >= col, 1.0, 0.0).astype(BF)
    row2 = lax.broadcasted_iota(jnp.int32, (2 * L, 2 * L), 0)
    col2 = lax.broadcasted_iota(jnp.int32, (2 * L, 2 * L), 1)
    cc = jnp.where(col2 >= L, col2 - L, col2)
    keep = jnp.where(row2 < L, row2 - 1, row2 - L) >= cc
    sls = [slice(h * DH, (h + 1) * DH) for h in range(H)]
    chains = [(ib, h) for ib in range(nb) for h in range(H)]
    pre = {}
    for ib in range(nb):
        lw = chunk("lw", ib)
        cum = _mm_mask_l(tri, lw)
        p_in = jnp.exp(cum)
        p_inv = jnp.exp(-cum)
        p_last = p_in[L - 1:L, :]
        kh = chunk("k", ib) * p_inv
        bh = chunk("b", ib) * p_inv
        pre[ib] = dict(
            kr=jnp.concatenate([(chunk("kk", ib) * jnp.exp(cum - lw)).astype(BF),
                                (chunk("r", ib) * p_in).astype(BF)], axis=0),
            kb=jnp.concatenate([kh.astype(BF), bh.astype(BF)], axis=0),
            kbl=jnp.concatenate([(kh * p_last).astype(BF), (bh * p_last).astype(BF)], axis=0),
            v=chunk("v", ib).astype(BF), p_last=p_last)
    op = lambda u, name: pre[u[0]][name][:, sls[u[1]]]
    gm = {u: jnp.where(keep, _mm_nt(op(u, "kr"), op(u, "kb")), 0.0) for u in chains}
    akr = {u: gm[u][:, 0:L].astype(BF) for u in chains}
    r_b = {u: gm[u][L:2 * L, L:2 * L].astype(BF) for u in chains}
    xs = {u: -gm[u][0:L, L:2 * L] for u in chains}
    invs = {u: eye + xs[u] for u in chains}
    if L > 2:
        xs = {u: _mm(xs[u], xs[u]) for u in chains}
    n = 4
    while n < L:
        st = {u: _mm(jnp.concatenate([invs[u], xs[u]], axis=0), xs[u]) for u in chains}
        invs = {u: invs[u] + st[u][0:L] for u in chains}
        xs = {u: st[u][L:2 * L] for u in chains}
        n *= 2
    if L > 2:
        invs = {u: invs[u] + _mm(invs[u], xs[u]) for u in chains}
    akrv = {u: _mm(akr[u], op(u, "v")) for u in chains}
    ss = {u: s_scr[u] for u in chains}
    krs = {u: _mm_nt(op(u, "kr"), ss[u]) for u in chains}
    us = {u: _mm(invs[u], krs[u][0:L] + akrv[u][0:L]) for u in chains}
    ys = {u: krs[u][L:2 * L] + akrv[u][L:2 * L] - _mm(r_b[u], us[u]) for u in chains}
    for u in chains:
        vu = jnp.concatenate([op(u, "v"), (-us[u]).astype(BF)], axis=0)
        s_scr[u] = ss[u] * pre[u[0]]["p_last"][:, sls[u[1]]] + _mm_tn(vu, op(u, "kbl"))
    for ib in range(nb):
        y_ref[ib] = jnp.concatenate([ys[ib, h] for h in range(H)], axis=1)

    @pl.when(c == nc - 1)
    def _():
        s1_ref[...] = s_scr[...]


def _rwkv(p3, shift, mu, w0, a0, wba_bf, gb_bf, kkw, kaw, rk, ones_bd, s0, L, nb):
    b, t, _ = p3.shape
    nc = t // L
    vec = lambda: pl.BlockSpec((1, W), lambda i, c: (0, 0))
    carry = lambda width, j: pl.BlockSpec((nb, 1, width), lambda i, c: (i, 0, j))
    mix = lambda width, j: pl.BlockSpec((1, width), lambda i, c: (0, j))
    blk = lambda: pl.BlockSpec((nb, L, W), lambda i, c: (i, c, 0))
    st = pl.BlockSpec((nb, H, DH, DH), lambda i, c: (i, 0, 0, 0))
    sh3 = shift.reshape(b, 1, -1)
    mu2 = mu.reshape(1, -1)
    return pl.pallas_call(
        functools.partial(_rwkv_kernel, L=L, nb=nb, nc=nc),
        grid=(b // nb, nc),
        in_specs=[pl.BlockSpec((nb, L, W), lambda i, c: (i, c, 2)),
                  pl.BlockSpec((nb, L, W), lambda i, c: (i, c, 3)),
                  pl.BlockSpec((nb, L, W), lambda i, c: (i, c, 4)),
                  pl.BlockSpec((nb, L, 256), lambda i, c: (i, c, 10)),
                  carry(W, 0), carry(W, 1), carry(W, 2), carry(256, 3 * W // 256),
                  mix(W, 0), mix(W, 1), mix(W, 2), mix(256, 3 * W // 256),
                  vec(), vec(),
                  pl.BlockSpec((128, 2 * W), lambda i, c: (0, 0)),
                  pl.BlockSpec((LORA_G, W), lambda i, c: (0, 0)),
                  vec(), vec(), vec(),
                  pl.BlockSpec((W, W), lambda i, c: (0, 0)),
                  st],
        out_specs=[blk(), blk(), blk(), st],
        out_shape=[jax.ShapeDtypeStruct((b, t, W), F32)] * 3 + [jax.ShapeDtypeStruct((b, H, DH, DH), F32)],
        scratch_shapes=[pltpu.VMEM((nb, H, DH, DH), F32), pltpu.VMEM((nb, 1, W), F32), pltpu.VMEM((nb, 1, W), F32),
                        pltpu.VMEM((nb, 1, W), F32), pltpu.VMEM((nb, 1, 256), F32)],
        compiler_params=_params(("parallel", "arbitrary")),
        name="rwkv",
    )(p3, p3, p3, p3,
      sh3, sh3, sh3, sh3, mu2, mu2, mu2, mu2,
      w0.reshape(1, W), a0.reshape(1, W), wba_bf, gb_bf, kkw.reshape(1, W), kaw.reshape(1, W),
      rk.reshape(1, W), ones_bd, s0)


def _block_diag(w):
    nb, bw, _ = w.shape
    return (jnp.eye(nb, dtype=w.dtype)[:, None, :, None] * w[:, :, None, :]).reshape(nb * bw, nb * bw)


def _chunk(t, target):
    return target if t % target == 0 else t


def _scan_blocking(b, t, target):
    L = _chunk(t, target)
    return L, min(b, 8)


def _run_group(x, mods, st, wts):
    b, t, d = x.shape
    x2 = x.reshape(b * t, d)
    (mc, mn, mm, lh, lconv, ccb, rs, rsh) = st

    sh1, sc1, gt1, sh2, sc2, gt2 = mods[0]
    e = wts["even"]
    p = _in_proj(x2, wts["g_mix"][0], sh1, sc1, e["w_in"], t)
    p3 = p.reshape(b, t, IN_EVEN_PAD)
    gif_t = jnp.transpose(p3[:, :, 6 * W:6 * W + 2 * H], (0, 2, 1))
    hm, c1, n1, m1 = _mlstm(p3, gif_t, e["b_if"], mc[0], mn[0], mm[0], *_scan_blocking(b, t, 128))
    hl, lh1 = _lru(p3, lconv[0], lh[0], e["w_conv"], e["b_conv"], e["w_gate"], e["b_gate"], e["lam"],
                   _chunk(t, 256), min(b, 4 if t >= 256 else 8))
    assert t >= LRU_CONV - 1
    conv1 = p3[:, t - (LRU_CONV - 1):, 4 * W:5 * W]
    x2 = _out_proj(x2, hl.reshape(b * t, W), hm.reshape(b * t, W), p, 3, None, e["g_head"], jnp.zeros((W,), F32),
                   wts["ones_bd"], e["w_out"], gt1, t, True, RMS_EPS, True)
    m = wts["moe"][0]
    x2 = _moe(x2, wts["g_ffn"][0], sh2, sc2, gt2, m["wr_hi"], m["wr_lo"], m["br"], wts["w1"], wts["w3"], wts["w2"],
              0, wts["g_final"], t, False)

    sh1, sc1, gt1, sh2, sc2, gt2 = mods[1]
    o = wts["odd"]
    p = _in_proj(x2, wts["g_mix"][1], sh1, sc1, o["w_in"], t)
    p3 = p.reshape(b, t, IN_ODD)
    cc, cc1 = _conf(p3, ccb[0], o["b_glu"], o["w_dw"], o["b_dw"], o["g_ln"], o["b_ln"], _chunk(t, 256),
                    min(b, 1 if t >= 256 else 8))
    y, bonus, g, s1 = _rwkv(p3, rsh[0], o["mu"], o["w0"], o["a0"], o["wba"], o["gb"], o["kkw"], o["kaw"], o["rk"],
                            wts["ones_bd"], rs[0], *_scan_blocking(b, t, 64))
    sh_out = p3[:, t - 1, 2 * W:]
    x2 = _out_proj(x2, cc.reshape(b * t, W), y.reshape(b * t, W), g.reshape(b * t, W), 0, bonus.reshape(b * t, W),
                   o["g_gn"], o["b_gn"], wts["ones_bd"], o["w_out"], gt1, t, False, RWKV_GN_EPS, False)
    m = wts["moe"][1]
    y2 = _moe(x2, wts["g_ffn"][1], sh2, sc2, gt2, m["wr_hi"], m["wr_lo"], m["br"], wts["w1"], wts["w3"], wts["w2"],
              1, wts["g_final"], t, True)
    states = (c1[None], n1[None], m1.reshape(1, b, H), lh1.reshape(1, b, W), conv1[None], cc1[None], s1[None],
              sh_out[None])
    return y2.reshape(b, t, d), states


def kernel(x_prompt, x_sample, c_prompt, c_sample, state_mlstm_C, state_mlstm_n, state_mlstm_m, state_lru_h,
           cache_lru_conv, cache_conformer_conv, state_rwkv_S, cache_rwkv_shift, w_ada, b_ada, g_norm_mix,
           g_norm_ffn, w_in_even, b_mlstm_if, g_mlstm_head, w_lru_conv, b_lru_conv, w_lru_r, b_lru_r, w_lru_i,
           b_lru_i, lru_lambda, w_out_even, w_in_odd, b_glu, w_cc_dw, b_cc_dw, g_cc_ln, b_cc_ln, rwkv_mu,
           rwkv_w0, rwkv_wB, rwkv_a0, rwkv_aB, rwkv_gB, rwkv_kk, rwkv_ka, rwkv_rk, g_rwkv_gn, b_rwkv_gn,
           w_out_odd, w_router_g, b_router_g, w_router_e, b_router_e, w_exp_gate, w_exp_up, w_exp_down, g_final):
    bp, bs = x_prompt.shape[0], x_sample.shape[0]

    wi = w_in_even[0]
    gcol = 4 * W
    w_in_e = jnp.concatenate([wi[:, :gcol], wi[:, gcol + 2 * H:], wi[:, gcol:gcol + 2 * H],
                              jnp.zeros((D_MODEL, LANES - 2 * H), F32)], axis=1).astype(BF)
    even = dict(
        w_in=w_in_e, b_if=b_mlstm_if[0], g_head=g_mlstm_head[0], w_conv=w_lru_conv[0], b_conv=b_lru_conv[0],
        w_gate=jnp.concatenate([_block_diag(w_lru_r[0]), _block_diag(w_lru_i[0])], axis=1).astype(BF),
        b_gate=jnp.concatenate([b_lru_r[0], b_lru_i[0]]), lam=lru_lambda[0], w_out=w_out_even[0].astype(BF))
    zl = jnp.zeros((LORA_W, W), F32)
    odd = dict(
        w_in=w_in_odd[0].astype(BF), b_glu=b_glu[0], w_dw=w_cc_dw[0], b_dw=b_cc_dw[0], g_ln=g_cc_ln[0],
        b_ln=b_cc_ln[0], mu=rwkv_mu[0], w0=rwkv_w0[0], a0=rwkv_a0[0],
        wba=jnp.concatenate([jnp.concatenate([rwkv_wB[0], zl], axis=1),
                             jnp.concatenate([zl, rwkv_aB[0]], axis=1)], axis=0).astype(BF),
        gb=rwkv_gB[0].astype(BF), kkw=rwkv_kk[0], kaw=rwkv_ka[0], rk=rwkv_rk[0], g_gn=g_rwkv_gn[0],
        b_gn=b_rwkv_gn[0], w_out=w_out_odd[0].astype(BF))
    moe = []
    for l in range(DEPTH):
        wr = jnp.concatenate([w_router_g[l], w_router_e[l],
                              jnp.zeros((D_MODEL, LANES - N_GROUPS - N_EXPERTS), F32)], axis=1)
        wr_hi = wr.astype(BF)
        wr_lo = (wr - wr_hi.astype(F32)).astype(BF)
        br = jnp.concatenate([b_router_g[l], b_router_e[l],
                              jnp.zeros((LANES - N_GROUPS - N_EXPERTS,), F32)]).reshape(1, LANES)
        moe.append(dict(wr_hi=wr_hi, wr_lo=wr_lo, br=br))
    ones_bd = _block_diag(jnp.ones((H, DH, DH), F32)).astype(BF)
    wts = dict(even=even, odd=odd, moe=moe, g_mix=g_norm_mix, g_ffn=g_norm_ffn, g_final=g_final, ones_bd=ones_bd,
               w1=w_exp_gate.astype(BF).reshape(DEPTH * N_EXPERTS, D_MODEL, D_EXPERT),
               w3=w_exp_up.astype(BF).reshape(DEPTH * N_EXPERTS, D_MODEL, D_EXPERT),
               w2=w_exp_down.astype(BF).reshape(DEPTH * N_EXPERTS, D_EXPERT, D_MODEL))

    rows = 2 * bs
    assert bp <= bs
    c_all = jnp.concatenate([c_sample, c_prompt, jnp.zeros((rows - bs - bp, D_MODEL), F32)], axis=0)
    mod = _ada(c_all, w_ada, b_ada)

    def mods_of(lo):
        return [tuple(_Mod(mod, l * rows + lo, j) for j in range(6)) for l in range(DEPTH)]

    z = lambda *s: jnp.zeros(s, F32)
    st_p = (z(1, bp, H, DH, DH), z(1, bp, H, DH), z(1, bp, H), z(1, bp, W), z(1, bp, LRU_CONV - 1, W),
            z(1, bp, CONV_C - 1, W), z(1, bp, H, DH, DH), z(1, bp, 3 * W + LORA_W + LORA_A + LORA_G))
    st_s = (state_mlstm_C, state_mlstm_n, state_mlstm_m, state_lru_h, cache_lru_conv, cache_conformer_conv,
            state_rwkv_S, cache_rwkv_shift)
    y_p, out_p = _run_group(x_prompt, mods_of(bs), st_p, wts)
    y_s, out_s = _run_group(x_sample, mods_of(0), st_s, wts)
    return (y_p, y_s) + tuple(out_p) + tuple(out_s)
```

```python
import functools
from typing import NamedTuple

import jax
import jax.numpy as jnp
from jax import lax
from jax.experimental import pallas as pl
from jax.experimental.pallas import tpu as pltpu

F32 = jnp.float32
BF = jnp.bfloat16

D_MODEL = 1024
DEPTH = 2
H = 8
DH = 64
W = 512
LRU_CONV = 4
LRU_C = 8.0
CONV_C = 31
LORA_W = 64
LORA_A = 64
LORA_G = 128
RWKV_DECAY = 0.606531
RWKV_GN_EPS = 64e-5
N_GROUPS = 4
E_PER_GROUP = 4
N_EXPERTS = 16
D_EXPERT = 256
RMS_EPS = 1e-6
LN_EPS = 1e-5
LANES = 128
IN_EVEN_PAD = 6 * W + LANES
IN_ODD = 2 * W + 3 * W + LORA_W + LORA_A + LORA_G

ROW_TILE = 1024
MOE_ROW_TILE = 512
VMEM_LIMIT = 48 * 1024 * 1024


def _mm(a, b):
    return jnp.dot(a.astype(BF), b.astype(BF), preferred_element_type=F32)


def _mm_nt(a, b):
    return lax.dot_general(a.astype(BF), b.astype(BF), (((1,), (1,)), ((), ())),
                           preferred_element_type=F32)


def _mm_tn(a, b):
    return lax.dot_general(a.astype(BF), b.astype(BF), (((0,), (0,)), ((), ())),
                           preferred_element_type=F32)


def _split3(x):
    hi = x.astype(BF)
    r = x - hi.astype(F32)
    mid = r.astype(BF)
    lo = (r - mid.astype(F32)).astype(BF)
    return hi, mid, lo


def _mm_mask_l(mask, x):
    return sum(jnp.dot(mask, p, preferred_element_type=F32) for p in _split3(x))


def _mm_mask_r(x, mask):
    return sum(jnp.dot(p, mask, preferred_element_type=F32) for p in _split3(x))


def _sigmoid(x):
    return 1.0 / (1.0 + jnp.exp(-x))


def _rows(v, tm):
    nb, _, c = v.shape
    if nb == 1:
        return v[0]
    return jnp.broadcast_to(v, (nb, tm // nb, c)).reshape(tm, c)


class _Mod(NamedTuple):
    arr: jax.Array
    row0: int
    col: int


def _mod_spec(m, T, tm):
    if tm <= T:
        per = T // tm
        return pl.BlockSpec((1, 1, D_MODEL), lambda i, *_: (m.row0 + i // per, 0, m.col))
    nbk = tm // T
    assert m.row0 % nbk == 0
    return pl.BlockSpec((nbk, 1, D_MODEL), lambda i, *_: (m.row0 // nbk + i, 0, m.col))


def _norm_mod(x, g, sh, sc):
    y = x * lax.rsqrt(jnp.mean(x * x, axis=-1, keepdims=True) + RMS_EPS) * g
    return y * (1.0 + sc) + sh


def _params(sem):
    return pltpu.CompilerParams(dimension_semantics=sem, vmem_limit_bytes=VMEM_LIMIT)


def _ada_kernel(c_ref, w_ref, b_ref, o_ref):
    y = _mm(c_ref[...], w_ref[0]) + b_ref[0]
    o_ref[...] = y.reshape(o_ref.shape)


def _ada(c_all, w, b):
    nb = c_all.shape[0]
    tn = 1536
    return pl.pallas_call(
        _ada_kernel,
        grid=(DEPTH, 6 * D_MODEL // tn),
        in_specs=[pl.BlockSpec((nb, D_MODEL), lambda l, j: (0, 0)),
                  pl.BlockSpec((1, D_MODEL, tn), lambda l, j: (l, 0, j)),
                  pl.BlockSpec((1, 1, tn), lambda l, j: (l, 0, j))],
        out_specs=pl.BlockSpec((nb, 1, tn), lambda l, j: (l, 0, j)),
        out_shape=jax.ShapeDtypeStruct((DEPTH * nb, 1, 6 * D_MODEL), F32),
        compiler_params=_params(("parallel", "parallel")),
        name="ada",
    )(c_all, w, b.reshape(DEPTH, 1, 6 * D_MODEL))


def _inproj_kernel(x_ref, g_ref, sh_ref, sc_ref, w_ref, o_ref, h_scr):
    tm = x_ref.shape[0]

    @pl.when(pl.program_id(1) == 0)
    def _():
        h = _norm_mod(x_ref[...], g_ref[...], _rows(sh_ref[...], tm), _rows(sc_ref[...], tm))
        h_scr[...] = h.astype(BF)

    o_ref[...] = jnp.dot(h_scr[...], w_ref[...], preferred_element_type=F32)


INPROJ_ROW_TILE = 512


def _in_proj(x2, g, sh, sc, w_bf, T):
    n, d = x2.shape
    cols = w_bf.shape[1]
    tm = INPROJ_ROW_TILE
    tn = cols
    return pl.pallas_call(
        _inproj_kernel,
        grid=(n // tm, cols // tn),
        in_specs=[pl.BlockSpec((tm, d), lambda i, j: (i, 0)),
                  pl.BlockSpec((1, d), lambda i, j: (0, 0)),
                  _mod_spec(sh, T, tm), _mod_spec(sc, T, tm),
                  pl.BlockSpec((d, tn), lambda i, j: (0, j))],
        out_specs=pl.BlockSpec((tm, tn), lambda i, j: (i, j)),
        out_shape=jax.ShapeDtypeStruct((n, cols), F32),
        scratch_shapes=[pltpu.VMEM((tm, d), BF)],
        compiler_params=_params(("parallel", "arbitrary")),
        name="in_proj",
    )(x2, g.reshape(1, d), sh.arr, sc.arr, w_bf)


def _seg_sum(x, ones_bd):
    hi = x.astype(BF)
    lo = (x - hi.astype(F32)).astype(BF)
    return (jnp.dot(hi, ones_bd, preferred_element_type=F32)
            + jnp.dot(lo, ones_bd, preferred_element_type=F32))


def _outproj_kernel(*refs, pre_first, eps, has_add, sigmoid_mul):
    if has_add:
        x_ref, a_ref, pre_ref, mul_ref, add_ref, gain_ref, bias_ref, ones_ref, w_ref, gt_ref, o_ref = refs
    else:
        x_ref, a_ref, pre_ref, mul_ref, gain_ref, bias_ref, ones_ref, w_ref, gt_ref, o_ref = refs
    tm = x_ref.shape[0]
    pre = pre_ref[...]
    ones_bd = ones_ref[...]
    dev = pre - _seg_sum(pre, ones_bd) * (1.0 / DH)
    var = _seg_sum(dev * dev, ones_bd) * (1.0 / DH)
    y = dev * lax.rsqrt(var + eps) * gain_ref[...] + bias_ref[...]
    if has_add:
        y = y + add_ref[...]
    m = mul_ref[...]
    y = (y * (_sigmoid(m) if sigmoid_mul else m)).astype(BF)
    first, second = (y, a_ref[...]) if pre_first else (a_ref[...], y)
    mix = (jnp.dot(first, w_ref[0:W, :], preferred_element_type=F32)
           + jnp.dot(second, w_ref[W:2 * W, :], preferred_element_type=F32))
    o_ref[...] = x_ref[...] + _rows(gt_ref[...], tm) * mix


def _out_proj(x2, a2, pre2, mul2, mul_col, add2, gain, bias, ones_bd, w_bf, gt, T, pre_first, eps, sigmoid_mul):
    n, d = x2.shape
    tm = ROW_TILE
    has_add = add2 is not None
    row = lambda c=0: pl.BlockSpec((tm, W), lambda i, c=c: (i, c))
    vec = lambda: pl.BlockSpec((1, W), lambda i: (0, 0))
    in_specs = [pl.BlockSpec((tm, d), lambda i: (i, 0)), row(), row(), row(mul_col)]
    args = [x2, a2, pre2, mul2]
    if has_add:
        in_specs.append(row())
        args.append(add2)
    in_specs += [vec(), vec(), pl.BlockSpec((W, W), lambda i: (0, 0)), pl.BlockSpec((2 * W, d), lambda i: (0, 0)),
                 _mod_spec(gt, T, tm)]
    args += [gain.reshape(1, W), bias.reshape(1, W), ones_bd, w_bf, gt.arr]
    return pl.pallas_call(
        functools.partial(_outproj_kernel, pre_first=pre_first, eps=eps, has_add=has_add, sigmoid_mul=sigmoid_mul),
        grid=(n // tm,),
        in_specs=in_specs,
        out_specs=pl.BlockSpec((tm, d), lambda i: (i, 0)),
        out_shape=jax.ShapeDtypeStruct((n, d), F32),
        compiler_params=_params(("parallel",)),
        name="out_proj",
    )(*args)


def _route(logits):
    lane = lax.broadcasted_iota(jnp.int32, logits.shape, 1).astype(F32)
    neg = -jnp.inf
    is_g = lane < N_GROUPS
    lg = jnp.where(is_g, logits, neg)
    mg = jnp.max(lg, axis=1, keepdims=True)
    gsel = jnp.min(jnp.where(lg == mg, lane, float(LANES)), axis=1, keepdims=True)
    psum = jnp.sum(jnp.where(is_g, jnp.exp(lg - mg), 0.0), axis=1, keepdims=True)
    pg_sel = 1.0 / psum
    lo = N_GROUPS + E_PER_GROUP * gsel
    le = jnp.where((lane >= lo) & (lane < lo + E_PER_GROUP), logits, neg)
    v1 = jnp.max(le, axis=1, keepdims=True)
    i1 = jnp.min(jnp.where(le == v1, lane, float(LANES)), axis=1, keepdims=True)
    le2 = jnp.where(lane == i1, neg, le)
    v2 = jnp.max(le2, axis=1, keepdims=True)
    i2 = jnp.min(jnp.where(le2 == v2, lane, float(LANES)), axis=1, keepdims=True)
    e2 = jnp.exp(v2 - v1)
    p1 = 1.0 / (1.0 + e2)
    p2 = e2 / (1.0 + e2)
    return pg_sel * jnp.where(lane == i1, p1, jnp.where(lane == i2, p2, 0.0))


def _moe_kernel(x_ref, g_ref, sh_ref, sc_ref, gt_ref, wrh_ref, wrl_ref, br_ref, w1_ref, w3_ref, w2_ref,
                gf_ref, o_ref, *, final_norm):
    tm = x_ref.shape[0]
    h = _norm_mod(x_ref[...], g_ref[...], _rows(sh_ref[...], tm), _rows(sc_ref[...], tm))
    hb = h.astype(BF)
    hl = (h - hb.astype(F32)).astype(BF)
    logits = (jnp.dot(hb, wrh_ref[...], preferred_element_type=F32)
              + jnp.dot(hl, wrh_ref[...], preferred_element_type=F32)
              + jnp.dot(hb, wrl_ref[...], preferred_element_type=F32)) + br_ref[...]
    gate = _route(logits)
    lane = lax.broadcasted_iota(jnp.int32, gate.shape, 1)
    acc = None
    for e in range(N_EXPERTS):
        hg = jnp.dot(hb, w1_ref[e], preferred_element_type=F32)
        hu = jnp.dot(hb, w3_ref[e], preferred_element_type=F32)
        ge = jnp.sum(jnp.where(lane == e + N_GROUPS, gate, 0.0), axis=1, keepdims=True)
        hh = hg * _sigmoid(hg) * hu * ge
        part = jnp.dot(hh.astype(BF), w2_ref[e], preferred_element_type=F32)
        acc = part if acc is None else acc + part
    y = x_ref[...] + _rows(gt_ref[...], tm) * acc
    if final_norm:
        y = y * lax.rsqrt(jnp.mean(y * y, axis=-1, keepdims=True) + RMS_EPS) * gf_ref[...]
    o_ref[...] = y


def _moe(x2, g, sh, sc, gt, wr_hi, wr_lo, br, w1, w3, w2, layer, g_final, T, final_norm):
    n, d = x2.shape
    tm = MOE_ROW_TILE
    resident = lambda shape: pl.BlockSpec(shape, lambda i: (layer, 0, 0), pipeline_mode=pl.Buffered(1))
    return pl.pallas_call(
        functools.partial(_moe_kernel, final_norm=final_norm),
        grid=(n // tm,),
        in_specs=[pl.BlockSpec((tm, d), lambda i: (i, 0)),
                  pl.BlockSpec((1, d), lambda i: (0, 0)),
                  _mod_spec(sh, T, tm), _mod_spec(sc, T, tm), _mod_spec(gt, T, tm),
                  pl.BlockSpec((d, LANES), lambda i: (0, 0)),
                  pl.BlockSpec((d, LANES), lambda i: (0, 0)),
                  pl.BlockSpec((1, LANES), lambda i: (0, 0)),
                  resident((N_EXPERTS, d, D_EXPERT)), resident((N_EXPERTS, d, D_EXPERT)),
                  resident((N_EXPERTS, D_EXPERT, d)),
                  pl.BlockSpec((1, d), lambda i: (0, 0))],
        out_specs=pl.BlockSpec((tm, d), lambda i: (i, 0)),
        out_shape=jax.ShapeDtypeStruct((n, d), F32),
        compiler_params=_params(("parallel",)),
        name="moe",
    )(x2, g.reshape(1, d), sh.arr, sc.arr, gt.arr, wr_hi, wr_lo, br, w1, w3, w2, g_final.reshape(1, d))


def _log_sigmoid(x):
    return jnp.minimum(x, 0.0) - jnp.log(1.0 + jnp.exp(-jnp.abs(x)))


def _mlstm_kernel(q_ref, k_ref, v_ref, g_ref, gt_ref, bif_ref, bift_ref, c0_ref, n0_ref, m0_ref,
                  h_ref, c1_ref, n1_ref, m1_ref, st_scr, m_scr, *, L, nb, nc):
    c = pl.program_id(1)
    NP = H // 2
    r64 = lax.broadcasted_iota(jnp.int32, (DH, DH), 0)
    c64 = lax.broadcasted_iota(jnp.int32, (DH, DH), 1)
    eye64 = r64 == c64

    @pl.when(c == 0)
    def _():
        st_scr[...] = jnp.zeros_like(st_scr)
        for ib in range(nb):
            for h in range(H):
                p, o = divmod(h, 2)
                rs = slice(o * DH, (o + 1) * DH)
                st_scr[ib, p, rs, o * DH:(o + 1) * DH] = c0_ref[ib, h]
                n_col = jnp.sum(jnp.where(eye64, n0_ref[ib, h:h + 1, :], 0.0), axis=1, keepdims=True)
                st_scr[ib, p, rs, 2 * DH + o * DH:2 * DH + (o + 1) * DH] = jnp.broadcast_to(n_col, (DH, DH))
        m_scr[...] = m0_ref[...]

    row = lax.broadcasted_iota(jnp.int32, (L, L), 0)
    col = lax.broadcasted_iota(jnp.int32, (L, L), 1)
    causal = row >= col
    tri = jnp.where(causal, 1.0, 0.0).astype(BF)
    tri_u = jnp.where(row <= col, 1.0, 0.0).astype(BF)
    even = lax.broadcasted_iota(jnp.int32, (L, 2 * DH), 1) < DH
    row_s = lax.broadcasted_iota(jnp.int32, (2 * DH, 4 * DH), 0)
    lane_s = lax.broadcasted_iota(jnp.int32, (2 * DH, 4 * DH), 1)
    top = row_s < DH
    same_head = jnp.where(top, 0, DH) == jnp.bitwise_and(lane_s, DH)
    ones_l = jnp.ones((L, 2 * DH), BF)
    ibs = range(nb)
    g = [g_ref[ib] + bif_ref[...] for ib in ibs]
    gt = [gt_ref[ib] + bift_ref[...] for ib in ibs]
    bcum = [_mm_mask_l(tri, _log_sigmoid(g[ib])) for ib in ibs]
    bcum_t = [_mm_mask_r(_log_sigmoid(gt[ib]), tri_u) for ib in ibs]
    m_prev = [m_scr[ib] for ib in ibs]
    ch = [(ib, h) for ib in ibs for h in range(H)]
    prs = [(ib, p) for ib in ibs for p in range(NP)]
    pair_of = lambda u: (u[0], u[1] // 2)
    wide = lambda x: jnp.broadcast_to(x, (L, 2 * DH))
    lanes = {u: slice(u[1] * 2 * DH, (u[1] + 1) * 2 * DH) for u in prs}
    q_f = {u: q_ref[u[0], :, lanes[u]] for u in prs}
    q_b = {u: q_f[u].astype(BF) for u in prs}
    k_f = {u: k_ref[u[0], :, lanes[u]] * (DH ** -0.5) for u in prs}
    k_b = {u: k_f[u].astype(BF) for u in prs}
    rhs = {u: jnp.concatenate([v_ref[u[0], :, lanes[u]].astype(BF), ones_l], axis=1) for u in prs}
    odd = lax.broadcasted_iota(jnp.int32, (L, 2 * DH), 1) >= DH
    qk = {u: _mm_nt(jnp.where(odd if u[1] % 2 else even, q_f[pair_of(u)], 0.0).astype(BF), k_b[pair_of(u)])
          for u in ch}
    st = {u: st_scr[u[0], u[1]] for u in prs}
    qst = {u: _mm(q_b[u], st[u]) for u in prs}
    bc = {u: wide(bcum[u[0]][:, H + u[1]:H + u[1] + 1]) for u in ch}
    ic = {u: wide(g[u[0]][:, u[1]:u[1] + 1]) for u in ch}
    mp = {u: m_prev[u[0]][:, u[1]:u[1] + 1] for u in ch}
    dmat = {u: jnp.where(causal, bc[u][:, 0:L] + (gt[u[0]][u[1]:u[1] + 1, :] - bcum_t[u[0]][H + u[1]:H + u[1] + 1, :]),
                         -jnp.inf) for u in ch}
    g_inter = {u: bc[u] + mp[u] for u in ch}
    m_t = {u: jnp.maximum(g_inter[u], jnp.max(dmat[u], axis=1, keepdims=True)) for u in ch}
    s = {u: (qk[u] * jnp.exp(dmat[u] - m_t[u][:, 0:L])).astype(BF) for u in ch}
    w_inter = {u: jnp.exp(g_inter[u] - m_t[u]) for u in ch}
    sv = {u: _mm(s[u], rhs[pair_of(u)]) for u in ch}
    m_new = {u: m_t[u][L - 1:L, 0:1] for u in ch}
    b_last = {u: bcum[u[0]][L - 1:L, H + u[1]:H + u[1] + 1] for u in ch}
    w_s = {u: jnp.exp(b_last[u] - bc[u] + ic[u] - m_new[u]) for u in ch}
    decay = {u: jnp.exp(b_last[u] + mp[u] - m_new[u]) for u in ch}
    pick = lambda d, u: jnp.where(even, d[u[0], 2 * u[1]], d[u[0], 2 * u[1] + 1])
    hh = {}
    for u in prs:
        e, o = (u[0], 2 * u[1]), (u[0], 2 * u[1] + 1)
        w_pair = pick(w_inter, u)
        num = jnp.where(even, sv[e][:, 0:2 * DH], sv[o][:, 0:2 * DH]) + w_pair * qst[u][:, 0:2 * DH]
        den = jnp.where(even, sv[e][:, 2 * DH:], sv[o][:, 2 * DH:]) + w_pair * qst[u][:, 2 * DH:]
        hh[u] = num / jnp.maximum(jnp.abs(den), jnp.exp(-pick(m_t, u)))
        inc = _mm_tn((k_f[u] * pick(w_s, u)).astype(BF), rhs[u])
        st_scr[u[0], u[1]] = jnp.where(top, decay[e], decay[o]) * st[u] + jnp.where(same_head, inc, 0.0)
    for ib in ibs:
        h_ref[ib] = jnp.concatenate([hh[ib, p] for p in range(NP)], axis=1)
        m_scr[ib] = jnp.concatenate([m_new[ib, h] for h in range(H)], axis=1)

    @pl.when(c == nc - 1)
    def _():
        for ib in range(nb):
            for h in range(H):
                p, o = divmod(h, 2)
                rs = slice(o * DH, (o + 1) * DH)
                c1_ref[ib, h] = st_scr[ib, p, rs, o * DH:(o + 1) * DH]
                n_rep = st_scr[ib, p, rs, 2 * DH + o * DH:2 * DH + (o + 1) * DH]
                n1_ref[ib, h:h + 1, :] = jnp.sum(jnp.where(eye64, n_rep, 0.0), axis=0, keepdims=True)
        m1_ref[...] = m_scr[...]


def _mlstm(p3, gif_t, bif, c0, n0, m0, L, nb):
    b, t, _ = p3.shape
    nc = t // L
    col = lambda j: pl.BlockSpec((nb, L, W), lambda i, c, j=j: (i, c, j))
    bif_pad = jnp.zeros((1, LANES), F32).at[0, :2 * H].set(bif)
    return pl.pallas_call(
        functools.partial(_mlstm_kernel, L=L, nb=nb, nc=nc),
        grid=(b // nb, nc),
        in_specs=[col(0), col(1), col(2),
                  pl.BlockSpec((nb, L, LANES), lambda i, c: (i, c, 6 * W // LANES)),
                  pl.BlockSpec((nb, 2 * H, L), lambda i, c: (i, 0, c)),
                  pl.BlockSpec((1, LANES), lambda i, c: (0, 0)),
                  pl.BlockSpec((2 * H, 1), lambda i, c: (0, 0)),
                  pl.BlockSpec((nb, H, DH, DH), lambda i, c: (i, 0, 0, 0)),
                  pl.BlockSpec((nb, H, DH), lambda i, c: (i, 0, 0)),
                  pl.BlockSpec((nb, 1, H), lambda i, c: (i, 0, 0))],
        out_specs=[pl.BlockSpec((nb, L, W), lambda i, c: (i, c, 0)),
                   pl.BlockSpec((nb, H, DH, DH), lambda i, c: (i, 0, 0, 0)),
                   pl.BlockSpec((nb, H, DH), lambda i, c: (i, 0, 0)),
                   pl.BlockSpec((nb, 1, H), lambda i, c: (i, 0, 0))],
        out_shape=[jax.ShapeDtypeStruct((b, t, W), F32),
                   jax.ShapeDtypeStruct((b, H, DH, DH), F32),
                   jax.ShapeDtypeStruct((b, H, DH), F32),
                   jax.ShapeDtypeStruct((b, 1, H), F32)],
        scratch_shapes=[pltpu.VMEM((nb, H // 2, 2 * DH, 4 * DH), F32), pltpu.VMEM((nb, 1, H), F32)],
        compiler_params=_params(("parallel", "arbitrary")),
        name="mlstm",
    )(p3, p3, p3, p3, gif_t, bif_pad, bif.reshape(2 * H, 1), c0, n0, m0.reshape(b, 1, H))


def _gelu_tanh(x):
    return 0.5 * x * (1.0 + jnp.tanh(0.7978845608028654 * (x + 0.044715 * x * x * x)))


def _lru_kernel(xr_ref, xg_ref, cache_ref, h0_ref, wc_ref, bc_ref, wg_ref, bg_ref, lam_ref,
                o_ref, h1_ref, ext_scr, a_scr, u_scr, hs_scr, hc_scr, *, Tc, nb, nc):
    c = pl.program_id(1)
    K1 = LRU_CONV - 1

    @pl.when(c == 0)
    def _():
        ext_scr[:, 8 - K1:8, :] = cache_ref[...]
        hc_scr[...] = h0_ref[...]

    xcs = []
    for ib in range(nb):
        x = xr_ref[ib]
        ext_scr[ib, 8:8 + Tc, :] = x
        xc = bc_ref[...] + wc_ref[K1:K1 + 1, :] * x
        for d in range(1, LRU_CONV):
            xc = xc + wc_ref[K1 - d:K1 - d + 1, :] * ext_scr[ib, 8 - d:8 - d + Tc, :]
        ext_scr[ib, 8 - K1:8, :] = ext_scr[ib, 8 + Tc - K1:8 + Tc, :]
        xcs.append(xc)
    xc = jnp.concatenate(xcs, axis=0)
    gates = _mm(xc, wg_ref[...]) + bg_ref[...]
    r = _sigmoid(gates[:, 0:W])
    ig = _sigmoid(gates[:, W:2 * W])
    lam = lam_ref[...]
    softplus_neg = jnp.maximum(-lam, 0.0) + jnp.log(1.0 + jnp.exp(-jnp.abs(lam)))
    log_a = -LRU_C * r * softplus_neg
    a_scr[...] = jnp.exp(log_a).reshape(nb, Tc, W)
    th = jnp.tanh(log_a)
    one_minus_a2 = -2.0 * th / (1.0 - th)
    u_scr[...] = (jnp.sqrt(one_minus_a2) * (ig * xc)).reshape(nb, Tc, W)

    def body(t, hs):
        new = []
        for ib in range(nb):
            h = a_scr[ib, pl.ds(t, 1), :] * hs[ib] + u_scr[ib, pl.ds(t, 1), :]
            hs_scr[ib, pl.ds(t, 1), :] = h
            new.append(h)
        return tuple(new)

    h_fin = lax.fori_loop(0, Tc, body, tuple(hc_scr[ib] for ib in range(nb)), unroll=8)
    for ib in range(nb):
        hc_scr[ib] = h_fin[ib]
        o_ref[ib] = (hs_scr[ib] * _gelu_tanh(xg_ref[ib])).astype(BF)

    @pl.when(c == nc - 1)
    def _():
        h1_ref[...] = hc_scr[...]


def _lru(p3, cache, h0, w_conv, b_conv, wg_bf, bg, lam, Tc, nb):
    b, t, _ = p3.shape
    nc = t // Tc
    return pl.pallas_call(
        functools.partial(_lru_kernel, Tc=Tc, nb=nb, nc=nc),
        grid=(b // nb, nc),
        in_specs=[pl.BlockSpec((nb, Tc, W), lambda i, c: (i, c, 4)),
                  pl.BlockSpec((nb, Tc, W), lambda i, c: (i, c, 5)),
                  pl.BlockSpec((nb, LRU_CONV - 1, W), lambda i, c: (i, 0, 0)),
                  pl.BlockSpec((nb, 1, W), lambda i, c: (i, 0, 0)),
                  pl.BlockSpec((LRU_CONV, W), lambda i, c: (0, 0)),
                  pl.BlockSpec((1, W), lambda i, c: (0, 0)),
                  pl.BlockSpec((W, 2 * W), lambda i, c: (0, 0)),
                  pl.BlockSpec((1, 2 * W), lambda i, c: (0, 0)),
                  pl.BlockSpec((1, W), lambda i, c: (0, 0))],
        out_specs=[pl.BlockSpec((nb, Tc, W), lambda i, c: (i, c, 0)),
                   pl.BlockSpec((nb, 1, W), lambda i, c: (i, 0, 0))],
        out_shape=[jax.ShapeDtypeStruct((b, t, W), BF), jax.ShapeDtypeStruct((b, 1, W), F32)],
        scratch_shapes=[pltpu.VMEM((nb, 8 + Tc, W), F32), pltpu.VMEM((nb, Tc, W), F32), pltpu.VMEM((nb, Tc, W), F32),
                        pltpu.VMEM((nb, Tc, W), F32), pltpu.VMEM((nb, 1, W), F32)],
        compiler_params=_params(("parallel", "arbitrary")),
        name="lru",
    )(p3, p3, cache, h0.reshape(b, 1, W), w_conv, b_conv.reshape(1, W), wg_bf, bg.reshape(1, 2 * W),
      lam.reshape(1, W))


CONF_ROWS = 64


def _conf_kernel(u_ref, gte_ref, cache_ref, bu_ref, bg_ref, wdw_ref, bdw_ref, gln_ref, bln_ref,
                 o_ref, cache1_ref, ext_scr, sh_scr, *, Tc, nb, nc):
    c = pl.program_id(1)
    K1 = CONV_C - 1
    base = 32 - K1

    @pl.when(c == 0)
    def _():
        ext_scr[:, base:32, :] = cache_ref[...]

    for ib in range(nb):
        u = u_ref[ib] + bu_ref[...]
        gte = gte_ref[ib] + bg_ref[...]
        ext_scr[ib, 32:32 + Tc, :] = u * _sigmoid(gte)
        for b in range(1, 8):
            sh_scr[b - 1] = ext_scr[ib, b:b + Tc + 24, :]

        def window(off, r0, rb):
            a, b = divmod(off, 8)
            if b == 0:
                return ext_scr[ib, 8 * a + r0:8 * a + r0 + rb, :]
            return sh_scr[b - 1, 8 * a + r0:8 * a + r0 + rb, :]

        rb = min(CONF_ROWS, Tc)
        for r0 in range(0, Tc, rb):
            acc = bdw_ref[...] + wdw_ref[0:1, :] * window(base, r0, rb)
            for j in range(1, CONV_C):
                acc = acc + wdw_ref[j:j + 1, :] * window(base + j, r0, rb)
            mu = jnp.mean(acc, axis=1, keepdims=True)
            var = jnp.mean(jnp.square(acc - mu), axis=1, keepdims=True)
            y = (acc - mu) * lax.rsqrt(var + LN_EPS) * gln_ref[...] + bln_ref[...]
            o_ref[ib, r0:r0 + rb, :] = (y * _sigmoid(y)).astype(BF)
        ext_scr[ib, base:32, :] = ext_scr[ib, base + Tc:32 + Tc, :]

    @pl.when(c == nc - 1)
    def _():
        cache1_ref[...] = ext_scr[:, base:32, :]


def _conf(p3, cache, b_glu, w_dw, b_dw, g_ln, b_ln, Tc, nb):
    b, t, _ = p3.shape
    nc = t // Tc
    vec = lambda: pl.BlockSpec((1, W), lambda i, c: (0, 0))
    return pl.pallas_call(
        functools.partial(_conf_kernel, Tc=Tc, nb=nb, nc=nc),
        grid=(b // nb, nc),
        in_specs=[pl.BlockSpec((nb, Tc, W), lambda i, c: (i, c, 0)),
                  pl.BlockSpec((nb, Tc, W), lambda i, c: (i, c, 1)),
                  pl.BlockSpec((nb, CONV_C - 1, W), lambda i, c: (i, 0, 0)),
                  vec(), pl.BlockSpec((1, W), lambda i, c: (0, 1)),
                  pl.BlockSpec((CONV_C, W), lambda i, c: (0, 0)),
                  vec(), vec(), vec()],
        out_specs=[pl.BlockSpec((nb, Tc, W), lambda i, c: (i, c, 0)),
                   pl.BlockSpec((nb, CONV_C - 1, W), lambda i, c: (i, 0, 0))],
        out_shape=[jax.ShapeDtypeStruct((b, t, W), BF), jax.ShapeDtypeStruct((b, CONV_C - 1, W), F32)],
        scratch_shapes=[pltpu.VMEM((nb, 32 + Tc, W), F32), pltpu.VMEM((7, Tc + 24, W), F32)],
        compiler_params=_params(("parallel", "arbitrary")),
        name="conformer",
    )(p3, p3, cache, b_glu.reshape(1, 2 * W), b_glu.reshape(1, 2 * W), w_dw, b_dw.reshape(1, W),
      g_ln.reshape(1, W), b_ln.reshape(1, W))


def _rwkv_prep(x_refs, prev_scrs, mu_refs, w0_ref, a0_ref, wba_ref, gb_ref, kkw_ref, kaw_ref, rk_ref, ones_ref,
               L, nb):
    def shift_mix(x_ref, prev_scr, mu_ref):
        x = x_ref[...].reshape(nb * L, x_ref.shape[-1])
        first = jnp.bitwise_and(lax.broadcasted_iota(jnp.int32, x.shape, 0), L - 1) == 0
        carried = jnp.concatenate([jnp.broadcast_to(prev_scr[ib], (L, x.shape[1])) for ib in range(nb)], axis=0)
        prev = jnp.where(first, carried, pltpu.roll(x, 1, 0))
        for ib in range(nb):
            prev_scr[ib] = x[(ib + 1) * L - 1:(ib + 1) * L, :]
        return x + (prev - x) * mu_ref[...]

    r, k, v, z = (shift_mix(x, p, m) for x, p, m in zip(x_refs, prev_scrs, mu_refs))
    zwa = z[:, 0:128]
    lane = lax.broadcasted_iota(jnp.int32, zwa.shape, 1)
    wa = _mm(jnp.where(lane < LORA_W, jnp.tanh(zwa), zwa), wba_ref[...])
    lw = -RWKV_DECAY * _sigmoid(w0_ref[...] + wa[:, 0:W])
    a = _sigmoid(a0_ref[...] + wa[:, W:2 * W])
    g = _mm(_sigmoid(z[:, 128:256]), gb_ref[...])
    ones_bd = ones_ref[...]
    kk = k * kkw_ref[...]
    kk = kk * lax.rsqrt(_seg_sum(kk * kk, ones_bd) + 1e-12)
    k2 = k * (1.0 + (a - 1.0) * kaw_ref[...])
    bonus = _seg_sum(r * k2 * rk_ref[...], ones_bd) * v
    return dict(r=r, lw=lw, k=k2, v=v, kk=kk, b=kk * a), bonus, g


def _rwkv_kernel(pr_ref, pk_ref, pv_ref, pz_ref, sr_ref, sk_ref, sv_ref, sz_ref, mr_ref, mk_ref, mv_ref, mz_ref,
                 w0_ref, a0_ref, wba_ref, gb_ref, kkw_ref, kaw_ref, rk_ref, ones_ref, s0_ref,
                 y_ref, bonus_ref, g_ref, s1_ref, s_scr, qr_scr, qk_scr, qv_scr, qz_scr, *, L, nb, nc):
    c = pl.program_id(1)

    @pl.when(c == 0)
    def _():
        s_scr[...] = s0_ref[...]
        qr_scr[...] = sr_ref[...]
        qk_scr[...] = sk_ref[...]
        qv_scr[...] = sv_ref[...]
        qz_scr[...] = sz_ref[...]

    vals, bonus, g = _rwkv_prep((pr_ref, pk_ref, pv_ref, pz_ref), (qr_scr, qk_scr, qv_scr, qz_scr),
                                (mr_ref, mk_ref, mv_ref, mz_ref), w0_ref, a0_ref, wba_ref, gb_ref, kkw_ref,
                                kaw_ref, rk_ref, ones_ref, L, nb)
    bonus_ref[...] = bonus.reshape(nb, L, W)
    g_ref[...] = g.reshape(nb, L, W)
    chunk = lambda name, ib: vals[name][ib * L:(ib + 1) * L, :]
    row = lax.broadcasted_iota(jnp.int32, (L, L), 0)
    col = lax.broadcasted_iota(jnp.int32, (L, L), 1)
    eye = jnp.where(row == col, 1.0, 0.0)
    tri = jnp.where(row >= col, 1.0, 0.0).astype(BF)
    row2 = lax.broadcasted_iota(jnp.int32, (2 * L, 2 * L), 0)
    col2 = lax.broadcasted_iota(jnp.int32, (2 * L, 2 * L), 1)
    cc = jnp.where(col2 >= L, col2 - L, col2)
    keep = jnp.where(row2 < L, row2 - 1, row2 - L) >= cc
    sls = [slice(h * DH, (h + 1) * DH) for h in range(H)]
    chains = [(ib, h) for ib in range(nb) for h in range(H)]
    pre = {}
    for ib in range(nb):
        lw = chunk("lw", ib)
        cum = _mm_mask_l(tri, lw)
        p_in = jnp.exp(cum)
        p_inv = jnp.exp(-cum)
        p_last = p_in[L - 1:L, :]
        kh = chunk("k", ib) * p_inv
        bh = chunk("b", ib) * p_inv
        pre[ib] = dict(
            kr=jnp.concatenate([(chunk("kk", ib) * jnp.exp(cum - lw)).astype(BF),
                                (chunk("r", ib) * p_in).astype(BF)], axis=0),
            kb=jnp.concatenate([kh.astype(BF), bh.astype(BF)], axis=0),
            kbl=jnp.concatenate([(kh * p_last).astype(BF), (bh * p_last).astype(BF)], axis=0),
            v=chunk("v", ib).astype(BF), p_last=p_last)
    op = lambda u, name: pre[u[0]][name][:, sls[u[1]]]
    gm = {u: jnp.where(keep, _mm_nt(op(u, "kr"), op(u, "kb")), 0.0) for u in chains}
    akr = {u: gm[u][:, 0:L].astype(BF) for u in chains}
    r_b = {u: gm[u][L:2 * L, L:2 * L].astype(BF) for u in chains}
    xs = {u: -gm[u][0:L, L:2 * L] for u in chains}
    invs = {u: eye + xs[u] for u in chains}
    if L > 2:
        xs = {u: _mm(xs[u], xs[u]) for u in chains}
    n = 4
    while n < L:
        st = {u: _mm(jnp.concatenate([invs[u], xs[u]], axis=0), xs[u]) for u in chains}
        invs = {u: invs[u] + st[u][0:L] for u in chains}
        xs = {u: st[u][L:2 * L] for u in chains}
        n *= 2
    if L > 2:
        invs = {u: invs[u] + _mm(invs[u], xs[u]) for u in chains}
    akrv = {u: _mm(akr[u], op(u, "v")) for u in chains}
    ss = {u: s_scr[u] for u in chains}
    krs = {u: _mm_nt(op(u, "kr"), ss[u]) for u in chains}
    us = {u: _mm(invs[u], krs[u][0:L] + akrv[u][0:L]) for u in chains}
    ys = {u: krs[u][L:2 * L] + akrv[u][L:2 * L] - _mm(r_b[u], us[u]) for u in chains}
    for u in chains:
        vu = jnp.concatenate([op(u, "v"), (-us[u]).astype(BF)], axis=0)
        s_scr[u] = ss[u] * pre[u[0]]["p_last"][:, sls[u[1]]] + _mm_tn(vu, op(u, "kbl"))
    for ib in range(nb):
        y_ref[ib] = jnp.concatenate([ys[ib, h] for h in range(H)], axis=1)

    @pl.when(c == nc - 1)
    def _():
        s1_ref[...] = s_scr[...]


def _rwkv(p3, shift, mu, w0, a0, wba_bf, gb_bf, kkw, kaw, rk, ones_bd, s0, L, nb):
    b, t, _ = p3.shape
    nc = t // L
    vec = lambda: pl.BlockSpec((1, W), lambda i, c: (0, 0))
    carry = lambda width, j: pl.BlockSpec((nb, 1, width), lambda i, c: (i, 0, j))
    mix = lambda width, j: pl.BlockSpec((1, width), lambda i, c: (0, j))
    blk = lambda: pl.BlockSpec((nb, L, W), lambda i, c: (i, c, 0))
    st = pl.BlockSpec((nb, H, DH, DH), lambda i, c: (i, 0, 0, 0))
    sh3 = shift.reshape(b, 1, -1)
    mu2 = mu.reshape(1, -1)
    return pl.pallas_call(
        functools.partial(_rwkv_kernel, L=L, nb=nb, nc=nc),
        grid=(b // nb, nc),
        in_specs=[pl.BlockSpec((nb, L, W), lambda i, c: (i, c, 2)),
                  pl.BlockSpec((nb, L, W), lambda i, c: (i, c, 3)),
                  pl.BlockSpec((nb, L, W), lambda i, c: (i, c, 4)),
                  pl.BlockSpec((nb, L, 256), lambda i, c: (i, c, 10)),
                  carry(W, 0), carry(W, 1), carry(W, 2), carry(256, 3 * W // 256),
                  mix(W, 0), mix(W, 1), mix(W, 2), mix(256, 3 * W // 256),
                  vec(), vec(),
                  pl.BlockSpec((128, 2 * W), lambda i, c: (0, 0)),
                  pl.BlockSpec((LORA_G, W), lambda i, c: (0, 0)),
                  vec(), vec(), vec(),
                  pl.BlockSpec((W, W), lambda i, c: (0, 0)),
                  st],
        out_specs=[blk(), blk(), blk(), st],
        out_shape=[jax.ShapeDtypeStruct((b, t, W), F32)] * 3 + [jax.ShapeDtypeStruct((b, H, DH, DH), F32)],
        scratch_shapes=[pltpu.VMEM((nb, H, DH, DH), F32), pltpu.VMEM((nb, 1, W), F32), pltpu.VMEM((nb, 1, W), F32),
                        pltpu.VMEM((nb, 1, W), F32), pltpu.VMEM((nb, 1, 256), F32)],
        compiler_params=_params(("parallel", "arbitrary")),
        name="rwkv",
    )(p3, p3, p3, p3,
      sh3, sh3, sh3, sh3, mu2, mu2, mu2, mu2,
      w0.reshape(1, W), a0.reshape(1, W), wba_bf, gb_bf, kkw.reshape(1, W), kaw.reshape(1, W),
      rk.reshape(1, W), ones_bd, s0)


def _block_diag(w):
    nb, bw, _ = w.shape
    return (jnp.eye(nb, dtype=w.dtype)[:, None, :, None] * w[:, :, None, :]).reshape(nb * bw, nb * bw)


def _chunk(t, target):
    return target if t % target == 0 else t


def _scan_blocking(b, t, target):
    L = _chunk(t, target)
    return L, min(b, 8)


def _run_group(x, mods, st, wts):
    b, t, d = x.shape
    x2 = x.reshape(b * t, d)
    (mc, mn, mm, lh, lconv, ccb, rs, rsh) = st

    sh1, sc1, gt1, sh2, sc2, gt2 = mods[0]
    e = wts["even"]
    p = _in_proj(x2, wts["g_mix"][0], sh1, sc1, e["w_in"], t)
    p3 = p.reshape(b, t, IN_EVEN_PAD)
    gif_t = jnp.transpose(p3[:, :, 6 * W:6 * W + 2 * H], (0, 2, 1))
    hm, c1, n1, m1 = _mlstm(p3, gif_t, e["b_if"], mc[0], mn[0], mm[0], *_scan_blocking(b, t, 128))
    hl, lh1 = _lru(p3, lconv[0], lh[0], e["w_conv"], e["b_conv"], e["w_gate"], e["b_gate"], e["lam"],
                   _chunk(t, 256), min(b, 4 if t >= 256 else 8))
    assert t >= LRU_CONV - 1
    conv1 = p3[:, t - (LRU_CONV - 1):, 4 * W:5 * W]
    x2 = _out_proj(x2, hl.reshape(b * t, W), hm.reshape(b * t, W), p, 3, None, e["g_head"], jnp.zeros((W,), F32),
                   wts["ones_bd"], e["w_out"], gt1, t, True, RMS_EPS, True)
    m = wts["moe"][0]
    x2 = _moe(x2, wts["g_ffn"][0], sh2, sc2, gt2, m["wr_hi"], m["wr_lo"], m["br"], wts["w1"], wts["w3"], wts["w2"],
              0, wts["g_final"], t, False)

    sh1, sc1, gt1, sh2, sc2, gt2 = mods[1]
    o = wts["odd"]
    p = _in_proj(x2, wts["g_mix"][1], sh1, sc1, o["w_in"], t)
    p3 = p.reshape(b, t, IN_ODD)
    cc, cc1 = _conf(p3, ccb[0], o["b_glu"], o["w_dw"], o["b_dw"], o["g_ln"], o["b_ln"], _chunk(t, 512),
                    min(b, 1 if t >= 256 else 8))
    y, bonus, g, s1 = _rwkv(p3, rsh[0], o["mu"], o["w0"], o["a0"], o["wba"], o["gb"], o["kkw"], o["kaw"], o["rk"],
                            wts["ones_bd"], rs[0], *_scan_blocking(b, t, 64))
    sh_out = p3[:, t - 1, 2 * W:]
    x2 = _out_proj(x2, cc.reshape(b * t, W), y.reshape(b * t, W), g.reshape(b * t, W), 0, bonus.reshape(b * t, W),
                   o["g_gn"], o["b_gn"], wts["ones_bd"], o["w_out"], gt1, t, False, RWKV_GN_EPS, False)
    m = wts["moe"][1]
    y2 = _moe(x2, wts["g_ffn"][1], sh2, sc2, gt2, m["wr_hi"], m["wr_lo"], m["br"], wts["w1"], wts["w3"], wts["w2"],
              1, wts["g_final"], t, True)
    states = (c1[None], n1[None], m1.reshape(1, b, H), lh1.reshape(1, b, W), conv1[None], cc1[None], s1[None],
              sh_out[None])
    return y2.reshape(b, t, d), states


def kernel(x_prompt, x_sample, c_prompt, c_sample, state_mlstm_C, state_mlstm_n, state_mlstm_m, state_lru_h,
           cache_lru_conv, cache_conformer_conv, state_rwkv_S, cache_rwkv_shift, w_ada, b_ada, g_norm_mix,
           g_norm_ffn, w_in_even, b_mlstm_if, g_mlstm_head, w_lru_conv, b_lru_conv, w_lru_r, b_lru_r, w_lru_i,
           b_lru_i, lru_lambda, w_out_even, w_in_odd, b_glu, w_cc_dw, b_cc_dw, g_cc_ln, b_cc_ln, rwkv_mu,
           rwkv_w0, rwkv_wB, rwkv_a0, rwkv_aB, rwkv_gB, rwkv_kk, rwkv_ka, rwkv_rk, g_rwkv_gn, b_rwkv_gn,
           w_out_odd, w_router_g, b_router_g, w_router_e, b_router_e, w_exp_gate, w_exp_up, w_exp_down, g_final):
    bp, bs = x_prompt.shape[0], x_sample.shape[0]

    wi = w_in_even[0]
    gcol = 4 * W
    w_in_e = jnp.concatenate([wi[:, :gcol], wi[:, gcol + 2 * H:], wi[:, gcol:gcol + 2 * H],
                              jnp.zeros((D_MODEL, LANES - 2 * H), F32)], axis=1).astype(BF)
    even = dict(
        w_in=w_in_e, b_if=b_mlstm_if[0], g_head=g_mlstm_head[0], w_conv=w_lru_conv[0], b_conv=b_lru_conv[0],
        w_gate=jnp.concatenate([_block_diag(w_lru_r[0]), _block_diag(w_lru_i[0])], axis=1).astype(BF),
        b_gate=jnp.concatenate([b_lru_r[0], b_lru_i[0]]), lam=lru_lambda[0], w_out=w_out_even[0].astype(BF))
    zl = jnp.zeros((LORA_W, W), F32)
    odd = dict(
        w_in=w_in_odd[0].astype(BF), b_glu=b_glu[0], w_dw=w_cc_dw[0], b_dw=b_cc_dw[0], g_ln=g_cc_ln[0],
        b_ln=b_cc_ln[0], mu=rwkv_mu[0], w0=rwkv_w0[0], a0=rwkv_a0[0],
        wba=jnp.concatenate([jnp.concatenate([rwkv_wB[0], zl], axis=1),
                             jnp.concatenate([zl, rwkv_aB[0]], axis=1)], axis=0).astype(BF),
        gb=rwkv_gB[0].astype(BF), kkw=rwkv_kk[0], kaw=rwkv_ka[0], rk=rwkv_rk[0], g_gn=g_rwkv_gn[0],
        b_gn=b_rwkv_gn[0], w_out=w_out_odd[0].astype(BF))
    moe = []
    for l in range(DEPTH):
        wr = jnp.concatenate([w_router_g[l], w_router_e[l],
                              jnp.zeros((D_MODEL, LANES - N_GROUPS - N_EXPERTS), F32)], axis=1)
        wr_hi = wr.astype(BF)
        wr_lo = (wr - wr_hi.astype(F32)).astype(BF)
        br = jnp.concatenate([b_router_g[l], b_router_e[l],
                              jnp.zeros((LANES - N_GROUPS - N_EXPERTS,), F32)]).reshape(1, LANES)
        moe.append(dict(wr_hi=wr_hi, wr_lo=wr_lo, br=br))
    ones_bd = _block_diag(jnp.ones((H, DH, DH), F32)).astype(BF)
    wts = dict(even=even, odd=odd, moe=moe, g_mix=g_norm_mix, g_ffn=g_norm_ffn, g_final=g_final, ones_bd=ones_bd,
               w1=w_exp_gate.astype(BF).reshape(DEPTH * N_EXPERTS, D_MODEL, D_EXPERT),
               w3=w_exp_up.astype(BF).reshape(DEPTH * N_EXPERTS, D_MODEL, D_EXPERT),
               w2=w_exp_down.astype(BF).reshape(DEPTH * N_EXPERTS, D_EXPERT, D_MODEL))

    rows = 2 * bs
    assert bp <= bs
    c_all = jnp.concatenate([c_sample, c_prompt, jnp.zeros((rows - bs - bp, D_MODEL), F32)], axis=0)
    mod = _ada(c_all, w_ada, b_ada)

    def mods_of(lo):
        return [tuple(_Mod(mod, l * rows + lo, j) for j in range(6)) for l in range(DEPTH)]

    z = lambda *s: jnp.zeros(s, F32)
    st_p = (z(1, bp, H, DH, DH), z(1, bp, H, DH), z(1, bp, H), z(1, bp, W), z(1, bp, LRU_CONV - 1, W),
            z(1, bp, CONV_C - 1, W), z(1, bp, H, DH, DH), z(1, bp, 3 * W + LORA_W + LORA_A + LORA_G))
    st_s = (state_mlstm_C, state_mlstm_n, state_mlstm_m, state_lru_h, cache_lru_conv, cache_conformer_conv,
            state_rwkv_S, cache_rwkv_shift)
    y_p, out_p = _run_group(x_prompt, mods_of(bs), st_p, wts)
    y_s, out_s = _run_group(x_sample, mods_of(0), st_s, wts)
    return (y_p, y_s) + tuple(out_p) + tuple(out_s)
```

```python
import functools
from typing import NamedTuple

import jax
import jax.numpy as jnp
from jax import lax
from jax.experimental import pallas as pl
from jax.experimental.pallas import tpu as pltpu

F32 = jnp.float32
BF = jnp.bfloat16

D_MODEL = 1024
DEPTH = 2
H = 8
DH = 64
W = 512
LRU_CONV = 4
LRU_C = 8.0
CONV_C = 31
LORA_W = 64
LORA_A = 64
LORA_G = 128
RWKV_DECAY = 0.606531
RWKV_GN_EPS = 64e-5
N_GROUPS = 4
E_PER_GROUP = 4
N_EXPERTS = 16
D_EXPERT = 256
RMS_EPS = 1e-6
LN_EPS = 1e-5
LANES = 128
IN_EVEN_PAD = 6 * W + LANES
IN_ODD = 2 * W + 3 * W + LORA_W + LORA_A + LORA_G

ROW_TILE = 1024
MOE_ROW_TILE = 512
VMEM_LIMIT = 48 * 1024 * 1024


def _mm(a, b):
    return jnp.dot(a.astype(BF), b.astype(BF), preferred_element_type=F32)


def _mm_nt(a, b):
    return lax.dot_general(a.astype(BF), b.astype(BF), (((1,), (1,)), ((), ())),
                           preferred_element_type=F32)


def _mm_tn(a, b):
    return lax.dot_general(a.astype(BF), b.astype(BF), (((0,), (0,)), ((), ())),
                           preferred_element_type=F32)


def _split3(x):
    hi = x.astype(BF)
    r = x - hi.astype(F32)
    mid = r.astype(BF)
    lo = (r - mid.astype(F32)).astype(BF)
    return hi, mid, lo


def _mm_mask_l(mask, x):
    return sum(jnp.dot(mask, p, preferred_element_type=F32) for p in _split3(x))


def _mm_mask_r(x, mask):
    return sum(jnp.dot(p, mask, preferred_element_type=F32) for p in _split3(x))


def _sigmoid(x):
    return 1.0 / (1.0 + jnp.exp(-x))


def _rows(v, tm):
    nb, _, c = v.shape
    if nb == 1:
        return v[0]
    return jnp.broadcast_to(v, (nb, tm // nb, c)).reshape(tm, c)


class _Mod(NamedTuple):
    arr: jax.Array
    row0: int
    col: int


def _mod_spec(m, T, tm):
    if tm <= T:
        per = T // tm
        return pl.BlockSpec((1, 1, D_MODEL), lambda i, *_: (m.row0 + i // per, 0, m.col))
    nbk = tm // T
    assert m.row0 % nbk == 0
    return pl.BlockSpec((nbk, 1, D_MODEL), lambda i, *_: (m.row0 // nbk + i, 0, m.col))


def _norm_mod(x, g, sh, sc):
    y = x * lax.rsqrt(jnp.mean(x * x, axis=-1, keepdims=True) + RMS_EPS) * g
    return y * (1.0 + sc) + sh


def _params(sem):
    return pltpu.CompilerParams(dimension_semantics=sem, vmem_limit_bytes=VMEM_LIMIT)


def _ada_kernel(c_ref, w_ref, b_ref, o_ref):
    y = _mm(c_ref[...], w_ref[0]) + b_ref[0]
    o_ref[...] = y.reshape(o_ref.shape)


def _ada(c_all, w, b):
    nb = c_all.shape[0]
    tn = 1536
    return pl.pallas_call(
        _ada_kernel,
        grid=(DEPTH, 6 * D_MODEL // tn),
        in_specs=[pl.BlockSpec((nb, D_MODEL), lambda l, j: (0, 0)),
                  pl.BlockSpec((1, D_MODEL, tn), lambda l, j: (l, 0, j)),
                  pl.BlockSpec((1, 1, tn), lambda l, j: (l, 0, j))],
        out_specs=pl.BlockSpec((nb, 1, tn), lambda l, j: (l, 0, j)),
        out_shape=jax.ShapeDtypeStruct((DEPTH * nb, 1, 6 * D_MODEL), F32),
        compiler_params=_params(("parallel", "parallel")),
        name="ada",
    )(c_all, w, b.reshape(DEPTH, 1, 6 * D_MODEL))


def _inproj_kernel(x_ref, g_ref, sh_ref, sc_ref, w_ref, o_ref, h_scr):
    tm = x_ref.shape[0]

    @pl.when(pl.program_id(1) == 0)
    def _():
        h = _norm_mod(x_ref[...], g_ref[...], _rows(sh_ref[...], tm), _rows(sc_ref[...], tm))
        h_scr[...] = h.astype(BF)

    o_ref[...] = jnp.dot(h_scr[...], w_ref[...], preferred_element_type=F32)


INPROJ_ROW_TILE = 512


def _in_proj(x2, g, sh, sc, w_bf, T):
    n, d = x2.shape
    cols = w_bf.shape[1]
    tm = INPROJ_ROW_TILE
    tn = cols
    return pl.pallas_call(
        _inproj_kernel,
        grid=(n // tm, cols // tn),
        in_specs=[pl.BlockSpec((tm, d), lambda i, j: (i, 0)),
                  pl.BlockSpec((1, d), lambda i, j: (0, 0)),
                  _mod_spec(sh, T, tm), _mod_spec(sc, T, tm),
                  pl.BlockSpec((d, tn), lambda i, j: (0, j))],
        out_specs=pl.BlockSpec((tm, tn), lambda i, j: (i, j)),
        out_shape=jax.ShapeDtypeStruct((n, cols), F32),
        scratch_shapes=[pltpu.VMEM((tm, d), BF)],
        compiler_params=_params(("parallel", "arbitrary")),
        name="in_proj",
    )(x2, g.reshape(1, d), sh.arr, sc.arr, w_bf)


def _seg_sum(x, ones_bd):
    hi = x.astype(BF)
    lo = (x - hi.astype(F32)).astype(BF)
    return (jnp.dot(hi, ones_bd, preferred_element_type=F32)
            + jnp.dot(lo, ones_bd, preferred_element_type=F32))


def _outproj_kernel(*refs, pre_first, eps, has_add, sigmoid_mul):
    if has_add:
        x_ref, a_ref, pre_ref, mul_ref, add_ref, gain_ref, bias_ref, ones_ref, w_ref, gt_ref, o_ref = refs
    else:
        x_ref, a_ref, pre_ref, mul_ref, gain_ref, bias_ref, ones_ref, w_ref, gt_ref, o_ref = refs
    tm = x_ref.shape[0]
    pre = pre_ref[...]
    ones_bd = ones_ref[...]
    dev = pre - _seg_sum(pre, ones_bd) * (1.0 / DH)
    var = _seg_sum(dev * dev, ones_bd) * (1.0 / DH)
    y = dev * lax.rsqrt(var + eps) * gain_ref[...] + bias_ref[...]
    if has_add:
        y = y + add_ref[...]
    m = mul_ref[...]
    y = (y * (_sigmoid(m) if sigmoid_mul else m)).astype(BF)
    first, second = (y, a_ref[...]) if pre_first else (a_ref[...], y)
    mix = (jnp.dot(first, w_ref[0:W, :], preferred_element_type=F32)
           + jnp.dot(second, w_ref[W:2 * W, :], preferred_element_type=F32))
    o_ref[...] = x_ref[...] + _rows(gt_ref[...], tm) * mix


def _out_proj(x2, a2, pre2, mul2, mul_col, add2, gain, bias, ones_bd, w_bf, gt, T, pre_first, eps, sigmoid_mul):
    n, d = x2.shape
    tm = ROW_TILE
    has_add = add2 is not None
    row = lambda c=0: pl.BlockSpec((tm, W), lambda i, c=c: (i, c))
    vec = lambda: pl.BlockSpec((1, W), lambda i: (0, 0))
    in_specs = [pl.BlockSpec((tm, d), lambda i: (i, 0)), row(), row(), row(mul_col)]
    args = [x2, a2, pre2, mul2]
    if has_add:
        in_specs.append(row())
        args.append(add2)
    in_specs += [vec(), vec(), pl.BlockSpec((W, W), lambda i: (0, 0)), pl.BlockSpec((2 * W, d), lambda i: (0, 0)),
                 _mod_spec(gt, T, tm)]
    args += [gain.reshape(1, W), bias.reshape(1, W), ones_bd, w_bf, gt.arr]
    return pl.pallas_call(
        functools.partial(_outproj_kernel, pre_first=pre_first, eps=eps, has_add=has_add, sigmoid_mul=sigmoid_mul),
        grid=(n // tm,),
        in_specs=in_specs,
        out_specs=pl.BlockSpec((tm, d), lambda i: (i, 0)),
        out_shape=jax.ShapeDtypeStruct((n, d), F32),
        compiler_params=_params(("parallel",)),
        name="out_proj",
    )(*args)


def _route(logits):
    lane = lax.broadcasted_iota(jnp.int32, logits.shape, 1).astype(F32)
    neg = -jnp.inf
    is_g = lane < N_GROUPS
    lg = jnp.where(is_g, logits, neg)
    mg = jnp.max(lg, axis=1, keepdims=True)
    gsel = jnp.min(jnp.where(lg == mg, lane, float(LANES)), axis=1, keepdims=True)
    psum = jnp.sum(jnp.where(is_g, jnp.exp(lg - mg), 0.0), axis=1, keepdims=True)
    pg_sel = 1.0 / psum
    lo = N_GROUPS + E_PER_GROUP * gsel
    le = jnp.where((lane >= lo) & (lane < lo + E_PER_GROUP), logits, neg)
    v1 = jnp.max(le, axis=1, keepdims=True)
    i1 = jnp.min(jnp.where(le == v1, lane, float(LANES)), axis=1, keepdims=True)
    le2 = jnp.where(lane == i1, neg, le)
    v2 = jnp.max(le2, axis=1, keepdims=True)
    i2 = jnp.min(jnp.where(le2 == v2, lane, float(LANES)), axis=1, keepdims=True)
    e2 = jnp.exp(v2 - v1)
    p1 = 1.0 / (1.0 + e2)
    p2 = e2 / (1.0 + e2)
    return pg_sel * jnp.where(lane == i1, p1, jnp.where(lane == i2, p2, 0.0))


def _moe_kernel(x_ref, g_ref, sh_ref, sc_ref, gt_ref, wrh_ref, wrl_ref, br_ref, w1_ref, w3_ref, w2_ref,
                gf_ref, o_ref, *, final_norm):
    tm = x_ref.shape[0]
    h = _norm_mod(x_ref[...], g_ref[...], _rows(sh_ref[...], tm), _rows(sc_ref[...], tm))
    hb = h.astype(BF)
    hl = (h - hb.astype(F32)).astype(BF)
    logits = (jnp.dot(hb, wrh_ref[...], preferred_element_type=F32)
              + jnp.dot(hl, wrh_ref[...], preferred_element_type=F32)
              + jnp.dot(hb, wrl_ref[...], preferred_element_type=F32)) + br_ref[...]
    gate = _route(logits)
    lane = lax.broadcasted_iota(jnp.int32, gate.shape, 1)
    acc = None
    for e in range(N_EXPERTS):
        hg = jnp.dot(hb, w1_ref[e], preferred_element_type=F32)
        hu = jnp.dot(hb, w3_ref[e], preferred_element_type=F32)
        ge = jnp.sum(jnp.where(lane == e + N_GROUPS, gate, 0.0), axis=1, keepdims=True)
        hh = hg * _sigmoid(hg) * hu * ge
        part = jnp.dot(hh.astype(BF), w2_ref[e], preferred_element_type=F32)
        acc = part if acc is None else acc + part
    y = x_ref[...] + _rows(gt_ref[...], tm) * acc
    if final_norm:
        y = y * lax.rsqrt(jnp.mean(y * y, axis=-1, keepdims=True) + RMS_EPS) * gf_ref[...]
    o_ref[...] = y


def _moe(x2, g, sh, sc, gt, wr_hi, wr_lo, br, w1, w3, w2, layer, g_final, T, final_norm):
    n, d = x2.shape
    tm = MOE_ROW_TILE
    resident = lambda shape: pl.BlockSpec(shape, lambda i: (layer, 0, 0), pipeline_mode=pl.Buffered(1))
    return pl.pallas_call(
        functools.partial(_moe_kernel, final_norm=final_norm),
        grid=(n // tm,),
        in_specs=[pl.BlockSpec((tm, d), lambda i: (i, 0)),
                  pl.BlockSpec((1, d), lambda i: (0, 0)),
                  _mod_spec(sh, T, tm), _mod_spec(sc, T, tm), _mod_spec(gt, T, tm),
                  pl.BlockSpec((d, LANES), lambda i: (0, 0)),
                  pl.BlockSpec((d, LANES), lambda i: (0, 0)),
                  pl.BlockSpec((1, LANES), lambda i: (0, 0)),
                  resident((N_EXPERTS, d, D_EXPERT)), resident((N_EXPERTS, d, D_EXPERT)),
                  resident((N_EXPERTS, D_EXPERT, d)),
                  pl.BlockSpec((1, d), lambda i: (0, 0))],
        out_specs=pl.BlockSpec((tm, d), lambda i: (i, 0)),
        out_shape=jax.ShapeDtypeStruct((n, d), F32),
        compiler_params=_params(("parallel",)),
        name="moe",
    )(x2, g.reshape(1, d), sh.arr, sc.arr, gt.arr, wr_hi, wr_lo, br, w1, w3, w2, g_final.reshape(1, d))


def _log_sigmoid(x):
    return jnp.minimum(x, 0.0) - jnp.log(1.0 + jnp.exp(-jnp.abs(x)))


def _mlstm_kernel(q_ref, k_ref, v_ref, g_ref, gt_ref, bif_ref, bift_ref, c0_ref, n0_ref, m0_ref,
                  h_ref, c1_ref, n1_ref, m1_ref, st_scr, m_scr, *, L, nb, nc):
    c = pl.program_id(1)
    NP = H // 2
    r64 = lax.broadcasted_iota(jnp.int32, (DH, DH), 0)
    c64 = lax.broadcasted_iota(jnp.int32, (DH, DH), 1)
    eye64 = r64 == c64

    @pl.when(c == 0)
    def _():
        st_scr[...] = jnp.zeros_like(st_scr)
        for ib in range(nb):
            for h in range(H):
                p, o = divmod(h, 2)
                rs = slice(o * DH, (o + 1) * DH)
                st_scr[ib, p, rs, o * DH:(o + 1) * DH] = c0_ref[ib, h]
                n_col = jnp.sum(jnp.where(eye64, n0_ref[ib, h:h + 1, :], 0.0), axis=1, keepdims=True)
                st_scr[ib, p, rs, 2 * DH + o * DH:2 * DH + (o + 1) * DH] = jnp.broadcast_to(n_col, (DH, DH))
        m_scr[...] = m0_ref[...]

    row = lax.broadcasted_iota(jnp.int32, (L, L), 0)
    col = lax.broadcasted_iota(jnp.int32, (L, L), 1)
    causal = row >= col
    tri = jnp.where(causal, 1.0, 0.0).astype(BF)
    tri_u = jnp.where(row <= col, 1.0, 0.0).astype(BF)
    even = lax.broadcasted_iota(jnp.int32, (L, 2 * DH), 1) < DH
    row_s = lax.broadcasted_iota(jnp.int32, (2 * DH, 4 * DH), 0)
    lane_s = lax.broadcasted_iota(jnp.int32, (2 * DH, 4 * DH), 1)
    top = row_s < DH
    same_head = jnp.where(top, 0, DH) == jnp.bitwise_and(lane_s, DH)
    ones_l = jnp.ones((L, 2 * DH), BF)
    ibs = range(nb)
    g = [g_ref[ib] + bif_ref[...] for ib in ibs]
    gt = [gt_ref[ib] + bift_ref[...] for ib in ibs]
    bcum = [_mm_mask_l(tri, _log_sigmoid(g[ib])) for ib in ibs]
    bcum_t = [_mm_mask_r(_log_sigmoid(gt[ib]), tri_u) for ib in ibs]
    m_prev = [m_scr[ib] for ib in ibs]
    ch = [(ib, h) for ib in ibs for h in range(H)]
    prs = [(ib, p) for ib in ibs for p in range(NP)]
    pair_of = lambda u: (u[0], u[1] // 2)
    wide = lambda x: jnp.broadcast_to(x, (L, 2 * DH))
    lanes = {u: slice(u[1] * 2 * DH, (u[1] + 1) * 2 * DH) for u in prs}
    q_f = {u: q_ref[u[0], :, lanes[u]] for u in prs}
    q_b = {u: q_f[u].astype(BF) for u in prs}
    k_f = {u: k_ref[u[0], :, lanes[u]] * (DH ** -0.5) for u in prs}
    k_b = {u: k_f[u].astype(BF) for u in prs}
    rhs = {u: jnp.concatenate([v_ref[u[0], :, lanes[u]].astype(BF), ones_l], axis=1) for u in prs}
    odd = lax.broadcasted_iota(jnp.int32, (L, 2 * DH), 1) >= DH
    qk = {u: _mm_nt(jnp.where(odd if u[1] % 2 else even, q_f[pair_of(u)], 0.0).astype(BF), k_b[pair_of(u)])
          for u in ch}
    st = {u: st_scr[u[0], u[1]] for u in prs}
    qst = {u: _mm(q_b[u], st[u]) for u in prs}
    bc = {u: wide(bcum[u[0]][:, H + u[1]:H + u[1] + 1]) for u in ch}
    ic = {u: wide(g[u[0]][:, u[1]:u[1] + 1]) for u in ch}
    mp = {u: m_prev[u[0]][:, u[1]:u[1] + 1] for u in ch}
    dmat = {u: jnp.where(causal, bc[u][:, 0:L] + (gt[u[0]][u[1]:u[1] + 1, :] - bcum_t[u[0]][H + u[1]:H + u[1] + 1, :]),
                         -jnp.inf) for u in ch}
    g_inter = {u: bc[u] + mp[u] for u in ch}
    m_t = {u: jnp.maximum(g_inter[u], jnp.max(dmat[u], axis=1, keepdims=True)) for u in ch}
    s = {u: (qk[u] * jnp.exp(dmat[u] - m_t[u][:, 0:L])).astype(BF) for u in ch}
    w_inter = {u: jnp.exp(g_inter[u] - m_t[u]) for u in ch}
    sv = {u: _mm(s[u], rhs[pair_of(u)]) for u in ch}
    m_new = {u: m_t[u][L - 1:L, 0:1] for u in ch}
    b_last = {u: bcum[u[0]][L - 1:L, H + u[1]:H + u[1] + 1] for u in ch}
    w_s = {u: jnp.exp(b_last[u] - bc[u] + ic[u] - m_new[u]) for u in ch}
    decay = {u: jnp.exp(b_last[u] + mp[u] - m_new[u]) for u in ch}
    pick = lambda d, u: jnp.where(even, d[u[0], 2 * u[1]], d[u[0], 2 * u[1] + 1])
    hh = {}
    for u in prs:
        e, o = (u[0], 2 * u[1]), (u[0], 2 * u[1] + 1)
        w_pair = pick(w_inter, u)
        num = jnp.where(even, sv[e][:, 0:2 * DH], sv[o][:, 0:2 * DH]) + w_pair * qst[u][:, 0:2 * DH]
        den = jnp.where(even, sv[e][:, 2 * DH:], sv[o][:, 2 * DH:]) + w_pair * qst[u][:, 2 * DH:]
        hh[u] = num / jnp.maximum(jnp.abs(den), jnp.exp(-pick(m_t, u)))
        inc = _mm_tn((k_f[u] * pick(w_s, u)).astype(BF), rhs[u])
        st_scr[u[0], u[1]] = jnp.where(top, decay[e], decay[o]) * st[u] + jnp.where(same_head, inc, 0.0)
    for ib in ibs:
        h_ref[ib] = jnp.concatenate([hh[ib, p] for p in range(NP)], axis=1)
        m_scr[ib] = jnp.concatenate([m_new[ib, h] for h in range(H)], axis=1)

    @pl.when(c == nc - 1)
    def _():
        for ib in range(nb):
            for h in range(H):
                p, o = divmod(h, 2)
                rs = slice(o * DH, (o + 1) * DH)
                c1_ref[ib, h] = st_scr[ib, p, rs, o * DH:(o + 1) * DH]
                n_rep = st_scr[ib, p, rs, 2 * DH + o * DH:2 * DH + (o + 1) * DH]
                n1_ref[ib, h:h + 1, :] = jnp.sum(jnp.where(eye64, n_rep, 0.0), axis=0, keepdims=True)
        m1_ref[...] = m_scr[...]


def _mlstm_kernel2(q_ref, k_ref, v_ref, g_ref, gt_ref, bif_ref, bift_ref, c0_ref, n0_ref, m0_ref,
                   h_ref, c1_ref, n1_ref, m1_ref, st_scr, m_scr, *, L, ns, nb, nc):
    c = pl.program_id(1)
    NP = H // 2
    W2 = 2 * L
    r64 = lax.broadcasted_iota(jnp.int32, (DH, DH), 0)
    c64 = lax.broadcasted_iota(jnp.int32, (DH, DH), 1)
    eye64 = r64 == c64

    @pl.when(c == 0)
    def _():
        st_scr[...] = jnp.zeros_like(st_scr)
        for ib in range(nb):
            for h in range(H):
                p, o = divmod(h, 2)
                rs = slice(o * DH, (o + 1) * DH)
                st_scr[ib, p, rs, o * DH:(o + 1) * DH] = c0_ref[ib, h]
                n_col = jnp.sum(jnp.where(eye64, n0_ref[ib, h:h + 1, :], 0.0), axis=1, keepdims=True)
                st_scr[ib, p, rs, 2 * DH + o * DH:2 * DH + (o + 1) * DH] = jnp.broadcast_to(n_col, (DH, DH))
        m_scr[...] = m0_ref[...]

    row = lax.broadcasted_iota(jnp.int32, (L, L), 0)
    col = lax.broadcasted_iota(jnp.int32, (L, L), 1)
    tri = jnp.where(row >= col, 1.0, 0.0).astype(BF)
    tri_u = jnp.where(row <= col, 1.0, 0.0).astype(BF)
    row2 = lax.broadcasted_iota(jnp.int32, (L, W2), 0)
    col2 = lax.broadcasted_iota(jnp.int32, (L, W2), 1)
    even_s = col2 < L
    causal2 = row2 >= jnp.where(even_s, col2, col2 - L)
    lane_p = lax.broadcasted_iota(jnp.int32, (L, 2 * DH), 1)
    even = lane_p < DH
    odd = lane_p >= DH
    even_r = jnp.bitwise_and(lax.broadcasted_iota(jnp.int32, (L, 4 * DH), 1), DH) == 0
    row_s = lax.broadcasted_iota(jnp.int32, (2 * DH, 4 * DH), 0)
    lane_s = lax.broadcasted_iota(jnp.int32, (2 * DH, 4 * DH), 1)
    top = row_s < DH
    same_head = jnp.where(top, 0, DH) == jnp.bitwise_and(lane_s, DH)
    ones_l = jnp.ones((L, 2 * DH), BF)
    ibs = range(nb)
    ch = [(ib, h) for ib in ibs for h in range(H)]
    prs = [(ib, p) for ib in ibs for p in range(NP)]
    wide = lambda x: jnp.broadcast_to(x, (L, 2 * DH))
    lanes = {u: slice(u[1] * 2 * DH, (u[1] + 1) * 2 * DH) for u in prs}
    pick = lambda d, u: jnp.where(even, d[u[0], 2 * u[1]], d[u[0], 2 * u[1] + 1])
    pick_s = lambda d, u: jnp.where(even_s, d[u[0], 2 * u[1]][:, 0:W2], d[u[0], 2 * u[1] + 1][:, 0:W2])
    m_prev = [m_scr[ib] for ib in ibs]
    for j in range(ns):
        rows = slice(j * L, (j + 1) * L)
        g = [g_ref[ib, rows, :] + bif_ref[...] for ib in ibs]
        gt = [gt_ref[ib, :, rows] + bift_ref[...] for ib in ibs]
        bcum = [_mm_mask_l(tri, _log_sigmoid(g[ib])) for ib in ibs]
        bcum_t = [_mm_mask_r(_log_sigmoid(gt[ib]), tri_u) for ib in ibs]
        q_b = {u: q_ref[u[0], rows, lanes[u]].astype(BF) for u in prs}
        k_f = {u: k_ref[u[0], rows, lanes[u]] * (DH ** -0.5) for u in prs}
        kst = {u: jnp.concatenate([jnp.where(even, k_f[u], 0.0), jnp.where(odd, k_f[u], 0.0)], axis=0).astype(BF)
               for u in prs}
        rhs = {u: jnp.concatenate([v_ref[u[0], rows, lanes[u]].astype(BF), ones_l], axis=1) for u in prs}
        rhs2 = {u: jnp.concatenate([jnp.where(even_r, rhs[u], 0), jnp.where(even_r, 0, rhs[u])], axis=0)
                for u in prs}
        qk2 = {u: _mm_nt(q_b[u], kst[u]) for u in prs}
        st = {u: st_scr[u[0], u[1]] for u in prs}
        qst = {u: _mm(q_b[u], st[u]) for u in prs}
        bc = {u: wide(bcum[u[0]][:, H + u[1]:H + u[1] + 1]) for u in ch}
        ic = {u: wide(g[u[0]][:, u[1]:u[1] + 1]) for u in ch}
        mp = {u: m_prev[u[0]][:, u[1]:u[1] + 1] for u in ch}
        xrow = {u: gt[u[0]][u[1]:u[1] + 1, :] - bcum_t[u[0]][H + u[1]:H + u[1] + 1, :] for u in ch}
        dmat2 = {u: jnp.where(causal2, pick_s(bc, u)
                              + jnp.concatenate([xrow[u[0], 2 * u[1]], xrow[u[0], 2 * u[1] + 1]], axis=1), -jnp.inf)
                 for u in prs}
        m_in = {}
        for u in prs:
            m_in[u[0], 2 * u[1]] = jnp.max(jnp.where(even_s, dmat2[u], -jnp.inf), axis=1, keepdims=True)
            m_in[u[0], 2 * u[1] + 1] = jnp.max(jnp.where(even_s, -jnp.inf, dmat2[u]), axis=1, keepdims=True)
        g_inter = {u: bc[u] + mp[u] for u in ch}
        m_t = {u: jnp.maximum(g_inter[u], m_in[u]) for u in ch}
        s2 = {u: (qk2[u] * jnp.exp(dmat2[u] - pick_s(m_t, u))).astype(BF) for u in prs}
        sv2 = {u: _mm(s2[u], rhs2[u]) for u in prs}
        w_inter = {u: jnp.exp(g_inter[u] - m_t[u]) for u in ch}
        m_new = {u: m_t[u][L - 1:L, 0:1] for u in ch}
        b_last = {u: bcum[u[0]][L - 1:L, H + u[1]:H + u[1] + 1] for u in ch}
        w_s = {u: jnp.exp(b_last[u] - bc[u] + ic[u] - m_new[u]) for u in ch}
        decay = {u: jnp.exp(b_last[u] + mp[u] - m_new[u]) for u in ch}
        hh = {}
        for u in prs:
            e, o = (u[0], 2 * u[1]), (u[0], 2 * u[1] + 1)
            w_pair = pick(w_inter, u)
            num = sv2[u][:, 0:2 * DH] + w_pair * qst[u][:, 0:2 * DH]
            den = sv2[u][:, 2 * DH:] + w_pair * qst[u][:, 2 * DH:]
            hh[u] = num / jnp.maximum(jnp.abs(den), jnp.exp(-pick(m_t, u)))
            inc = _mm_tn((k_f[u] * pick(w_s, u)).astype(BF), rhs[u])
            st_scr[u[0], u[1]] = jnp.where(top, decay[e], decay[o]) * st[u] + jnp.where(same_head, inc, 0.0)
        for ib in ibs:
            h_ref[ib, rows, :] = jnp.concatenate([hh[ib, p] for p in range(NP)], axis=1)
        m_prev = [jnp.concatenate([m_new[ib, h] for h in range(H)], axis=1) for ib in ibs]
    for ib in ibs:
        m_scr[ib] = m_prev[ib]

    @pl.when(c == nc - 1)
    def _():
        for ib in range(nb):
            for h in range(H):
                p, o = divmod(h, 2)
                rs = slice(o * DH, (o + 1) * DH)
                c1_ref[ib, h] = st_scr[ib, p, rs, o * DH:(o + 1) * DH]
                n_rep = st_scr[ib, p, rs, 2 * DH + o * DH:2 * DH + (o + 1) * DH]
                n1_ref[ib, h:h + 1, :] = jnp.sum(jnp.where(eye64, n_rep, 0.0), axis=0, keepdims=True)
        m1_ref[...] = m_scr[...]


def _mlstm(p3, gif_t, bif, c0, n0, m0, Lb, nb):
    b, t, _ = p3.shape
    nc = t // Lb
    ns = max(1, Lb // DH)
    L = Lb
    col = lambda j: pl.BlockSpec((nb, L, W), lambda i, c, j=j: (i, c, j))
    bif_pad = jnp.zeros((1, LANES), F32).at[0, :2 * H].set(bif)
    return pl.pallas_call(
        functools.partial(_mlstm_kernel2, L=Lb // ns, ns=ns, nb=nb, nc=nc),
        grid=(b // nb, nc),
        in_specs=[col(0), col(1), col(2),
                  pl.BlockSpec((nb, L, LANES), lambda i, c: (i, c, 6 * W // LANES)),
                  pl.BlockSpec((nb, 2 * H, L), lambda i, c: (i, 0, c)),
                  pl.BlockSpec((1, LANES), lambda i, c: (0, 0)),
                  pl.BlockSpec((2 * H, 1), lambda i, c: (0, 0)),
                  pl.BlockSpec((nb, H, DH, DH), lambda i, c: (i, 0, 0, 0)),
                  pl.BlockSpec((nb, H, DH), lambda i, c: (i, 0, 0)),
                  pl.BlockSpec((nb, 1, H), lambda i, c: (i, 0, 0))],
        out_specs=[pl.BlockSpec((nb, L, W), lambda i, c: (i, c, 0)),
                   pl.BlockSpec((nb, H, DH, DH), lambda i, c: (i, 0, 0, 0)),
                   pl.BlockSpec((nb, H, DH), lambda i, c: (i, 0, 0)),
                   pl.BlockSpec((nb, 1, H), lambda i, c: (i, 0, 0))],
        out_shape=[jax.ShapeDtypeStruct((b, t, W), F32),
                   jax.ShapeDtypeStruct((b, H, DH, DH), F32),
                   jax.ShapeDtypeStruct((b, H, DH), F32),
                   jax.ShapeDtypeStruct((b, 1, H), F32)],
        scratch_shapes=[pltpu.VMEM((nb, H // 2, 2 * DH, 4 * DH), F32), pltpu.VMEM((nb, 1, H), F32)],
        compiler_params=_params(("parallel", "arbitrary")),
        name="mlstm",
    )(p3, p3, p3, p3, gif_t, bif_pad, bif.reshape(2 * H, 1), c0, n0, m0.reshape(b, 1, H))


def _gelu_tanh(x):
    return 0.5 * x * (1.0 + jnp.tanh(0.7978845608028654 * (x + 0.044715 * x * x * x)))


def _lru_kernel(xr_ref, xg_ref, cache_ref, h0_ref, wc_ref, bc_ref, wg_ref, bg_ref, lam_ref,
                o_ref, h1_ref, ext_scr, a_scr, u_scr, hs_scr, hc_scr, *, Tc, nb, nc):
    c = pl.program_id(1)
    K1 = LRU_CONV - 1

    @pl.when(c == 0)
    def _():
        ext_scr[:, 8 - K1:8, :] = cache_ref[...]
        hc_scr[...] = h0_ref[...]

    xcs = []
    for ib in range(nb):
        x = xr_ref[ib]
        ext_scr[ib, 8:8 + Tc, :] = x
        xc = bc_ref[...] + wc_ref[K1:K1 + 1, :] * x
        for d in range(1, LRU_CONV):
            xc = xc + wc_ref[K1 - d:K1 - d + 1, :] * ext_scr[ib, 8 - d:8 - d + Tc, :]
        ext_scr[ib, 8 - K1:8, :] = ext_scr[ib, 8 + Tc - K1:8 + Tc, :]
        xcs.append(xc)
    xc = jnp.concatenate(xcs, axis=0)
    gates = _mm(xc, wg_ref[...]) + bg_ref[...]
    r = _sigmoid(gates[:, 0:W])
    ig = _sigmoid(gates[:, W:2 * W])
    lam = lam_ref[...]
    softplus_neg = jnp.maximum(-lam, 0.0) + jnp.log(1.0 + jnp.exp(-jnp.abs(lam)))
    log_a = -LRU_C * r * softplus_neg
    a_scr[...] = jnp.exp(log_a).reshape(nb, Tc, W)
    th = jnp.tanh(log_a)
    one_minus_a2 = -2.0 * th / (1.0 - th)
    u_scr[...] = (jnp.sqrt(one_minus_a2) * (ig * xc)).reshape(nb, Tc, W)

    def body(t, hs):
        new = []
        for ib in range(nb):
            h = a_scr[ib, pl.ds(t, 1), :] * hs[ib] + u_scr[ib, pl.ds(t, 1), :]
            hs_scr[ib, pl.ds(t, 1), :] = h
            new.append(h)
        return tuple(new)

    h_fin = lax.fori_loop(0, Tc, body, tuple(hc_scr[ib] for ib in range(nb)), unroll=8)
    for ib in range(nb):
        hc_scr[ib] = h_fin[ib]
        o_ref[ib] = (hs_scr[ib] * _gelu_tanh(xg_ref[ib])).astype(BF)

    @pl.when(c == nc - 1)
    def _():
        h1_ref[...] = hc_scr[...]


def _lru(p3, cache, h0, w_conv, b_conv, wg_bf, bg, lam, Tc, nb):
    b, t, _ = p3.shape
    nc = t // Tc
    return pl.pallas_call(
        functools.partial(_lru_kernel, Tc=Tc, nb=nb, nc=nc),
        grid=(b // nb, nc),
        in_specs=[pl.BlockSpec((nb, Tc, W), lambda i, c: (i, c, 4)),
                  pl.BlockSpec((nb, Tc, W), lambda i, c: (i, c, 5)),
                  pl.BlockSpec((nb, LRU_CONV - 1, W), lambda i, c: (i, 0, 0)),
                  pl.BlockSpec((nb, 1, W), lambda i, c: (i, 0, 0)),
                  pl.BlockSpec((LRU_CONV, W), lambda i, c: (0, 0)),
                  pl.BlockSpec((1, W), lambda i, c: (0, 0)),
                  pl.BlockSpec((W, 2 * W), lambda i, c: (0, 0)),
                  pl.BlockSpec((1, 2 * W), lambda i, c: (0, 0)),
                  pl.BlockSpec((1, W), lambda i, c: (0, 0))],
        out_specs=[pl.BlockSpec((nb, Tc, W), lambda i, c: (i, c, 0)),
                   pl.BlockSpec((nb, 1, W), lambda i, c: (i, 0, 0))],
        out_shape=[jax.ShapeDtypeStruct((b, t, W), BF), jax.ShapeDtypeStruct((b, 1, W), F32)],
        scratch_shapes=[pltpu.VMEM((nb, 8 + Tc, W), F32), pltpu.VMEM((nb, Tc, W), F32), pltpu.VMEM((nb, Tc, W), F32),
                        pltpu.VMEM((nb, Tc, W), F32), pltpu.VMEM((nb, 1, W), F32)],
        compiler_params=_params(("parallel", "arbitrary")),
        name="lru",
    )(p3, p3, cache, h0.reshape(b, 1, W), w_conv, b_conv.reshape(1, W), wg_bf, bg.reshape(1, 2 * W),
      lam.reshape(1, W))


CONF_ROWS = 64


def _conf_kernel(u_ref, gte_ref, cache_ref, bu_ref, bg_ref, wdw_ref, bdw_ref, gln_ref, bln_ref,
                 o_ref, cache1_ref, ext_scr, sh_scr, *, Tc, nb, nc):
    c = pl.program_id(1)
    K1 = CONV_C - 1
    base = 32 - K1

    @pl.when(c == 0)
    def _():
        ext_scr[:, base:32, :] = cache_ref[...]

    for ib in range(nb):
        u = u_ref[ib] + bu_ref[...]
        gte = gte_ref[ib] + bg_ref[...]
        ext_scr[ib, 32:32 + Tc, :] = u * _sigmoid(gte)
        for b in range(1, 8):
            sh_scr[b - 1] = ext_scr[ib, b:b + Tc + 24, :]

        def window(off, r0, rb):
            a, b = divmod(off, 8)
            if b == 0:
                return ext_scr[ib, 8 * a + r0:8 * a + r0 + rb, :]
            return sh_scr[b - 1, 8 * a + r0:8 * a + r0 + rb, :]

        rb = min(CONF_ROWS, Tc)
        for r0 in range(0, Tc, rb):
            acc = bdw_ref[...] + wdw_ref[0:1, :] * window(base, r0, rb)
            for j in range(1, CONV_C):
                acc = acc + wdw_ref[j:j + 1, :] * window(base + j, r0, rb)
            mu = jnp.mean(acc, axis=1, keepdims=True)
            var = jnp.mean(jnp.square(acc - mu), axis=1, keepdims=True)
            y = (acc - mu) * lax.rsqrt(var + LN_EPS) * gln_ref[...] + bln_ref[...]
            o_ref[ib, r0:r0 + rb, :] = (y * _sigmoid(y)).astype(BF)
        ext_scr[ib, base:32, :] = ext_scr[ib, base + Tc:32 + Tc, :]

    @pl.when(c == nc - 1)
    def _():
        cache1_ref[...] = ext_scr[:, base:32, :]


def _conf(p3, cache, b_glu, w_dw, b_dw, g_ln, b_ln, Tc, nb):
    b, t, _ = p3.shape
    nc = t // Tc
    vec = lambda: pl.BlockSpec((1, W), lambda i, c: (0, 0))
    return pl.pallas_call(
        functools.partial(_conf_kernel, Tc=Tc, nb=nb, nc=nc),
        grid=(b // nb, nc),
        in_specs=[pl.BlockSpec((nb, Tc, W), lambda i, c: (i, c, 0)),
                  pl.BlockSpec((nb, Tc, W), lambda i, c: (i, c, 1)),
                  pl.BlockSpec((nb, CONV_C - 1, W), lambda i, c: (i, 0, 0)),
                  vec(), pl.BlockSpec((1, W), lambda i, c: (0, 1)),
                  pl.BlockSpec((CONV_C, W), lambda i, c: (0, 0)),
                  vec(), vec(), vec()],
        out_specs=[pl.BlockSpec((nb, Tc, W), lambda i, c: (i, c, 0)),
                   pl.BlockSpec((nb, CONV_C - 1, W), lambda i, c: (i, 0, 0))],
        out_shape=[jax.ShapeDtypeStruct((b, t, W), BF), jax.ShapeDtypeStruct((b, CONV_C - 1, W), F32)],
        scratch_shapes=[pltpu.VMEM((nb, 32 + Tc, W), F32), pltpu.VMEM((7, Tc + 24, W), F32)],
        compiler_params=_params(("parallel", "arbitrary")),
        name="conformer",
    )(p3, p3, cache, b_glu.reshape(1, 2 * W), b_glu.reshape(1, 2 * W), w_dw, b_dw.reshape(1, W),
      g_ln.reshape(1, W), b_ln.reshape(1, W))


def _rwkv_prep(x_refs, prev_scrs, mu_refs, w0_ref, a0_ref, wba_ref, gb_ref, kkw_ref, kaw_ref, rk_ref, ones_ref,
               L, nb):
    def shift_mix(x_ref, prev_scr, mu_ref):
        x = x_ref[...].reshape(nb * L, x_ref.shape[-1])
        first = jnp.bitwise_and(lax.broadcasted_iota(jnp.int32, x.shape, 0), L - 1) == 0
        carried = jnp.concatenate([jnp.broadcast_to(prev_scr[ib], (L, x.shape[1])) for ib in range(nb)], axis=0)
        prev = jnp.where(first, carried, pltpu.roll(x, 1, 0))
        for ib in range(nb):
            prev_scr[ib] = x[(ib + 1) * L - 1:(ib + 1) * L, :]
        return x + (prev - x) * mu_ref[...]

    r, k, v, z = (shift_mix(x, p, m) for x, p, m in zip(x_refs, prev_scrs, mu_refs))
    zwa = z[:, 0:128]
    lane = lax.broadcasted_iota(jnp.int32, zwa.shape, 1)
    wa = _mm(jnp.where(lane < LORA_W, jnp.tanh(zwa), zwa), wba_ref[...])
    lw = -RWKV_DECAY * _sigmoid(w0_ref[...] + wa[:, 0:W])
    a = _sigmoid(a0_ref[...] + wa[:, W:2 * W])
    g = _mm(_sigmoid(z[:, 128:256]), gb_ref[...])
    ones_bd = ones_ref[...]
    kk = k * kkw_ref[...]
    kk = kk * lax.rsqrt(_seg_sum(kk * kk, ones_bd) + 1e-12)
    k2 = k * (1.0 + (a - 1.0) * kaw_ref[...])
    bonus = _seg_sum(r * k2 * rk_ref[...], ones_bd) * v
    return dict(r=r, lw=lw, k=k2, v=v, kk=kk, b=kk * a), bonus, g


def _rwkv_kernel(pr_ref, pk_ref, pv_ref, pz_ref, sr_ref, sk_ref, sv_ref, sz_ref, mr_ref, mk_ref, mv_ref, mz_ref,
                 w0_ref, a0_ref, wba_ref, gb_ref, kkw_ref, kaw_ref, rk_ref, ones_ref, s0_ref,
                 y_ref, bonus_ref, g_ref, s1_ref, s_scr, qr_scr, qk_scr, qv_scr, qz_scr, *, L, nb, nc):
    c = pl.program_id(1)

    @pl.when(c == 0)
    def _():
        s_scr[...] = s0_ref[...]
        qr_scr[...] = sr_ref[...]
        qk_scr[...] = sk_ref[...]
        qv_scr[...] = sv_ref[...]
        qz_scr[...] = sz_ref[...]

    vals, bonus, g = _rwkv_prep((pr_ref, pk_ref, pv_ref, pz_ref), (qr_scr, qk_scr, qv_scr, qz_scr),
                                (mr_ref, mk_ref, mv_ref, mz_ref), w0_ref, a0_ref, wba_ref, gb_ref, kkw_ref,
                                kaw_ref, rk_ref, ones_ref, L, nb)
    bonus_ref[...] = bonus.reshape(nb, L, W)
    g_ref[...] = g.reshape(nb, L, W)
    chunk = lambda name, ib: vals[name][ib * L:(ib + 1) * L, :]
    row = lax.broadcasted_iota(jnp.int32, (L, L), 0)
    col = lax.broadcasted_iota(jnp.int32, (L, L), 1)
    eye = jnp.where(row == col, 1.0, 0.0)
    tri = jnp.where(row >= col, 1.0, 0.0).astype(BF)
    row2 = lax.broadcasted_iota(jnp.int32, (2 * L, 2 * L), 0)
    col2 = lax.broadcasted_iota(jnp.int32, (2 * L, 2 * L), 1)
    cc = jnp.where(col2 >= L, col2 - L, col2)
    keep = jnp.where(row2 < L, row2 - 1, row2 - L) >= cc
    sls = [slice(h * DH, (h + 1) * DH) for h in range(H)]
    chains = [(ib, h) for ib in range(nb) for h in range(H)]
    pre = {}
    for ib in range(nb):
        lw = chunk("lw", ib)
        cum = _mm_mask_l(tri, lw)
        p_in = jnp.exp(cum)
        p_inv = jnp.exp(-cum)
        p_last = p_in[L - 1:L, :]
        kh = chunk("k", ib) * p_inv
        bh = chunk("b", ib) * p_inv
        pre[ib] = dict(
            kr=jnp.concatenate([(chunk("kk", ib) * jnp.exp(cum - lw)).astype(BF),
                                (chunk("r", ib) * p_in).astype(BF)], axis=0),
            kb=jnp.concatenate([kh.astype(BF), bh.astype(BF)], axis=0),
            kbl=jnp.concatenate([(kh * p_last).astype(BF), (bh * p_last).astype(BF)], axis=0),
            v=chunk("v", ib).astype(BF), p_last=p_last)
    op = lambda u, name: pre[u[0]][name][:, sls[u[1]]]
    gm = {u: jnp.where(keep, _mm_nt(op(u, "kr"), op(u, "kb")), 0.0) for u in chains}
    akr = {u: gm[u][:, 0:L].astype(BF) for u in chains}
    r_b = {u: gm[u][L:2 * L, L:2 * L].astype(BF) for u in chains}
    xs = {u: -gm[u][0:L, L:2 * L] for u in chains}
    invs = {u: eye + xs[u] for u in chains}
    if L > 2:
        xs = {u: _mm(xs[u], xs[u]) for u in chains}
    n = 4
    while n < L:
        st = {u: _mm(jnp.concatenate([invs[u], xs[u]], axis=0), xs[u]) for u in chains}
        invs = {u: invs[u] + st[u][0:L] for u in chains}
        xs = {u: st[u][L:2 * L] for u in chains}
        n *= 2
    if L > 2:
        invs = {u: invs[u] + _mm(invs[u], xs[u]) for u in chains}
    akrv = {u: _mm(akr[u], op(u, "v")) for u in chains}
    ss = {u: s_scr[u] for u in chains}
    krs = {u: _mm_nt(op(u, "kr"), ss[u]) for u in chains}
    us = {u: _mm(invs[u], krs[u][0:L] + akrv[u][0:L]) for u in chains}
    ys = {u: krs[u][L:2 * L] + akrv[u][L:2 * L] - _mm(r_b[u], us[u]) for u in chains}
    for u in chains:
        vu = jnp.concatenate([op(u, "v"), (-us[u]).astype(BF)], axis=0)
        s_scr[u] = ss[u] * pre[u[0]]["p_last"][:, sls[u[1]]] + _mm_tn(vu, op(u, "kbl"))
    for ib in range(nb):
        y_ref[ib] = jnp.concatenate([ys[ib, h] for h in range(H)], axis=1)

    @pl.when(c == nc - 1)
    def _():
        s1_ref[...] = s_scr[...]


def _rwkv(p3, shift, mu, w0, a0, wba_bf, gb_bf, kkw, kaw, rk, ones_bd, s0, L, nb):
    b, t, _ = p3.shape
    nc = t // L
    vec = lambda: pl.BlockSpec((1, W), lambda i, c: (0, 0))
    carry = lambda width, j: pl.BlockSpec((nb, 1, width), lambda i, c: (i, 0, j))
    mix = lambda width, j: pl.BlockSpec((1, width), lambda i, c: (0, j))
    blk = lambda: pl.BlockSpec((nb, L, W), lambda i, c: (i, c, 0))
    st = pl.BlockSpec((nb, H, DH, DH), lambda i, c: (i, 0, 0, 0))
    sh3 = shift.reshape(b, 1, -1)
    mu2 = mu.reshape(1, -1)
    return pl.pallas_call(
        functools.partial(_rwkv_kernel, L=L, nb=nb, nc=nc),
        grid=(b // nb, nc),
        in_specs=[pl.BlockSpec((nb, L, W), lambda i, c: (i, c, 2)),
                  pl.BlockSpec((nb, L, W), lambda i, c: (i, c, 3)),
                  pl.BlockSpec((nb, L, W), lambda i, c: (i, c, 4)),
                  pl.BlockSpec((nb, L, 256), lambda i, c: (i, c, 10)),
                  carry(W, 0), carry(W, 1), carry(W, 2), carry(256, 3 * W // 256),
                  mix(W, 0), mix(W, 1), mix(W, 2), mix(256, 3 * W // 256),
                  vec(), vec(),
                  pl.BlockSpec((128, 2 * W), lambda i, c: (0, 0)),
                  pl.BlockSpec((LORA_G, W), lambda i, c: (0, 0)),
                  vec(), vec(), vec(),
                  pl.BlockSpec((W, W), lambda i, c: (0, 0)),
                  st],
        out_specs=[blk(), blk(), blk(), st],
        out_shape=[jax.ShapeDtypeStruct((b, t, W), F32)] * 3 + [jax.ShapeDtypeStruct((b, H, DH, DH), F32)],
        scratch_shapes=[pltpu.VMEM((nb, H, DH, DH), F32), pltpu.VMEM((nb, 1, W), F32), pltpu.VMEM((nb, 1, W), F32),
                        pltpu.VMEM((nb, 1, W), F32), pltpu.VMEM((nb, 1, 256), F32)],
        compiler_params=_params(("parallel", "arbitrary")),
        name="rwkv",
    )(p3, p3, p3, p3,
      sh3, sh3, sh3, sh3, mu2, mu2, mu2, mu2,
      w0.reshape(1, W), a0.reshape(1, W), wba_bf, gb_bf, kkw.reshape(1, W), kaw.reshape(1, W),
      rk.reshape(1, W), ones_bd, s0)


def _block_diag(w):
    nb, bw, _ = w.shape
    return (jnp.eye(nb, dtype=w.dtype)[:, None, :, None] * w[:, :, None, :]).reshape(nb * bw, nb * bw)


def _chunk(t, target):
    return target if t % target == 0 else t


def _scan_blocking(b, t, target):
    L = _chunk(t, target)
    return L, min(b, 8)


def _run_group(x, mods, st, wts):
    b, t, d = x.shape
    x2 = x.reshape(b * t, d)
    (mc, mn, mm, lh, lconv, ccb, rs, rsh) = st

    sh1, sc1, gt1, sh2, sc2, gt2 = mods[0]
    e = wts["even"]
    p = _in_proj(x2, wts["g_mix"][0], sh1, sc1, e["w_in"], t)
    p3 = p.reshape(b, t, IN_EVEN_PAD)
    gif_t = jnp.transpose(p3[:, :, 6 * W:6 * W + 2 * H], (0, 2, 1))
    hm, c1, n1, m1 = _mlstm(p3, gif_t, e["b_if"], mc[0], mn[0], mm[0], *_scan_blocking(b, t, 128))
    hl, lh1 = _lru(p3, lconv[0], lh[0], e["w_conv"], e["b_conv"], e["w_gate"], e["b_gate"], e["lam"],
                   _chunk(t, 256), min(b, 4 if t >= 256 else 8))
    assert t >= LRU_CONV - 1
    conv1 = p3[:, t - (LRU_CONV - 1):, 4 * W:5 * W]
    x2 = _out_proj(x2, hl.reshape(b * t, W), hm.reshape(b * t, W), p, 3, None, e["g_head"], jnp.zeros((W,), F32),
                   wts["ones_bd"], e["w_out"], gt1, t, True, RMS_EPS, True)
    m = wts["moe"][0]
    x2 = _moe(x2, wts["g_ffn"][0], sh2, sc2, gt2, m["wr_hi"], m["wr_lo"], m["br"], wts["w1"], wts["w3"], wts["w2"],
              0, wts["g_final"], t, False)

    sh1, sc1, gt1, sh2, sc2, gt2 = mods[1]
    o = wts["odd"]
    p = _in_proj(x2, wts["g_mix"][1], sh1, sc1, o["w_in"], t)
    p3 = p.reshape(b, t, IN_ODD)
    cc, cc1 = _conf(p3, ccb[0], o["b_glu"], o["w_dw"], o["b_dw"], o["g_ln"], o["b_ln"], _chunk(t, 512),
                    min(b, 1 if t >= 256 else 8))
    y, bonus, g, s1 = _rwkv(p3, rsh[0], o["mu"], o["w0"], o["a0"], o["wba"], o["gb"], o["kkw"], o["kaw"], o["rk"],
                            wts["ones_bd"], rs[0], *_scan_blocking(b, t, 64))
    sh_out = p3[:, t - 1, 2 * W:]
    x2 = _out_proj(x2, cc.reshape(b * t, W), y.reshape(b * t, W), g.reshape(b * t, W), 0, bonus.reshape(b * t, W),
                   o["g_gn"], o["b_gn"], wts["ones_bd"], o["w_out"], gt1, t, False, RWKV_GN_EPS, False)
    m = wts["moe"][1]
    y2 = _moe(x2, wts["g_ffn"][1], sh2, sc2, gt2, m["wr_hi"], m["wr_lo"], m["br"], wts["w1"], wts["w3"], wts["w2"],
              1, wts["g_final"], t, True)
    states = (c1[None], n1[None], m1.reshape(1, b, H), lh1.reshape(1, b, W), conv1[None], cc1[None], s1[None],
              sh_out[None])
    return y2.reshape(b, t, d), states


def kernel(x_prompt, x_sample, c_prompt, c_sample, state_mlstm_C, state_mlstm_n, state_mlstm_m, state_lru_h,
           cache_lru_conv, cache_conformer_conv, state_rwkv_S, cache_rwkv_shift, w_ada, b_ada, g_norm_mix,
           g_norm_ffn, w_in_even, b_mlstm_if, g_mlstm_head, w_lru_conv, b_lru_conv, w_lru_r, b_lru_r, w_lru_i,
           b_lru_i, lru_lambda, w_out_even, w_in_odd, b_glu, w_cc_dw, b_cc_dw, g_cc_ln, b_cc_ln, rwkv_mu,
           rwkv_w0, rwkv_wB, rwkv_a0, rwkv_aB, rwkv_gB, rwkv_kk, rwkv_ka, rwkv_rk, g_rwkv_gn, b_rwkv_gn,
           w_out_odd, w_router_g, b_router_g, w_router_e, b_router_e, w_exp_gate, w_exp_up, w_exp_down, g_final):
    bp, bs = x_prompt.shape[0], x_sample.shape[0]

    wi = w_in_even[0]
    gcol = 4 * W
    w_in_e = jnp.concatenate([wi[:, :gcol], wi[:, gcol + 2 * H:], wi[:, gcol:gcol + 2 * H],
                              jnp.zeros((D_MODEL, LANES - 2 * H), F32)], axis=1).astype(BF)
    even = dict(
        w_in=w_in_e, b_if=b_mlstm_if[0], g_head=g_mlstm_head[0], w_conv=w_lru_conv[0], b_conv=b_lru_conv[0],
        w_gate=jnp.concatenate([_block_diag(w_lru_r[0]), _block_diag(w_lru_i[0])], axis=1).astype(BF),
        b_gate=jnp.concatenate([b_lru_r[0], b_lru_i[0]]), lam=lru_lambda[0], w_out=w_out_even[0].astype(BF))
    zl = jnp.zeros((LORA_W, W), F32)
    odd = dict(
        w_in=w_in_odd[0].astype(BF), b_glu=b_glu[0], w_dw=w_cc_dw[0], b_dw=b_cc_dw[0], g_ln=g_cc_ln[0],
        b_ln=b_cc_ln[0], mu=rwkv_mu[0], w0=rwkv_w0[0], a0=rwkv_a0[0],
        wba=jnp.concatenate([jnp.concatenate([rwkv_wB[0], zl], axis=1),
                             jnp.concatenate([zl, rwkv_aB[0]], axis=1)], axis=0).astype(BF),
        gb=rwkv_gB[0].astype(BF), kkw=rwkv_kk[0], kaw=rwkv_ka[0], rk=rwkv_rk[0], g_gn=g_rwkv_gn[0],
        b_gn=b_rwkv_gn[0], w_out=w_out_odd[0].astype(BF))
    moe = []
    for l in range(DEPTH):
        wr = jnp.concatenate([w_router_g[l], w_router_e[l],
                              jnp.zeros((D_MODEL, LANES - N_GROUPS - N_EXPERTS), F32)], axis=1)
        wr_hi = wr.astype(BF)
        wr_lo = (wr - wr_hi.astype(F32)).astype(BF)
        br = jnp.concatenate([b_router_g[l], b_router_e[l],
                              jnp.zeros((LANES - N_GROUPS - N_EXPERTS,), F32)]).reshape(1, LANES)
        moe.append(dict(wr_hi=wr_hi, wr_lo=wr_lo, br=br))
    ones_bd = _block_diag(jnp.ones((H, DH, DH), F32)).astype(BF)
    wts = dict(even=even, odd=odd, moe=moe, g_mix=g_norm_mix, g_ffn=g_norm_ffn, g_final=g_final, ones_bd=ones_bd,
               w1=w_exp_gate.astype(BF).reshape(DEPTH * N_EXPERTS, D_MODEL, D_EXPERT),
               w3=w_exp_up.astype(BF).reshape(DEPTH * N_EXPERTS, D_MODEL, D_EXPERT),
               w2=w_exp_down.astype(BF).reshape(DEPTH * N_EXPERTS, D_EXPERT, D_MODEL))

    rows = 2 * bs
    assert bp <= bs
    c_all = jnp.concatenate([c_sample, c_prompt, jnp.zeros((rows - bs - bp, D_MODEL), F32)], axis=0)
    mod = _ada(c_all, w_ada, b_ada)

    def mods_of(lo):
        return [tuple(_Mod(mod, l * rows + lo, j) for j in range(6)) for l in range(DEPTH)]

    z = lambda *s: jnp.zeros(s, F32)
    st_p = (z(1, bp, H, DH, DH), z(1, bp, H, DH), z(1, bp, H), z(1, bp, W), z(1, bp, LRU_CONV - 1, W),
            z(1, bp, CONV_C - 1, W), z(1, bp, H, DH, DH), z(1, bp, 3 * W + LORA_W + LORA_A + LORA_G))
    st_s = (state_mlstm_C, state_mlstm_n, state_mlstm_m, state_lru_h, cache_lru_conv, cache_conformer_conv,
            state_rwkv_S, cache_rwkv_shift)
    y_p, out_p = _run_group(x_prompt, mods_of(bs), st_p, wts)
    y_s, out_s = _run_group(x_sample, mods_of(0), st_s, wts)
    return (y_p, y_s) + tuple(out_p) + tuple(out_s)
```

```python
import functools
from typing import NamedTuple

import jax
import jax.numpy as jnp
from jax import lax
from jax.experimental import pallas as pl
from jax.experimental.pallas import tpu as pltpu

F32 = jnp.float32
BF = jnp.bfloat16

D_MODEL = 1024
DEPTH = 2
H = 8
DH = 64
W = 512
LRU_CONV = 4
LRU_C = 8.0
CONV_C = 31
LORA_W = 64
LORA_A = 64
LORA_G = 128
RWKV_DECAY = 0.606531
RWKV_GN_EPS = 64e-5
N_GROUPS = 4
E_PER_GROUP = 4
N_EXPERTS = 16
D_EXPERT = 256
RMS_EPS = 1e-6
LN_EPS = 1e-5
LANES = 128
IN_EVEN_PAD = 6 * W + LANES
IN_ODD = 2 * W + 3 * W + LORA_W + LORA_A + LORA_G

ROW_TILE = 1024
MOE_ROW_TILE = 512
VMEM_LIMIT = 48 * 1024 * 1024


def _mm(a, b):
    return jnp.dot(a.astype(BF), b.astype(BF), preferred_element_type=F32)


def _mm_nt(a, b):
    return lax.dot_general(a.astype(BF), b.astype(BF), (((1,), (1,)), ((), ())),
                           preferred_element_type=F32)


def _mm_tn(a, b):
    return lax.dot_general(a.astype(BF), b.astype(BF), (((0,), (0,)), ((), ())),
                           preferred_element_type=F32)


def _split3(x):
    hi = x.astype(BF)
    r = x - hi.astype(F32)
    mid = r.astype(BF)
    lo = (r - mid.astype(F32)).astype(BF)
    return hi, mid, lo


def _mm_mask_l(mask, x):
    return sum(jnp.dot(mask, p, preferred_element_type=F32) for p in _split3(x))


def _mm_mask_r(x, mask):
    return sum(jnp.dot(p, mask, preferred_element_type=F32) for p in _split3(x))


def _sigmoid(x):
    return 1.0 / (1.0 + jnp.exp(-x))


def _rows(v, tm):
    nb, _, c = v.shape
    if nb == 1:
        return v[0]
    return jnp.broadcast_to(v, (nb, tm // nb, c)).reshape(tm, c)


class _Mod(NamedTuple):
    arr: jax.Array
    row0: int
    col: int


def _mod_spec(m, T, tm):
    if tm <= T:
        per = T // tm
        return pl.BlockSpec((1, 1, D_MODEL), lambda i, *_: (m.row0 + i // per, 0, m.col))
    nbk = tm // T
    assert m.row0 % nbk == 0
    return pl.BlockSpec((nbk, 1, D_MODEL), lambda i, *_: (m.row0 // nbk + i, 0, m.col))


def _norm_mod(x, g, sh, sc):
    y = x * lax.rsqrt(jnp.mean(x * x, axis=-1, keepdims=True) + RMS_EPS) * g
    return y * (1.0 + sc) + sh


def _params(sem):
    return pltpu.CompilerParams(dimension_semantics=sem, vmem_limit_bytes=VMEM_LIMIT)


def _ada_kernel(c_ref, w_ref, b_ref, o_ref):
    y = _mm(c_ref[...], w_ref[0]) + b_ref[0]
    o_ref[...] = y.reshape(o_ref.shape)


def _ada(c_all, w, b):
    nb = c_all.shape[0]
    tn = 1536
    return pl.pallas_call(
        _ada_kernel,
        grid=(DEPTH, 6 * D_MODEL // tn),
        in_specs=[pl.BlockSpec((nb, D_MODEL), lambda l, j: (0, 0)),
                  pl.BlockSpec((1, D_MODEL, tn), lambda l, j: (l, 0, j)),
                  pl.BlockSpec((1, 1, tn), lambda l, j: (l, 0, j))],
        out_specs=pl.BlockSpec((nb, 1, tn), lambda l, j: (l, 0, j)),
        out_shape=jax.ShapeDtypeStruct((DEPTH * nb, 1, 6 * D_MODEL), F32),
        compiler_params=_params(("parallel", "parallel")),
        name="ada",
    )(c_all, w, b.reshape(DEPTH, 1, 6 * D_MODEL))


def _inproj_kernel(x_ref, g_ref, sh_ref, sc_ref, w_ref, o_ref, h_scr):
    tm = x_ref.shape[0]

    @pl.when(pl.program_id(1) == 0)
    def _():
        h = _norm_mod(x_ref[...], g_ref[...], _rows(sh_ref[...], tm), _rows(sc_ref[...], tm))
        h_scr[...] = h.astype(BF)

    o_ref[...] = jnp.dot(h_scr[...], w_ref[...], preferred_element_type=F32)


INPROJ_ROW_TILE = 512


def _in_proj(x2, g, sh, sc, w_bf, T):
    n, d = x2.shape
    cols = w_bf.shape[1]
    tm = INPROJ_ROW_TILE
    tn = cols
    return pl.pallas_call(
        _inproj_kernel,
        grid=(n // tm, cols // tn),
        in_specs=[pl.BlockSpec((tm, d), lambda i, j: (i, 0)),
                  pl.BlockSpec((1, d), lambda i, j: (0, 0)),
                  _mod_spec(sh, T, tm), _mod_spec(sc, T, tm),
                  pl.BlockSpec((d, tn), lambda i, j: (0, j))],
        out_specs=pl.BlockSpec((tm, tn), lambda i, j: (i, j)),
        out_shape=jax.ShapeDtypeStruct((n, cols), F32),
        scratch_shapes=[pltpu.VMEM((tm, d), BF)],
        compiler_params=_params(("parallel", "arbitrary")),
        name="in_proj",
    )(x2, g.reshape(1, d), sh.arr, sc.arr, w_bf)


def _seg_sum(x, ones_bd):
    hi = x.astype(BF)
    lo = (x - hi.astype(F32)).astype(BF)
    return (jnp.dot(hi, ones_bd, preferred_element_type=F32)
            + jnp.dot(lo, ones_bd, preferred_element_type=F32))


def _outproj_kernel(*refs, pre_first, eps, has_add, sigmoid_mul):
    if has_add:
        x_ref, a_ref, pre_ref, mul_ref, add_ref, gain_ref, bias_ref, ones_ref, w_ref, gt_ref, o_ref = refs
    else:
        x_ref, a_ref, pre_ref, mul_ref, gain_ref, bias_ref, ones_ref, w_ref, gt_ref, o_ref = refs
    tm = x_ref.shape[0]
    even = lax.broadcasted_iota(jnp.int32, (tm, 2 * DH), 1) < DH

    def head_mean(x):
        se = jnp.sum(jnp.where(even, x, 0.0), axis=1, keepdims=True)
        so = jnp.sum(jnp.where(even, 0.0, x), axis=1, keepdims=True)
        return jnp.where(even, se, so) * (1.0 / DH)

    ys = []
    for p in range(H // 2):
        sl = slice(p * 2 * DH, (p + 1) * 2 * DH)
        pre = pre_ref[:, sl]
        dev = pre - head_mean(pre)
        ys.append(dev * lax.rsqrt(head_mean(dev * dev) + eps))
    y = jnp.concatenate(ys, axis=1) * gain_ref[...] + bias_ref[...]
    if has_add:
        y = y + add_ref[...]
    m = mul_ref[...]
    y = (y * (_sigmoid(m) if sigmoid_mul else m)).astype(BF)
    first, second = (y, a_ref[...]) if pre_first else (a_ref[...], y)
    mix = (jnp.dot(first, w_ref[0:W, :], preferred_element_type=F32)
           + jnp.dot(second, w_ref[W:2 * W, :], preferred_element_type=F32))
    o_ref[...] = x_ref[...] + _rows(gt_ref[...], tm) * mix


def _out_proj(x2, a2, pre2, mul2, mul_col, add2, gain, bias, ones_bd, w_bf, gt, T, pre_first, eps, sigmoid_mul):
    n, d = x2.shape
    tm = ROW_TILE
    has_add = add2 is not None
    row = lambda c=0: pl.BlockSpec((tm, W), lambda i, c=c: (i, c))
    vec = lambda: pl.BlockSpec((1, W), lambda i: (0, 0))
    in_specs = [pl.BlockSpec((tm, d), lambda i: (i, 0)), row(), row(), row(mul_col)]
    args = [x2, a2, pre2, mul2]
    if has_add:
        in_specs.append(row())
        args.append(add2)
    in_specs += [vec(), vec(), pl.BlockSpec((W, W), lambda i: (0, 0)), pl.BlockSpec((2 * W, d), lambda i: (0, 0)),
                 _mod_spec(gt, T, tm)]
    args += [gain.reshape(1, W), bias.reshape(1, W), ones_bd, w_bf, gt.arr]
    return pl.pallas_call(
        functools.partial(_outproj_kernel, pre_first=pre_first, eps=eps, has_add=has_add, sigmoid_mul=sigmoid_mul),
        grid=(n // tm,),
        in_specs=in_specs,
        out_specs=pl.BlockSpec((tm, d), lambda i: (i, 0)),
        out_shape=jax.ShapeDtypeStruct((n, d), F32),
        compiler_params=_params(("parallel",)),
        name="out_proj",
    )(*args)


def _route(logits):
    lane = lax.broadcasted_iota(jnp.int32, logits.shape, 1).astype(F32)
    neg = -jnp.inf
    is_g = lane < N_GROUPS
    lg = jnp.where(is_g, logits, neg)
    mg = jnp.max(lg, axis=1, keepdims=True)
    gsel = jnp.min(jnp.where(lg == mg, lane, float(LANES)), axis=1, keepdims=True)
    psum = jnp.sum(jnp.where(is_g, jnp.exp(lg - mg), 0.0), axis=1, keepdims=True)
    pg_sel = 1.0 / psum
    lo = N_GROUPS + E_PER_GROUP * gsel
    le = jnp.where((lane >= lo) & (lane < lo + E_PER_GROUP), logits, neg)
    v1 = jnp.max(le, axis=1, keepdims=True)
    i1 = jnp.min(jnp.where(le == v1, lane, float(LANES)), axis=1, keepdims=True)
    le2 = jnp.where(lane == i1, neg, le)
    v2 = jnp.max(le2, axis=1, keepdims=True)
    i2 = jnp.min(jnp.where(le2 == v2, lane, float(LANES)), axis=1, keepdims=True)
    e2 = jnp.exp(v2 - v1)
    p1 = 1.0 / (1.0 + e2)
    p2 = e2 / (1.0 + e2)
    return pg_sel * jnp.where(lane == i1, p1, jnp.where(lane == i2, p2, 0.0))


def _moe_kernel(x_ref, g_ref, sh_ref, sc_ref, gt_ref, wrh_ref, wrl_ref, br_ref, w1_ref, w3_ref, w2_ref,
                gf_ref, o_ref, *, final_norm):
    tm = x_ref.shape[0]
    h = _norm_mod(x_ref[...], g_ref[...], _rows(sh_ref[...], tm), _rows(sc_ref[...], tm))
    hb = h.astype(BF)
    hl = (h - hb.astype(F32)).astype(BF)
    logits = (jnp.dot(hb, wrh_ref[...], preferred_element_type=F32)
              + jnp.dot(hl, wrh_ref[...], preferred_element_type=F32)
              + jnp.dot(hb, wrl_ref[...], preferred_element_type=F32)) + br_ref[...]
    gate = _route(logits)
    lane = lax.broadcasted_iota(jnp.int32, gate.shape, 1)
    acc = None
    for e in range(N_EXPERTS):
        hg = jnp.dot(hb, w1_ref[e], preferred_element_type=F32)
        hu = jnp.dot(hb, w3_ref[e], preferred_element_type=F32)
        ge = jnp.sum(jnp.where(lane == e + N_GROUPS, gate, 0.0), axis=1, keepdims=True)
        hh = hg * _sigmoid(hg) * hu * ge
        part = jnp.dot(hh.astype(BF), w2_ref[e], preferred_element_type=F32)
        acc = part if acc is None else acc + part
    y = x_ref[...] + _rows(gt_ref[...], tm) * acc
    if final_norm:
        y = y * lax.rsqrt(jnp.mean(y * y, axis=-1, keepdims=True) + RMS_EPS) * gf_ref[...]
    o_ref[...] = y


def _moe(x2, g, sh, sc, gt, wr_hi, wr_lo, br, w1, w3, w2, layer, g_final, T, final_norm):
    n, d = x2.shape
    tm = MOE_ROW_TILE
    resident = lambda shape: pl.BlockSpec(shape, lambda i: (layer, 0, 0), pipeline_mode=pl.Buffered(1))
    return pl.pallas_call(
        functools.partial(_moe_kernel, final_norm=final_norm),
        grid=(n // tm,),
        in_specs=[pl.BlockSpec((tm, d), lambda i: (i, 0)),
                  pl.BlockSpec((1, d), lambda i: (0, 0)),
                  _mod_spec(sh, T, tm), _mod_spec(sc, T, tm), _mod_spec(gt, T, tm),
                  pl.BlockSpec((d, LANES), lambda i: (0, 0)),
                  pl.BlockSpec((d, LANES), lambda i: (0, 0)),
                  pl.BlockSpec((1, LANES), lambda i: (0, 0)),
                  resident((N_EXPERTS, d, D_EXPERT)), resident((N_EXPERTS, d, D_EXPERT)),
                  resident((N_EXPERTS, D_EXPERT, d)),
                  pl.BlockSpec((1, d), lambda i: (0, 0))],
        out_specs=pl.BlockSpec((tm, d), lambda i: (i, 0)),
        out_shape=jax.ShapeDtypeStruct((n, d), F32),
        compiler_params=_params(("parallel",)),
        name="moe",
    )(x2, g.reshape(1, d), sh.arr, sc.arr, gt.arr, wr_hi, wr_lo, br, w1, w3, w2, g_final.reshape(1, d))


def _log_sigmoid(x):
    return jnp.minimum(x, 0.0) - jnp.log(1.0 + jnp.exp(-jnp.abs(x)))


def _mlstm_kernel(q_ref, k_ref, v_ref, g_ref, gt_ref, bif_ref, bift_ref, c0_ref, n0_ref, m0_ref,
                  h_ref, c1_ref, n1_ref, m1_ref, st_scr, m_scr, *, L, nb, nc):
    c = pl.program_id(1)
    NP = H // 2
    r64 = lax.broadcasted_iota(jnp.int32, (DH, DH), 0)
    c64 = lax.broadcasted_iota(jnp.int32, (DH, DH), 1)
    eye64 = r64 == c64

    @pl.when(c == 0)
    def _():
        st_scr[...] = jnp.zeros_like(st_scr)
        for ib in range(nb):
            for h in range(H):
                p, o = divmod(h, 2)
                rs = slice(o * DH, (o + 1) * DH)
                st_scr[ib, p, rs, o * DH:(o + 1) * DH] = c0_ref[ib, h]
                n_col = jnp.sum(jnp.where(eye64, n0_ref[ib, h:h + 1, :], 0.0), axis=1, keepdims=True)
                st_scr[ib, p, rs, 2 * DH + o * DH:2 * DH + (o + 1) * DH] = jnp.broadcast_to(n_col, (DH, DH))
        m_scr[...] = m0_ref[...]

    row = lax.broadcasted_iota(jnp.int32, (L, L), 0)
    col = lax.broadcasted_iota(jnp.int32, (L, L), 1)
    causal = row >= col
    tri = jnp.where(causal, 1.0, 0.0).astype(BF)
    tri_u = jnp.where(row <= col, 1.0, 0.0).astype(BF)
    even = lax.broadcasted_iota(jnp.int32, (L, 2 * DH), 1) < DH
    row_s = lax.broadcasted_iota(jnp.int32, (2 * DH, 4 * DH), 0)
    lane_s = lax.broadcasted_iota(jnp.int32, (2 * DH, 4 * DH), 1)
    top = row_s < DH
    same_head = jnp.where(top, 0, DH) == jnp.bitwise_and(lane_s, DH)
    ones_l = jnp.ones((L, 2 * DH), BF)
    ibs = range(nb)
    g = [g_ref[ib] + bif_ref[...] for ib in ibs]
    gt = [gt_ref[ib] + bift_ref[...] for ib in ibs]
    bcum = [_mm_mask_l(tri, _log_sigmoid(g[ib])) for ib in ibs]
    bcum_t = [_mm_mask_r(_log_sigmoid(gt[ib]), tri_u) for ib in ibs]
    m_prev = [m_scr[ib] for ib in ibs]
    ch = [(ib, h) for ib in ibs for h in range(H)]
    prs = [(ib, p) for ib in ibs for p in range(NP)]
    pair_of = lambda u: (u[0], u[1] // 2)
    wide = lambda x: jnp.broadcast_to(x, (L, 2 * DH))
    lanes = {u: slice(u[1] * 2 * DH, (u[1] + 1) * 2 * DH) for u in prs}
    q_f = {u: q_ref[u[0], :, lanes[u]] for u in prs}
    q_b = {u: q_f[u].astype(BF) for u in prs}
    k_f = {u: k_ref[u[0], :, lanes[u]] * (DH ** -0.5) for u in prs}
    k_b = {u: k_f[u].astype(BF) for u in prs}
    rhs = {u: jnp.concatenate([v_ref[u[0], :, lanes[u]].astype(BF), ones_l], axis=1) for u in prs}
    odd = lax.broadcasted_iota(jnp.int32, (L, 2 * DH), 1) >= DH
    qk = {u: _mm_nt(jnp.where(odd if u[1] % 2 else even, q_f[pair_of(u)], 0.0).astype(BF), k_b[pair_of(u)])
          for u in ch}
    st = {u: st_scr[u[0], u[1]] for u in prs}
    qst = {u: _mm(q_b[u], st[u]) for u in prs}
    bc = {u: wide(bcum[u[0]][:, H + u[1]:H + u[1] + 1]) for u in ch}
    ic = {u: wide(g[u[0]][:, u[1]:u[1] + 1]) for u in ch}
    mp = {u: m_prev[u[0]][:, u[1]:u[1] + 1] for u in ch}
    dmat = {u: jnp.where(causal, bc[u][:, 0:L] + (gt[u[0]][u[1]:u[1] + 1, :] - bcum_t[u[0]][H + u[1]:H + u[1] + 1, :]),
                         -jnp.inf) for u in ch}
    g_inter = {u: bc[u] + mp[u] for u in ch}
    m_t = {u: jnp.maximum(g_inter[u], jnp.max(dmat[u], axis=1, keepdims=True)) for u in ch}
    s = {u: (qk[u] * jnp.exp(dmat[u] - m_t[u][:, 0:L])).astype(BF) for u in ch}
    w_inter = {u: jnp.exp(g_inter[u] - m_t[u]) for u in ch}
    sv = {u: _mm(s[u], rhs[pair_of(u)]) for u in ch}
    m_new = {u: m_t[u][L - 1:L, 0:1] for u in ch}
    b_last = {u: bcum[u[0]][L - 1:L, H + u[1]:H + u[1] + 1] for u in ch}
    w_s = {u: jnp.exp(b_last[u] - bc[u] + ic[u] - m_new[u]) for u in ch}
    decay = {u: jnp.exp(b_last[u] + mp[u] - m_new[u]) for u in ch}
    pick = lambda d, u: jnp.where(even, d[u[0], 2 * u[1]], d[u[0], 2 * u[1] + 1])
    hh = {}
    for u in prs:
        e, o = (u[0], 2 * u[1]), (u[0], 2 * u[1] + 1)
        w_pair = pick(w_inter, u)
        num = jnp.where(even, sv[e][:, 0:2 * DH], sv[o][:, 0:2 * DH]) + w_pair * qst[u][:, 0:2 * DH]
        den = jnp.where(even, sv[e][:, 2 * DH:], sv[o][:, 2 * DH:]) + w_pair * qst[u][:, 2 * DH:]
        hh[u] = num / jnp.maximum(jnp.abs(den), jnp.exp(-pick(m_t, u)))
        inc = _mm_tn((k_f[u] * pick(w_s, u)).astype(BF), rhs[u])
        st_scr[u[0], u[1]] = jnp.where(top, decay[e], decay[o]) * st[u] + jnp.where(same_head, inc, 0.0)
    for ib in ibs:
        h_ref[ib] = jnp.concatenate([hh[ib, p] for p in range(NP)], axis=1)
        m_scr[ib] = jnp.concatenate([m_new[ib, h] for h in range(H)], axis=1)

    @pl.when(c == nc - 1)
    def _():
        for ib in range(nb):
            for h in range(H):
                p, o = divmod(h, 2)
                rs = slice(o * DH, (o + 1) * DH)
                c1_ref[ib, h] = st_scr[ib, p, rs, o * DH:(o + 1) * DH]
                n_rep = st_scr[ib, p, rs, 2 * DH + o * DH:2 * DH + (o + 1) * DH]
                n1_ref[ib, h:h + 1, :] = jnp.sum(jnp.where(eye64, n_rep, 0.0), axis=0, keepdims=True)
        m1_ref[...] = m_scr[...]


def _mlstm(p3, gif_t, bif, c0, n0, m0, L, nb):
    b, t, _ = p3.shape
    nc = t // L
    col = lambda j: pl.BlockSpec((nb, L, W), lambda i, c, j=j: (i, c, j))
    bif_pad = jnp.zeros((1, LANES), F32).at[0, :2 * H].set(bif)
    return pl.pallas_call(
        functools.partial(_mlstm_kernel, L=L, nb=nb, nc=nc),
        grid=(b // nb, nc),
        in_specs=[col(0), col(1), col(2),
                  pl.BlockSpec((nb, L, LANES), lambda i, c: (i, c, 6 * W // LANES)),
                  pl.BlockSpec((nb, 2 * H, L), lambda i, c: (i, 0, c)),
                  pl.BlockSpec((1, LANES), lambda i, c: (0, 0)),
                  pl.BlockSpec((2 * H, 1), lambda i, c: (0, 0)),
                  pl.BlockSpec((nb, H, DH, DH), lambda i, c: (i, 0, 0, 0)),
                  pl.BlockSpec((nb, H, DH), lambda i, c: (i, 0, 0)),
                  pl.BlockSpec((nb, 1, H), lambda i, c: (i, 0, 0))],
        out_specs=[pl.BlockSpec((nb, L, W), lambda i, c: (i, c, 0)),
                   pl.BlockSpec((nb, H, DH, DH), lambda i, c: (i, 0, 0, 0)),
                   pl.BlockSpec((nb, H, DH), lambda i, c: (i, 0, 0)),
                   pl.BlockSpec((nb, 1, H), lambda i, c: (i, 0, 0))],
        out_shape=[jax.ShapeDtypeStruct((b, t, W), F32),
                   jax.ShapeDtypeStruct((b, H, DH, DH), F32),
                   jax.ShapeDtypeStruct((b, H, DH), F32),
                   jax.ShapeDtypeStruct((b, 1, H), F32)],
        scratch_shapes=[pltpu.VMEM((nb, H // 2, 2 * DH, 4 * DH), F32), pltpu.VMEM((nb, 1, H), F32)],
        compiler_params=_params(("parallel", "arbitrary")),
        name="mlstm",
    )(p3, p3, p3, p3, gif_t, bif_pad, bif.reshape(2 * H, 1), c0, n0, m0.reshape(b, 1, H))


def _gelu_tanh(x):
    return 0.5 * x * (1.0 + jnp.tanh(0.7978845608028654 * (x + 0.044715 * x * x * x)))


def _lru_kernel(xr_ref, xg_ref, cache_ref, h0_ref, wc_ref, bc_ref, wg_ref, bg_ref, lam_ref,
                o_ref, h1_ref, ext_scr, a_scr, u_scr, hs_scr, hc_scr, *, Tc, nb, nc):
    c = pl.program_id(1)
    K1 = LRU_CONV - 1

    @pl.when(c == 0)
    def _():
        ext_scr[:, 8 - K1:8, :] = cache_ref[...]
        hc_scr[...] = h0_ref[...]

    xcs = []
    for ib in range(nb):
        x = xr_ref[ib]
        ext_scr[ib, 8:8 + Tc, :] = x
        xc = bc_ref[...] + wc_ref[K1:K1 + 1, :] * x
        for d in range(1, LRU_CONV):
            xc = xc + wc_ref[K1 - d:K1 - d + 1, :] * ext_scr[ib, 8 - d:8 - d + Tc, :]
        ext_scr[ib, 8 - K1:8, :] = ext_scr[ib, 8 + Tc - K1:8 + Tc, :]
        xcs.append(xc)
    xc = jnp.concatenate(xcs, axis=0)
    gates = _mm(xc, wg_ref[...]) + bg_ref[...]
    r = _sigmoid(gates[:, 0:W])
    ig = _sigmoid(gates[:, W:2 * W])
    lam = lam_ref[...]
    softplus_neg = jnp.maximum(-lam, 0.0) + jnp.log(1.0 + jnp.exp(-jnp.abs(lam)))
    log_a = -LRU_C * r * softplus_neg
    a_scr[...] = jnp.exp(log_a).reshape(nb, Tc, W)
    th = jnp.tanh(log_a)
    one_minus_a2 = -2.0 * th / (1.0 - th)
    u_scr[...] = (jnp.sqrt(one_minus_a2) * (ig * xc)).reshape(nb, Tc, W)

    def body(t, hs):
        new = []
        for ib in range(nb):
            h = a_scr[ib, pl.ds(t, 1), :] * hs[ib] + u_scr[ib, pl.ds(t, 1), :]
            hs_scr[ib, pl.ds(t, 1), :] = h
            new.append(h)
        return tuple(new)

    h_fin = lax.fori_loop(0, Tc, body, tuple(hc_scr[ib] for ib in range(nb)), unroll=8)
    for ib in range(nb):
        hc_scr[ib] = h_fin[ib]
        o_ref[ib] = (hs_scr[ib] * _gelu_tanh(xg_ref[ib])).astype(BF)

    @pl.when(c == nc - 1)
    def _():
        h1_ref[...] = hc_scr[...]


def _lru(p3, cache, h0, w_conv, b_conv, wg_bf, bg, lam, Tc, nb):
    b, t, _ = p3.shape
    nc = t // Tc
    return pl.pallas_call(
        functools.partial(_lru_kernel, Tc=Tc, nb=nb, nc=nc),
        grid=(b // nb, nc),
        in_specs=[pl.BlockSpec((nb, Tc, W), lambda i, c: (i, c, 4)),
                  pl.BlockSpec((nb, Tc, W), lambda i, c: (i, c, 5)),
                  pl.BlockSpec((nb, LRU_CONV - 1, W), lambda i, c: (i, 0, 0)),
                  pl.BlockSpec((nb, 1, W), lambda i, c: (i, 0, 0)),
                  pl.BlockSpec((LRU_CONV, W), lambda i, c: (0, 0)),
                  pl.BlockSpec((1, W), lambda i, c: (0, 0)),
                  pl.BlockSpec((W, 2 * W), lambda i, c: (0, 0)),
                  pl.BlockSpec((1, 2 * W), lambda i, c: (0, 0)),
                  pl.BlockSpec((1, W), lambda i, c: (0, 0))],
        out_specs=[pl.BlockSpec((nb, Tc, W), lambda i, c: (i, c, 0)),
                   pl.BlockSpec((nb, 1, W), lambda i, c: (i, 0, 0))],
        out_shape=[jax.ShapeDtypeStruct((b, t, W), BF), jax.ShapeDtypeStruct((b, 1, W), F32)],
        scratch_shapes=[pltpu.VMEM((nb, 8 + Tc, W), F32), pltpu.VMEM((nb, Tc, W), F32), pltpu.VMEM((nb, Tc, W), F32),
                        pltpu.VMEM((nb, Tc, W), F32), pltpu.VMEM((nb, 1, W), F32)],
        compiler_params=_params(("parallel", "arbitrary")),
        name="lru",
    )(p3, p3, cache, h0.reshape(b, 1, W), w_conv, b_conv.reshape(1, W), wg_bf, bg.reshape(1, 2 * W),
      lam.reshape(1, W))


CONF_ROWS = 64


def _conf_kernel(u_ref, gte_ref, cache_ref, bu_ref, bg_ref, wdw_ref, bdw_ref, gln_ref, bln_ref,
                 o_ref, cache1_ref, ext_scr, sh_scr, *, Tc, nb, nc):
    c = pl.program_id(1)
    K1 = CONV_C - 1
    base = 32 - K1

    @pl.when(c == 0)
    def _():
        ext_scr[:, base:32, :] = cache_ref[...]

    for ib in range(nb):
        u = u_ref[ib] + bu_ref[...]
        gte = gte_ref[ib] + bg_ref[...]
        ext_scr[ib, 32:32 + Tc, :] = u * _sigmoid(gte)
        for b in range(1, 8):
            sh_scr[b - 1] = ext_scr[ib, b:b + Tc + 24, :]

        def window(off, r0, rb):
            a, b = divmod(off, 8)
            if b == 0:
                return ext_scr[ib, 8 * a + r0:8 * a + r0 + rb, :]
            return sh_scr[b - 1, 8 * a + r0:8 * a + r0 + rb, :]

        rb = min(CONF_ROWS, Tc)
        for r0 in range(0, Tc, rb):
            acc = bdw_ref[...] + wdw_ref[0:1, :] * window(base, r0, rb)
            for j in range(1, CONV_C):
                acc = acc + wdw_ref[j:j + 1, :] * window(base + j, r0, rb)
            mu = jnp.mean(acc, axis=1, keepdims=True)
            var = jnp.mean(jnp.square(acc - mu), axis=1, keepdims=True)
            y = (acc - mu) * lax.rsqrt(var + LN_EPS) * gln_ref[...] + bln_ref[...]
            o_ref[ib, r0:r0 + rb, :] = (y * _sigmoid(y)).astype(BF)
        ext_scr[ib, base:32, :] = ext_scr[ib, base + Tc:32 + Tc, :]

    @pl.when(c == nc - 1)
    def _():
        cache1_ref[...] = ext_scr[:, base:32, :]


def _conf(p3, cache, b_glu, w_dw, b_dw, g_ln, b_ln, Tc, nb):
    b, t, _ = p3.shape
    nc = t // Tc
    vec = lambda: pl.BlockSpec((1, W), lambda i, c: (0, 0))
    return pl.pallas_call(
        functools.partial(_conf_kernel, Tc=Tc, nb=nb, nc=nc),
        grid=(b // nb, nc),
        in_specs=[pl.BlockSpec((nb, Tc, W), lambda i, c: (i, c, 0)),
                  pl.BlockSpec((nb, Tc, W), lambda i, c: (i, c, 1)),
                  pl.BlockSpec((nb, CONV_C - 1, W), lambda i, c: (i, 0, 0)),
                  vec(), pl.BlockSpec((1, W), lambda i, c: (0, 1)),
                  pl.BlockSpec((CONV_C, W), lambda i, c: (0, 0)),
                  vec(), vec(), vec()],
        out_specs=[pl.BlockSpec((nb, Tc, W), lambda i, c: (i, c, 0)),
                   pl.BlockSpec((nb, CONV_C - 1, W), lambda i, c: (i, 0, 0))],
        out_shape=[jax.ShapeDtypeStruct((b, t, W), BF), jax.ShapeDtypeStruct((b, CONV_C - 1, W), F32)],
        scratch_shapes=[pltpu.VMEM((nb, 32 + Tc, W), F32), pltpu.VMEM((7, Tc + 24, W), F32)],
        compiler_params=_params(("parallel", "arbitrary")),
        name="conformer",
    )(p3, p3, cache, b_glu.reshape(1, 2 * W), b_glu.reshape(1, 2 * W), w_dw, b_dw.reshape(1, W),
      g_ln.reshape(1, W), b_ln.reshape(1, W))


def _rwkv_prep(x_refs, prev_scrs, mu_refs, w0_ref, a0_ref, wba_ref, gb_ref, kkw_ref, kaw_ref, rk_ref, ones_ref,
               L, nb):
    def shift_mix(x_ref, prev_scr, mu_ref):
        x = x_ref[...].reshape(nb * L, x_ref.shape[-1])
        first = jnp.bitwise_and(lax.broadcasted_iota(jnp.int32, x.shape, 0), L - 1) == 0
        carried = jnp.concatenate([jnp.broadcast_to(prev_scr[ib], (L, x.shape[1])) for ib in range(nb)], axis=0)
        prev = jnp.where(first, carried, pltpu.roll(x, 1, 0))
        for ib in range(nb):
            prev_scr[ib] = x[(ib + 1) * L - 1:(ib + 1) * L, :]
        return x + (prev - x) * mu_ref[...]

    r, k, v, z = (shift_mix(x, p, m) for x, p, m in zip(x_refs, prev_scrs, mu_refs))
    zwa = z[:, 0:128]
    lane = lax.broadcasted_iota(jnp.int32, zwa.shape, 1)
    wa = _mm(jnp.where(lane < LORA_W, jnp.tanh(zwa), zwa), wba_ref[...])
    lw = -RWKV_DECAY * _sigmoid(w0_ref[...] + wa[:, 0:W])
    a = _sigmoid(a0_ref[...] + wa[:, W:2 * W])
    g = _mm(_sigmoid(z[:, 128:256]), gb_ref[...])
    ones_bd = ones_ref[...]
    kk = k * kkw_ref[...]
    kk = kk * lax.rsqrt(_seg_sum(kk * kk, ones_bd) + 1e-12)
    k2 = k * (1.0 + (a - 1.0) * kaw_ref[...])
    bonus = _seg_sum(r * k2 * rk_ref[...], ones_bd) * v
    return dict(r=r, lw=lw, k=k2, v=v, kk=kk, b=kk * a), bonus, g


def _rwkv_kernel(pr_ref, pk_ref, pv_ref, pz_ref, sr_ref, sk_ref, sv_ref, sz_ref, mr_ref, mk_ref, mv_ref, mz_ref,
                 w0_ref, a0_ref, wba_ref, gb_ref, kkw_ref, kaw_ref, rk_ref, ones_ref, s0_ref,
                 y_ref, bonus_ref, g_ref, s1_ref, s_scr, qr_scr, qk_scr, qv_scr, qz_scr, *, L, nb, nc):
    c = pl.program_id(1)

    @pl.when(c == 0)
    def _():
        s_scr[...] = s0_ref[...]
        qr_scr[...] = sr_ref[...]
        qk_scr[...] = sk_ref[...]
        qv_scr[...] = sv_ref[...]
        qz_scr[...] = sz_ref[...]

    vals, bonus, g = _rwkv_prep((pr_ref, pk_ref, pv_ref, pz_ref), (qr_scr, qk_scr, qv_scr, qz_scr),
                                (mr_ref, mk_ref, mv_ref, mz_ref), w0_ref, a0_ref, wba_ref, gb_ref, kkw_ref,
                                kaw_ref, rk_ref, ones_ref, L, nb)
    bonus_ref[...] = bonus.reshape(nb, L, W)
    g_ref[...] = g.reshape(nb, L, W)
    chunk = lambda name, ib: vals[name][ib * L:(ib + 1) * L, :]
    row = lax.broadcasted_iota(jnp.int32, (L, L), 0)
    col = lax.broadcasted_iota(jnp.int32, (L, L), 1)
    eye = jnp.where(row == col, 1.0, 0.0)
    tri = jnp.where(row >= col, 1.0, 0.0).astype(BF)
    row2 = lax.broadcasted_iota(jnp.int32, (2 * L, 2 * L), 0)
    col2 = lax.broadcasted_iota(jnp.int32, (2 * L, 2 * L), 1)
    cc = jnp.where(col2 >= L, col2 - L, col2)
    keep = jnp.where(row2 < L, row2 - 1, row2 - L) >= cc
    sls = [slice(h * DH, (h + 1) * DH) for h in range(H)]
    chains = [(ib, h) for ib in range(nb) for h in range(H)]
    pre = {}
    for ib in range(nb):
        lw = chunk("lw", ib)
        cum = _mm_mask_l(tri, lw)
        p_in = jnp.exp(cum)
        p_inv = jnp.exp(-cum)
        p_last = p_in[L - 1:L, :]
        kh = chunk("k", ib) * p_inv
        bh = chunk("b", ib) * p_inv
        pre[ib] = dict(
            kr=jnp.concatenate([(chunk("kk", ib) * jnp.exp(cum - lw)).astype(BF),
                                (chunk("r", ib) * p_in).astype(BF)], axis=0),
            kb=jnp.concatenate([kh.astype(BF), bh.astype(BF)], axis=0),
            kbl=jnp.concatenate([(kh * p_last).astype(BF), (bh * p_last).astype(BF)], axis=0),
            v=chunk("v", ib).astype(BF), p_last=p_last)
    op = lambda u, name: pre[u[0]][name][:, sls[u[1]]]
    gm = {u: jnp.where(keep, _mm_nt(op(u, "kr"), op(u, "kb")), 0.0) for u in chains}
    akr = {u: gm[u][:, 0:L].astype(BF) for u in chains}
    r_b = {u: gm[u][L:2 * L, L:2 * L].astype(BF) for u in chains}
    xs = {u: -gm[u][0:L, L:2 * L] for u in chains}
    invs = {u: eye + xs[u] for u in chains}
    if L > 2:
        xs = {u: _mm(xs[u], xs[u]) for u in chains}
    n = 4
    while n < L:
        st = {u: _mm(jnp.concatenate([invs[u], xs[u]], axis=0), xs[u]) for u in chains}
        invs = {u: invs[u] + st[u][0:L] for u in chains}
        xs = {u: st[u][L:2 * L] for u in chains}
        n *= 2
    if L > 2:
        invs = {u: invs[u] + _mm(invs[u], xs[u]) for u in chains}
    akrv = {u: _mm(akr[u], op(u, "v")) for u in chains}
    ss = {u: s_scr[u] for u in chains}
    krs = {u: _mm_nt(op(u, "kr"), ss[u]) for u in chains}
    us = {u: _mm(invs[u], krs[u][0:L] + akrv[u][0:L]) for u in chains}
    ys = {u: krs[u][L:2 * L] + akrv[u][L:2 * L] - _mm(r_b[u], us[u]) for u in chains}
    for u in chains:
        vu = jnp.concatenate([op(u, "v"), (-us[u]).astype(BF)], axis=0)
        s_scr[u] = ss[u] * pre[u[0]]["p_last"][:, sls[u[1]]] + _mm_tn(vu, op(u, "kbl"))
    for ib in range(nb):
        y_ref[ib] = jnp.concatenate([ys[ib, h] for h in range(H)], axis=1)

    @pl.when(c == nc - 1)
    def _():
        s1_ref[...] = s_scr[...]


def _rwkv(p3, shift, mu, w0, a0, wba_bf, gb_bf, kkw, kaw, rk, ones_bd, s0, L, nb):
    b, t, _ = p3.shape
    nc = t // L
    vec = lambda: pl.BlockSpec((1, W), lambda i, c: (0, 0))
    carry = lambda width, j: pl.BlockSpec((nb, 1, width), lambda i, c: (i, 0, j))
    mix = lambda width, j: pl.BlockSpec((1, width), lambda i, c: (0, j))
    blk = lambda: pl.BlockSpec((nb, L, W), lambda i, c: (i, c, 0))
    st = pl.BlockSpec((nb, H, DH, DH), lambda i, c: (i, 0, 0, 0))
    sh3 = shift.reshape(b, 1, -1)
    mu2 = mu.reshape(1, -1)
    return pl.pallas_call(
        functools.partial(_rwkv_kernel, L=L, nb=nb, nc=nc),
        grid=(b // nb, nc),
        in_specs=[pl.BlockSpec((nb, L, W), lambda i, c: (i, c, 2)),
                  pl.BlockSpec((nb, L, W), lambda i, c: (i, c, 3)),
                  pl.BlockSpec((nb, L, W), lambda i, c: (i, c, 4)),
                  pl.BlockSpec((nb, L, 256), lambda i, c: (i, c, 10)),
                  carry(W, 0), carry(W, 1), carry(W, 2), carry(256, 3 * W // 256),
                  mix(W, 0), mix(W, 1), mix(W, 2), mix(256, 3 * W // 256),
                  vec(), vec(),
                  pl.BlockSpec((128, 2 * W), lambda i, c: (0, 0)),
                  pl.BlockSpec((LORA_G, W), lambda i, c: (0, 0)),
                  vec(), vec(), vec(),
                  pl.BlockSpec((W, W), lambda i, c: (0, 0)),
                  st],
        out_specs=[blk(), blk(), blk(), st],
        out_shape=[jax.ShapeDtypeStruct((b, t, W), F32)] * 3 + [jax.ShapeDtypeStruct((b, H, DH, DH), F32)],
        scratch_shapes=[pltpu.VMEM((nb, H, DH, DH), F32), pltpu.VMEM((nb, 1, W), F32), pltpu.VMEM((nb, 1, W), F32),
                        pltpu.VMEM((nb, 1, W), F32), pltpu.VMEM((nb, 1, 256), F32)],
        compiler_params=_params(("parallel", "arbitrary")),
        name="rwkv",
    )(p3, p3, p3, p3,
      sh3, sh3, sh3, sh3, mu2, mu2, mu2, mu2,
      w0.reshape(1, W), a0.reshape(1, W), wba_bf, gb_bf, kkw.reshape(1, W), kaw.reshape(1, W),
      rk.reshape(1, W), ones_bd, s0)


def _block_diag(w):
    nb, bw, _ = w.shape
    return (jnp.eye(nb, dtype=w.dtype)[:, None, :, None] * w[:, :, None, :]).reshape(nb * bw, nb * bw)


def _chunk(t, target):
    return target if t % target == 0 else t


def _scan_blocking(b, t, target):
    L = _chunk(t, target)
    return L, min(b, 8)


def _run_group(x, mods, st, wts):
    b, t, d = x.shape
    x2 = x.reshape(b * t, d)
    (mc, mn, mm, lh, lconv, ccb, rs, rsh) = st

    sh1, sc1, gt1, sh2, sc2, gt2 = mods[0]
    e = wts["even"]
    p = _in_proj(x2, wts["g_mix"][0], sh1, sc1, e["w_in"], t)
    p3 = p.reshape(b, t, IN_EVEN_PAD)
    gif_t = jnp.transpose(p3[:, :, 6 * W:6 * W + 2 * H], (0, 2, 1))
    hm, c1, n1, m1 = _mlstm(p3, gif_t, e["b_if"], mc[0], mn[0], mm[0], *_scan_blocking(b, t, 128))
    hl, lh1 = _lru(p3, lconv[0], lh[0], e["w_conv"], e["b_conv"], e["w_gate"], e["b_gate"], e["lam"],
                   _chunk(t, 256), min(b, 4 if t >= 256 else 8))
    assert t >= LRU_CONV - 1
    conv1 = p3[:, t - (LRU_CONV - 1):, 4 * W:5 * W]
    x2 = _out_proj(x2, hl.reshape(b * t, W), hm.reshape(b * t, W), p, 3, None, e["g_head"], jnp.zeros((W,), F32),
                   wts["ones_bd"], e["w_out"], gt1, t, True, RMS_EPS, True)
    m = wts["moe"][0]
    x2 = _moe(x2, wts["g_ffn"][0], sh2, sc2, gt2, m["wr_hi"], m["wr_lo"], m["br"], wts["w1"], wts["w3"], wts["w2"],
              0, wts["g_final"], t, False)

    sh1, sc1, gt1, sh2, sc2, gt2 = mods[1]
    o = wts["odd"]
    p = _in_proj(x2, wts["g_mix"][1], sh1, sc1, o["w_in"], t)
    p3 = p.reshape(b, t, IN_ODD)
    cc, cc1 = _conf(p3, ccb[0], o["b_glu"], o["w_dw"], o["b_dw"], o["g_ln"], o["b_ln"], _chunk(t, 512),
                    min(b, 1 if t >= 256 else 8))
    y, bonus, g, s1 = _rwkv(p3, rsh[0], o["mu"], o["w0"], o["a0"], o["wba"], o["gb"], o["kkw"], o["kaw"], o["rk"],
                            wts["ones_bd"], rs[0], *_scan_blocking(b, t, 64))
    sh_out = p3[:, t - 1, 2 * W:]
    x2 = _out_proj(x2, cc.reshape(b * t, W), y.reshape(b * t, W), g.reshape(b * t, W), 0, bonus.reshape(b * t, W),
                   o["g_gn"], o["b_gn"], wts["ones_bd"], o["w_out"], gt1, t, False, RWKV_GN_EPS, False)
    m = wts["moe"][1]
    y2 = _moe(x2, wts["g_ffn"][1], sh2, sc2, gt2, m["wr_hi"], m["wr_lo"], m["br"], wts["w1"], wts["w3"], wts["w2"],
              1, wts["g_final"], t, True)
    states = (c1[None], n1[None], m1.reshape(1, b, H), lh1.reshape(1, b, W), conv1[None], cc1[None], s1[None],
              sh_out[None])
    return y2.reshape(b, t, d), states


def kernel(x_prompt, x_sample, c_prompt, c_sample, state_mlstm_C, state_mlstm_n, state_mlstm_m, state_lru_h,
           cache_lru_conv, cache_conformer_conv, state_rwkv_S, cache_rwkv_shift, w_ada, b_ada, g_norm_mix,
           g_norm_ffn, w_in_even, b_mlstm_if, g_mlstm_head, w_lru_conv, b_lru_conv, w_lru_r, b_lru_r, w_lru_i,
           b_lru_i, lru_lambda, w_out_even, w_in_odd, b_glu, w_cc_dw, b_cc_dw, g_cc_ln, b_cc_ln, rwkv_mu,
           rwkv_w0, rwkv_wB, rwkv_a0, rwkv_aB, rwkv_gB, rwkv_kk, rwkv_ka, rwkv_rk, g_rwkv_gn, b_rwkv_gn,
           w_out_odd, w_router_g, b_router_g, w_router_e, b_router_e, w_exp_gate, w_exp_up, w_exp_down, g_final):
    bp, bs = x_prompt.shape[0], x_sample.shape[0]

    wi = w_in_even[0]
    gcol = 4 * W
    w_in_e = jnp.concatenate([wi[:, :gcol], wi[:, gcol + 2 * H:], wi[:, gcol:gcol + 2 * H],
                              jnp.zeros((D_MODEL, LANES - 2 * H), F32)], axis=1).astype(BF)
    even = dict(
        w_in=w_in_e, b_if=b_mlstm_if[0], g_head=g_mlstm_head[0], w_conv=w_lru_conv[0], b_conv=b_lru_conv[0],
        w_gate=jnp.concatenate([_block_diag(w_lru_r[0]), _block_diag(w_lru_i[0])], axis=1).astype(BF),
        b_gate=jnp.concatenate([b_lru_r[0], b_lru_i[0]]), lam=lru_lambda[0], w_out=w_out_even[0].astype(BF))
    zl = jnp.zeros((LORA_W, W), F32)
    odd = dict(
        w_in=w_in_odd[0].astype(BF), b_glu=b_glu[0], w_dw=w_cc_dw[0], b_dw=b_cc_dw[0], g_ln=g_cc_ln[0],
        b_ln=b_cc_ln[0], mu=rwkv_mu[0], w0=rwkv_w0[0], a0=rwkv_a0[0],
        wba=jnp.concatenate([jnp.concatenate([rwkv_wB[0], zl], axis=1),
                             jnp.concatenate([zl, rwkv_aB[0]], axis=1)], axis=0).astype(BF),
        gb=rwkv_gB[0].astype(BF), kkw=rwkv_kk[0], kaw=rwkv_ka[0], rk=rwkv_rk[0], g_gn=g_rwkv_gn[0],
        b_gn=b_rwkv_gn[0], w_out=w_out_odd[0].astype(BF))
    moe = []
    for l in range(DEPTH):
        wr = jnp.concatenate([w_router_g[l], w_router_e[l],
                              jnp.zeros((D_MODEL, LANES - N_GROUPS - N_EXPERTS), F32)], axis=1)
        wr_hi = wr.astype(BF)
        wr_lo = (wr - wr_hi.astype(F32)).astype(BF)
        br = jnp.concatenate([b_router_g[l], b_router_e[l],
                              jnp.zeros((LANES - N_GROUPS - N_EXPERTS,), F32)]).reshape(1, LANES)
        moe.append(dict(wr_hi=wr_hi, wr_lo=wr_lo, br=br))
    ones_bd = _block_diag(jnp.ones((H, DH, DH), F32)).astype(BF)
    wts = dict(even=even, odd=odd, moe=moe, g_mix=g_norm_mix, g_ffn=g_norm_ffn, g_final=g_final, ones_bd=ones_bd,
               w1=w_exp_gate.astype(BF).reshape(DEPTH * N_EXPERTS, D_MODEL, D_EXPERT),
               w3=w_exp_up.astype(BF).reshape(DEPTH * N_EXPERTS, D_MODEL, D_EXPERT),
               w2=w_exp_down.astype(BF).reshape(DEPTH * N_EXPERTS, D_EXPERT, D_MODEL))

    rows = 2 * bs
    assert bp <= bs
    c_all = jnp.concatenate([c_sample, c_prompt, jnp.zeros((rows - bs - bp, D_MODEL), F32)], axis=0)
    mod = _ada(c_all, w_ada, b_ada)

    def mods_of(lo):
        return [tuple(_Mod(mod, l * rows + lo, j) for j in range(6)) for l in range(DEPTH)]

    z = lambda *s: jnp.zeros(s, F32)
    st_p = (z(1, bp, H, DH, DH), z(1, bp, H, DH), z(1, bp, H), z(1, bp, W), z(1, bp, LRU_CONV - 1, W),
            z(1, bp, CONV_C - 1, W), z(1, bp, H, DH, DH), z(1, bp, 3 * W + LORA_W + LORA_A + LORA_G))
    st_s = (state_mlstm_C, state_mlstm_n, state_mlstm_m, state_lru_h, cache_lru_conv, cache_conformer_conv,
            state_rwkv_S, cache_rwkv_shift)
    y_p, out_p = _run_group(x_prompt, mods_of(bs), st_p, wts)
    y_s, out_s = _run_group(x_sample, mods_of(0), st_s, wts)
    return (y_p, y_s) + tuple(out_p) + tuple(out_s)
```

```python
import functools
from typing import NamedTuple

import jax
import jax.numpy as jnp
from jax import lax
from jax.experimental import pallas as pl
from jax.experimental.pallas import tpu as pltpu

F32 = jnp.float32
BF = jnp.bfloat16

D_MODEL = 1024
DEPTH = 2
H = 8
DH = 64
W = 512
LRU_CONV = 4
LRU_C = 8.0
CONV_C = 31
LORA_W = 64
LORA_A = 64
LORA_G = 128
RWKV_DECAY = 0.606531
RWKV_GN_EPS = 64e-5
N_GROUPS = 4
E_PER_GROUP = 4
N_EXPERTS = 16
D_EXPERT = 256
RMS_EPS = 1e-6
LN_EPS = 1e-5
LANES = 128
IN_EVEN_PAD = 6 * W + LANES
IN_ODD = 2 * W + 3 * W + LORA_W + LORA_A + LORA_G

ROW_TILE = 1024
MOE_ROW_TILE = 512
VMEM_LIMIT = 48 * 1024 * 1024


def _mm(a, b):
    return jnp.dot(a.astype(BF), b.astype(BF), preferred_element_type=F32)


def _mm_nt(a, b):
    return lax.dot_general(a.astype(BF), b.astype(BF), (((1,), (1,)), ((), ())),
                           preferred_element_type=F32)


def _mm_tn(a, b):
    return lax.dot_general(a.astype(BF), b.astype(BF), (((0,), (0,)), ((), ())),
                           preferred_element_type=F32)


def _split3(x):
    hi = x.astype(BF)
    r = x - hi.astype(F32)
    mid = r.astype(BF)
    lo = (r - mid.astype(F32)).astype(BF)
    return hi, mid, lo


def _mm_mask_l(mask, x):
    return sum(jnp.dot(mask, p, preferred_element_type=F32) for p in _split3(x))


def _mm_mask_r(x, mask):
    return sum(jnp.dot(p, mask, preferred_element_type=F32) for p in _split3(x))


def _sigmoid(x):
    return 1.0 / (1.0 + jnp.exp(-x))


def _rows(v, tm):
    nb, _, c = v.shape
    if nb == 1:
        return v[0]
    return jnp.broadcast_to(v, (nb, tm // nb, c)).reshape(tm, c)


class _Mod(NamedTuple):
    arr: jax.Array
    row0: int
    col: int


def _mod_spec(m, T, tm):
    if tm <= T:
        per = T // tm
        return pl.BlockSpec((1, 1, D_MODEL), lambda i, *_: (m.row0 + i // per, 0, m.col))
    nbk = tm // T
    assert m.row0 % nbk == 0
    return pl.BlockSpec((nbk, 1, D_MODEL), lambda i, *_: (m.row0 // nbk + i, 0, m.col))


def _norm_mod(x, g, sh, sc):
    y = x * lax.rsqrt(jnp.mean(x * x, axis=-1, keepdims=True) + RMS_EPS) * g
    return y * (1.0 + sc) + sh


def _params(sem):
    return pltpu.CompilerParams(dimension_semantics=sem, vmem_limit_bytes=VMEM_LIMIT)


def _ada_kernel(c_ref, w_ref, b_ref, o_ref):
    y = _mm(c_ref[...], w_ref[0]) + b_ref[0]
    o_ref[...] = y.reshape(o_ref.shape)


def _ada(c_all, w, b):
    nb = c_all.shape[0]
    tn = 1536
    return pl.pallas_call(
        _ada_kernel,
        grid=(DEPTH, 6 * D_MODEL // tn),
        in_specs=[pl.BlockSpec((nb, D_MODEL), lambda l, j: (0, 0)),
                  pl.BlockSpec((1, D_MODEL, tn), lambda l, j: (l, 0, j)),
                  pl.BlockSpec((1, 1, tn), lambda l, j: (l, 0, j))],
        out_specs=pl.BlockSpec((nb, 1, tn), lambda l, j: (l, 0, j)),
        out_shape=jax.ShapeDtypeStruct((DEPTH * nb, 1, 6 * D_MODEL), F32),
        compiler_params=_params(("parallel", "parallel")),
        name="ada",
    )(c_all, w, b.reshape(DEPTH, 1, 6 * D_MODEL))


def _inproj_kernel(x_ref, g_ref, sh_ref, sc_ref, w_ref, o_ref, h_scr):
    tm = x_ref.shape[0]

    @pl.when(pl.program_id(1) == 0)
    def _():
        h = _norm_mod(x_ref[...], g_ref[...], _rows(sh_ref[...], tm), _rows(sc_ref[...], tm))
        h_scr[...] = h.astype(BF)

    o_ref[...] = jnp.dot(h_scr[...], w_ref[...], preferred_element_type=F32)


INPROJ_ROW_TILE = 512


def _in_proj(x2, g, sh, sc, w_bf, T):
    n, d = x2.shape
    cols = w_bf.shape[1]
    tm = INPROJ_ROW_TILE
    tn = cols
    return pl.pallas_call(
        _inproj_kernel,
        grid=(n // tm, cols // tn),
        in_specs=[pl.BlockSpec((tm, d), lambda i, j: (i, 0)),
                  pl.BlockSpec((1, d), lambda i, j: (0, 0)),
                  _mod_spec(sh, T, tm), _mod_spec(sc, T, tm),
                  pl.BlockSpec((d, tn), lambda i, j: (0, j))],
        out_specs=pl.BlockSpec((tm, tn), lambda i, j: (i, j)),
        out_shape=jax.ShapeDtypeStruct((n, cols), F32),
        scratch_shapes=[pltpu.VMEM((tm, d), BF)],
        compiler_params=_params(("parallel", "arbitrary")),
        name="in_proj",
    )(x2, g.reshape(1, d), sh.arr, sc.arr, w_bf)


def _seg_sum(x, ones_bd):
    hi = x.astype(BF)
    lo = (x - hi.astype(F32)).astype(BF)
    return (jnp.dot(hi, ones_bd, preferred_element_type=F32)
            + jnp.dot(lo, ones_bd, preferred_element_type=F32))


def _outproj_kernel(*refs, pre_first, eps, has_add, sigmoid_mul):
    if has_add:
        x_ref, a_ref, pre_ref, mul_ref, add_ref, gain_ref, bias_ref, w_ref, gt_ref, o_ref = refs
    else:
        x_ref, a_ref, pre_ref, mul_ref, gain_ref, bias_ref, w_ref, gt_ref, o_ref = refs
    tm = x_ref.shape[0]
    even = lax.broadcasted_iota(jnp.int32, (tm, 2 * DH), 1) < DH

    def head_mean(x):
        se = jnp.sum(jnp.where(even, x, 0.0), axis=1, keepdims=True)
        so = jnp.sum(jnp.where(even, 0.0, x), axis=1, keepdims=True)
        return jnp.where(even, se, so) * (1.0 / DH)

    ys = []
    for p in range(H // 2):
        sl = slice(p * 2 * DH, (p + 1) * 2 * DH)
        pre = pre_ref[:, sl]
        dev = pre - head_mean(pre)
        ys.append(dev * lax.rsqrt(head_mean(dev * dev) + eps))
    y = jnp.concatenate(ys, axis=1) * gain_ref[...] + bias_ref[...]
    if has_add:
        y = y + add_ref[...]
    m = mul_ref[...]
    y = (y * (_sigmoid(m) if sigmoid_mul else m)).astype(BF)
    first, second = (y, a_ref[...]) if pre_first else (a_ref[...], y)
    mix = (jnp.dot(first, w_ref[0:W, :], preferred_element_type=F32)
           + jnp.dot(second, w_ref[W:2 * W, :], preferred_element_type=F32))
    o_ref[...] = x_ref[...] + _rows(gt_ref[...], tm) * mix


def _out_proj(x2, a2, pre2, mul2, mul_col, add2, gain, bias, w_bf, gt, T, pre_first, eps, sigmoid_mul):
    n, d = x2.shape
    tm = ROW_TILE
    has_add = add2 is not None
    row = lambda c=0: pl.BlockSpec((tm, W), lambda i, c=c: (i, c))
    vec = lambda: pl.BlockSpec((1, W), lambda i: (0, 0))
    in_specs = [pl.BlockSpec((tm, d), lambda i: (i, 0)), row(), row(), row(mul_col)]
    args = [x2, a2, pre2, mul2]
    if has_add:
        in_specs.append(row())
        args.append(add2)
    in_specs += [vec(), vec(), pl.BlockSpec((2 * W, d), lambda i: (0, 0)), _mod_spec(gt, T, tm)]
    args += [gain.reshape(1, W), bias.reshape(1, W), w_bf, gt.arr]
    return pl.pallas_call(
        functools.partial(_outproj_kernel, pre_first=pre_first, eps=eps, has_add=has_add, sigmoid_mul=sigmoid_mul),
        grid=(n // tm,),
        in_specs=in_specs,
        out_specs=pl.BlockSpec((tm, d), lambda i: (i, 0)),
        out_shape=jax.ShapeDtypeStruct((n, d), F32),
        compiler_params=_params(("parallel",)),
        name="out_proj",
    )(*args)


def _route(logits):
    lane = lax.broadcasted_iota(jnp.int32, logits.shape, 1).astype(F32)
    neg = -jnp.inf
    is_g = lane < N_GROUPS
    lg = jnp.where(is_g, logits, neg)
    mg = jnp.max(lg, axis=1, keepdims=True)
    gsel = jnp.min(jnp.where(lg == mg, lane, float(LANES)), axis=1, keepdims=True)
    psum = jnp.sum(jnp.where(is_g, jnp.exp(lg - mg), 0.0), axis=1, keepdims=True)
    pg_sel = 1.0 / psum
    lo = N_GROUPS + E_PER_GROUP * gsel
    le = jnp.where((lane >= lo) & (lane < lo + E_PER_GROUP), logits, neg)
    v1 = jnp.max(le, axis=1, keepdims=True)
    i1 = jnp.min(jnp.where(le == v1, lane, float(LANES)), axis=1, keepdims=True)
    le2 = jnp.where(lane == i1, neg, le)
    v2 = jnp.max(le2, axis=1, keepdims=True)
    i2 = jnp.min(jnp.where(le2 == v2, lane, float(LANES)), axis=1, keepdims=True)
    e2 = jnp.exp(v2 - v1)
    p1 = 1.0 / (1.0 + e2)
    p2 = e2 / (1.0 + e2)
    return pg_sel * jnp.where(lane == i1, p1, jnp.where(lane == i2, p2, 0.0))


def _moe_kernel(x_ref, g_ref, sh_ref, sc_ref, gt_ref, wrh_ref, wrl_ref, br_ref, w1_ref, w3_ref, w2_ref,
                gf_ref, o_ref, *, final_norm):
    tm = x_ref.shape[0]
    h = _norm_mod(x_ref[...], g_ref[...], _rows(sh_ref[...], tm), _rows(sc_ref[...], tm))
    hb = h.astype(BF)
    hl = (h - hb.astype(F32)).astype(BF)
    logits = (jnp.dot(hb, wrh_ref[...], preferred_element_type=F32)
              + jnp.dot(hl, wrh_ref[...], preferred_element_type=F32)
              + jnp.dot(hb, wrl_ref[...], preferred_element_type=F32)) + br_ref[...]
    gate = _route(logits)
    lane = lax.broadcasted_iota(jnp.int32, gate.shape, 1)
    acc = None
    for e in range(N_EXPERTS):
        hg = jnp.dot(hb, w1_ref[e], preferred_element_type=F32)
        hu = jnp.dot(hb, w3_ref[e], preferred_element_type=F32)
        ge = jnp.sum(jnp.where(lane == e + N_GROUPS, gate, 0.0), axis=1, keepdims=True)
        hh = hg * _sigmoid(hg) * hu * ge
        part = jnp.dot(hh.astype(BF), w2_ref[e], preferred_element_type=F32)
        acc = part if acc is None else acc + part
    y = x_ref[...] + _rows(gt_ref[...], tm) * acc
    if final_norm:
        y = y * lax.rsqrt(jnp.mean(y * y, axis=-1, keepdims=True) + RMS_EPS) * gf_ref[...]
    o_ref[...] = y


def _moe(x2, g, sh, sc, gt, wr_hi, wr_lo, br, w1, w3, w2, layer, g_final, T, final_norm):
    n, d = x2.shape
    tm = MOE_ROW_TILE
    resident = lambda shape: pl.BlockSpec(shape, lambda i: (layer, 0, 0), pipeline_mode=pl.Buffered(1))
    return pl.pallas_call(
        functools.partial(_moe_kernel, final_norm=final_norm),
        grid=(n // tm,),
        in_specs=[pl.BlockSpec((tm, d), lambda i: (i, 0)),
                  pl.BlockSpec((1, d), lambda i: (0, 0)),
                  _mod_spec(sh, T, tm), _mod_spec(sc, T, tm), _mod_spec(gt, T, tm),
                  pl.BlockSpec((d, LANES), lambda i: (0, 0)),
                  pl.BlockSpec((d, LANES), lambda i: (0, 0)),
                  pl.BlockSpec((1, LANES), lambda i: (0, 0)),
                  resident((N_EXPERTS, d, D_EXPERT)), resident((N_EXPERTS, d, D_EXPERT)),
                  resident((N_EXPERTS, D_EXPERT, d)),
                  pl.BlockSpec((1, d), lambda i: (0, 0))],
        out_specs=pl.BlockSpec((tm, d), lambda i: (i, 0)),
        out_shape=jax.ShapeDtypeStruct((n, d), F32),
        compiler_params=_params(("parallel",)),
        name="moe",
    )(x2, g.reshape(1, d), sh.arr, sc.arr, gt.arr, wr_hi, wr_lo, br, w1, w3, w2, g_final.reshape(1, d))


def _log_sigmoid(x):
    return jnp.minimum(x, 0.0) - jnp.log(1.0 + jnp.exp(-jnp.abs(x)))


def _mlstm_kernel(q_ref, k_ref, v_ref, g_ref, gt_ref, bif_ref, bift_ref, c0_ref, n0_ref, m0_ref,
                  h_ref, c1_ref, n1_ref, m1_ref, st_scr, m_scr, *, L, nb, nc):
    c = pl.program_id(1)
    NP = H // 2
    r64 = lax.broadcasted_iota(jnp.int32, (DH, DH), 0)
    c64 = lax.broadcasted_iota(jnp.int32, (DH, DH), 1)
    eye64 = r64 == c64

    @pl.when(c == 0)
    def _():
        st_scr[...] = jnp.zeros_like(st_scr)
        for ib in range(nb):
            for h in range(H):
                p, o = divmod(h, 2)
                rs = slice(o * DH, (o + 1) * DH)
                st_scr[ib, p, rs, o * DH:(o + 1) * DH] = c0_ref[ib, h]
                n_col = jnp.sum(jnp.where(eye64, n0_ref[ib, h:h + 1, :], 0.0), axis=1, keepdims=True)
                st_scr[ib, p, rs, 2 * DH + o * DH:2 * DH + (o + 1) * DH] = jnp.broadcast_to(n_col, (DH, DH))
        m_scr[...] = m0_ref[...]

    row = lax.broadcasted_iota(jnp.int32, (L, L), 0)
    col = lax.broadcasted_iota(jnp.int32, (L, L), 1)
    causal = row >= col
    tri = jnp.where(causal, 1.0, 0.0).astype(BF)
    tri_u = jnp.where(row <= col, 1.0, 0.0).astype(BF)
    even = lax.broadcasted_iota(jnp.int32, (L, 2 * DH), 1) < DH
    row_s = lax.broadcasted_iota(jnp.int32, (2 * DH, 4 * DH), 0)
    lane_s = lax.broadcasted_iota(jnp.int32, (2 * DH, 4 * DH), 1)
    top = row_s < DH
    same_head = jnp.where(top, 0, DH) == jnp.bitwise_and(lane_s, DH)
    ones_l = jnp.ones((L, 2 * DH), BF)
    ibs = range(nb)
    g = [g_ref[ib] + bif_ref[...] for ib in ibs]
    gt = [gt_ref[ib] + bift_ref[...] for ib in ibs]
    bcum = [_mm_mask_l(tri, _log_sigmoid(g[ib])) for ib in ibs]
    bcum_t = [_mm_mask_r(_log_sigmoid(gt[ib]), tri_u) for ib in ibs]
    m_prev = [m_scr[ib] for ib in ibs]
    ch = [(ib, h) for ib in ibs for h in range(H)]
    prs = [(ib, p) for ib in ibs for p in range(NP)]
    pair_of = lambda u: (u[0], u[1] // 2)
    wide = lambda x: jnp.broadcast_to(x, (L, 2 * DH))
    lanes = {u: slice(u[1] * 2 * DH, (u[1] + 1) * 2 * DH) for u in prs}
    q_f = {u: q_ref[u[0], :, lanes[u]] for u in prs}
    q_b = {u: q_f[u].astype(BF) for u in prs}
    k_f = {u: k_ref[u[0], :, lanes[u]] * (DH ** -0.5) for u in prs}
    k_b = {u: k_f[u].astype(BF) for u in prs}
    rhs = {u: jnp.concatenate([v_ref[u[0], :, lanes[u]].astype(BF), ones_l], axis=1) for u in prs}
    odd = lax.broadcasted_iota(jnp.int32, (L, 2 * DH), 1) >= DH
    qk = {u: _mm_nt(jnp.where(odd if u[1] % 2 else even, q_f[pair_of(u)], 0.0).astype(BF), k_b[pair_of(u)])
          for u in ch}
    st = {u: st_scr[u[0], u[1]] for u in prs}
    qst = {u: _mm(q_b[u], st[u]) for u in prs}
    bc = {u: wide(bcum[u[0]][:, H + u[1]:H + u[1] + 1]) for u in ch}
    ic = {u: wide(g[u[0]][:, u[1]:u[1] + 1]) for u in ch}
    mp = {u: m_prev[u[0]][:, u[1]:u[1] + 1] for u in ch}
    dmat = {u: jnp.where(causal, bc[u][:, 0:L] + (gt[u[0]][u[1]:u[1] + 1, :] - bcum_t[u[0]][H + u[1]:H + u[1] + 1, :]),
                         -jnp.inf) for u in ch}
    g_inter = {u: bc[u] + mp[u] for u in ch}
    m_t = {u: jnp.maximum(g_inter[u], jnp.max(dmat[u], axis=1, keepdims=True)) for u in ch}
    s = {u: (qk[u] * jnp.exp(dmat[u] - m_t[u][:, 0:L])).astype(BF) for u in ch}
    w_inter = {u: jnp.exp(g_inter[u] - m_t[u]) for u in ch}
    sv = {u: _mm(s[u], rhs[pair_of(u)]) for u in ch}
    m_new = {u: m_t[u][L - 1:L, 0:1] for u in ch}
    b_last = {u: bcum[u[0]][L - 1:L, H + u[1]:H + u[1] + 1] for u in ch}
    w_s = {u: jnp.exp(b_last[u] - bc[u] + ic[u] - m_new[u]) for u in ch}
    decay = {u: jnp.exp(b_last[u] + mp[u] - m_new[u]) for u in ch}
    pick = lambda d, u: jnp.where(even, d[u[0], 2 * u[1]], d[u[0], 2 * u[1] + 1])
    hh = {}
    for u in prs:
        e, o = (u[0], 2 * u[1]), (u[0], 2 * u[1] + 1)
        w_pair = pick(w_inter, u)
        num = jnp.where(even, sv[e][:, 0:2 * DH], sv[o][:, 0:2 * DH]) + w_pair * qst[u][:, 0:2 * DH]
        den = jnp.where(even, sv[e][:, 2 * DH:], sv[o][:, 2 * DH:]) + w_pair * qst[u][:, 2 * DH:]
        hh[u] = num / jnp.maximum(jnp.abs(den), jnp.exp(-pick(m_t, u)))
        inc = _mm_tn((k_f[u] * pick(w_s, u)).astype(BF), rhs[u])
        st_scr[u[0], u[1]] = jnp.where(top, decay[e], decay[o]) * st[u] + jnp.where(same_head, inc, 0.0)
    for ib in ibs:
        h_ref[ib] = jnp.concatenate([hh[ib, p] for p in range(NP)], axis=1)
        m_scr[ib] = jnp.concatenate([m_new[ib, h] for h in range(H)], axis=1)

    @pl.when(c == nc - 1)
    def _():
        for ib in range(nb):
            for h in range(H):
                p, o = divmod(h, 2)
                rs = slice(o * DH, (o + 1) * DH)
                c1_ref[ib, h] = st_scr[ib, p, rs, o * DH:(o + 1) * DH]
                n_rep = st_scr[ib, p, rs, 2 * DH + o * DH:2 * DH + (o + 1) * DH]
                n1_ref[ib, h:h + 1, :] = jnp.sum(jnp.where(eye64, n_rep, 0.0), axis=0, keepdims=True)
        m1_ref[...] = m_scr[...]


def _mlstm(p3, gif_t, bif, c0, n0, m0, L, nb):
    b, t, _ = p3.shape
    nc = t // L
    col = lambda j: pl.BlockSpec((nb, L, W), lambda i, c, j=j: (i, c, j))
    bif_pad = jnp.zeros((1, LANES), F32).at[0, :2 * H].set(bif)
    return pl.pallas_call(
        functools.partial(_mlstm_kernel, L=L, nb=nb, nc=nc),
        grid=(b // nb, nc),
        in_specs=[col(0), col(1), col(2),
                  pl.BlockSpec((nb, L, LANES), lambda i, c: (i, c, 6 * W // LANES)),
                  pl.BlockSpec((nb, 2 * H, L), lambda i, c: (i, 0, c)),
                  pl.BlockSpec((1, LANES), lambda i, c: (0, 0)),
                  pl.BlockSpec((2 * H, 1), lambda i, c: (0, 0)),
                  pl.BlockSpec((nb, H, DH, DH), lambda i, c: (i, 0, 0, 0)),
                  pl.BlockSpec((nb, H, DH), lambda i, c: (i, 0, 0)),
                  pl.BlockSpec((nb, 1, H), lambda i, c: (i, 0, 0))],
        out_specs=[pl.BlockSpec((nb, L, W), lambda i, c: (i, c, 0)),
                   pl.BlockSpec((nb, H, DH, DH), lambda i, c: (i, 0, 0, 0)),
                   pl.BlockSpec((nb, H, DH), lambda i, c: (i, 0, 0)),
                   pl.BlockSpec((nb, 1, H), lambda i, c: (i, 0, 0))],
        out_shape=[jax.ShapeDtypeStruct((b, t, W), F32),
                   jax.ShapeDtypeStruct((b, H, DH, DH), F32),
                   jax.ShapeDtypeStruct((b, H, DH), F32),
                   jax.ShapeDtypeStruct((b, 1, H), F32)],
        scratch_shapes=[pltpu.VMEM((nb, H // 2, 2 * DH, 4 * DH), F32), pltpu.VMEM((nb, 1, H), F32)],
        compiler_params=_params(("parallel", "arbitrary")),
        name="mlstm",
    )(p3, p3, p3, p3, gif_t, bif_pad, bif.reshape(2 * H, 1), c0, n0, m0.reshape(b, 1, H))


def _gelu_tanh(x):
    return 0.5 * x * (1.0 + jnp.tanh(0.7978845608028654 * (x + 0.044715 * x * x * x)))


def _lru_kernel(xr_ref, xg_ref, cache_ref, h0_ref, wc_ref, bc_ref, wg_ref, bg_ref, lam_ref,
                o_ref, h1_ref, ext_scr, a_scr, u_scr, hs_scr, hc_scr, *, Tc, nb, nc):
    c = pl.program_id(1)
    K1 = LRU_CONV - 1

    @pl.when(c == 0)
    def _():
        ext_scr[:, 8 - K1:8, :] = cache_ref[...]
        hc_scr[...] = h0_ref[...]

    xcs = []
    for ib in range(nb):
        x = xr_ref[ib]
        ext_scr[ib, 8:8 + Tc, :] = x
        xc = bc_ref[...] + wc_ref[K1:K1 + 1, :] * x
        for d in range(1, LRU_CONV):
            xc = xc + wc_ref[K1 - d:K1 - d + 1, :] * ext_scr[ib, 8 - d:8 - d + Tc, :]
        ext_scr[ib, 8 - K1:8, :] = ext_scr[ib, 8 + Tc - K1:8 + Tc, :]
        xcs.append(xc)
    xc = jnp.concatenate(xcs, axis=0)
    gates = _mm(xc, wg_ref[...]) + bg_ref[...]
    r = _sigmoid(gates[:, 0:W])
    ig = _sigmoid(gates[:, W:2 * W])
    lam = lam_ref[...]
    softplus_neg = jnp.maximum(-lam, 0.0) + jnp.log(1.0 + jnp.exp(-jnp.abs(lam)))
    log_a = -LRU_C * r * softplus_neg
    a_scr[...] = jnp.exp(log_a).reshape(nb, Tc, W)
    th = jnp.tanh(log_a)
    one_minus_a2 = -2.0 * th / (1.0 - th)
    u_scr[...] = (jnp.sqrt(one_minus_a2) * (ig * xc)).reshape(nb, Tc, W)

    def body(t, hs):
        new = []
        for ib in range(nb):
            h = a_scr[ib, pl.ds(t, 1), :] * hs[ib] + u_scr[ib, pl.ds(t, 1), :]
            hs_scr[ib, pl.ds(t, 1), :] = h
            new.append(h)
        return tuple(new)

    h_fin = lax.fori_loop(0, Tc, body, tuple(hc_scr[ib] for ib in range(nb)), unroll=8)
    for ib in range(nb):
        hc_scr[ib] = h_fin[ib]
        o_ref[ib] = (hs_scr[ib] * _gelu_tanh(xg_ref[ib])).astype(BF)

    @pl.when(c == nc - 1)
    def _():
        h1_ref[...] = hc_scr[...]


def _lru(p3, cache, h0, w_conv, b_conv, wg_bf, bg, lam, Tc, nb):
    b, t, _ = p3.shape
    nc = t // Tc
    return pl.pallas_call(
        functools.partial(_lru_kernel, Tc=Tc, nb=nb, nc=nc),
        grid=(b // nb, nc),
        in_specs=[pl.BlockSpec((nb, Tc, W), lambda i, c: (i, c, 4)),
                  pl.BlockSpec((nb, Tc, W), lambda i, c: (i, c, 5)),
                  pl.BlockSpec((nb, LRU_CONV - 1, W), lambda i, c: (i, 0, 0)),
                  pl.BlockSpec((nb, 1, W), lambda i, c: (i, 0, 0)),
                  pl.BlockSpec((LRU_CONV, W), lambda i, c: (0, 0)),
                  pl.BlockSpec((1, W), lambda i, c: (0, 0)),
                  pl.BlockSpec((W, 2 * W), lambda i, c: (0, 0)),
                  pl.BlockSpec((1, 2 * W), lambda i, c: (0, 0)),
                  pl.BlockSpec((1, W), lambda i, c: (0, 0))],
        out_specs=[pl.BlockSpec((nb, Tc, W), lambda i, c: (i, c, 0)),
                   pl.BlockSpec((nb, 1, W), lambda i, c: (i, 0, 0))],
        out_shape=[jax.ShapeDtypeStruct((b, t, W), BF), jax.ShapeDtypeStruct((b, 1, W), F32)],
        scratch_shapes=[pltpu.VMEM((nb, 8 + Tc, W), F32), pltpu.VMEM((nb, Tc, W), F32), pltpu.VMEM((nb, Tc, W), F32),
                        pltpu.VMEM((nb, Tc, W), F32), pltpu.VMEM((nb, 1, W), F32)],
        compiler_params=_params(("parallel", "arbitrary")),
        name="lru",
    )(p3, p3, cache, h0.reshape(b, 1, W), w_conv, b_conv.reshape(1, W), wg_bf, bg.reshape(1, 2 * W),
      lam.reshape(1, W))


CONF_ROWS = 64


def _conf_kernel(u_ref, gte_ref, cache_ref, bu_ref, bg_ref, wdw_ref, bdw_ref, gln_ref, bln_ref,
                 o_ref, cache1_ref, ext_scr, sh_scr, *, Tc, nb, nc):
    c = pl.program_id(1)
    K1 = CONV_C - 1
    base = 32 - K1

    @pl.when(c == 0)
    def _():
        ext_scr[:, base:32, :] = cache_ref[...]

    for ib in range(nb):
        u = u_ref[ib] + bu_ref[...]
        gte = gte_ref[ib] + bg_ref[...]
        ext_scr[ib, 32:32 + Tc, :] = u * _sigmoid(gte)
        for b in range(1, 8):
            sh_scr[b - 1] = ext_scr[ib, b:b + Tc + 24, :]

        def window(off, r0, rb):
            a, b = divmod(off, 8)
            if b == 0:
                return ext_scr[ib, 8 * a + r0:8 * a + r0 + rb, :]
            return sh_scr[b - 1, 8 * a + r0:8 * a + r0 + rb, :]

        rb = min(CONF_ROWS, Tc)
        for r0 in range(0, Tc, rb):
            acc = bdw_ref[...] + wdw_ref[0:1, :] * window(base, r0, rb)
            for j in range(1, CONV_C):
                acc = acc + wdw_ref[j:j + 1, :] * window(base + j, r0, rb)
            mu = jnp.mean(acc, axis=1, keepdims=True)
            var = jnp.mean(jnp.square(acc - mu), axis=1, keepdims=True)
            y = (acc - mu) * lax.rsqrt(var + LN_EPS) * gln_ref[...] + bln_ref[...]
            o_ref[ib, r0:r0 + rb, :] = (y * _sigmoid(y)).astype(BF)
        ext_scr[ib, base:32, :] = ext_scr[ib, base + Tc:32 + Tc, :]

    @pl.when(c == nc - 1)
    def _():
        cache1_ref[...] = ext_scr[:, base:32, :]


def _conf(p3, cache, b_glu, w_dw, b_dw, g_ln, b_ln, Tc, nb):
    b, t, _ = p3.shape
    nc = t // Tc
    vec = lambda: pl.BlockSpec((1, W), lambda i, c: (0, 0))
    return pl.pallas_call(
        functools.partial(_conf_kernel, Tc=Tc, nb=nb, nc=nc),
        grid=(b // nb, nc),
        in_specs=[pl.BlockSpec((nb, Tc, W), lambda i, c: (i, c, 0)),
                  pl.BlockSpec((nb, Tc, W), lambda i, c: (i, c, 1)),
                  pl.BlockSpec((nb, CONV_C - 1, W), lambda i, c: (i, 0, 0)),
                  vec(), pl.BlockSpec((1, W), lambda i, c: (0, 1)),
                  pl.BlockSpec((CONV_C, W), lambda i, c: (0, 0)),
                  vec(), vec(), vec()],
        out_specs=[pl.BlockSpec((nb, Tc, W), lambda i, c: (i, c, 0)),
                   pl.BlockSpec((nb, CONV_C - 1, W), lambda i, c: (i, 0, 0))],
        out_shape=[jax.ShapeDtypeStruct((b, t, W), BF), jax.ShapeDtypeStruct((b, CONV_C - 1, W), F32)],
        scratch_shapes=[pltpu.VMEM((nb, 32 + Tc, W), F32), pltpu.VMEM((7, Tc + 24, W), F32)],
        compiler_params=_params(("parallel", "arbitrary")),
        name="conformer",
    )(p3, p3, cache, b_glu.reshape(1, 2 * W), b_glu.reshape(1, 2 * W), w_dw, b_dw.reshape(1, W),
      g_ln.reshape(1, W), b_ln.reshape(1, W))


def _rwkv_prep(x_refs, prev_scrs, mu_refs, w0_ref, a0_ref, wba_ref, gb_ref, kkw_ref, kaw_ref, rk_ref, ones_ref,
               L, nb):
    def shift_mix(x_ref, prev_scr, mu_ref):
        x = x_ref[...].reshape(nb * L, x_ref.shape[-1])
        first = jnp.bitwise_and(lax.broadcasted_iota(jnp.int32, x.shape, 0), L - 1) == 0
        carried = jnp.concatenate([jnp.broadcast_to(prev_scr[ib], (L, x.shape[1])) for ib in range(nb)], axis=0)
        prev = jnp.where(first, carried, pltpu.roll(x, 1, 0))
        for ib in range(nb):
            prev_scr[ib] = x[(ib + 1) * L - 1:(ib + 1) * L, :]
        return x + (prev - x) * mu_ref[...]

    r, k, v, z = (shift_mix(x, p, m) for x, p, m in zip(x_refs, prev_scrs, mu_refs))
    zwa = z[:, 0:128]
    lane = lax.broadcasted_iota(jnp.int32, zwa.shape, 1)
    wa = _mm(jnp.where(lane < LORA_W, jnp.tanh(zwa), zwa), wba_ref[...])
    lw = -RWKV_DECAY * _sigmoid(w0_ref[...] + wa[:, 0:W])
    a = _sigmoid(a0_ref[...] + wa[:, W:2 * W])
    g = _mm(_sigmoid(z[:, 128:256]), gb_ref[...])
    ones_bd = ones_ref[...]
    kk = k * kkw_ref[...]
    kk = kk * lax.rsqrt(_seg_sum(kk * kk, ones_bd) + 1e-12)
    k2 = k * (1.0 + (a - 1.0) * kaw_ref[...])
    bonus = _seg_sum(r * k2 * rk_ref[...], ones_bd) * v
    return dict(r=r, lw=lw, k=k2, v=v, kk=kk, b=kk * a), bonus, g


def _rwkv_kernel(pr_ref, pk_ref, pv_ref, pz_ref, sr_ref, sk_ref, sv_ref, sz_ref, mr_ref, mk_ref, mv_ref, mz_ref,
                 w0_ref, a0_ref, wba_ref, gb_ref, kkw_ref, kaw_ref, rk_ref, ones_ref, s0_ref,
                 y_ref, bonus_ref, g_ref, s1_ref, s_scr, qr_scr, qk_scr, qv_scr, qz_scr, *, L, nb, nc):
    c = pl.program_id(1)

    @pl.when(c == 0)
    def _():
        s_scr[...] = s0_ref[...]
        qr_scr[...] = sr_ref[...]
        qk_scr[...] = sk_ref[...]
        qv_scr[...] = sv_ref[...]
        qz_scr[...] = sz_ref[...]

    vals, bonus, g = _rwkv_prep((pr_ref, pk_ref, pv_ref, pz_ref), (qr_scr, qk_scr, qv_scr, qz_scr),
                                (mr_ref, mk_ref, mv_ref, mz_ref), w0_ref, a0_ref, wba_ref, gb_ref, kkw_ref,
                                kaw_ref, rk_ref, ones_ref, L, nb)
    bonus_ref[...] = bonus.reshape(nb, L, W)
    g_ref[...] = g.reshape(nb, L, W)
    chunk = lambda name, ib: vals[name][ib * L:(ib + 1) * L, :]
    row = lax.broadcasted_iota(jnp.int32, (L, L), 0)
    col = lax.broadcasted_iota(jnp.int32, (L, L), 1)
    eye = jnp.where(row == col, 1.0, 0.0)
    tri = jnp.where(row >= col, 1.0, 0.0).astype(BF)
    row2 = lax.broadcasted_iota(jnp.int32, (2 * L, 2 * L), 0)
    col2 = lax.broadcasted_iota(jnp.int32, (2 * L, 2 * L), 1)
    cc = jnp.where(col2 >= L, col2 - L, col2)
    keep = jnp.where(row2 < L, row2 - 1, row2 - L) >= cc
    sls = [slice(h * DH, (h + 1) * DH) for h in range(H)]
    chains = [(ib, h) for ib in range(nb) for h in range(H)]
    pre = {}
    for ib in range(nb):
        lw = chunk("lw", ib)
        cum = _mm_mask_l(tri, lw)
        p_in = jnp.exp(cum)
        p_inv = jnp.exp(-cum)
        p_last = p_in[L - 1:L, :]
        kh = chunk("k", ib) * p_inv
        bh = chunk("b", ib) * p_inv
        pre[ib] = dict(
            kr=jnp.concatenate([(chunk("kk", ib) * jnp.exp(cum - lw)).astype(BF),
                                (chunk("r", ib) * p_in).astype(BF)], axis=0),
            kb=jnp.concatenate([kh.astype(BF), bh.astype(BF)], axis=0),
            kbl=jnp.concatenate([(kh * p_last).astype(BF), (bh * p_last).astype(BF)], axis=0),
            v=chunk("v", ib).astype(BF), p_last=p_last)
    op = lambda u, name: pre[u[0]][name][:, sls[u[1]]]
    gm = {u: jnp.where(keep, _mm_nt(op(u, "kr"), op(u, "kb")), 0.0) for u in chains}
    akr = {u: gm[u][:, 0:L].astype(BF) for u in chains}
    r_b = {u: gm[u][L:2 * L, L:2 * L].astype(BF) for u in chains}
    xs = {u: -gm[u][0:L, L:2 * L] for u in chains}
    invs = {u: eye + xs[u] for u in chains}
    if L > 2:
        xs = {u: _mm(xs[u], xs[u]) for u in chains}
    n = 4
    while n < L:
        st = {u: _mm(jnp.concatenate([invs[u], xs[u]], axis=0), xs[u]) for u in chains}
        invs = {u: invs[u] + st[u][0:L] for u in chains}
        xs = {u: st[u][L:2 * L] for u in chains}
        n *= 2
    if L > 2:
        invs = {u: invs[u] + _mm(invs[u], xs[u]) for u in chains}
    akrv = {u: _mm(akr[u], op(u, "v")) for u in chains}
    ss = {u: s_scr[u] for u in chains}
    krs = {u: _mm_nt(op(u, "kr"), ss[u]) for u in chains}
    us = {u: _mm(invs[u], krs[u][0:L] + akrv[u][0:L]) for u in chains}
    ys = {u: krs[u][L:2 * L] + akrv[u][L:2 * L] - _mm(r_b[u], us[u]) for u in chains}
    for u in chains:
        vu = jnp.concatenate([op(u, "v"), (-us[u]).astype(BF)], axis=0)
        s_scr[u] = ss[u] * pre[u[0]]["p_last"][:, sls[u[1]]] + _mm_tn(vu, op(u, "kbl"))
    for ib in range(nb):
        y_ref[ib] = jnp.concatenate([ys[ib, h] for h in range(H)], axis=1)

    @pl.when(c == nc - 1)
    def _():
        s1_ref[...] = s_scr[...]


def _rwkv(p3, shift, mu, w0, a0, wba_bf, gb_bf, kkw, kaw, rk, ones_bd, s0, L, nb):
    b, t, _ = p3.shape
    nc = t // L
    vec = lambda: pl.BlockSpec((1, W), lambda i, c: (0, 0))
    carry = lambda width, j: pl.BlockSpec((nb, 1, width), lambda i, c: (i, 0, j))
    mix = lambda width, j: pl.BlockSpec((1, width), lambda i, c: (0, j))
    blk = lambda: pl.BlockSpec((nb, L, W), lambda i, c: (i, c, 0))
    st = pl.BlockSpec((nb, H, DH, DH), lambda i, c: (i, 0, 0, 0))
    sh3 = shift.reshape(b, 1, -1)
    mu2 = mu.reshape(1, -1)
    return pl.pallas_call(
        functools.partial(_rwkv_kernel, L=L, nb=nb, nc=nc),
        grid=(b // nb, nc),
        in_specs=[pl.BlockSpec((nb, L, W), lambda i, c: (i, c, 2)),
                  pl.BlockSpec((nb, L, W), lambda i, c: (i, c, 3)),
                  pl.BlockSpec((nb, L, W), lambda i, c: (i, c, 4)),
                  pl.BlockSpec((nb, L, 256), lambda i, c: (i, c, 10)),
                  carry(W, 0), carry(W, 1), carry(W, 2), carry(256, 3 * W // 256),
                  mix(W, 0), mix(W, 1), mix(W, 2), mix(256, 3 * W // 256),
                  vec(), vec(),
                  pl.BlockSpec((128, 2 * W), lambda i, c: (0, 0)),
                  pl.BlockSpec((LORA_G, W), lambda i, c: (0, 0)),
                  vec(), vec(), vec(),
                  pl.BlockSpec((W, W), lambda i, c: (0, 0)),
                  st],
        out_specs=[blk(), blk(), blk(), st],
        out_shape=[jax.ShapeDtypeStruct((b, t, W), F32)] * 3 + [jax.ShapeDtypeStruct((b, H, DH, DH), F32)],
        scratch_shapes=[pltpu.VMEM((nb, H, DH, DH), F32), pltpu.VMEM((nb, 1, W), F32), pltpu.VMEM((nb, 1, W), F32),
                        pltpu.VMEM((nb, 1, W), F32), pltpu.VMEM((nb, 1, 256), F32)],
        compiler_params=_params(("parallel", "arbitrary")),
        name="rwkv",
    )(p3, p3, p3, p3,
      sh3, sh3, sh3, sh3, mu2, mu2, mu2, mu2,
      w0.reshape(1, W), a0.reshape(1, W), wba_bf, gb_bf, kkw.reshape(1, W), kaw.reshape(1, W),
      rk.reshape(1, W), ones_bd, s0)


def _block_diag(w):
    nb, bw, _ = w.shape
    return (jnp.eye(nb, dtype=w.dtype)[:, None, :, None] * w[:, :, None, :]).reshape(nb * bw, nb * bw)


def _chunk(t, target):
    return target if t % target == 0 else t


def _scan_blocking(b, t, target):
    L = _chunk(t, target)
    return L, min(b, 8)


def _run_group(x, mods, st, wts):
    b, t, d = x.shape
    x2 = x.reshape(b * t, d)
    (mc, mn, mm, lh, lconv, ccb, rs, rsh) = st

    sh1, sc1, gt1, sh2, sc2, gt2 = mods[0]
    e = wts["even"]
    p = _in_proj(x2, wts["g_mix"][0], sh1, sc1, e["w_in"], t)
    p3 = p.reshape(b, t, IN_EVEN_PAD)
    gif_t = jnp.transpose(p3[:, :, 6 * W:6 * W + 2 * H], (0, 2, 1))
    hm, c1, n1, m1 = _mlstm(p3, gif_t, e["b_if"], mc[0], mn[0], mm[0], *_scan_blocking(b, t, 128))
    hl, lh1 = _lru(p3, lconv[0], lh[0], e["w_conv"], e["b_conv"], e["w_gate"], e["b_gate"], e["lam"],
                   _chunk(t, 256), min(b, 4 if t >= 256 else 8))
    assert t >= LRU_CONV - 1
    conv1 = p3[:, t - (LRU_CONV - 1):, 4 * W:5 * W]
    x2 = _out_proj(x2, hl.reshape(b * t, W), hm.reshape(b * t, W), p, 3, None, e["g_head"], jnp.zeros((W,), F32),
                   e["w_out"], gt1, t, True, RMS_EPS, True)
    m = wts["moe"][0]
    x2 = _moe(x2, wts["g_ffn"][0], sh2, sc2, gt2, m["wr_hi"], m["wr_lo"], m["br"], wts["w1"], wts["w3"], wts["w2"],
              0, wts["g_final"], t, False)

    sh1, sc1, gt1, sh2, sc2, gt2 = mods[1]
    o = wts["odd"]
    p = _in_proj(x2, wts["g_mix"][1], sh1, sc1, o["w_in"], t)
    p3 = p.reshape(b, t, IN_ODD)
    cc, cc1 = _conf(p3, ccb[0], o["b_glu"], o["w_dw"], o["b_dw"], o["g_ln"], o["b_ln"], _chunk(t, 512),
                    min(b, 1 if t >= 256 else 8))
    y, bonus, g, s1 = _rwkv(p3, rsh[0], o["mu"], o["w0"], o["a0"], o["wba"], o["gb"], o["kkw"], o["kaw"], o["rk"],
                            wts["ones_bd"], rs[0], *_scan_blocking(b, t, 64))
    sh_out = p3[:, t - 1, 2 * W:]
    x2 = _out_proj(x2, cc.reshape(b * t, W), y.reshape(b * t, W), g.reshape(b * t, W), 0, bonus.reshape(b * t, W),
                   o["g_gn"], o["b_gn"], o["w_out"], gt1, t, False, RWKV_GN_EPS, False)
    m = wts["moe"][1]
    y2 = _moe(x2, wts["g_ffn"][1], sh2, sc2, gt2, m["wr_hi"], m["wr_lo"], m["br"], wts["w1"], wts["w3"], wts["w2"],
              1, wts["g_final"], t, True)
    states = (c1[None], n1[None], m1.reshape(1, b, H), lh1.reshape(1, b, W), conv1[None], cc1[None], s1[None],
              sh_out[None])
    return y2.reshape(b, t, d), states


def kernel(x_prompt, x_sample, c_prompt, c_sample, state_mlstm_C, state_mlstm_n, state_mlstm_m, state_lru_h,
           cache_lru_conv, cache_conformer_conv, state_rwkv_S, cache_rwkv_shift, w_ada, b_ada, g_norm_mix,
           g_norm_ffn, w_in_even, b_mlstm_if, g_mlstm_head, w_lru_conv, b_lru_conv, w_lru_r, b_lru_r, w_lru_i,
           b_lru_i, lru_lambda, w_out_even, w_in_odd, b_glu, w_cc_dw, b_cc_dw, g_cc_ln, b_cc_ln, rwkv_mu,
           rwkv_w0, rwkv_wB, rwkv_a0, rwkv_aB, rwkv_gB, rwkv_kk, rwkv_ka, rwkv_rk, g_rwkv_gn, b_rwkv_gn,
           w_out_odd, w_router_g, b_router_g, w_router_e, b_router_e, w_exp_gate, w_exp_up, w_exp_down, g_final):
    bp, bs = x_prompt.shape[0], x_sample.shape[0]

    wi = w_in_even[0]
    gcol = 4 * W
    w_in_e = jnp.concatenate([wi[:, :gcol], wi[:, gcol + 2 * H:], wi[:, gcol:gcol + 2 * H],
                              jnp.zeros((D_MODEL, LANES - 2 * H), F32)], axis=1).astype(BF)
    even = dict(
        w_in=w_in_e, b_if=b_mlstm_if[0], g_head=g_mlstm_head[0], w_conv=w_lru_conv[0], b_conv=b_lru_conv[0],
        w_gate=jnp.concatenate([_block_diag(w_lru_r[0]), _block_diag(w_lru_i[0])], axis=1).astype(BF),
        b_gate=jnp.concatenate([b_lru_r[0], b_lru_i[0]]), lam=lru_lambda[0], w_out=w_out_even[0].astype(BF))
    zl = jnp.zeros((LORA_W, W), F32)
    odd = dict(
        w_in=w_in_odd[0].astype(BF), b_glu=b_glu[0], w_dw=w_cc_dw[0], b_dw=b_cc_dw[0], g_ln=g_cc_ln[0],
        b_ln=b_cc_ln[0], mu=rwkv_mu[0], w0=rwkv_w0[0], a0=rwkv_a0[0],
        wba=jnp.concatenate([jnp.concatenate([rwkv_wB[0], zl], axis=1),
                             jnp.concatenate([zl, rwkv_aB[0]], axis=1)], axis=0).astype(BF),
        gb=rwkv_gB[0].astype(BF), kkw=rwkv_kk[0], kaw=rwkv_ka[0], rk=rwkv_rk[0], g_gn=g_rwkv_gn[0],
        b_gn=b_rwkv_gn[0], w_out=w_out_odd[0].astype(BF))
    moe = []
    for l in range(DEPTH):
        wr = jnp.concatenate([w_router_g[l], w_router_e[l],
                              jnp.zeros((D_MODEL, LANES - N_GROUPS - N_EXPERTS), F32)], axis=1)
        wr_hi = wr.astype(BF)
        wr_lo = (wr - wr_hi.astype(F32)).astype(BF)
        br = jnp.concatenate([b_router_g[l], b_router_e[l],
                              jnp.zeros((LANES - N_GROUPS - N_EXPERTS,), F32)]).reshape(1, LANES)
        moe.append(dict(wr_hi=wr_hi, wr_lo=wr_lo, br=br))
    ones_bd = _block_diag(jnp.ones((H, DH, DH), F32)).astype(BF)
    wts = dict(even=even, odd=odd, moe=moe, g_mix=g_norm_mix, g_ffn=g_norm_ffn, g_final=g_final, ones_bd=ones_bd,
               w1=w_exp_gate.astype(BF).reshape(DEPTH * N_EXPERTS, D_MODEL, D_EXPERT),
               w3=w_exp_up.astype(BF).reshape(DEPTH * N_EXPERTS, D_MODEL, D_EXPERT),
               w2=w_exp_down.astype(BF).reshape(DEPTH * N_EXPERTS, D_EXPERT, D_MODEL))

    rows = 2 * bs
    assert bp <= bs
    c_all = jnp.concatenate([c_sample, c_prompt, jnp.zeros((rows - bs - bp, D_MODEL), F32)], axis=0)
    mod = _ada(c_all, w_ada, b_ada)

    def mods_of(lo):
        return [tuple(_Mod(mod, l * rows + lo, j) for j in range(6)) for l in range(DEPTH)]

    z = lambda *s: jnp.zeros(s, F32)
    st_p = (z(1, bp, H, DH, DH), z(1, bp, H, DH), z(1, bp, H), z(1, bp, W), z(1, bp, LRU_CONV - 1, W),
            z(1, bp, CONV_C - 1, W), z(1, bp, H, DH, DH), z(1, bp, 3 * W + LORA_W + LORA_A + LORA_G))
    st_s = (state_mlstm_C, state_mlstm_n, state_mlstm_m, state_lru_h, cache_lru_conv, cache_conformer_conv,
            state_rwkv_S, cache_rwkv_shift)
    y_p, out_p = _run_group(x_prompt, mods_of(bs), st_p, wts)
    y_s, out_s = _run_group(x_sample, mods_of(0), st_s, wts)
    return (y_p, y_s) + tuple(out_p) + tuple(out_s)
```
